```python
import math
import jax, jax.numpy as jnp
from jax import lax
import numpy as np

D_MODEL = 2048
BATCH = 1
SEQ = 16384
DEPTH = 1

GRID_W = 64
WIN_H = 8
WIN_W = 16
Q_BLOCK = 128
HEAD_DIM = 128
NA_WIDTH = D_MODEL // 2
NA_HEADS = NA_WIDTH // HEAD_DIM
SSM_WIDTH = D_MODEL // 4
SSM_CH = 16
SSM_GROUPS = SSM_WIDTH // SSM_CH
SSM_STATE = 64
MEM_WIDTH = D_MODEL // 4
MEM_HEADS = 4
MEM_HEAD_DIM = MEM_WIDTH // MEM_HEADS
N_MEM = 256
MIX_WIDTH = NA_WIDTH + SSM_WIDTH + MEM_WIDTH
IN_WIDTH = 3 * NA_WIDTH + SSM_WIDTH + MEM_WIDTH
D_FF = 4 * D_MODEL
EPS = 1e-6
DT_MIN = 1e-3
DT_MAX = 1e-1
LAM_RE_MAX = -1e-4

kernel_name = "hybrid_natten_s5_memory_block"


def rmsnorm(x, g):
    xf = x.astype(jnp.float32)
    y = xf * lax.rsqrt(jnp.mean(xf * xf, axis=-1, keepdims=True) + EPS)
    return (y * g.astype(jnp.float32)).astype(x.dtype)


def neighbourhood_tables(S):
    rows = S // GRID_W
    kh = min(WIN_H, rows)
    kw = min(WIN_W, GRID_W)
    t = jnp.arange(S, dtype=jnp.int32)
    r = t // GRID_W
    c = t % GRID_W
    rs = jnp.clip(r - kh // 2, 0, rows - kh)
    cs = jnp.clip(c - kw // 2, 0, GRID_W - kw)
    kr = rs[:, None, None] + jnp.arange(kh, dtype=jnp.int32)[None, :, None]
    kc = cs[:, None, None] + jnp.arange(kw, dtype=jnp.int32)[None, None, :]
    nbr = (kr * GRID_W + kc).reshape(S, kh * kw)
    off = ((kr - r[:, None, None] + WIN_H - 1) * (2 * WIN_W - 1)
           + (kc - c[:, None, None] + WIN_W - 1)).reshape(S, kh * kw)
    return nbr, off


def neighbourhood_attention(q, k, v, rpb):
    B, S, H, Dh = q.shape
    nbr, off = neighbourhood_tables(S)
    nk = nbr.shape[1]
    nb = S // Q_BLOCK
    rpb_flat = rpb.reshape(H, -1).astype(jnp.float32)
    scale = Dh ** -0.5
    qb = q.reshape(B, nb, Q_BLOCK, H, Dh).transpose(1, 0, 2, 3, 4)

    def block(args):
        q_blk, idx, bidx = args
        k_g = k[:, idx]
        v_g = v[:, idx]
        s = jnp.einsum('bqhd,bqkhd->bhqk', q_blk, k_g,
                       preferred_element_type=jnp.float32) * scale + rpb_flat[:, bidx][None]
        p = jax.nn.softmax(s, axis=-1).astype(v.dtype)
        return jnp.einsum('bhqk,bqkhd->bqhd', p, v_g)

    out = lax.map(block, (qb, nbr.reshape(nb, Q_BLOCK, nk), off.reshape(nb, Q_BLOCK, nk)))
    return out.transpose(1, 0, 2, 3, 4).reshape(B, S, H * Dh)


def s5_direction(u, lam_re, lam_im, log_dt, b_re, b_im, c_re, c_im, reverse):
    lam = lax.complex(jnp.minimum(lam_re.astype(jnp.float32), LAM_RE_MAX),
                      lam_im.astype(jnp.float32))
    dt = jnp.exp(log_dt.astype(jnp.float32))[:, None]
    lam_bar = jnp.exp(lam * dt)
    b = lax.complex(b_re.astype(jnp.float32), b_im.astype(jnp.float32))
    b_bar = ((lam_bar - 1.0) / lam)[..., None] * b
    bu = jnp.einsum('bsgc,gpc->bsgp', u.astype(jnp.complex64), b_bar)
    a = jnp.broadcast_to(lam_bar, bu.shape)

    def combine(e1, e2):
        a1, x1 = e1
        a2, x2 = e2
        return a1 * a2, a2 * x1 + x2

    _, h = lax.associative_scan(combine, (a, bu), axis=1, reverse=reverse)
    c = lax.complex(c_re.astype(jnp.float32), c_im.astype(jnp.float32))
    return jnp.einsum('bsgp,gcp->bsgc', h, c).real


def s5_mixer(u_flat, lam_re, lam_im, log_dt, b_re, b_im, c_re, c_im, d, w_glu, b_glu):
    B, S, _ = u_flat.shape
    u = u_flat.reshape(B, S, SSM_GROUPS, SSM_CH).astype(jnp.float32)
    y_f = s5_direction(u, lam_re[0], lam_im[0], log_dt[0], b_re[0], b_im[0], c_re[0], c_im[0], False)
    y_b = s5_direction(u, lam_re[1], lam_im[1], log_dt[1], b_re[1], b_im[1], c_re[1], c_im[1], True)
    y = (y_f + y_b + d.astype(jnp.float32) * u).reshape(B, S, SSM_WIDTH)
    y = jax.nn.gelu(y)
    y = y * jax.nn.sigmoid(y @ w_glu.astype(jnp.float32) + b_glu.astype(jnp.float32))
    return y.astype(u_flat.dtype)


def memory_cross_attention(q_flat, mem, mem_norm, w_mem_kv):
    B, S, _ = q_flat.shape
    q = q_flat.reshape(B, S, MEM_HEADS, MEM_HEAD_DIM)
    kv = rmsnorm(mem, mem_norm) @ w_mem_kv
    M = kv.shape[1]
    k, v = jnp.split(kv, 2, axis=-1)
    k = k.reshape(B, M, MEM_HEADS, MEM_HEAD_DIM)
    v = v.reshape(B, M, MEM_HEADS, MEM_HEAD_DIM)
    s = jnp.einsum('bshd,bmhd->bhsm', q, k, preferred_element_type=jnp.float32) * MEM_HEAD_DIM ** -0.5
    p = jax.nn.softmax(s, axis=-1).astype(v.dtype)
    return jnp.einsum('bhsm,bmhd->bshd', p, v).reshape(B, S, MEM_WIDTH)


def setup_inputs(seed: int = 0) -> dict:
    key = jax.random.key(seed)
    ks = jax.random.split(key, 32)
    L = DEPTH
    f32 = jnp.float32

    def dense(k, shape, fan_in):
        return jax.random.normal(k, shape, f32) * fan_in ** -0.5

    def gain(k, shape):
        return 1.0 + 0.02 * jax.random.normal(k, shape, f32)

    n = jnp.arange(SSM_STATE, dtype=f32)
    G, P, Cg = SSM_GROUPS, SSM_STATE, SSM_CH
    return {
        "x": jax.random.normal(ks[0], (BATCH, SEQ, D_MODEL), f32),
        "mem": jax.random.normal(ks[1], (BATCH, N_MEM, D_MODEL), f32),
        "norm_mix_pre": gain(ks[2], (L, D_MODEL)),
        "w_in": dense(ks[3], (L, D_MODEL, IN_WIDTH), D_MODEL),
        "na_rpb": 0.02 * jax.random.normal(ks[4], (L, NA_HEADS, 2 * WIN_H - 1, 2 * WIN_W - 1), f32),
        "ssm_lam_re": -0.5 + 0.01 * jax.random.normal(ks[5], (L, 2, G, P), f32),
        "ssm_lam_im": math.pi * n + 0.01 * jax.random.normal(ks[6], (L, 2, G, P), f32),
        "ssm_log_dt": jax.random.uniform(ks[7], (L, 2, G), f32, math.log(DT_MIN), math.log(DT_MAX)),
        "ssm_b_re": dense(ks[8], (L, 2, G, P, Cg), 2 * Cg),
        "ssm_b_im": dense(ks[9], (L, 2, G, P, Cg), 2 * Cg),
        "ssm_c_re": dense(ks[10], (L, 2, G, Cg, P), 2 * P),
        "ssm_c_im": dense(ks[11], (L, 2, G, Cg, P), 2 * P),
        "ssm_d": jax.random.normal(ks[12], (L, G, Cg), f32),
        "w_glu": dense(ks[13], (L, SSM_WIDTH, SSM_WIDTH), SSM_WIDTH),
        "b_glu": 0.01 * jax.random.normal(ks[14], (L, SSM_WIDTH), f32),
        "mem_norm": gain(ks[15], (L, D_MODEL)),
        "w_mem_kv": dense(ks[16], (L, D_MODEL, 2 * MEM_WIDTH), D_MODEL),
        "out_norm_na": gain(ks[17], (L, NA_WIDTH)),
        "out_norm_ssm": gain(ks[18], (L, SSM_WIDTH)),
        "out_norm_mem": gain(ks[19], (L, MEM_WIDTH)),
        "w_out": dense(ks[20], (L, MIX_WIDTH, D_MODEL), MIX_WIDTH),
        "norm_mix_post": gain(ks[21], (L, D_MODEL)),
        "norm_mlp_pre": gain(ks[22], (L, D_MODEL)),
        "w_ff1": dense(ks[23], (L, D_MODEL, D_FF), D_MODEL),
        "w_ff2": dense(ks[24], (L, D_FF, D_MODEL), D_FF),
        "norm_mlp_post": gain(ks[25], (L, D_MODEL)),
    }


def reference(x, mem, norm_mix_pre, w_in, na_rpb, ssm_lam_re, ssm_lam_im, ssm_log_dt,
              ssm_b_re, ssm_b_im, ssm_c_re, ssm_c_im, ssm_d, w_glu, b_glu, mem_norm,
              w_mem_kv, out_norm_na, out_norm_ssm, out_norm_mem, w_out, norm_mix_post,
              norm_mlp_pre, w_ff1, w_ff2, norm_mlp_post):
    B, S, _ = x.shape
    for l in range(DEPTH):
        h = rmsnorm(x, norm_mix_pre[l])
        proj = h @ w_in[l]
        q_na, k_na, v_na, u_ssm, q_mem = jnp.split(
            proj, [NA_WIDTH, 2 * NA_WIDTH, 3 * NA_WIDTH, 3 * NA_WIDTH + SSM_WIDTH], axis=-1)
        hs = (B, S, NA_HEADS, HEAD_DIM)
        y_na = neighbourhood_attention(q_na.reshape(hs), k_na.reshape(hs), v_na.reshape(hs), na_rpb[l])
        y_ssm = s5_mixer(u_ssm, ssm_lam_re[l], ssm_lam_im[l], ssm_log_dt[l], ssm_b_re[l], ssm_b_im[l],
                         ssm_c_re[l], ssm_c_im[l], ssm_d[l], w_glu[l], b_glu[l])
        y_mem = memory_cross_attention(q_mem, mem, mem_norm[l], w_mem_kv[l])
        y = jnp.concatenate([rmsnorm(y_na, out_norm_na[l]),
                             rmsnorm(y_ssm, out_norm_ssm[l]),
                             rmsnorm(y_mem, out_norm_mem[l])], axis=-1)
        x = x + rmsnorm(y @ w_out[l], norm_mix_post[l])
        h = rmsnorm(x, norm_mlp_pre[l])
        f = jnp.square(jax.nn.relu(h @ w_ff1[l])) @ w_ff2[l]
        x = x + rmsnorm(f, norm_mlp_post[l])
    return x
```

```python
import functools

import numpy as np
import jax
import jax.numpy as jnp
from jax import lax
from jax.experimental import pallas as pl
from jax.experimental.pallas import tpu as pltpu

f32 = jnp.float32
bf16 = jnp.bfloat16

D_MODEL = 2048
GRID_W = 64
WIN_H = 8
WIN_W = 16
HEAD_DIM = 128
NA_WIDTH = 1024
NA_HEADS = 8
SSM_WIDTH = 512
SSM_CH = 16
SSM_GROUPS = 32
SSM_STATE = 64
MEM_WIDTH = 512
MEM_HEADS = 4
N_MEM = 256
IN_WIDTH = 4096
D_FF = 8192
EPS = 1e-6
LAM_RE_MAX = -1e-4

LANES = 128
SUBLANES = 8
GROUPS_PER_TILE = LANES // SSM_CH
N_TILES = SSM_WIDTH // LANES
TILE_STATES = GROUPS_PER_TILE * SSM_STATE
STATE_COLS = 4 * TILE_STATES
SSM_T = 8
MASK_VALUE = -1e30
VMEM_LIMIT = 56 * 1024 * 1024


def _rms(x, g):
    ms = jnp.mean(x * x, axis=-1, keepdims=True)
    return x * lax.rsqrt(ms + EPS) * g


def _params(sem):
    return pltpu.CompilerParams(dimension_semantics=sem, vmem_limit_bytes=VMEM_LIMIT)


def _proj_kernel(x_ref, g_ref, w_ref, p_ref, u_ref, h_scr, *, nchunk, scale):
    h_scr[...] = _rms(x_ref[...], g_ref[...]).astype(bf16)
    width = IN_WIDTH // nchunk
    for j in range(nchunk):
        acc = jnp.dot(h_scr[...], w_ref[:, j * width:(j + 1) * width], preferred_element_type=f32)
        if j == 0:
            acc = acc * scale
        if j == nchunk - 1:
            u_ref[...] = acc[:, :SSM_WIDTH]
            col = lax.broadcasted_iota(jnp.int32, (1, width), 1)
            acc = acc * jnp.where(col >= SSM_WIDTH, scale, 1.0)
        p_ref[:, j * width:(j + 1) * width] = acc.astype(bf16)


def _proj(x2, g, w_bf, tm=512):
    S = x2.shape[0]
    return pl.pallas_call(
        functools.partial(_proj_kernel, nchunk=4, scale=HEAD_DIM ** -0.5),
        grid=(S // tm,),
        in_specs=[pl.BlockSpec((tm, D_MODEL), lambda i: (i, 0)),
                  pl.BlockSpec((1, D_MODEL), lambda i: (0, 0)),
                  pl.BlockSpec((D_MODEL, IN_WIDTH), lambda i: (0, 0), pipeline_mode=pl.Buffered(1))],
        out_specs=[pl.BlockSpec((tm, IN_WIDTH), lambda i: (i, 0)),
                   pl.BlockSpec((tm, SSM_WIDTH), lambda i: (i, 0))],
        out_shape=[jax.ShapeDtypeStruct((S, IN_WIDTH), bf16),
                   jax.ShapeDtypeStruct((S, SSM_WIDTH), f32)],
        scratch_shapes=[pltpu.VMEM((tm, D_MODEL), bf16)],
        compiler_params=_params(("parallel",)),
        name="proj",
    )(x2, g, w_bf)


NA_QB = 128
NA_KB = 5
NA_EDGE = 2


def _na_window_start(b, nb):
    return jnp.clip(b - 2, 0, nb - NA_KB)


def _na_bias_index(S):
    rows = S // GRID_W
    nb = S // NA_QB
    variants = list(range(NA_EDGE)) + [NA_EDGE] + list(range(nb - NA_EDGE, nb))
    q = np.arange(NA_QB)
    n = np.arange(NA_KB * NA_QB)
    offs, valids = [], []
    for b in variants:
        wb = min(max(b - 2, 0), nb - NA_KB)
        r = (b * NA_QB + q) // GRID_W
        c = (b * NA_QB + q) % GRID_W
        rs = np.clip(r - WIN_H // 2, 0, rows - WIN_H)
        cs = np.clip(c - WIN_W // 2, 0, GRID_W - WIN_W)
        kr = (wb * NA_QB + n) // GRID_W
        kc = (wb * NA_QB + n) % GRID_W
        valid = ((kr[None, :] >= rs[:, None]) & (kr[None, :] < rs[:, None] + WIN_H)
                 & (kc[None, :] >= cs[:, None]) & (kc[None, :] < cs[:, None] + WIN_W))
        off = ((kr[None, :] - r[:, None] + WIN_H - 1) * (2 * WIN_W - 1)
               + (kc[None, :] - c[:, None] + WIN_W - 1))
        offs.append(np.where(valid, off, 0))
        valids.append(valid)
    return np.stack(offs).astype(np.int32), np.stack(valids)


def _na_kernel(*refs):
    q_ref = refs[0]
    k_refs = refs[1:1 + NA_KB]
    v_refs = refs[1 + NA_KB:1 + 2 * NA_KB]
    tab_ref, g_ref, o_ref = refs[1 + 2 * NA_KB:]
    outs = []
    for h in range(NA_HEADS):
        sl = slice(h * HEAD_DIM, (h + 1) * HEAD_DIM)
        q = q_ref[:, sl]
        s = jnp.concatenate(
            [lax.dot_general(q, kr[:, sl], (((1,), (1,)), ((), ())), preferred_element_type=f32)
             for kr in k_refs], axis=1) + tab_ref[0, h]
        m = jnp.max(s, axis=-1, keepdims=True)
        p = jnp.exp(s - m)
        l = jnp.sum(p, axis=-1, keepdims=True)
        pb = p.astype(bf16)
        o = jnp.dot(pb[:, :NA_QB], v_refs[0][:, sl], preferred_element_type=f32)
        for i in range(1, NA_KB):
            o = o + jnp.dot(pb[:, i * NA_QB:(i + 1) * NA_QB], v_refs[i][:, sl], preferred_element_type=f32)
        outs.append(o / l)
    y = jnp.concatenate(outs, axis=1)
    o_ref[...] = _rms(y, g_ref[...]).astype(bf16)


def _na_attn(P, tab, g):
    S = P.shape[0]
    nb = S // NA_QB

    def variant(b):
        return jnp.where(b < NA_EDGE, b, jnp.where(b >= nb - NA_EDGE, b - (nb - 2 * NA_EDGE - 1), NA_EDGE))

    kv_specs = [pl.BlockSpec((NA_QB, NA_WIDTH), (lambda b, i=i, c=c: (_na_window_start(b, nb) + i, c)))
                for c in (1, 2) for i in range(NA_KB)]
    return pl.pallas_call(
        _na_kernel,
        grid=(nb,),
        in_specs=[pl.BlockSpec((NA_QB, NA_WIDTH), lambda b: (b, 0))] + kv_specs + [
            pl.BlockSpec((1, NA_HEADS, NA_QB, NA_KB * NA_QB), lambda b: (variant(b), 0, 0, 0)),
            pl.BlockSpec((1, NA_WIDTH), lambda b: (0, 0))],
        out_specs=pl.BlockSpec((NA_QB, NA_WIDTH), lambda b: (b, 0)),
        out_shape=jax.ShapeDtypeStruct((S, NA_WIDTH), bf16),
        compiler_params=_params(("parallel",)),
        name="na_attn",
    )(P, *([P] * (2 * NA_KB)), tab, g)


def _memkv_kernel(m_ref, g_ref, w_ref, o_ref):
    o_ref[...] = jnp.dot(_rms(m_ref[...], g_ref[...]).astype(bf16), w_ref[...],
                         preferred_element_type=f32).astype(bf16)


def _mem_kv(mem2, g, w_bf):
    return pl.pallas_call(
        _memkv_kernel,
        out_shape=jax.ShapeDtypeStruct((N_MEM, 2 * MEM_WIDTH), bf16),
        compiler_params=pltpu.CompilerParams(vmem_limit_bytes=VMEM_LIMIT),
        name="mem_kv",
    )(mem2, g, w_bf)


def _memattn_kernel(q_ref, kv_ref, g_ref, o_ref):
    outs = []
    for h in range(MEM_HEADS):
        sl = slice(h * HEAD_DIM, (h + 1) * HEAD_DIM)
        s = lax.dot_general(q_ref[:, sl], kv_ref[:, sl], (((1,), (1,)), ((), ())),
                            preferred_element_type=f32)
        m = jnp.max(s, axis=-1, keepdims=True)
        p = jnp.exp(s - m)
        l = jnp.sum(p, axis=-1, keepdims=True)
        o = jnp.dot(p.astype(bf16), kv_ref[:, MEM_WIDTH + h * HEAD_DIM:MEM_WIDTH + (h + 1) * HEAD_DIM],
                    preferred_element_type=f32)
        outs.append(o / l)
    o_ref[...] = _rms(jnp.concatenate(outs, axis=1), g_ref[...]).astype(bf16)


def _mem_attn(P, kv, g, tm=512):
    S = P.shape[0]
    qblk = (IN_WIDTH - MEM_WIDTH) // MEM_WIDTH
    return pl.pallas_call(
        _memattn_kernel,
        grid=(S // tm,),
        in_specs=[pl.BlockSpec((tm, MEM_WIDTH), lambda i: (i, qblk)),
                  pl.BlockSpec((N_MEM, 2 * MEM_WIDTH), lambda i: (0, 0)),
                  pl.BlockSpec((1, MEM_WIDTH), lambda i: (0, 0))],
        out_specs=pl.BlockSpec((tm, MEM_WIDTH), lambda i: (i, 0)),
        out_shape=jax.ShapeDtypeStruct((S, MEM_WIDTH), bf16),
        compiler_params=_params(("parallel",)),
        name="mem_attn",
    )(P, kv, g)


def _ssm_prep_kernel(lre_r, lim_r, ldt_r, lre_c, lim_c, ldt_c, bre_ref, bim_ref, cre_ref, cim_ref,
                     wtoep_ref, wstate_ref, wcarry_ref, apow_ref):
    T = SSM_T
    TS = TILE_STATES
    reps = LANES // SSM_CH

    def same_group(shape, row_div, col_div):
        r = lax.broadcasted_iota(jnp.int32, shape, 0) // row_div
        c = lax.broadcasted_iota(jnp.int32, shape, 1) // col_div
        return (r == c).astype(f32)

    mask_state = same_group((LANES, TS), SSM_CH, SSM_STATE)
    mask_chan = same_group((LANES, LANES), SSM_CH, SSM_CH)

    kk = []
    for d in range(2):
        lre = jnp.minimum(lre_r[d, 0], LAM_RE_MAX)
        lim = lim_r[d, 0]
        dt = jnp.exp(ldt_r[d, 0])
        zr = lre * dt
        zi = lim * dt
        er = jnp.exp(zr)
        nr = er * jnp.cos(zi) - 1.0
        ni = er * jnp.sin(zi)
        den = lre * lre + lim * lim
        qr = (nr * lre + ni * lim) / den
        qi = (ni * lre - nr * lim) / den
        bre = bre_ref[d, 0]
        bim = bim_ref[d, 0]
        bbr = qr * bre - qi * bim
        bbi = qr * bim + qi * bre
        wc_r, wc_i = [], []
        for e in range(T):
            mag = jnp.exp(zr * float(e))
            pr = mag * jnp.cos(zi * float(e))
            pi = mag * jnp.sin(zi * float(e))
            wc_r.append(pr * bbr - pi * bbi)
            wc_i.append(pr * bbi + pi * bbr)

        for s in range(T):
            e = T - 1 - s if d == 0 else s
            rows = slice(s * LANES, (s + 1) * LANES)
            base = d * 2 * TS
            wstate_ref[0, rows, base:base + TS] = (
                jnp.concatenate([wc_r[e]] * reps, axis=0) * mask_state).astype(bf16)
            wstate_ref[0, rows, base + TS:base + 2 * TS] = (
                jnp.concatenate([wc_i[e]] * reps, axis=0) * mask_state).astype(bf16)

        cre = cre_ref[d, 0]
        cim = cim_ref[d, 0]
        kc = (jnp.dot(jnp.concatenate(wc_r, axis=0), cre, precision=lax.Precision.HIGHEST,
                      preferred_element_type=f32)
              - jnp.dot(jnp.concatenate(wc_i, axis=0), cim, precision=lax.Precision.HIGHEST,
                        preferred_element_type=f32))
        kk.append([jnp.concatenate([kc[e * SSM_CH:(e + 1) * SSM_CH]] * reps, axis=0) * mask_chan
                   for e in range(T)])

        lre_col = jnp.minimum(lre_c[d, 0], LAM_RE_MAX)
        dt_col = jnp.exp(ldt_c[d, 0])
        zr_col = lre_col * dt_col
        zi_col = lim_c[d, 0] * dt_col
        for t in range(T):
            e = float(t + 1 if d == 0 else T - t)
            mag = jnp.exp(zr_col * e)
            pr = mag * jnp.cos(zi_col * e)
            pi = mag * jnp.sin(zi_col * e)
            cols = slice(t * LANES, (t + 1) * LANES)
            base = d * 2 * TS
            wcarry_ref[0, base:base + TS, cols] = (cre * pr - cim * pi).astype(bf16)
            wcarry_ref[0, base + TS:base + 2 * TS, cols] = (-(cre * pi + cim * pr)).astype(bf16)

        k1 = (lax.broadcasted_iota(jnp.int32, (SUBLANES, TS), 0) + 1).astype(f32) * float(T)
        mag = jnp.exp(zr * k1)
        apow_ref[0, :, d * 2 * TS:d * 2 * TS + TS] = mag * jnp.cos(zi * k1)
        apow_ref[0, :, d * 2 * TS + TS:(d + 1) * 2 * TS] = mag * jnp.sin(zi * k1)

    for s in range(T):
        for t in range(T):
            if t > s:
                blk = kk[0][t - s]
            elif t < s:
                blk = kk[1][s - t]
            else:
                blk = kk[0][0] + kk[1][0]
            wtoep_ref[0, s * LANES:(s + 1) * LANES, t * LANES:(t + 1) * LANES] = blk.astype(bf16)


def _ssm_prep(lam_re, lam_im, log_dt, b_re, b_im, c_re, c_im):
    TS, NT, T = TILE_STATES, N_TILES, SSM_T
    gp = GROUPS_PER_TILE
    ldt = jnp.broadcast_to(log_dt[:, :, None], lam_re.shape)
    rows = [a.reshape(2, NT, 1, TS) for a in (lam_re, lam_im, ldt)]
    cols = [a.reshape(2, NT, TS, 1) for a in (lam_re, lam_im, ldt)]

    def b_layout(b):
        return b.reshape(2, NT, gp, SSM_STATE, SSM_CH).transpose(0, 1, 4, 2, 3).reshape(2, NT, SSM_CH, TS)

    eye = jnp.eye(gp, dtype=f32)

    def c_layout(c):
        ct = c.reshape(2, NT, gp, SSM_CH, SSM_STATE).transpose(0, 1, 2, 4, 3)
        return (ct[:, :, :, :, None, :] * eye[None, None, :, None, :, None]).reshape(2, NT, TS, LANES)

    row_spec = pl.BlockSpec((2, 1, 1, TS), lambda g: (0, g, 0, 0))
    col_spec = pl.BlockSpec((2, 1, TS, 1), lambda g: (0, g, 0, 0))
    b_spec = pl.BlockSpec((2, 1, SSM_CH, TS), lambda g: (0, g, 0, 0))
    c_spec = pl.BlockSpec((2, 1, TS, LANES), lambda g: (0, g, 0, 0))
    return pl.pallas_call(
        _ssm_prep_kernel,
        grid=(NT,),
        in_specs=[row_spec] * 3 + [col_spec] * 3 + [b_spec] * 2 + [c_spec] * 2,
        out_specs=[pl.BlockSpec((1, T * LANES, T * LANES), lambda g: (g, 0, 0)),
                   pl.BlockSpec((1, T * LANES, STATE_COLS), lambda g: (g, 0, 0)),
                   pl.BlockSpec((1, STATE_COLS, T * LANES), lambda g: (g, 0, 0)),
                   pl.BlockSpec((1, SUBLANES, STATE_COLS), lambda g: (g, 0, 0))],
        out_shape=[jax.ShapeDtypeStruct((NT, T * LANES, T * LANES), bf16),
                   jax.ShapeDtypeStruct((NT, T * LANES, STATE_COLS), bf16),
                   jax.ShapeDtypeStruct((NT, STATE_COLS, T * LANES), bf16),
                   jax.ShapeDtypeStruct((NT, SUBLANES, STATE_COLS), f32)],
        compiler_params=_params(("parallel",)),
        name="ssm_prep",
    )(*rows, *cols, b_layout(b_re), b_layout(b_im), c_layout(c_re), c_layout(c_im))


def _cmul_add(xr, xi, ar, ai, sr, si):
    return xr + ar * sr - ai * si, xi + ar * si + ai * sr


def _scan_rows(xr, xi, apr, api, cr, ci, reverse):
    n = SUBLANES
    rows = lax.broadcasted_iota(jnp.int32, xr.shape, 0)
    for k in (1, 2, 4):
        ar, ai = apr[k - 1:k], api[k - 1:k]
        if reverse:
            sr, si = pltpu.roll(xr, n - k, 0), pltpu.roll(xi, n - k, 0)
            keep = rows < n - k
        else:
            sr, si = pltpu.roll(xr, k, 0), pltpu.roll(xi, k, 0)
            keep = rows >= k
        xr, xi = _cmul_add(xr, xi, ar, ai, jnp.where(keep, sr, 0.0), jnp.where(keep, si, 0.0))
    if reverse:
        pwr = jnp.concatenate([apr[n - 1 - j:n - j] for j in range(n)], axis=0)
        pwi = jnp.concatenate([api[n - 1 - j:n - j] for j in range(n)], axis=0)
    else:
        pwr, pwi = apr, api
    hr, hi = _cmul_add(xr, xi, pwr, pwi, cr, ci)
    if reverse:
        inr = jnp.where(rows == n - 1, cr, pltpu.roll(hr, n - 1, 0))
        ini = jnp.where(rows == n - 1, ci, pltpu.roll(hi, n - 1, 0))
        return inr, ini, hr[0:1], hi[0:1]
    inr = jnp.where(rows == 0, cr, pltpu.roll(hr, 1, 0))
    ini = jnp.where(rows == 0, ci, pltpu.roll(hi, 1, 0))
    return inr, ini, hr[n - 1:n], hi[n - 1:n]


def _ssm_kernel(u_ref, d_ref, wtoep_ref, wstate_ref, wcarry_ref, apow_ref, o_ref, h_scr, *, nb, cb):
    T = SSM_T
    TS = TILE_STATES
    j = pl.program_id(1)

    def chunk_rows():
        return [u_ref[pl.ds(t, cb, stride=T), :] for t in range(T)]

    @pl.when(j < nb)
    def _():
        x = jnp.concatenate([ut.astype(bf16) for ut in chunk_rows()], axis=1)
        row0 = pl.multiple_of(j * cb, cb)
        h_scr[pl.ds(row0, cb), :] = jnp.dot(x, wstate_ref[0], preferred_element_type=f32)

    @pl.when(j == nb - 1)
    def _():
        nblk = nb * cb // SUBLANES
        zero = jnp.zeros((1, TS), f32)

        def body(i, carry):
            cfr, cfi, cbr, cbi = carry
            rf = pl.multiple_of(i * SUBLANES, SUBLANES)
            rb = pl.multiple_of((nblk - 1 - i) * SUBLANES, SUBLANES)
            inr, ini, cfr, cfi = _scan_rows(
                h_scr[pl.ds(rf, SUBLANES), 0:TS], h_scr[pl.ds(rf, SUBLANES), TS:2 * TS],
                apow_ref[0, :, 0:TS], apow_ref[0, :, TS:2 * TS], cfr, cfi, False)
            h_scr[pl.ds(rf, SUBLANES), 0:TS] = inr
            h_scr[pl.ds(rf, SUBLANES), TS:2 * TS] = ini
            inr, ini, cbr, cbi = _scan_rows(
                h_scr[pl.ds(rb, SUBLANES), 2 * TS:3 * TS], h_scr[pl.ds(rb, SUBLANES), 3 * TS:4 * TS],
                apow_ref[0, :, 2 * TS:3 * TS], apow_ref[0, :, 3 * TS:4 * TS], cbr, cbi, True)
            h_scr[pl.ds(rb, SUBLANES), 2 * TS:3 * TS] = inr
            h_scr[pl.ds(rb, SUBLANES), 3 * TS:4 * TS] = ini
            return cfr, cfi, cbr, cbi

        lax.fori_loop(0, nblk, body, (zero, zero, zero, zero))

    @pl.when(j >= nb)
    def _():
        us = chunk_rows()
        x = jnp.concatenate([ut.astype(bf16) for ut in us], axis=1)
        row0 = pl.multiple_of((j - nb) * cb, cb)
        y = (jnp.dot(x, wtoep_ref[0], preferred_element_type=f32)
             + jnp.dot(h_scr[pl.ds(row0, cb), :].astype(bf16), wcarry_ref[0], preferred_element_type=f32))
        for t in range(T):
            yt = y[:, t * LANES:(t + 1) * LANES] + d_ref[...] * us[t]
            o_ref[pl.ds(t, cb, stride=T), :] = jax.nn.gelu(yt)


def _ssm_main(u, d_row, wtoep, wstate, wcarry, apow, tb=4096):
    S = u.shape[0]
    T = SSM_T
    nb = S // tb
    cb = tb // T
    return pl.pallas_call(
        functools.partial(_ssm_kernel, nb=nb, cb=cb),
        grid=(N_TILES, 2 * nb),
        in_specs=[pl.BlockSpec((tb, LANES), lambda g, j: (j % nb, g)),
                  pl.BlockSpec((1, LANES), lambda g, j: (0, g)),
                  pl.BlockSpec((1, T * LANES, T * LANES), lambda g, j: (g, 0, 0), pipeline_mode=pl.Buffered(1)),
                  pl.BlockSpec((1, T * LANES, STATE_COLS), lambda g, j: (g, 0, 0), pipeline_mode=pl.Buffered(1)),
                  pl.BlockSpec((1, STATE_COLS, T * LANES), lambda g, j: (g, 0, 0), pipeline_mode=pl.Buffered(1)),
                  pl.BlockSpec((1, SUBLANES, STATE_COLS), lambda g, j: (g, 0, 0))],
        out_specs=pl.BlockSpec((tb, LANES), lambda g, j: (jnp.maximum(j - nb, 0), g)),
        out_shape=jax.ShapeDtypeStruct((S, SSM_WIDTH), f32),
        scratch_shapes=[pltpu.VMEM((S // T, STATE_COLS), f32)],
        compiler_params=_params(("arbitrary", "arbitrary")),
        name="ssm_main",
    )(u, d_row, wtoep, wstate, wcarry, apow)


def _outproj_kernel(yna_ref, yssm_ref, ymem_ref, x_ref, wglu_ref, bglu_ref, gssm_ref, wout_ref,
                    gpost_ref, gmlp_ref, x1_ref, h2_ref):
    ya = yssm_ref[...]
    gate = jax.nn.sigmoid(jnp.dot(ya.astype(bf16), wglu_ref[...], preferred_element_type=f32) + bglu_ref[...])
    ys = _rms(ya * gate, gssm_ref[...]).astype(bf16)
    a, b = NA_WIDTH, NA_WIDTH + SSM_WIDTH
    acc = (jnp.dot(yna_ref[...], wout_ref[0:a, :], preferred_element_type=f32)
           + jnp.dot(ys, wout_ref[a:b, :], preferred_element_type=f32)
           + jnp.dot(ymem_ref[...], wout_ref[b:, :], preferred_element_type=f32))
    x1 = x_ref[...] + _rms(acc, gpost_ref[...])
    x1_ref[...] = x1
    h2_ref[...] = _rms(x1, gmlp_ref[...]).astype(bf16)


def _out_proj(y_na, y_ssm, y_mem, x2, wglu_bf, b_glu, g_ssm, wout_bf, g_post, g_mlp, tm=512):
    S = x2.shape[0]
    row = lambda w: pl.BlockSpec((tm, w), lambda i: (i, 0))
    vec = lambda w: pl.BlockSpec((1, w), lambda i: (0, 0))
    return pl.pallas_call(
        _outproj_kernel,
        grid=(S // tm,),
        in_specs=[row(NA_WIDTH), row(SSM_WIDTH), row(MEM_WIDTH), row(D_MODEL),
                  pl.BlockSpec((SSM_WIDTH, SSM_WIDTH), lambda i: (0, 0)), vec(SSM_WIDTH), vec(SSM_WIDTH),
                  pl.BlockSpec((D_MODEL, D_MODEL), lambda i: (0, 0), pipeline_mode=pl.Buffered(1)),
                  vec(D_MODEL), vec(D_MODEL)],
        out_specs=[row(D_MODEL), row(D_MODEL)],
        out_shape=[jax.ShapeDtypeStruct((S, D_MODEL), f32), jax.ShapeDtypeStruct((S, D_MODEL), bf16)],
        compiler_params=_params(("parallel",)),
        name="out_proj",
    )(y_na, y_ssm, y_mem, x2, wglu_bf, b_glu, g_ssm, wout_bf, g_post, g_mlp)


def _mlp_kernel(h_ref, w1_ref, w2_ref, x1_ref, g_ref, o_ref, acc_ref):
    k = pl.program_id(1)
    hid = jnp.dot(h_ref[...], w1_ref[...], preferred_element_type=f32)
    hid = jnp.square(jnp.maximum(hid, 0.0)).astype(bf16)
    part = jnp.dot(hid, w2_ref[...], preferred_element_type=f32)

    @pl.when(k == 0)
    def _():
        acc_ref[...] = part

    @pl.when(k > 0)
    def _():
        acc_ref[...] += part

    @pl.when(k == pl.num_programs(1) - 1)
    def _():
        o_ref[...] = x1_ref[...] + _rms(acc_ref[...], g_ref[...])


def _mlp(h2, w1_bf, w2_bf, x1, g, tm=512, tk=512):
    S = h2.shape[0]
    return pl.pallas_call(
        _mlp_kernel,
        grid=(S // tm, D_FF // tk),
        in_specs=[pl.BlockSpec((tm, D_MODEL), lambda i, k: (i, 0)),
                  pl.BlockSpec((D_MODEL, tk), lambda i, k: (0, k)),
                  pl.BlockSpec((tk, D_MODEL), lambda i, k: (k, 0)),
                  pl.BlockSpec((tm, D_MODEL), lambda i, k: (i, 0)),
                  pl.BlockSpec((1, D_MODEL), lambda i, k: (0, 0))],
        out_specs=pl.BlockSpec((tm, D_MODEL), lambda i, k: (i, 0)),
        out_shape=jax.ShapeDtypeStruct((S, D_MODEL), f32),
        scratch_shapes=[pltpu.VMEM((tm, D_MODEL), f32)],
        compiler_params=_params(("parallel", "arbitrary")),
        name="mlp",
    )(h2, w1_bf, w2_bf, x1, g)


def kernel(x, mem, norm_mix_pre, w_in, na_rpb, ssm_lam_re, ssm_lam_im, ssm_log_dt, ssm_b_re, ssm_b_im, ssm_c_re, ssm_c_im, ssm_d, w_glu, b_glu, mem_norm, w_mem_kv, out_norm_na, out_norm_ssm, out_norm_mem, w_out, norm_mix_post, norm_mlp_pre, w_ff1, w_ff2, norm_mlp_post):
    B, S, _ = x.shape
    assert B == 1 and x.shape[2] == D_MODEL and S % 4096 == 0 and w_in.shape[0] == 1
    x2 = x[0]
    l = 0
    vec = lambda a: a[l].reshape(1, -1)

    P, u = _proj(x2, vec(norm_mix_pre), w_in[l].astype(bf16))

    off, valid = _na_bias_index(S)
    rpb_flat = na_rpb[l].reshape(NA_HEADS, -1)
    tab = jnp.where(valid[None], rpb_flat[:, off], MASK_VALUE).transpose(1, 0, 2, 3)
    y_na = _na_attn(P, tab, vec(out_norm_na))

    kv = _mem_kv(mem[0], vec(mem_norm), w_mem_kv[l].astype(bf16))
    y_mem = _mem_attn(P, kv, vec(out_norm_mem))

    wtoep, wstate, wcarry, apow = _ssm_prep(ssm_lam_re[l], ssm_lam_im[l], ssm_log_dt[l], ssm_b_re[l],
                                            ssm_b_im[l], ssm_c_re[l], ssm_c_im[l])
    y_ssm = _ssm_main(u, ssm_d[l].reshape(1, SSM_WIDTH), wtoep, wstate, wcarry, apow)

    x1, h2 = _out_proj(y_na, y_ssm, y_mem, x2, w_glu[l].astype(bf16), vec(b_glu), vec(out_norm_ssm),
                       w_out[l].astype(bf16), vec(norm_mix_post), vec(norm_mlp_pre))
    out = _mlp(h2, w_ff1[l].astype(bf16), w_ff2[l].astype(bf16), x1, vec(norm_mlp_post))
    return out[None]
```

```python
import functools

import numpy as np
import jax
import jax.numpy as jnp
from jax import lax
from jax.experimental import pallas as pl
from jax.experimental.pallas import tpu as pltpu

f32 = jnp.float32
bf16 = jnp.bfloat16

D_MODEL = 2048
GRID_W = 64
WIN_H = 8
WIN_W = 16
HEAD_DIM = 128
NA_WIDTH = 1024
NA_HEADS = 8
SSM_WIDTH = 512
SSM_CH = 16
SSM_GROUPS = 32
SSM_STATE = 64
MEM_WIDTH = 512
MEM_HEADS = 4
N_MEM = 256
IN_WIDTH = 4096
D_FF = 8192
EPS = 1e-6
LAM_RE_MAX = -1e-4

LANES = 128
SUBLANES = 8
GROUPS_PER_TILE = LANES // SSM_CH
N_TILES = SSM_WIDTH // LANES
TILE_STATES = GROUPS_PER_TILE * SSM_STATE
STATE_COLS = 4 * TILE_STATES
SSM_T = 8
MASK_VALUE = -1e30
VMEM_LIMIT = 56 * 1024 * 1024


def _rms(x, g):
    ms = jnp.mean(x * x, axis=-1, keepdims=True)
    return x * lax.rsqrt(ms + EPS) * g


def _params(sem):
    return pltpu.CompilerParams(dimension_semantics=sem, vmem_limit_bytes=VMEM_LIMIT)


def _proj_kernel(x_ref, g_ref, w_ref, p_ref, u_ref, h_scr, *, nchunk, scale):
    h_scr[...] = _rms(x_ref[...], g_ref[...]).astype(bf16)
    width = IN_WIDTH // nchunk
    for j in range(nchunk):
        acc = jnp.dot(h_scr[...], w_ref[:, j * width:(j + 1) * width], preferred_element_type=f32)
        if j == 0:
            acc = acc * scale
        if j == nchunk - 1:
            u_ref[...] = acc[:, :SSM_WIDTH]
            col = lax.broadcasted_iota(jnp.int32, (1, width), 1)
            acc = acc * jnp.where(col >= SSM_WIDTH, scale, 1.0)
        p_ref[:, j * width:(j + 1) * width] = acc.astype(bf16)


def _proj(x2, g, w_bf, tm=512):
    S = x2.shape[0]
    return pl.pallas_call(
        functools.partial(_proj_kernel, nchunk=4, scale=HEAD_DIM ** -0.5),
        grid=(S // tm,),
        in_specs=[pl.BlockSpec((tm, D_MODEL), lambda i: (i, 0)),
                  pl.BlockSpec((1, D_MODEL), lambda i: (0, 0)),
                  pl.BlockSpec((D_MODEL, IN_WIDTH), lambda i: (0, 0), pipeline_mode=pl.Buffered(1))],
        out_specs=[pl.BlockSpec((tm, IN_WIDTH), lambda i: (i, 0)),
                   pl.BlockSpec((tm, SSM_WIDTH), lambda i: (i, 0))],
        out_shape=[jax.ShapeDtypeStruct((S, IN_WIDTH), bf16),
                   jax.ShapeDtypeStruct((S, SSM_WIDTH), f32)],
        scratch_shapes=[pltpu.VMEM((tm, D_MODEL), bf16)],
        compiler_params=_params(("parallel",)),
        name="proj",
    )(x2, g, w_bf)


NA_QB = 128
NA_KB = 5
NA_EDGE = 2


def _na_window_start(b, nb):
    return jnp.clip(b - 2, 0, nb - NA_KB)


def _na_bias_table(rpb, S):
    rows = S // GRID_W
    nb = S // NA_QB
    heads = rpb.shape[0]
    c = np.arange(GRID_W)
    cs = np.clip(c - WIN_W // 2, 0, GRID_W - WIN_W)
    colvalid = (c[None, :] >= cs[:, None]) & (c[None, :] < cs[:, None] + WIN_W)
    dx = c[None, :] - c[:, None] + WIN_W - 1
    onehot = ((dx[None] == np.arange(2 * WIN_W - 1)[:, None, None]) & colvalid[None]).astype(np.float32)
    blk = jnp.einsum('hyd,dck->hyck', rpb.astype(f32), onehot, precision=lax.Precision.HIGHEST)
    blk = jnp.where(colvalid[None, None], blk, MASK_VALUE)
    masked = jnp.full((heads, GRID_W, GRID_W), MASK_VALUE, f32)
    rows_per_q = NA_QB // GRID_W
    rows_per_win = NA_KB * NA_QB // GRID_W
    variants = list(range(NA_EDGE)) + [NA_EDGE] + list(range(nb - NA_EDGE, nb))
    pieces = []
    for b in variants:
        wb = min(max(b - 2, 0), nb - NA_KB)
        for rq in range(rows_per_q):
            r = b * rows_per_q + rq
            rs = min(max(r - WIN_H // 2, 0), rows - WIN_H)
            for j in range(rows_per_win):
                kr = wb * rows_per_q + j
                pieces.append(blk[:, kr - r + WIN_H - 1] if rs <= kr < rs + WIN_H else masked)
    tab = jnp.stack(pieces, axis=1).reshape(heads, len(variants), rows_per_q, rows_per_win, GRID_W, GRID_W)
    return tab.transpose(1, 0, 2, 4, 3, 5).reshape(len(variants), heads, NA_QB, NA_KB * NA_QB)


def _na_kernel(*refs):
    q_ref = refs[0]
    k_refs = refs[1:1 + NA_KB]
    v_refs = refs[1 + NA_KB:1 + 2 * NA_KB]
    tab_ref, g_ref, o_ref = refs[1 + 2 * NA_KB:]
    outs = []
    for h in range(NA_HEADS):
        sl = slice(h * HEAD_DIM, (h + 1) * HEAD_DIM)
        q = q_ref[:, sl]
        s = jnp.concatenate(
            [lax.dot_general(q, kr[:, sl], (((1,), (1,)), ((), ())), preferred_element_type=f32)
             for kr in k_refs], axis=1) + tab_ref[0, h]
        m = jnp.max(s, axis=-1, keepdims=True)
        p = jnp.exp(s - m)
        l = jnp.sum(p, axis=-1, keepdims=True)
        pb = p.astype(bf16)
        o = jnp.dot(pb[:, :NA_QB], v_refs[0][:, sl], preferred_element_type=f32)
        for i in range(1, NA_KB):
            o = o + jnp.dot(pb[:, i * NA_QB:(i + 1) * NA_QB], v_refs[i][:, sl], preferred_element_type=f32)
        outs.append(o / l)
    y = jnp.concatenate(outs, axis=1)
    o_ref[...] = _rms(y, g_ref[...]).astype(bf16)


def _na_attn(P, tab, g):
    S = P.shape[0]
    nb = S // NA_QB

    def variant(b):
        return jnp.where(b < NA_EDGE, b, jnp.where(b >= nb - NA_EDGE, b - (nb - 2 * NA_EDGE - 1), NA_EDGE))

    kv_specs = [pl.BlockSpec((NA_QB, NA_WIDTH), (lambda b, i=i, c=c: (_na_window_start(b, nb) + i, c)))
                for c in (1, 2) for i in range(NA_KB)]
    return pl.pallas_call(
        _na_kernel,
        grid=(nb,),
        in_specs=[pl.BlockSpec((NA_QB, NA_WIDTH), lambda b: (b, 0))] + kv_specs + [
            pl.BlockSpec((1, NA_HEADS, NA_QB, NA_KB * NA_QB), lambda b: (variant(b), 0, 0, 0)),
            pl.BlockSpec((1, NA_WIDTH), lambda b: (0, 0))],
        out_specs=pl.BlockSpec((NA_QB, NA_WIDTH), lambda b: (b, 0)),
        out_shape=jax.ShapeDtypeStruct((S, NA_WIDTH), bf16),
        compiler_params=_params(("parallel",)),
        name="na_attn",
    )(P, *([P] * (2 * NA_KB)), tab, g)


def _memkv_kernel(m_ref, g_ref, w_ref, o_ref):
    o_ref[...] = jnp.dot(_rms(m_ref[...], g_ref[...]).astype(bf16), w_ref[...],
                         preferred_element_type=f32).astype(bf16)


def _mem_kv(mem2, g, w_bf):
    return pl.pallas_call(
        _memkv_kernel,
        out_shape=jax.ShapeDtypeStruct((N_MEM, 2 * MEM_WIDTH), bf16),
        compiler_params=pltpu.CompilerParams(vmem_limit_bytes=VMEM_LIMIT),
        name="mem_kv",
    )(mem2, g, w_bf)


def _memattn_kernel(q_ref, kv_ref, g_ref, o_ref):
    outs = []
    for h in range(MEM_HEADS):
        sl = slice(h * HEAD_DIM, (h + 1) * HEAD_DIM)
        s = lax.dot_general(q_ref[:, sl], kv_ref[:, sl], (((1,), (1,)), ((), ())),
                            preferred_element_type=f32)
        m = jnp.max(s, axis=-1, keepdims=True)
        p = jnp.exp(s - m)
        l = jnp.sum(p, axis=-1, keepdims=True)
        o = jnp.dot(p.astype(bf16), kv_ref[:, MEM_WIDTH + h * HEAD_DIM:MEM_WIDTH + (h + 1) * HEAD_DIM],
                    preferred_element_type=f32)
        outs.append(o / l)
    o_ref[...] = _rms(jnp.concatenate(outs, axis=1), g_ref[...]).astype(bf16)


def _mem_attn(P, kv, g, tm=512):
    S = P.shape[0]
    qblk = (IN_WIDTH - MEM_WIDTH) // MEM_WIDTH
    return pl.pallas_call(
        _memattn_kernel,
        grid=(S // tm,),
        in_specs=[pl.BlockSpec((tm, MEM_WIDTH), lambda i: (i, qblk)),
                  pl.BlockSpec((N_MEM, 2 * MEM_WIDTH), lambda i: (0, 0)),
                  pl.BlockSpec((1, MEM_WIDTH), lambda i: (0, 0))],
        out_specs=pl.BlockSpec((tm, MEM_WIDTH), lambda i: (i, 0)),
        out_shape=jax.ShapeDtypeStruct((S, MEM_WIDTH), bf16),
        compiler_params=_params(("parallel",)),
        name="mem_attn",
    )(P, kv, g)


def _ssm_prep_kernel(lre_r, lim_r, ldt_r, lre_c, lim_c, ldt_c, bre_ref, bim_ref, cre_ref, cim_ref,
                     wtoep_ref, wstate_ref, wcarry_ref, apow_ref):
    T = SSM_T
    TS = TILE_STATES
    reps = LANES // SSM_CH

    def same_group(shape, row_div, col_div):
        r = lax.broadcasted_iota(jnp.int32, shape, 0) // row_div
        c = lax.broadcasted_iota(jnp.int32, shape, 1) // col_div
        return (r == c).astype(f32)

    mask_state = same_group((LANES, TS), SSM_CH, SSM_STATE)
    mask_chan = same_group((LANES, LANES), SSM_CH, SSM_CH)

    kk = []
    for d in range(2):
        lre = jnp.minimum(lre_r[d, 0], LAM_RE_MAX)
        lim = lim_r[d, 0]
        dt = jnp.exp(ldt_r[d, 0])
        zr = lre * dt
        zi = lim * dt
        er = jnp.exp(zr)
        nr = er * jnp.cos(zi) - 1.0
        ni = er * jnp.sin(zi)
        den = lre * lre + lim * lim
        qr = (nr * lre + ni * lim) / den
        qi = (ni * lre - nr * lim) / den
        bre = bre_ref[d, 0]
        bim = bim_ref[d, 0]
        bbr = qr * bre - qi * bim
        bbi = qr * bim + qi * bre
        wc_r, wc_i = [], []
        for e in range(T):
            mag = jnp.exp(zr * float(e))
            pr = mag * jnp.cos(zi * float(e))
            pi = mag * jnp.sin(zi * float(e))
            wc_r.append(pr * bbr - pi * bbi)
            wc_i.append(pr * bbi + pi * bbr)

        for s in range(T):
            e = T - 1 - s if d == 0 else s
            rows = slice(s * LANES, (s + 1) * LANES)
            base = d * 2 * TS
            wstate_ref[0, rows, base:base + TS] = (
                jnp.concatenate([wc_r[e]] * reps, axis=0) * mask_state).astype(bf16)
            wstate_ref[0, rows, base + TS:base + 2 * TS] = (
                jnp.concatenate([wc_i[e]] * reps, axis=0) * mask_state).astype(bf16)

        cre = cre_ref[d, 0]
        cim = cim_ref[d, 0]
        kc = (jnp.dot(jnp.concatenate(wc_r, axis=0), cre, precision=lax.Precision.HIGHEST,
                      preferred_element_type=f32)
              - jnp.dot(jnp.concatenate(wc_i, axis=0), cim, precision=lax.Precision.HIGHEST,
                        preferred_element_type=f32))
        kk.append([jnp.concatenate([kc[e * SSM_CH:(e + 1) * SSM_CH]] * reps, axis=0) * mask_chan
                   for e in range(T)])

        lre_col = jnp.minimum(lre_c[d, 0], LAM_RE_MAX)
        dt_col = jnp.exp(ldt_c[d, 0])
        zr_col = lre_col * dt_col
        zi_col = lim_c[d, 0] * dt_col
        for t in range(T):
            e = float(t + 1 if d == 0 else T - t)
            mag = jnp.exp(zr_col * e)
            pr = mag * jnp.cos(zi_col * e)
            pi = mag * jnp.sin(zi_col * e)
            cols = slice(t * LANES, (t + 1) * LANES)
            base = d * 2 * TS
            wcarry_ref[0, base:base + TS, cols] = (cre * pr - cim * pi).astype(bf16)
            wcarry_ref[0, base + TS:base + 2 * TS, cols] = (-(cre * pi + cim * pr)).astype(bf16)

        k1 = (lax.broadcasted_iota(jnp.int32, (SUBLANES, TS), 0) + 1).astype(f32) * float(T)
        mag = jnp.exp(zr * k1)
        apow_ref[0, :, d * 2 * TS:d * 2 * TS + TS] = mag * jnp.cos(zi * k1)
        apow_ref[0, :, d * 2 * TS + TS:(d + 1) * 2 * TS] = mag * jnp.sin(zi * k1)

    for s in range(T):
        for t in range(T):
            if t > s:
                blk = kk[0][t - s]
            elif t < s:
                blk = kk[1][s - t]
            else:
                blk = kk[0][0] + kk[1][0]
            wtoep_ref[0, s * LANES:(s + 1) * LANES, t * LANES:(t + 1) * LANES] = blk.astype(bf16)


def _ssm_prep(lam_re, lam_im, log_dt, b_re, b_im, c_re, c_im):
    TS, NT, T = TILE_STATES, N_TILES, SSM_T
    gp = GROUPS_PER_TILE
    ldt = jnp.broadcast_to(log_dt[:, :, None], lam_re.shape)
    rows = [a.reshape(2, NT, 1, TS) for a in (lam_re, lam_im, ldt)]
    cols = [a.reshape(2, NT, TS, 1) for a in (lam_re, lam_im, ldt)]

    def b_layout(b):
        return b.reshape(2, NT, gp, SSM_STATE, SSM_CH).transpose(0, 1, 4, 2, 3).reshape(2, NT, SSM_CH, TS)

    eye = jnp.eye(gp, dtype=f32)

    def c_layout(c):
        ct = c.reshape(2, NT, gp, SSM_CH, SSM_STATE).transpose(0, 1, 2, 4, 3)
        return (ct[:, :, :, :, None, :] * eye[None, None, :, None, :, None]).reshape(2, NT, TS, LANES)

    row_spec = pl.BlockSpec((2, 1, 1, TS), lambda g: (0, g, 0, 0))
    col_spec = pl.BlockSpec((2, 1, TS, 1), lambda g: (0, g, 0, 0))
    b_spec = pl.BlockSpec((2, 1, SSM_CH, TS), lambda g: (0, g, 0, 0))
    c_spec = pl.BlockSpec((2, 1, TS, LANES), lambda g: (0, g, 0, 0))
    return pl.pallas_call(
        _ssm_prep_kernel,
        grid=(NT,),
        in_specs=[row_spec] * 3 + [col_spec] * 3 + [b_spec] * 2 + [c_spec] * 2,
        out_specs=[pl.BlockSpec((1, T * LANES, T * LANES), lambda g: (g, 0, 0)),
                   pl.BlockSpec((1, T * LANES, STATE_COLS), lambda g: (g, 0, 0)),
                   pl.BlockSpec((1, STATE_COLS, T * LANES), lambda g: (g, 0, 0)),
                   pl.BlockSpec((1, SUBLANES, STATE_COLS), lambda g: (g, 0, 0))],
        out_shape=[jax.ShapeDtypeStruct((NT, T * LANES, T * LANES), bf16),
                   jax.ShapeDtypeStruct((NT, T * LANES, STATE_COLS), bf16),
                   jax.ShapeDtypeStruct((NT, STATE_COLS, T * LANES), bf16),
                   jax.ShapeDtypeStruct((NT, SUBLANES, STATE_COLS), f32)],
        compiler_params=_params(("parallel",)),
        name="ssm_prep",
    )(*rows, *cols, b_layout(b_re), b_layout(b_im), c_layout(c_re), c_layout(c_im))


def _cmul_add(xr, xi, ar, ai, sr, si):
    return xr + ar * sr - ai * si, xi + ar * si + ai * sr


def _scan_rows(xr, xi, apr, api, cr, ci, reverse):
    n = SUBLANES
    rows = lax.broadcasted_iota(jnp.int32, xr.shape, 0)
    for k in (1, 2, 4):
        ar, ai = apr[k - 1:k], api[k - 1:k]
        if reverse:
            sr, si = pltpu.roll(xr, n - k, 0), pltpu.roll(xi, n - k, 0)
            keep = rows < n - k
        else:
            sr, si = pltpu.roll(xr, k, 0), pltpu.roll(xi, k, 0)
            keep = rows >= k
        xr, xi = _cmul_add(xr, xi, ar, ai, jnp.where(keep, sr, 0.0), jnp.where(keep, si, 0.0))
    if reverse:
        pwr = jnp.concatenate([apr[n - 1 - j:n - j] for j in range(n)], axis=0)
        pwi = jnp.concatenate([api[n - 1 - j:n - j] for j in range(n)], axis=0)
    else:
        pwr, pwi = apr, api
    hr, hi = _cmul_add(xr, xi, pwr, pwi, cr, ci)
    if reverse:
        inr = jnp.where(rows == n - 1, cr, pltpu.roll(hr, n - 1, 0))
        ini = jnp.where(rows == n - 1, ci, pltpu.roll(hi, n - 1, 0))
        return inr, ini, hr[0:1], hi[0:1]
    inr = jnp.where(rows == 0, cr, pltpu.roll(hr, 1, 0))
    ini = jnp.where(rows == 0, ci, pltpu.roll(hi, 1, 0))
    return inr, ini, hr[n - 1:n], hi[n - 1:n]


def _ssm_kernel(u_ref, d_ref, wtoep_ref, wstate_ref, wcarry_ref, apow_ref, o_ref, h_scr, *, nb, cb):
    T = SSM_T
    TS = TILE_STATES
    j = pl.program_id(1)

    def chunk_rows():
        return [u_ref[pl.ds(t, cb, stride=T), :] for t in range(T)]

    @pl.when(j < nb)
    def _():
        x = jnp.concatenate([ut.astype(bf16) for ut in chunk_rows()], axis=1)
        row0 = pl.multiple_of(j * cb, cb)
        h_scr[pl.ds(row0, cb), :] = jnp.dot(x, wstate_ref[0], preferred_element_type=f32)

    @pl.when(j == nb - 1)
    def _():
        nblk = nb * cb // SUBLANES
        zero = jnp.zeros((1, TS), f32)

        def body(i, carry):
            cfr, cfi, cbr, cbi = carry
            rf = pl.multiple_of(i * SUBLANES, SUBLANES)
            rb = pl.multiple_of((nblk - 1 - i) * SUBLANES, SUBLANES)
            inr, ini, cfr, cfi = _scan_rows(
                h_scr[pl.ds(rf, SUBLANES), 0:TS], h_scr[pl.ds(rf, SUBLANES), TS:2 * TS],
                apow_ref[0, :, 0:TS], apow_ref[0, :, TS:2 * TS], cfr, cfi, False)
            h_scr[pl.ds(rf, SUBLANES), 0:TS] = inr
            h_scr[pl.ds(rf, SUBLANES), TS:2 * TS] = ini
            inr, ini, cbr, cbi = _scan_rows(
                h_scr[pl.ds(rb, SUBLANES), 2 * TS:3 * TS], h_scr[pl.ds(rb, SUBLANES), 3 * TS:4 * TS],
                apow_ref[0, :, 2 * TS:3 * TS], apow_ref[0, :, 3 * TS:4 * TS], cbr, cbi, True)
            h_scr[pl.ds(rb, SUBLANES), 2 * TS:3 * TS] = inr
            h_scr[pl.ds(rb, SUBLANES), 3 * TS:4 * TS] = ini
            return cfr, cfi, cbr, cbi

        lax.fori_loop(0, nblk, body, (zero, zero, zero, zero))

    @pl.when(j >= nb)
    def _():
        us = chunk_rows()
        x = jnp.concatenate([ut.astype(bf16) for ut in us], axis=1)
        row0 = pl.multiple_of((j - nb) * cb, cb)
        y = (jnp.dot(x, wtoep_ref[0], preferred_element_type=f32)
             + jnp.dot(h_scr[pl.ds(row0, cb), :].astype(bf16), wcarry_ref[0], preferred_element_type=f32))
        for t in range(T):
            yt = y[:, t * LANES:(t + 1) * LANES] + d_ref[...] * us[t]
            o_ref[pl.ds(t, cb, stride=T), :] = jax.nn.gelu(yt)


def _ssm_main(u, d_row, wtoep, wstate, wcarry, apow, tb=4096):
    S = u.shape[0]
    T = SSM_T
    nb = S // tb
    cb = tb // T
    return pl.pallas_call(
        functools.partial(_ssm_kernel, nb=nb, cb=cb),
        grid=(N_TILES, 2 * nb),
        in_specs=[pl.BlockSpec((tb, LANES), lambda g, j: (j % nb, g)),
                  pl.BlockSpec((1, LANES), lambda g, j: (0, g)),
                  pl.BlockSpec((1, T * LANES, T * LANES), lambda g, j: (g, 0, 0), pipeline_mode=pl.Buffered(1)),
                  pl.BlockSpec((1, T * LANES, STATE_COLS), lambda g, j: (g, 0, 0), pipeline_mode=pl.Buffered(1)),
                  pl.BlockSpec((1, STATE_COLS, T * LANES), lambda g, j: (g, 0, 0), pipeline_mode=pl.Buffered(1)),
                  pl.BlockSpec((1, SUBLANES, STATE_COLS), lambda g, j: (g, 0, 0))],
        out_specs=pl.BlockSpec((tb, LANES), lambda g, j: (jnp.maximum(j - nb, 0), g)),
        out_shape=jax.ShapeDtypeStruct((S, SSM_WIDTH), f32),
        scratch_shapes=[pltpu.VMEM((S // T, STATE_COLS), f32)],
        compiler_params=_params(("arbitrary", "arbitrary")),
        name="ssm_main",
    )(u, d_row, wtoep, wstate, wcarry, apow)


def _outproj_kernel(yna_ref, yssm_ref, ymem_ref, x_ref, wglu_ref, bglu_ref, gssm_ref, wout_ref,
                    gpost_ref, gmlp_ref, x1_ref, h2_ref):
    ya = yssm_ref[...]
    gate = jax.nn.sigmoid(jnp.dot(ya.astype(bf16), wglu_ref[...], preferred_element_type=f32) + bglu_ref[...])
    ys = _rms(ya * gate, gssm_ref[...]).astype(bf16)
    a, b = NA_WIDTH, NA_WIDTH + SSM_WIDTH
    acc = (jnp.dot(yna_ref[...], wout_ref[0:a, :], preferred_element_type=f32)
           + jnp.dot(ys, wout_ref[a:b, :], preferred_element_type=f32)
           + jnp.dot(ymem_ref[...], wout_ref[b:, :], preferred_element_type=f32))
    x1 = x_ref[...] + _rms(acc, gpost_ref[...])
    x1_ref[...] = x1
    h2_ref[...] = _rms(x1, gmlp_ref[...]).astype(bf16)


def _out_proj(y_na, y_ssm, y_mem, x2, wglu_bf, b_glu, g_ssm, wout_bf, g_post, g_mlp, tm=512):
    S = x2.shape[0]
    row = lambda w: pl.BlockSpec((tm, w), lambda i: (i, 0))
    vec = lambda w: pl.BlockSpec((1, w), lambda i: (0, 0))
    return pl.pallas_call(
        _outproj_kernel,
        grid=(S // tm,),
        in_specs=[row(NA_WIDTH), row(SSM_WIDTH), row(MEM_WIDTH), row(D_MODEL),
                  pl.BlockSpec((SSM_WIDTH, SSM_WIDTH), lambda i: (0, 0)), vec(SSM_WIDTH), vec(SSM_WIDTH),
                  pl.BlockSpec((D_MODEL, D_MODEL), lambda i: (0, 0), pipeline_mode=pl.Buffered(1)),
                  vec(D_MODEL), vec(D_MODEL)],
        out_specs=[row(D_MODEL), row(D_MODEL)],
        out_shape=[jax.ShapeDtypeStruct((S, D_MODEL), f32), jax.ShapeDtypeStruct((S, D_MODEL), bf16)],
        compiler_params=_params(("parallel",)),
        name="out_proj",
    )(y_na, y_ssm, y_mem, x2, wglu_bf, b_glu, g_ssm, wout_bf, g_post, g_mlp)


def _mlp_kernel(h_ref, w1_ref, w2_ref, x1_ref, g_ref, o_ref, *, nsplit):
    k = pl.program_id(1)

    @pl.when(k == 0)
    def _():
        o_ref[...] = jnp.zeros_like(o_ref)

    hid = jnp.dot(h_ref[...], w1_ref[...], preferred_element_type=f32)
    hid = jnp.square(jnp.maximum(hid, 0.0)).astype(bf16)
    wn = D_MODEL // nsplit
    for n in range(nsplit):
        cols = slice(n * wn, (n + 1) * wn)
        o_ref[:, cols] += jnp.dot(hid, w2_ref[:, cols], preferred_element_type=f32)

    @pl.when(k == pl.num_programs(1) - 1)
    def _():
        o_ref[...] = x1_ref[...] + _rms(o_ref[...], g_ref[...])


def _mlp(h2, w1_bf, w2_bf, x1, g, tm=1024, tk=512):
    S = h2.shape[0]
    return pl.pallas_call(
        functools.partial(_mlp_kernel, nsplit=4),
        grid=(S // tm, D_FF // tk),
        in_specs=[pl.BlockSpec((tm, D_MODEL), lambda i, k: (i, 0)),
                  pl.BlockSpec((D_MODEL, tk), lambda i, k: (0, k)),
                  pl.BlockSpec((tk, D_MODEL), lambda i, k: (k, 0)),
                  pl.BlockSpec((tm, D_MODEL), lambda i, k: (i, 0), pipeline_mode=pl.Buffered(1)),
                  pl.BlockSpec((1, D_MODEL), lambda i, k: (0, 0))],
        out_specs=pl.BlockSpec((tm, D_MODEL), lambda i, k: (i, 0)),
        out_shape=jax.ShapeDtypeStruct((S, D_MODEL), f32),
        compiler_params=_params(("parallel", "arbitrary")),
        name="mlp",
    )(h2, w1_bf, w2_bf, x1, g)


def kernel(x, mem, norm_mix_pre, w_in, na_rpb, ssm_lam_re, ssm_lam_im, ssm_log_dt, ssm_b_re, ssm_b_im, ssm_c_re, ssm_c_im, ssm_d, w_glu, b_glu, mem_norm, w_mem_kv, out_norm_na, out_norm_ssm, out_norm_mem, w_out, norm_mix_post, norm_mlp_pre, w_ff1, w_ff2, norm_mlp_post):
    B, S, _ = x.shape
    assert B == 1 and x.shape[2] == D_MODEL and S % 4096 == 0 and w_in.shape[0] == 1
    x2 = x[0]
    l = 0
    vec = lambda a: a[l].reshape(1, -1)

    P, u = _proj(x2, vec(norm_mix_pre), w_in[l].astype(bf16))

    y_na = _na_attn(P, _na_bias_table(na_rpb[l], S), vec(out_norm_na))

    kv = _mem_kv(mem[0], vec(mem_norm), w_mem_kv[l].astype(bf16))
    y_mem = _mem_attn(P, kv, vec(out_norm_mem))

    wtoep, wstate, wcarry, apow = _ssm_prep(ssm_lam_re[l], ssm_lam_im[l], ssm_log_dt[l], ssm_b_re[l],
                                            ssm_b_im[l], ssm_c_re[l], ssm_c_im[l])
    y_ssm = _ssm_main(u, ssm_d[l].reshape(1, SSM_WIDTH), wtoep, wstate, wcarry, apow)

    x1, h2 = _out_proj(y_na, y_ssm, y_mem, x2, w_glu[l].astype(bf16), vec(b_glu), vec(out_norm_ssm),
                       w_out[l].astype(bf16), vec(norm_mix_post), vec(norm_mlp_pre))
    out = _mlp(h2, w_ff1[l].astype(bf16), w_ff2[l].astype(bf16), x1, vec(norm_mlp_post))
    return out[None]
```

```python
import functools

import numpy as np
import jax
import jax.numpy as jnp
from jax import lax
from jax.experimental import pallas as pl
from jax.experimental.pallas import tpu as pltpu

f32 = jnp.float32
bf16 = jnp.bfloat16

D_MODEL = 2048
GRID_W = 64
WIN_H = 8
WIN_W = 16
HEAD_DIM = 128
NA_WIDTH = 1024
NA_HEADS = 8
SSM_WIDTH = 512
SSM_CH = 16
SSM_GROUPS = 32
SSM_STATE = 64
MEM_WIDTH = 512
MEM_HEADS = 4
N_MEM = 256
IN_WIDTH = 4096
D_FF = 8192
EPS = 1e-6
LAM_RE_MAX = -1e-4

LANES = 128
SUBLANES = 8
GROUPS_PER_TILE = LANES // SSM_CH
N_TILES = SSM_WIDTH // LANES
TILE_STATES = GROUPS_PER_TILE * SSM_STATE
STATE_COLS = 4 * TILE_STATES
SSM_T = 8
MASK_VALUE = -1e30
VMEM_LIMIT = 56 * 1024 * 1024


def _rms(x, g):
    ms = jnp.mean(x * x, axis=-1, keepdims=True)
    return x * lax.rsqrt(ms + EPS) * g


def _params(sem):
    return pltpu.CompilerParams(dimension_semantics=sem, vmem_limit_bytes=VMEM_LIMIT)


def _proj_kernel(x_ref, g_ref, w_ref, p_ref, u_ref, h_scr, *, nchunk, scale):
    h_scr[...] = _rms(x_ref[...], g_ref[...]).astype(bf16)
    width = IN_WIDTH // nchunk
    for j in range(nchunk):
        acc = jnp.dot(h_scr[...], w_ref[:, j * width:(j + 1) * width], preferred_element_type=f32)
        if j == 0:
            acc = acc * scale
        if j == nchunk - 1:
            u_ref[...] = acc[:, :SSM_WIDTH]
            col = lax.broadcasted_iota(jnp.int32, (1, width), 1)
            acc = acc * jnp.where(col >= SSM_WIDTH, scale, 1.0)
        p_ref[:, j * width:(j + 1) * width] = acc.astype(bf16)


def _proj(x2, g, w_bf, tm=512):
    S = x2.shape[0]
    return pl.pallas_call(
        functools.partial(_proj_kernel, nchunk=4, scale=HEAD_DIM ** -0.5),
        grid=(S // tm,),
        in_specs=[pl.BlockSpec((tm, D_MODEL), lambda i: (i, 0)),
                  pl.BlockSpec((1, D_MODEL), lambda i: (0, 0)),
                  pl.BlockSpec((D_MODEL, IN_WIDTH), lambda i: (0, 0), pipeline_mode=pl.Buffered(1))],
        out_specs=[pl.BlockSpec((tm, IN_WIDTH), lambda i: (i, 0)),
                   pl.BlockSpec((tm, SSM_WIDTH), lambda i: (i, 0))],
        out_shape=[jax.ShapeDtypeStruct((S, IN_WIDTH), bf16),
                   jax.ShapeDtypeStruct((S, SSM_WIDTH), f32)],
        scratch_shapes=[pltpu.VMEM((tm, D_MODEL), bf16)],
        compiler_params=_params(("parallel",)),
        name="proj",
    )(x2, g, w_bf)


NA_QB = 128
NA_KB = 5
NA_EDGE = 2
NA_STEP = 2
NA_ROWS_Q = NA_QB // GRID_W
NA_ROWS_WIN = NA_KB * NA_QB // GRID_W


def _na_window_start(b, nb):
    return jnp.clip(b - 2, 0, nb - NA_KB)


def _na_bias_kernel(v_ref, o_ref, *, nb, rows):
    c = lax.broadcasted_iota(jnp.int32, (GRID_W, LANES), 0)
    lane = lax.broadcasted_iota(jnp.int32, (GRID_W, LANES), 1)
    kc = lane % GRID_W
    cs = jnp.clip(c - WIN_W // 2, 0, GRID_W - WIN_W)
    col_ok = (kc >= cs) & (kc < cs + WIN_W)
    half_ok = {(True, True): col_ok, (True, False): col_ok & (lane < GRID_W), (False, True): col_ok & (lane >= GRID_W)}
    masked = jnp.full((GRID_W, LANES), MASK_VALUE, f32)
    variants = list(range(NA_EDGE)) + [NA_EDGE] + list(range(nb - NA_EDGE, nb))
    for vi, b in enumerate(variants):
        wb = min(max(b - 2, 0), nb - NA_KB)
        for rq in range(NA_ROWS_Q):
            r = b * NA_ROWS_Q + rq
            rs = min(max(r - WIN_H // 2, 0), rows - WIN_H)
            for m in range(NA_ROWS_WIN // 2):
                kr = wb * NA_ROWS_Q + 2 * m
                ok = (rs <= kr < rs + WIN_H, rs <= kr + 1 < rs + WIN_H)
                if ok == (False, False):
                    blk = masked
                else:
                    d = kr - r + WIN_H
                    src = jnp.broadcast_to(v_ref[0, d:d + 1, :], (GRID_W, LANES))
                    rolled = pltpu.roll(src, LANES - (WIN_W - 1), 1, stride=1, stride_axis=0)
                    blk = jnp.where(half_ok[ok], rolled, MASK_VALUE)
                o_ref[vi, 0, rq * GRID_W:(rq + 1) * GRID_W, m * LANES:(m + 1) * LANES] = blk


def _na_bias(rpb, S):
    heads, nrow, ncol = rpb.shape
    nb = S // NA_QB
    nvar = 2 * NA_EDGE + 1
    padded = jnp.zeros((heads, nrow + 2, GRID_W), f32).at[:, 1:nrow + 1, :ncol].set(rpb.astype(f32))
    pairs = jnp.concatenate([padded[:, :-1], padded[:, 1:]], axis=-1)
    return pl.pallas_call(
        functools.partial(_na_bias_kernel, nb=nb, rows=S // GRID_W),
        grid=(heads,),
        in_specs=[pl.BlockSpec((1, nrow + 1, LANES), lambda h: (h, 0, 0))],
        out_specs=pl.BlockSpec((nvar, 1, NA_QB, NA_KB * NA_QB), lambda h: (0, h, 0, 0)),
        out_shape=jax.ShapeDtypeStruct((nvar, heads, NA_QB, NA_KB * NA_QB), f32),
        compiler_params=_params(("parallel",)),
        name="na_bias",
    )(pairs)


def _na_step_window_start(i, nb):
    return jnp.clip(NA_STEP * i - 2, 0, nb - NA_KB - NA_STEP + 1)


def _na_kernel(q_ref, k_ref, v_ref, *rest, nb):
    tab_refs = rest[:NA_STEP]
    g_ref, o_ref, s_scr, p_scr = rest[NA_STEP:]
    i = pl.program_id(0)
    win0 = _na_step_window_start(i, nb)
    nt = (((1,), (1,)), ((), ()))
    starts = []
    for jb in range(NA_STEP):
        local = _na_window_start(NA_STEP * i + jb, nb) - win0
        starts.append(pl.multiple_of(local * NA_QB, NA_QB))
        for h in range(NA_HEADS):
            sl = slice(h * HEAD_DIM, (h + 1) * HEAD_DIM)
            k = k_ref[pl.ds(starts[jb], NA_KB * NA_QB), sl]
            s_scr[jb, h] = lax.dot_general(q_ref[jb * NA_QB:(jb + 1) * NA_QB, sl], k, nt,
                                           preferred_element_type=f32) + tab_refs[jb][0, h]
    inv = []
    for jb in range(NA_STEP):
        for h in range(NA_HEADS):
            s = s_scr[jb, h]
            p = jnp.exp(s - jnp.max(s, axis=-1, keepdims=True))
            inv.append(1.0 / jnp.sum(p, axis=-1, keepdims=True))
            p_scr[jb, h] = p.astype(bf16)
    for jb in range(NA_STEP):
        outs = []
        for h in range(NA_HEADS):
            sl = slice(h * HEAD_DIM, (h + 1) * HEAD_DIM)
            v = v_ref[pl.ds(starts[jb], NA_KB * NA_QB), sl]
            outs.append(jnp.dot(p_scr[jb, h], v, preferred_element_type=f32) * inv[jb * NA_HEADS + h])
        o_ref[jb * NA_QB:(jb + 1) * NA_QB, :] = _rms(jnp.concatenate(outs, axis=1), g_ref[...]).astype(bf16)


def _na_attn(P, tab, g):
    S = P.shape[0]
    nb = S // NA_QB
    win = (NA_KB + NA_STEP - 1) * NA_QB

    def variant(b):
        return jnp.where(b < NA_EDGE, b, jnp.where(b >= nb - NA_EDGE, b - (nb - 2 * NA_EDGE - 1), NA_EDGE))

    def window(col):
        return pl.BlockSpec((pl.Element(win), pl.Element(NA_WIDTH)),
                            lambda i: (_na_step_window_start(i, nb) * NA_QB, col))

    tab_specs = [pl.BlockSpec((1, NA_HEADS, NA_QB, NA_KB * NA_QB),
                              lambda i, jb=jb: (variant(NA_STEP * i + jb), 0, 0, 0)) for jb in range(NA_STEP)]
    return pl.pallas_call(
        functools.partial(_na_kernel, nb=nb),
        grid=(nb // NA_STEP,),
        in_specs=[pl.BlockSpec((NA_STEP * NA_QB, NA_WIDTH), lambda i: (i, 0)),
                  window(NA_WIDTH), window(2 * NA_WIDTH)] + tab_specs + [
                  pl.BlockSpec((1, NA_WIDTH), lambda i: (0, 0))],
        out_specs=pl.BlockSpec((NA_STEP * NA_QB, NA_WIDTH), lambda i: (i, 0)),
        out_shape=jax.ShapeDtypeStruct((S, NA_WIDTH), bf16),
        scratch_shapes=[pltpu.VMEM((NA_STEP, NA_HEADS, NA_QB, NA_KB * NA_QB), f32),
                        pltpu.VMEM((NA_STEP, NA_HEADS, NA_QB, NA_KB * NA_QB), bf16)],
        compiler_params=_params(("parallel",)),
        name="na_attn",
    )(P, P, P, *([tab] * NA_STEP), g)


def _memkv_kernel(m_ref, g_ref, w_ref, o_ref):
    o_ref[...] = jnp.dot(_rms(m_ref[...], g_ref[...]).astype(bf16), w_ref[...],
                         preferred_element_type=f32).astype(bf16)


def _mem_kv(mem2, g, w_bf):
    return pl.pallas_call(
        _memkv_kernel,
        out_shape=jax.ShapeDtypeStruct((N_MEM, 2 * MEM_WIDTH), bf16),
        compiler_params=pltpu.CompilerParams(vmem_limit_bytes=VMEM_LIMIT),
        name="mem_kv",
    )(mem2, g, w_bf)


def _memattn_kernel(q_ref, kv_ref, g_ref, o_ref):
    outs = []
    for h in range(MEM_HEADS):
        sl = slice(h * HEAD_DIM, (h + 1) * HEAD_DIM)
        s = lax.dot_general(q_ref[:, sl], kv_ref[:, sl], (((1,), (1,)), ((), ())),
                            preferred_element_type=f32)
        m = jnp.max(s, axis=-1, keepdims=True)
        p = jnp.exp(s - m)
        l = jnp.sum(p, axis=-1, keepdims=True)
        o = jnp.dot(p.astype(bf16), kv_ref[:, MEM_WIDTH + h * HEAD_DIM:MEM_WIDTH + (h + 1) * HEAD_DIM],
                    preferred_element_type=f32)
        outs.append(o / l)
    o_ref[...] = _rms(jnp.concatenate(outs, axis=1), g_ref[...]).astype(bf16)


def _mem_attn(P, kv, g, tm=512):
    S = P.shape[0]
    qblk = (IN_WIDTH - MEM_WIDTH) // MEM_WIDTH
    return pl.pallas_call(
        _memattn_kernel,
        grid=(S // tm,),
        in_specs=[pl.BlockSpec((tm, MEM_WIDTH), lambda i: (i, qblk)),
                  pl.BlockSpec((N_MEM, 2 * MEM_WIDTH), lambda i: (0, 0)),
                  pl.BlockSpec((1, MEM_WIDTH), lambda i: (0, 0))],
        out_specs=pl.BlockSpec((tm, MEM_WIDTH), lambda i: (i, 0)),
        out_shape=jax.ShapeDtypeStruct((S, MEM_WIDTH), bf16),
        compiler_params=_params(("parallel",)),
        name="mem_attn",
    )(P, kv, g)


def _ssm_prep_kernel(lre_r, lim_r, ldt_r, lre_c, lim_c, ldt_c, bre_ref, bim_ref, cre_ref, cim_ref,
                     wtoep_ref, wstate_ref, wcarry_ref, apow_ref):
    T = SSM_T
    TS = TILE_STATES
    reps = LANES // SSM_CH

    def same_group(shape, row_div, col_div):
        r = lax.broadcasted_iota(jnp.int32, shape, 0) // row_div
        c = lax.broadcasted_iota(jnp.int32, shape, 1) // col_div
        return (r == c).astype(f32)

    mask_state = same_group((LANES, TS), SSM_CH, SSM_STATE)
    mask_chan = same_group((LANES, LANES), SSM_CH, SSM_CH)

    kk = []
    for d in range(2):
        lre = jnp.minimum(lre_r[d, 0], LAM_RE_MAX)
        lim = lim_r[d, 0]
        dt = jnp.exp(ldt_r[d, 0])
        zr = lre * dt
        zi = lim * dt
        er = jnp.exp(zr)
        nr = er * jnp.cos(zi) - 1.0
        ni = er * jnp.sin(zi)
        den = lre * lre + lim * lim
        qr = (nr * lre + ni * lim) / den
        qi = (ni * lre - nr * lim) / den
        bre = bre_ref[d, 0]
        bim = bim_ref[d, 0]
        bbr = qr * bre - qi * bim
        bbi = qr * bim + qi * bre
        wc_r, wc_i = [], []
        for e in range(T):
            mag = jnp.exp(zr * float(e))
            pr = mag * jnp.cos(zi * float(e))
            pi = mag * jnp.sin(zi * float(e))
            wc_r.append(pr * bbr - pi * bbi)
            wc_i.append(pr * bbi + pi * bbr)

        for s in range(T):
            e = T - 1 - s if d == 0 else s
            rows = slice(s * LANES, (s + 1) * LANES)
            base = d * 2 * TS
            wstate_ref[0, rows, base:base + TS] = (
                jnp.concatenate([wc_r[e]] * reps, axis=0) * mask_state).astype(bf16)
            wstate_ref[0, rows, base + TS:base + 2 * TS] = (
                jnp.concatenate([wc_i[e]] * reps, axis=0) * mask_state).astype(bf16)

        cre = cre_ref[d, 0]
        cim = cim_ref[d, 0]
        kc = (jnp.dot(jnp.concatenate(wc_r, axis=0), cre, precision=lax.Precision.HIGHEST,
                      preferred_element_type=f32)
              - jnp.dot(jnp.concatenate(wc_i, axis=0), cim, precision=lax.Precision.HIGHEST,
                        preferred_element_type=f32))
        kk.append([jnp.concatenate([kc[e * SSM_CH:(e + 1) * SSM_CH]] * reps, axis=0) * mask_chan
                   for e in range(T)])

        lre_col = jnp.minimum(lre_c[d, 0], LAM_RE_MAX)
        dt_col = jnp.exp(ldt_c[d, 0])
        zr_col = lre_col * dt_col
        zi_col = lim_c[d, 0] * dt_col
        for t in range(T):
            e = float(t + 1 if d == 0 else T - t)
            mag = jnp.exp(zr_col * e)
            pr = mag * jnp.cos(zi_col * e)
            pi = mag * jnp.sin(zi_col * e)
            cols = slice(t * LANES, (t + 1) * LANES)
            base = d * 2 * TS
            wcarry_ref[0, base:base + TS, cols] = (cre * pr - cim * pi).astype(bf16)
            wcarry_ref[0, base + TS:base + 2 * TS, cols] = (-(cre * pi + cim * pr)).astype(bf16)

        k1 = (lax.broadcasted_iota(jnp.int32, (SUBLANES, TS), 0) + 1).astype(f32) * float(T)
        mag = jnp.exp(zr * k1)
        apow_ref[0, :, d * 2 * TS:d * 2 * TS + TS] = mag * jnp.cos(zi * k1)
        apow_ref[0, :, d * 2 * TS + TS:(d + 1) * 2 * TS] = mag * jnp.sin(zi * k1)

    for s in range(T):
        for t in range(T):
            if t > s:
                blk = kk[0][t - s]
            elif t < s:
                blk = kk[1][s - t]
            else:
                blk = kk[0][0] + kk[1][0]
            wtoep_ref[0, s * LANES:(s + 1) * LANES, t * LANES:(t + 1) * LANES] = blk.astype(bf16)


def _ssm_prep(lam_re, lam_im, log_dt, b_re, b_im, c_re, c_im):
    TS, NT, T = TILE_STATES, N_TILES, SSM_T
    gp = GROUPS_PER_TILE
    ldt = jnp.broadcast_to(log_dt[:, :, None], lam_re.shape)
    rows = [a.reshape(2, NT, 1, TS) for a in (lam_re, lam_im, ldt)]
    cols = [a.reshape(2, NT, TS, 1) for a in (lam_re, lam_im, ldt)]

    def b_layout(b):
        return b.reshape(2, NT, gp, SSM_STATE, SSM_CH).transpose(0, 1, 4, 2, 3).reshape(2, NT, SSM_CH, TS)

    eye = jnp.eye(gp, dtype=f32)

    def c_layout(c):
        ct = c.reshape(2, NT, gp, SSM_CH, SSM_STATE).transpose(0, 1, 2, 4, 3)
        return (ct[:, :, :, :, None, :] * eye[None, None, :, None, :, None]).reshape(2, NT, TS, LANES)

    row_spec = pl.BlockSpec((2, 1, 1, TS), lambda g: (0, g, 0, 0))
    col_spec = pl.BlockSpec((2, 1, TS, 1), lambda g: (0, g, 0, 0))
    b_spec = pl.BlockSpec((2, 1, SSM_CH, TS), lambda g: (0, g, 0, 0))
    c_spec = pl.BlockSpec((2, 1, TS, LANES), lambda g: (0, g, 0, 0))
    return pl.pallas_call(
        _ssm_prep_kernel,
        grid=(NT,),
        in_specs=[row_spec] * 3 + [col_spec] * 3 + [b_spec] * 2 + [c_spec] * 2,
        out_specs=[pl.BlockSpec((1, T * LANES, T * LANES), lambda g: (g, 0, 0)),
                   pl.BlockSpec((1, T * LANES, STATE_COLS), lambda g: (g, 0, 0)),
                   pl.BlockSpec((1, STATE_COLS, T * LANES), lambda g: (g, 0, 0)),
                   pl.BlockSpec((1, SUBLANES, STATE_COLS), lambda g: (g, 0, 0))],
        out_shape=[jax.ShapeDtypeStruct((NT, T * LANES, T * LANES), bf16),
                   jax.ShapeDtypeStruct((NT, T * LANES, STATE_COLS), bf16),
                   jax.ShapeDtypeStruct((NT, STATE_COLS, T * LANES), bf16),
                   jax.ShapeDtypeStruct((NT, SUBLANES, STATE_COLS), f32)],
        compiler_params=_params(("parallel",)),
        name="ssm_prep",
    )(*rows, *cols, b_layout(b_re), b_layout(b_im), c_layout(c_re), c_layout(c_im))


def _cmul_add(xr, xi, ar, ai, sr, si):
    return xr + ar * sr - ai * si, xi + ar * si + ai * sr


def _scan_rows(xr, xi, apr, api, cr, ci, reverse):
    n = SUBLANES
    rows = lax.broadcasted_iota(jnp.int32, xr.shape, 0)
    for k in (1, 2, 4):
        ar, ai = apr[k - 1:k], api[k - 1:k]
        if reverse:
            sr, si = pltpu.roll(xr, n - k, 0), pltpu.roll(xi, n - k, 0)
            keep = rows < n - k
        else:
            sr, si = pltpu.roll(xr, k, 0), pltpu.roll(xi, k, 0)
            keep = rows >= k
        xr, xi = _cmul_add(xr, xi, ar, ai, jnp.where(keep, sr, 0.0), jnp.where(keep, si, 0.0))
    if reverse:
        pwr = jnp.concatenate([apr[n - 1 - j:n - j] for j in range(n)], axis=0)
        pwi = jnp.concatenate([api[n - 1 - j:n - j] for j in range(n)], axis=0)
    else:
        pwr, pwi = apr, api
    hr, hi = _cmul_add(xr, xi, pwr, pwi, cr, ci)
    if reverse:
        inr = jnp.where(rows == n - 1, cr, pltpu.roll(hr, n - 1, 0))
        ini = jnp.where(rows == n - 1, ci, pltpu.roll(hi, n - 1, 0))
        return inr, ini, hr[0:1], hi[0:1]
    inr = jnp.where(rows == 0, cr, pltpu.roll(hr, 1, 0))
    ini = jnp.where(rows == 0, ci, pltpu.roll(hi, 1, 0))
    return inr, ini, hr[n - 1:n], hi[n - 1:n]


def _ssm_kernel(u_ref, d_ref, wtoep_ref, wstate_ref, wcarry_ref, apow_ref, o_ref, h_scr, *, nb, cb):
    T = SSM_T
    TS = TILE_STATES
    j = pl.program_id(1)

    def chunk_rows():
        return [u_ref[pl.ds(t, cb, stride=T), :] for t in range(T)]

    @pl.when(j < nb)
    def _():
        x = jnp.concatenate([ut.astype(bf16) for ut in chunk_rows()], axis=1)
        row0 = pl.multiple_of(j * cb, cb)
        h_scr[pl.ds(row0, cb), :] = jnp.dot(x, wstate_ref[0], preferred_element_type=f32)

    @pl.when(j == nb - 1)
    def _():
        nblk = nb * cb // SUBLANES
        zero = jnp.zeros((1, TS), f32)

        def body(i, carry):
            cfr, cfi, cbr, cbi = carry
            rf = pl.multiple_of(i * SUBLANES, SUBLANES)
            rb = pl.multiple_of((nblk - 1 - i) * SUBLANES, SUBLANES)
            inr, ini, cfr, cfi = _scan_rows(
                h_scr[pl.ds(rf, SUBLANES), 0:TS], h_scr[pl.ds(rf, SUBLANES), TS:2 * TS],
                apow_ref[0, :, 0:TS], apow_ref[0, :, TS:2 * TS], cfr, cfi, False)
            h_scr[pl.ds(rf, SUBLANES), 0:TS] = inr
            h_scr[pl.ds(rf, SUBLANES), TS:2 * TS] = ini
            inr, ini, cbr, cbi = _scan_rows(
                h_scr[pl.ds(rb, SUBLANES), 2 * TS:3 * TS], h_scr[pl.ds(rb, SUBLANES), 3 * TS:4 * TS],
                apow_ref[0, :, 2 * TS:3 * TS], apow_ref[0, :, 3 * TS:4 * TS], cbr, cbi, True)
            h_scr[pl.ds(rb, SUBLANES), 2 * TS:3 * TS] = inr
            h_scr[pl.ds(rb, SUBLANES), 3 * TS:4 * TS] = ini
            return cfr, cfi, cbr, cbi

        lax.fori_loop(0, nblk, body, (zero, zero, zero, zero))

    @pl.when(j >= nb)
    def _():
        us = chunk_rows()
        x = jnp.concatenate([ut.astype(bf16) for ut in us], axis=1)
        row0 = pl.multiple_of((j - nb) * cb, cb)
        y = (jnp.dot(x, wtoep_ref[0], preferred_element_type=f32)
             + jnp.dot(h_scr[pl.ds(row0, cb), :].astype(bf16), wcarry_ref[0], preferred_element_type=f32))
        for t in range(T):
            yt = y[:, t * LANES:(t + 1) * LANES] + d_ref[...] * us[t]
            o_ref[pl.ds(t, cb, stride=T), :] = jax.nn.gelu(yt)


def _ssm_main(u, d_row, wtoep, wstate, wcarry, apow, tb=4096):
    S = u.shape[0]
    T = SSM_T
    nb = S // tb
    cb = tb // T
    return pl.pallas_call(
        functools.partial(_ssm_kernel, nb=nb, cb=cb),
        grid=(N_TILES, 2 * nb),
        in_specs=[pl.BlockSpec((tb, LANES), lambda g, j: (j % nb, g)),
                  pl.BlockSpec((1, LANES), lambda g, j: (0, g)),
                  pl.BlockSpec((1, T * LANES, T * LANES), lambda g, j: (g, 0, 0), pipeline_mode=pl.Buffered(1)),
                  pl.BlockSpec((1, T * LANES, STATE_COLS), lambda g, j: (g, 0, 0), pipeline_mode=pl.Buffered(1)),
                  pl.BlockSpec((1, STATE_COLS, T * LANES), lambda g, j: (g, 0, 0), pipeline_mode=pl.Buffered(1)),
                  pl.BlockSpec((1, SUBLANES, STATE_COLS), lambda g, j: (g, 0, 0))],
        out_specs=pl.BlockSpec((tb, LANES), lambda g, j: (jnp.maximum(j - nb, 0), g)),
        out_shape=jax.ShapeDtypeStruct((S, SSM_WIDTH), f32),
        scratch_shapes=[pltpu.VMEM((S // T, STATE_COLS), f32)],
        compiler_params=_params(("arbitrary", "arbitrary")),
        name="ssm_main",
    )(u, d_row, wtoep, wstate, wcarry, apow)


def _outproj_kernel(yna_ref, yssm_ref, ymem_ref, x_ref, wglu_ref, bglu_ref, gssm_ref, wout_ref,
                    gpost_ref, gmlp_ref, x1_ref, h2_ref):
    ya = yssm_ref[...]
    gate = jax.nn.sigmoid(jnp.dot(ya.astype(bf16), wglu_ref[...], preferred_element_type=f32) + bglu_ref[...])
    ys = _rms(ya * gate, gssm_ref[...]).astype(bf16)
    a, b = NA_WIDTH, NA_WIDTH + SSM_WIDTH
    acc = (jnp.dot(yna_ref[...], wout_ref[0:a, :], preferred_element_type=f32)
           + jnp.dot(ys, wout_ref[a:b, :], preferred_element_type=f32)
           + jnp.dot(ymem_ref[...], wout_ref[b:, :], preferred_element_type=f32))
    x1 = x_ref[...] + _rms(acc, gpost_ref[...])
    x1_ref[...] = x1
    h2_ref[...] = _rms(x1, gmlp_ref[...]).astype(bf16)


def _out_proj(y_na, y_ssm, y_mem, x2, wglu_bf, b_glu, g_ssm, wout_bf, g_post, g_mlp, tm=512):
    S = x2.shape[0]
    row = lambda w: pl.BlockSpec((tm, w), lambda i: (i, 0))
    vec = lambda w: pl.BlockSpec((1, w), lambda i: (0, 0))
    return pl.pallas_call(
        _outproj_kernel,
        grid=(S // tm,),
        in_specs=[row(NA_WIDTH), row(SSM_WIDTH), row(MEM_WIDTH), row(D_MODEL),
                  pl.BlockSpec((SSM_WIDTH, SSM_WIDTH), lambda i: (0, 0)), vec(SSM_WIDTH), vec(SSM_WIDTH),
                  pl.BlockSpec((D_MODEL, D_MODEL), lambda i: (0, 0), pipeline_mode=pl.Buffered(1)),
                  vec(D_MODEL), vec(D_MODEL)],
        out_specs=[row(D_MODEL), row(D_MODEL)],
        out_shape=[jax.ShapeDtypeStruct((S, D_MODEL), f32), jax.ShapeDtypeStruct((S, D_MODEL), bf16)],
        compiler_params=_params(("parallel",)),
        name="out_proj",
    )(y_na, y_ssm, y_mem, x2, wglu_bf, b_glu, g_ssm, wout_bf, g_post, g_mlp)


def _mlp_kernel(h_ref, w1_ref, w2_ref, x1_ref, g_ref, o_ref, *, nsplit):
    k = pl.program_id(1)

    @pl.when(k == 0)
    def _():
        o_ref[...] = jnp.zeros_like(o_ref)

    hid = jnp.dot(h_ref[...], w1_ref[...], preferred_element_type=f32)
    hid = jnp.square(jnp.maximum(hid, 0.0)).astype(bf16)
    wn = D_MODEL // nsplit
    for n in range(nsplit):
        cols = slice(n * wn, (n + 1) * wn)
        o_ref[:, cols] += jnp.dot(hid, w2_ref[:, cols], preferred_element_type=f32)

    @pl.when(k == pl.num_programs(1) - 1)
    def _():
        o_ref[...] = x1_ref[...] + _rms(o_ref[...], g_ref[...])


def _mlp(h2, w1_bf, w2_bf, x1, g, tm=1024, tk=512):
    S = h2.shape[0]
    return pl.pallas_call(
        functools.partial(_mlp_kernel, nsplit=4),
        grid=(S // tm, D_FF // tk),
        in_specs=[pl.BlockSpec((tm, D_MODEL), lambda i, k: (i, 0)),
                  pl.BlockSpec((D_MODEL, tk), lambda i, k: (0, k)),
                  pl.BlockSpec((tk, D_MODEL), lambda i, k: (k, 0)),
                  pl.BlockSpec((tm, D_MODEL), lambda i, k: (i, 0), pipeline_mode=pl.Buffered(1)),
                  pl.BlockSpec((1, D_MODEL), lambda i, k: (0, 0))],
        out_specs=pl.BlockSpec((tm, D_MODEL), lambda i, k: (i, 0)),
        out_shape=jax.ShapeDtypeStruct((S, D_MODEL), f32),
        compiler_params=_params(("parallel", "arbitrary")),
        name="mlp",
    )(h2, w1_bf, w2_bf, x1, g)


def kernel(x, mem, norm_mix_pre, w_in, na_rpb, ssm_lam_re, ssm_lam_im, ssm_log_dt, ssm_b_re, ssm_b_im, ssm_c_re, ssm_c_im, ssm_d, w_glu, b_glu, mem_norm, w_mem_kv, out_norm_na, out_norm_ssm, out_norm_mem, w_out, norm_mix_post, norm_mlp_pre, w_ff1, w_ff2, norm_mlp_post):
    B, S, _ = x.shape
    assert B == 1 and x.shape[2] == D_MODEL and S % 4096 == 0 and w_in.shape[0] == 1
    x2 = x[0]
    l = 0
    vec = lambda a: a[l].reshape(1, -1)

    P, u = _proj(x2, vec(norm_mix_pre), w_in[l].astype(bf16))

    y_na = _na_attn(P, _na_bias(na_rpb[l], S), vec(out_norm_na))

    kv = _mem_kv(mem[0], vec(mem_norm), w_mem_kv[l].astype(bf16))
    y_mem = _mem_attn(P, kv, vec(out_norm_mem))

    wtoep, wstate, wcarry, apow = _ssm_prep(ssm_lam_re[l], ssm_lam_im[l], ssm_log_dt[l], ssm_b_re[l],
                                            ssm_b_im[l], ssm_c_re[l], ssm_c_im[l])
    y_ssm = _ssm_main(u, ssm_d[l].reshape(1, SSM_WIDTH), wtoep, wstate, wcarry, apow)

    x1, h2 = _out_proj(y_na, y_ssm, y_mem, x2, w_glu[l].astype(bf16), vec(b_glu), vec(out_norm_ssm),
                       w_out[l].astype(bf16), vec(norm_mix_post), vec(norm_mlp_pre))
    out = _mlp(h2, w_ff1[l].astype(bf16), w_ff2[l].astype(bf16), x1, vec(norm_mlp_post))
    return out[None]
```

```python
import functools

import numpy as np
import jax
import jax.numpy as jnp
from jax import lax
from jax.experimental import pallas as pl
from jax.experimental.pallas import tpu as pltpu

f32 = jnp.float32
bf16 = jnp.bfloat16

D_MODEL = 2048
GRID_W = 64
WIN_H = 8
WIN_W = 16
HEAD_DIM = 128
NA_WIDTH = 1024
NA_HEADS = 8
SSM_WIDTH = 512
SSM_CH = 16
SSM_GROUPS = 32
SSM_STATE = 64
MEM_WIDTH = 512
MEM_HEADS = 4
N_MEM = 256
IN_WIDTH = 4096
D_FF = 8192
EPS = 1e-6
LAM_RE_MAX = -1e-4

LANES = 128
SUBLANES = 8
GROUPS_PER_TILE = LANES // SSM_CH
N_TILES = SSM_WIDTH // LANES
TILE_STATES = GROUPS_PER_TILE * SSM_STATE
STATE_COLS = 4 * TILE_STATES
SSM_T = 8
MASK_VALUE = -1e30
VMEM_LIMIT = 56 * 1024 * 1024


def _rms(x, g):
    ms = jnp.mean(x * x, axis=-1, keepdims=True)
    return x * lax.rsqrt(ms + EPS) * g


def _params(sem):
    return pltpu.CompilerParams(dimension_semantics=sem, vmem_limit_bytes=VMEM_LIMIT)


def _proj_kernel(x_ref, g_ref, w_ref, p_ref, u_ref, h_scr, *, nchunk, scale):
    h_scr[...] = _rms(x_ref[...], g_ref[...]).astype(bf16)
    width = IN_WIDTH // nchunk
    for j in range(nchunk):
        acc = jnp.dot(h_scr[...], w_ref[:, j * width:(j + 1) * width], preferred_element_type=f32)
        if j == 0:
            acc = acc * scale
        if j == nchunk - 1:
            u_ref[...] = acc[:, :SSM_WIDTH]
            col = lax.broadcasted_iota(jnp.int32, (1, width), 1)
            acc = acc * jnp.where(col >= SSM_WIDTH, scale, 1.0)
        p_ref[:, j * width:(j + 1) * width] = acc.astype(bf16)


def _proj(x2, g, w_bf, tm=512):
    S = x2.shape[0]
    return pl.pallas_call(
        functools.partial(_proj_kernel, nchunk=4, scale=HEAD_DIM ** -0.5),
        grid=(S // tm,),
        in_specs=[pl.BlockSpec((tm, D_MODEL), lambda i: (i, 0)),
                  pl.BlockSpec((1, D_MODEL), lambda i: (0, 0)),
                  pl.BlockSpec((D_MODEL, IN_WIDTH), lambda i: (0, 0), pipeline_mode=pl.Buffered(1))],
        out_specs=[pl.BlockSpec((tm, IN_WIDTH), lambda i: (i, 0)),
                   pl.BlockSpec((tm, SSM_WIDTH), lambda i: (i, 0))],
        out_shape=[jax.ShapeDtypeStruct((S, IN_WIDTH), bf16),
                   jax.ShapeDtypeStruct((S, SSM_WIDTH), f32)],
        scratch_shapes=[pltpu.VMEM((tm, D_MODEL), bf16)],
        compiler_params=_params(("parallel",)),
        name="proj",
    )(x2, g, w_bf)


NA_QB = 128
NA_KB = 5
NA_EDGE = 2
NA_STEP = 2
NA_ROWS_Q = NA_QB // GRID_W
NA_ROWS_WIN = NA_KB * NA_QB // GRID_W


def _na_window_start(b, nb):
    return jnp.clip(b - 2, 0, nb - NA_KB)


def _na_bias_kernel(v_ref, o_ref, *, nb, rows):
    c = lax.broadcasted_iota(jnp.int32, (GRID_W, LANES), 0)
    lane = lax.broadcasted_iota(jnp.int32, (GRID_W, LANES), 1)
    kc = lane % GRID_W
    cs = jnp.clip(c - WIN_W // 2, 0, GRID_W - WIN_W)
    col_ok = (kc >= cs) & (kc < cs + WIN_W)
    half_ok = {(True, True): col_ok, (True, False): col_ok & (lane < GRID_W), (False, True): col_ok & (lane >= GRID_W)}
    masked = jnp.full((GRID_W, LANES), MASK_VALUE, f32)
    variants = list(range(NA_EDGE)) + [NA_EDGE] + list(range(nb - NA_EDGE, nb))
    for vi, b in enumerate(variants):
        wb = min(max(b - 2, 0), nb - NA_KB)
        for rq in range(NA_ROWS_Q):
            r = b * NA_ROWS_Q + rq
            rs = min(max(r - WIN_H // 2, 0), rows - WIN_H)
            for m in range(NA_ROWS_WIN // 2):
                kr = wb * NA_ROWS_Q + 2 * m
                ok = (rs <= kr < rs + WIN_H, rs <= kr + 1 < rs + WIN_H)
                if ok == (False, False):
                    blk = masked
                else:
                    d = kr - r + WIN_H
                    src = jnp.broadcast_to(v_ref[0, d:d + 1, :], (GRID_W, LANES))
                    rolled = pltpu.roll(src, LANES - (WIN_W - 1), 1, stride=1, stride_axis=0)
                    blk = jnp.where(half_ok[ok], rolled, MASK_VALUE)
                o_ref[vi, 0, rq * GRID_W:(rq + 1) * GRID_W, m * LANES:(m + 1) * LANES] = blk


def _na_bias(rpb, S):
    heads, nrow, ncol = rpb.shape
    nb = S // NA_QB
    nvar = 2 * NA_EDGE + 1
    padded = jnp.zeros((heads, nrow + 2, GRID_W), f32).at[:, 1:nrow + 1, :ncol].set(rpb.astype(f32))
    pairs = jnp.concatenate([padded[:, :-1], padded[:, 1:]], axis=-1)
    return pl.pallas_call(
        functools.partial(_na_bias_kernel, nb=nb, rows=S // GRID_W),
        grid=(heads,),
        in_specs=[pl.BlockSpec((1, nrow + 1, LANES), lambda h: (h, 0, 0))],
        out_specs=pl.BlockSpec((nvar, 1, NA_QB, NA_KB * NA_QB), lambda h: (0, h, 0, 0)),
        out_shape=jax.ShapeDtypeStruct((nvar, heads, NA_QB, NA_KB * NA_QB), f32),
        compiler_params=_params(("parallel",)),
        name="na_bias",
    )(pairs)


def _na_step_window_start(i, nb):
    return jnp.clip(NA_STEP * i - 2, 0, nb - NA_KB - NA_STEP + 1)


def _na_kernel(q_ref, k_ref, v_ref, *rest, nb):
    tab_refs = rest[:NA_STEP]
    g_ref, o_ref, s_scr, p_scr = rest[NA_STEP:]
    i = pl.program_id(0)
    win0 = _na_step_window_start(i, nb)
    nt = (((1,), (1,)), ((), ()))
    starts = []
    for jb in range(NA_STEP):
        local = _na_window_start(NA_STEP * i + jb, nb) - win0
        starts.append(pl.multiple_of(local * NA_QB, NA_QB))
        for h in range(NA_HEADS):
            sl = slice(h * HEAD_DIM, (h + 1) * HEAD_DIM)
            k = k_ref[pl.ds(starts[jb], NA_KB * NA_QB), sl]
            s_scr[jb, h] = lax.dot_general(q_ref[jb * NA_QB:(jb + 1) * NA_QB, sl], k, nt,
                                           preferred_element_type=f32) + tab_refs[jb][0, h]
    inv = []
    for jb in range(NA_STEP):
        for h in range(NA_HEADS):
            s = s_scr[jb, h]
            p = jnp.exp(s - jnp.max(s, axis=-1, keepdims=True))
            inv.append(1.0 / jnp.sum(p, axis=-1, keepdims=True))
            p_scr[jb, h] = p.astype(bf16)
    for jb in range(NA_STEP):
        outs = []
        for h in range(NA_HEADS):
            sl = slice(h * HEAD_DIM, (h + 1) * HEAD_DIM)
            v = v_ref[pl.ds(starts[jb], NA_KB * NA_QB), sl]
            outs.append(jnp.dot(p_scr[jb, h], v, preferred_element_type=f32) * inv[jb * NA_HEADS + h])
        o_ref[jb * NA_QB:(jb + 1) * NA_QB, :] = _rms(jnp.concatenate(outs, axis=1), g_ref[...]).astype(bf16)


def _na_attn(P, tab, g):
    S = P.shape[0]
    nb = S // NA_QB
    win = (NA_KB + NA_STEP - 1) * NA_QB

    def variant(b):
        return jnp.where(b < NA_EDGE, b, jnp.where(b >= nb - NA_EDGE, b - (nb - 2 * NA_EDGE - 1), NA_EDGE))

    def window(col):
        return pl.BlockSpec((pl.Element(win), pl.Element(NA_WIDTH)),
                            lambda i: (_na_step_window_start(i, nb) * NA_QB, col))

    tab_specs = [pl.BlockSpec((1, NA_HEADS, NA_QB, NA_KB * NA_QB),
                              lambda i, jb=jb: (variant(NA_STEP * i + jb), 0, 0, 0)) for jb in range(NA_STEP)]
    return pl.pallas_call(
        functools.partial(_na_kernel, nb=nb),
        grid=(nb // NA_STEP,),
        in_specs=[pl.BlockSpec((NA_STEP * NA_QB, NA_WIDTH), lambda i: (i, 0)),
                  window(NA_WIDTH), window(2 * NA_WIDTH)] + tab_specs + [
                  pl.BlockSpec((1, NA_WIDTH), lambda i: (0, 0))],
        out_specs=pl.BlockSpec((NA_STEP * NA_QB, NA_WIDTH), lambda i: (i, 0)),
        out_shape=jax.ShapeDtypeStruct((S, NA_WIDTH), bf16),
        scratch_shapes=[pltpu.VMEM((NA_STEP, NA_HEADS, NA_QB, NA_KB * NA_QB), f32),
                        pltpu.VMEM((NA_STEP, NA_HEADS, NA_QB, NA_KB * NA_QB), bf16)],
        compiler_params=_params(("parallel",)),
        name="na_attn",
    )(P, P, P, *([tab] * NA_STEP), g)


def _memkv_kernel(m_ref, g_ref, w_ref, o_ref):
    o_ref[...] = jnp.dot(_rms(m_ref[...], g_ref[...]).astype(bf16), w_ref[...],
                         preferred_element_type=f32).astype(bf16)


def _mem_kv(mem2, g, w_bf):
    return pl.pallas_call(
        _memkv_kernel,
        out_shape=jax.ShapeDtypeStruct((N_MEM, 2 * MEM_WIDTH), bf16),
        compiler_params=pltpu.CompilerParams(vmem_limit_bytes=VMEM_LIMIT),
        name="mem_kv",
    )(mem2, g, w_bf)


def _memattn_kernel(q_ref, kv_ref, g_ref, o_ref):
    outs = []
    for h in range(MEM_HEADS):
        sl = slice(h * HEAD_DIM, (h + 1) * HEAD_DIM)
        s = lax.dot_general(q_ref[:, sl], kv_ref[:, sl], (((1,), (1,)), ((), ())),
                            preferred_element_type=f32)
        m = jnp.max(s, axis=-1, keepdims=True)
        p = jnp.exp(s - m)
        l = jnp.sum(p, axis=-1, keepdims=True)
        o = jnp.dot(p.astype(bf16), kv_ref[:, MEM_WIDTH + h * HEAD_DIM:MEM_WIDTH + (h + 1) * HEAD_DIM],
                    preferred_element_type=f32)
        outs.append(o / l)
    o_ref[...] = _rms(jnp.concatenate(outs, axis=1), g_ref[...]).astype(bf16)


def _mem_attn(P, kv, g, tm=512):
    S = P.shape[0]
    qblk = (IN_WIDTH - MEM_WIDTH) // MEM_WIDTH
    return pl.pallas_call(
        _memattn_kernel,
        grid=(S // tm,),
        in_specs=[pl.BlockSpec((tm, MEM_WIDTH), lambda i: (i, qblk)),
                  pl.BlockSpec((N_MEM, 2 * MEM_WIDTH), lambda i: (0, 0)),
                  pl.BlockSpec((1, MEM_WIDTH), lambda i: (0, 0))],
        out_specs=pl.BlockSpec((tm, MEM_WIDTH), lambda i: (i, 0)),
        out_shape=jax.ShapeDtypeStruct((S, MEM_WIDTH), bf16),
        compiler_params=_params(("parallel",)),
        name="mem_attn",
    )(P, kv, g)


def _ssm_prep_kernel(lre_r, lim_r, ldt_r, bre_ref, bim_ref, cre_ref, cim_ref,
                     wtoep_ref, wstate_ref, wcarry_ref, apow_ref):
    T = SSM_T
    TS = TILE_STATES
    reps = LANES // SSM_CH
    assert T == SUBLANES

    def same_group(shape, row_div, col_div):
        r = lax.broadcasted_iota(jnp.int32, shape, 0) // row_div
        c = lax.broadcasted_iota(jnp.int32, shape, 1) // col_div
        return (r == c).astype(f32)

    mask_state = same_group((LANES, TS), SSM_CH, SSM_STATE)
    mask_chan = same_group((LANES, LANES), SSM_CH, SSM_CH)

    kk = []
    for d in range(2):
        lre = jnp.minimum(lre_r[d, 0], LAM_RE_MAX)
        lim = lim_r[d, 0]
        dt = jnp.exp(ldt_r[d, 0])
        zr = lre * dt
        zi = lim * dt
        er = jnp.exp(zr)
        nr = er * jnp.cos(zi) - 1.0
        ni = er * jnp.sin(zi)
        den = lre * lre + lim * lim
        qr = (nr * lre + ni * lim) / den
        qi = (ni * lre - nr * lim) / den
        bre = bre_ref[d, 0]
        bim = bim_ref[d, 0]
        bbr = qr * bre - qi * bim
        bbi = qr * bim + qi * bre
        e0 = lax.broadcasted_iota(jnp.int32, (T, TS), 0).astype(f32)
        mag0 = jnp.exp(zr * e0)
        p0r, p0i = mag0 * jnp.cos(zi * e0), mag0 * jnp.sin(zi * e0)
        mag1 = jnp.exp(zr * (e0 + 1.0))
        p1r, p1i = mag1 * jnp.cos(zi * (e0 + 1.0)), mag1 * jnp.sin(zi * (e0 + 1.0))
        wc_r = [p0r[e:e + 1] * bbr - p0i[e:e + 1] * bbi for e in range(T)]
        wc_i = [p0r[e:e + 1] * bbi + p0i[e:e + 1] * bbr for e in range(T)]

        for s in range(T):
            e = T - 1 - s if d == 0 else s
            rows = slice(s * LANES, (s + 1) * LANES)
            base = d * 2 * TS
            wstate_ref[0, rows, base:base + TS] = (
                jnp.concatenate([wc_r[e]] * reps, axis=0) * mask_state).astype(bf16)
            wstate_ref[0, rows, base + TS:base + 2 * TS] = (
                jnp.concatenate([wc_i[e]] * reps, axis=0) * mask_state).astype(bf16)

        cre = cre_ref[d, 0]
        cim = cim_ref[d, 0]
        nt = (((1,), (1,)), ((), ()))
        kc = (lax.dot_general(jnp.concatenate(wc_r, axis=0), jnp.concatenate([cre] * reps, axis=0) * mask_state,
                              nt, precision=lax.Precision.HIGHEST, preferred_element_type=f32)
              - lax.dot_general(jnp.concatenate(wc_i, axis=0), jnp.concatenate([cim] * reps, axis=0) * mask_state,
                                nt, precision=lax.Precision.HIGHEST, preferred_element_type=f32))
        kk.append([jnp.concatenate([kc[e * SSM_CH:(e + 1) * SSM_CH]] * reps, axis=0) * mask_chan
                   for e in range(T)])

        for t in range(T):
            e = t if d == 0 else T - 1 - t
            pr, pi = p1r[e:e + 1], p1i[e:e + 1]
            rows = slice(t * LANES, (t + 1) * LANES)
            base = d * 2 * TS
            wcarry_ref[0, rows, base:base + TS] = (
                jnp.concatenate([cre * pr - cim * pi] * reps, axis=0) * mask_state).astype(bf16)
            wcarry_ref[0, rows, base + TS:base + 2 * TS] = (
                jnp.concatenate([-(cre * pi + cim * pr)] * reps, axis=0) * mask_state).astype(bf16)

        k1 = (lax.broadcasted_iota(jnp.int32, (SUBLANES, TS), 0) + 1).astype(f32) * float(T)
        mag = jnp.exp(zr * k1)
        apow_ref[0, :, d * 2 * TS:d * 2 * TS + TS] = mag * jnp.cos(zi * k1)
        apow_ref[0, :, d * 2 * TS + TS:(d + 1) * 2 * TS] = mag * jnp.sin(zi * k1)

    for s in range(T):
        for t in range(T):
            if t > s:
                blk = kk[0][t - s]
            elif t < s:
                blk = kk[1][s - t]
            else:
                blk = kk[0][0] + kk[1][0]
            wtoep_ref[0, s * LANES:(s + 1) * LANES, t * LANES:(t + 1) * LANES] = blk.astype(bf16)


def _ssm_prep(lam_re, lam_im, log_dt, b_re, b_im, c_re, c_im):
    TS, NT, T = TILE_STATES, N_TILES, SSM_T
    gp = GROUPS_PER_TILE
    ldt = jnp.broadcast_to(log_dt[:, :, None], lam_re.shape)
    rows = [a.reshape(2, NT, 1, TS) for a in (lam_re, lam_im, ldt)]

    def b_layout(b):
        return b.reshape(2, NT, gp, SSM_STATE, SSM_CH).transpose(0, 1, 4, 2, 3).reshape(2, NT, SSM_CH, TS)

    def c_layout(c):
        return c.reshape(2, NT, gp, SSM_CH, SSM_STATE).transpose(0, 1, 3, 2, 4).reshape(2, NT, SSM_CH, TS)

    row_spec = pl.BlockSpec((2, 1, 1, TS), lambda g: (0, g, 0, 0))
    bc_spec = pl.BlockSpec((2, 1, SSM_CH, TS), lambda g: (0, g, 0, 0))
    return pl.pallas_call(
        _ssm_prep_kernel,
        grid=(NT,),
        in_specs=[row_spec] * 3 + [bc_spec] * 4,
        out_specs=[pl.BlockSpec((1, T * LANES, T * LANES), lambda g: (g, 0, 0)),
                   pl.BlockSpec((1, T * LANES, STATE_COLS), lambda g: (g, 0, 0)),
                   pl.BlockSpec((1, T * LANES, STATE_COLS), lambda g: (g, 0, 0)),
                   pl.BlockSpec((1, SUBLANES, STATE_COLS), lambda g: (g, 0, 0))],
        out_shape=[jax.ShapeDtypeStruct((NT, T * LANES, T * LANES), bf16),
                   jax.ShapeDtypeStruct((NT, T * LANES, STATE_COLS), bf16),
                   jax.ShapeDtypeStruct((NT, T * LANES, STATE_COLS), bf16),
                   jax.ShapeDtypeStruct((NT, SUBLANES, STATE_COLS), f32)],
        compiler_params=_params(("parallel",)),
        name="ssm_prep",
    )(*rows, b_layout(b_re), b_layout(b_im), c_layout(c_re), c_layout(c_im))


def _cmul_add(xr, xi, ar, ai, sr, si):
    return xr + ar * sr - ai * si, xi + ar * si + ai * sr


def _scan_rows(xr, xi, apr, api, cr, ci, reverse):
    n = SUBLANES
    rows = lax.broadcasted_iota(jnp.int32, xr.shape, 0)
    for k in (1, 2, 4):
        ar, ai = apr[k - 1:k], api[k - 1:k]
        if reverse:
            sr, si = pltpu.roll(xr, n - k, 0), pltpu.roll(xi, n - k, 0)
            keep = rows < n - k
        else:
            sr, si = pltpu.roll(xr, k, 0), pltpu.roll(xi, k, 0)
            keep = rows >= k
        xr, xi = _cmul_add(xr, xi, ar, ai, jnp.where(keep, sr, 0.0), jnp.where(keep, si, 0.0))
    if reverse:
        pwr = jnp.concatenate([apr[n - 1 - j:n - j] for j in range(n)], axis=0)
        pwi = jnp.concatenate([api[n - 1 - j:n - j] for j in range(n)], axis=0)
    else:
        pwr, pwi = apr, api
    hr, hi = _cmul_add(xr, xi, pwr, pwi, cr, ci)
    if reverse:
        inr = jnp.where(rows == n - 1, cr, pltpu.roll(hr, n - 1, 0))
        ini = jnp.where(rows == n - 1, ci, pltpu.roll(hi, n - 1, 0))
        return inr, ini, hr[0:1], hi[0:1]
    inr = jnp.where(rows == 0, cr, pltpu.roll(hr, 1, 0))
    ini = jnp.where(rows == 0, ci, pltpu.roll(hi, 1, 0))
    return inr, ini, hr[n - 1:n], hi[n - 1:n]


def _ssm_kernel(u_ref, d_ref, wtoep_ref, wstate_ref, wcarry_ref, apow_ref, o_ref, h_scr, *, nb, cb):
    T = SSM_T
    TS = TILE_STATES
    j = pl.program_id(1)

    def chunk_rows():
        return [u_ref[pl.ds(t, cb, stride=T), :] for t in range(T)]

    @pl.when(j < nb)
    def _():
        x = jnp.concatenate([ut.astype(bf16) for ut in chunk_rows()], axis=1)
        row0 = pl.multiple_of(j * cb, cb)
        h_scr[pl.ds(row0, cb), :] = jnp.dot(x, wstate_ref[0], preferred_element_type=f32)

    @pl.when(j == nb - 1)
    def _():
        nblk = nb * cb // SUBLANES
        zero = jnp.zeros((1, TS), f32)

        def body(i, carry):
            cfr, cfi, cbr, cbi = carry
            rf = pl.multiple_of(i * SUBLANES, SUBLANES)
            rb = pl.multiple_of((nblk - 1 - i) * SUBLANES, SUBLANES)
            inr, ini, cfr, cfi = _scan_rows(
                h_scr[pl.ds(rf, SUBLANES), 0:TS], h_scr[pl.ds(rf, SUBLANES), TS:2 * TS],
                apow_ref[0, :, 0:TS], apow_ref[0, :, TS:2 * TS], cfr, cfi, False)
            h_scr[pl.ds(rf, SUBLANES), 0:TS] = inr
            h_scr[pl.ds(rf, SUBLANES), TS:2 * TS] = ini
            inr, ini, cbr, cbi = _scan_rows(
                h_scr[pl.ds(rb, SUBLANES), 2 * TS:3 * TS], h_scr[pl.ds(rb, SUBLANES), 3 * TS:4 * TS],
                apow_ref[0, :, 2 * TS:3 * TS], apow_ref[0, :, 3 * TS:4 * TS], cbr, cbi, True)
            h_scr[pl.ds(rb, SUBLANES), 2 * TS:3 * TS] = inr
            h_scr[pl.ds(rb, SUBLANES), 3 * TS:4 * TS] = ini
            return cfr, cfi, cbr, cbi

        lax.fori_loop(0, nblk, body, (zero, zero, zero, zero))

    @pl.when(j >= nb)
    def _():
        us = chunk_rows()
        x = jnp.concatenate([ut.astype(bf16) for ut in us], axis=1)
        row0 = pl.multiple_of((j - nb) * cb, cb)
        y = (jnp.dot(x, wtoep_ref[0], preferred_element_type=f32)
             + lax.dot_general(h_scr[pl.ds(row0, cb), :].astype(bf16), wcarry_ref[0],
                               (((1,), (1,)), ((), ())), preferred_element_type=f32))
        for t in range(T):
            yt = y[:, t * LANES:(t + 1) * LANES] + d_ref[...] * us[t]
            o_ref[pl.ds(t, cb, stride=T), :] = jax.nn.gelu(yt)


def _ssm_main(u, d_row, wtoep, wstate, wcarry, apow, tb=4096):
    S = u.shape[0]
    T = SSM_T
    nb = S // tb
    cb = tb // T
    return pl.pallas_call(
        functools.partial(_ssm_kernel, nb=nb, cb=cb),
        grid=(N_TILES, 2 * nb),
        in_specs=[pl.BlockSpec((tb, LANES), lambda g, j: (j % nb, g)),
                  pl.BlockSpec((1, LANES), lambda g, j: (0, g)),
                  pl.BlockSpec((1, T * LANES, T * LANES), lambda g, j: (g, 0, 0), pipeline_mode=pl.Buffered(1)),
                  pl.BlockSpec((1, T * LANES, STATE_COLS), lambda g, j: (g, 0, 0), pipeline_mode=pl.Buffered(1)),
                  pl.BlockSpec((1, T * LANES, STATE_COLS), lambda g, j: (g, 0, 0), pipeline_mode=pl.Buffered(1)),
                  pl.BlockSpec((1, SUBLANES, STATE_COLS), lambda g, j: (g, 0, 0))],
        out_specs=pl.BlockSpec((tb, LANES), lambda g, j: (jnp.maximum(j - nb, 0), g)),
        out_shape=jax.ShapeDtypeStruct((S, SSM_WIDTH), f32),
        scratch_shapes=[pltpu.VMEM((S // T, STATE_COLS), f32)],
        compiler_params=_params(("arbitrary", "arbitrary")),
        name="ssm_main",
    )(u, d_row, wtoep, wstate, wcarry, apow)


def _outproj_kernel(yna_ref, yssm_ref, ymem_ref, x_ref, wglu_ref, bglu_ref, gssm_ref, wout_ref,
                    gpost_ref, gmlp_ref, x1_ref, h2_ref, *, nsub):
    a, b = NA_WIDTH, NA_WIDTH + SSM_WIDTH
    sub = x_ref.shape[0] // nsub
    tiles = [slice(r * sub, (r + 1) * sub) for r in range(nsub)]
    ys = []
    for rows in tiles:
        ya = yssm_ref[rows, :]
        gate = jax.nn.sigmoid(jnp.dot(ya.astype(bf16), wglu_ref[...], preferred_element_type=f32)
                              + bglu_ref[...])
        ys.append(_rms(ya * gate, gssm_ref[...]).astype(bf16))
    accs = [jnp.dot(yna_ref[rows, :], wout_ref[0:a, :], preferred_element_type=f32)
            + jnp.dot(ys[r], wout_ref[a:b, :], preferred_element_type=f32)
            + jnp.dot(ymem_ref[rows, :], wout_ref[b:, :], preferred_element_type=f32)
            for r, rows in enumerate(tiles)]
    for rows, acc in zip(tiles, accs):
        x1 = x_ref[rows, :] + _rms(acc, gpost_ref[...])
        x1_ref[rows, :] = x1
        h2_ref[rows, :] = _rms(x1, gmlp_ref[...]).astype(bf16)


def _out_proj(y_na, y_ssm, y_mem, x2, wglu_bf, b_glu, g_ssm, wout_bf, g_post, g_mlp, tm=512):
    S = x2.shape[0]
    row = lambda w: pl.BlockSpec((tm, w), lambda i: (i, 0))
    vec = lambda w: pl.BlockSpec((1, w), lambda i: (0, 0))
    return pl.pallas_call(
        functools.partial(_outproj_kernel, nsub=2),
        grid=(S // tm,),
        in_specs=[row(NA_WIDTH), row(SSM_WIDTH), row(MEM_WIDTH), row(D_MODEL),
                  pl.BlockSpec((SSM_WIDTH, SSM_WIDTH), lambda i: (0, 0)), vec(SSM_WIDTH), vec(SSM_WIDTH),
                  pl.BlockSpec((D_MODEL, D_MODEL), lambda i: (0, 0), pipeline_mode=pl.Buffered(1)),
                  vec(D_MODEL), vec(D_MODEL)],
        out_specs=[row(D_MODEL), row(D_MODEL)],
        out_shape=[jax.ShapeDtypeStruct((S, D_MODEL), f32), jax.ShapeDtypeStruct((S, D_MODEL), bf16)],
        compiler_params=_params(("parallel",)),
        name="out_proj",
    )(y_na, y_ssm, y_mem, x2, wglu_bf, b_glu, g_ssm, wout_bf, g_post, g_mlp)


def _mlp_kernel(h_ref, w1_ref, w2_ref, x1_hbm, g_ref, o_ref, x1_buf, x1_sem, *, nsplit):
    i = pl.program_id(0)
    k = pl.program_id(1)
    tm = o_ref.shape[0]

    def x1_copy():
        return pltpu.make_async_copy(x1_hbm.at[pl.ds(pl.multiple_of(i * tm, tm), tm), :], x1_buf, x1_sem)

    @pl.when(k == 0)
    def _():
        x1_copy().start()
        o_ref[...] = jnp.zeros_like(o_ref)

    hid = jnp.dot(h_ref[...], w1_ref[...], preferred_element_type=f32)
    hid = jnp.square(jnp.maximum(hid, 0.0)).astype(bf16)
    wn = D_MODEL // nsplit
    for n in range(nsplit):
        cols = slice(n * wn, (n + 1) * wn)
        o_ref[:, cols] += jnp.dot(hid, w2_ref[:, cols], preferred_element_type=f32)

    @pl.when(k == pl.num_programs(1) - 1)
    def _():
        x1_copy().wait()
        o_ref[...] = x1_buf[...] + _rms(o_ref[...], g_ref[...])


def _mlp(h2, w1_bf, w2_bf, x1, g, tm=1024, tk=512):
    S = h2.shape[0]
    return pl.pallas_call(
        functools.partial(_mlp_kernel, nsplit=4),
        grid=(S // tm, D_FF // tk),
        in_specs=[pl.BlockSpec((tm, D_MODEL), lambda i, k: (i, 0)),
                  pl.BlockSpec((D_MODEL, tk), lambda i, k: (0, k)),
                  pl.BlockSpec((tk, D_MODEL), lambda i, k: (k, 0)),
                  pl.BlockSpec(memory_space=pl.ANY),
                  pl.BlockSpec((1, D_MODEL), lambda i, k: (0, 0))],
        out_specs=pl.BlockSpec((tm, D_MODEL), lambda i, k: (i, 0)),
        out_shape=jax.ShapeDtypeStruct((S, D_MODEL), f32),
        scratch_shapes=[pltpu.VMEM((tm, D_MODEL), f32), pltpu.SemaphoreType.DMA(())],
        compiler_params=_params(("arbitrary", "arbitrary")),
        name="mlp",
    )(h2, w1_bf, w2_bf, x1, g)


def kernel(x, mem, norm_mix_pre, w_in, na_rpb, ssm_lam_re, ssm_lam_im, ssm_log_dt, ssm_b_re, ssm_b_im, ssm_c_re, ssm_c_im, ssm_d, w_glu, b_glu, mem_norm, w_mem_kv, out_norm_na, out_norm_ssm, out_norm_mem, w_out, norm_mix_post, norm_mlp_pre, w_ff1, w_ff2, norm_mlp_post):
    B, S, _ = x.shape
    assert B == 1 and x.shape[2] == D_MODEL and S % 4096 == 0 and w_in.shape[0] == 1
    x2 = x[0]
    l = 0
    vec = lambda a: a[l].reshape(1, -1)

    P, u = _proj(x2, vec(norm_mix_pre), w_in[l].astype(bf16))

    y_na = _na_attn(P, _na_bias(na_rpb[l], S), vec(out_norm_na))

    kv = _mem_kv(mem[0], vec(mem_norm), w_mem_kv[l].astype(bf16))
    y_mem = _mem_attn(P, kv, vec(out_norm_mem))

    wtoep, wstate, wcarry, apow = _ssm_prep(ssm_lam_re[l], ssm_lam_im[l], ssm_log_dt[l], ssm_b_re[l],
                                            ssm_b_im[l], ssm_c_re[l], ssm_c_im[l])
    y_ssm = _ssm_main(u, ssm_d[l].reshape(1, SSM_WIDTH), wtoep, wstate, wcarry, apow)

    x1, h2 = _out_proj(y_na, y_ssm, y_mem, x2, w_glu[l].astype(bf16), vec(b_glu), vec(out_norm_ssm),
                       w_out[l].astype(bf16), vec(norm_mix_post), vec(norm_mlp_pre))
    out = _mlp(h2, w_ff1[l].astype(bf16), w_ff2[l].astype(bf16), x1, vec(norm_mlp_post))
    return out[None]
```

```python
import functools

import numpy as np
import jax
import jax.numpy as jnp
from jax import lax
from jax.experimental import pallas as pl
from jax.experimental.pallas import tpu as pltpu

f32 = jnp.float32
bf16 = jnp.bfloat16

D_MODEL = 2048
GRID_W = 64
WIN_H = 8
WIN_W = 16
HEAD_DIM = 128
NA_WIDTH = 1024
NA_HEADS = 8
SSM_WIDTH = 512
SSM_CH = 16
SSM_GROUPS = 32
SSM_STATE = 64
MEM_WIDTH = 512
MEM_HEADS = 4
N_MEM = 256
IN_WIDTH = 4096
D_FF = 8192
EPS = 1e-6
LAM_RE_MAX = -1e-4

LANES = 128
SUBLANES = 8
GROUPS_PER_TILE = LANES // SSM_CH
N_TILES = SSM_WIDTH // LANES
TILE_STATES = GROUPS_PER_TILE * SSM_STATE
STATE_COLS = 4 * TILE_STATES
SSM_T = 8
MASK_VALUE = -1e30
VMEM_LIMIT = 56 * 1024 * 1024


def _rms(x, g):
    ms = jnp.mean(x * x, axis=-1, keepdims=True)
    return x * lax.rsqrt(ms + EPS) * g


def _params(sem):
    return pltpu.CompilerParams(dimension_semantics=sem, vmem_limit_bytes=VMEM_LIMIT)


def _softmax_pv(s, v):
    p = jnp.exp(s - jnp.max(s, axis=-1, keepdims=True))
    inv = 1.0 / jnp.sum(p, axis=-1, keepdims=True)
    return jnp.dot(p.astype(bf16), v, preferred_element_type=f32) * inv


def _proj_kernel(x_ref, g_ref, w_ref, kv_ref, gmem_ref, *rest, ncast, scale):
    cast_in, rest = rest[:ncast], rest[ncast:]
    p_ref, u_ref, ymem_ref = rest[:3]
    cast_out, h_scr = rest[3:3 + ncast], rest[3 + ncast]
    for src, dst in zip(cast_in, cast_out):
        dst[...] = src[...].astype(bf16)
    h_scr[...] = _rms(x_ref[...], g_ref[...]).astype(bf16)
    nt = (((1,), (1,)), ((), ()))

    acc = jnp.dot(h_scr[...], w_ref[:, 3 * NA_WIDTH:], preferred_element_type=f32)
    u_ref[...] = acc[:, :SSM_WIDTH]
    qm = (acc[:, SSM_WIDTH:] * scale).astype(bf16)
    outs = []
    for h in range(MEM_HEADS):
        sl = slice(h * HEAD_DIM, (h + 1) * HEAD_DIM)
        s = lax.dot_general(qm[:, sl], kv_ref[:, sl], nt, preferred_element_type=f32)
        outs.append(_softmax_pv(s, kv_ref[:, MEM_WIDTH + h * HEAD_DIM:MEM_WIDTH + (h + 1) * HEAD_DIM]))
    ymem_ref[...] = _rms(jnp.concatenate(outs, axis=1), gmem_ref[...]).astype(bf16)

    for j in range(3):
        cols = slice(j * NA_WIDTH, (j + 1) * NA_WIDTH)
        acc = jnp.dot(h_scr[...], w_ref[:, cols], preferred_element_type=f32)
        if j == 0:
            acc = acc * scale
        p_ref[:, cols] = acc.astype(bf16)


def _proj(x2, g, w_bf, kv, g_mem, cast_weights, tm=512):
    S = x2.shape[0]
    steps = S // tm
    cast_specs = [pl.BlockSpec((w.shape[0] // steps, w.shape[1]), lambda i: (i, 0)) for w in cast_weights]
    return pl.pallas_call(
        functools.partial(_proj_kernel, ncast=len(cast_weights), scale=HEAD_DIM ** -0.5),
        grid=(steps,),
        in_specs=[pl.BlockSpec((tm, D_MODEL), lambda i: (i, 0)),
                  pl.BlockSpec((1, D_MODEL), lambda i: (0, 0)),
                  pl.BlockSpec((D_MODEL, IN_WIDTH), lambda i: (0, 0), pipeline_mode=pl.Buffered(1)),
                  pl.BlockSpec((N_MEM, 2 * MEM_WIDTH), lambda i: (0, 0)),
                  pl.BlockSpec((1, MEM_WIDTH), lambda i: (0, 0))] + cast_specs,
        out_specs=[pl.BlockSpec((tm, 3 * NA_WIDTH), lambda i: (i, 0)),
                   pl.BlockSpec((tm, SSM_WIDTH), lambda i: (i, 0)),
                   pl.BlockSpec((tm, MEM_WIDTH), lambda i: (i, 0))] + cast_specs,
        out_shape=[jax.ShapeDtypeStruct((S, 3 * NA_WIDTH), bf16),
                   jax.ShapeDtypeStruct((S, SSM_WIDTH), f32),
                   jax.ShapeDtypeStruct((S, MEM_WIDTH), bf16)]
                  + [jax.ShapeDtypeStruct(w.shape, bf16) for w in cast_weights],
        scratch_shapes=[pltpu.VMEM((tm, D_MODEL), bf16)],
        compiler_params=_params(("parallel",)),
        name="proj",
    )(x2, g, w_bf, kv, g_mem, *cast_weights)


NA_QB = 128
NA_KB = 5
NA_EDGE = 2
NA_STEP = 2
NA_ROWS_Q = NA_QB // GRID_W
NA_ROWS_WIN = NA_KB * NA_QB // GRID_W


def _na_window_start(b, nb):
    return jnp.clip(b - 2, 0, nb - NA_KB)


def _na_bias_kernel(v_ref, o_ref, *, nb, rows):
    c = lax.broadcasted_iota(jnp.int32, (GRID_W, LANES), 0)
    lane = lax.broadcasted_iota(jnp.int32, (GRID_W, LANES), 1)
    kc = lane % GRID_W
    cs = jnp.clip(c - WIN_W // 2, 0, GRID_W - WIN_W)
    col_ok = (kc >= cs) & (kc < cs + WIN_W)
    half_ok = {(True, True): col_ok, (True, False): col_ok & (lane < GRID_W), (False, True): col_ok & (lane >= GRID_W)}
    masked = jnp.full((GRID_W, LANES), MASK_VALUE, f32)
    variants = list(range(NA_EDGE)) + [NA_EDGE] + list(range(nb - NA_EDGE, nb))
    for vi, b in enumerate(variants):
        wb = min(max(b - 2, 0), nb - NA_KB)
        for rq in range(NA_ROWS_Q):
            r = b * NA_ROWS_Q + rq
            rs = min(max(r - WIN_H // 2, 0), rows - WIN_H)
            for m in range(NA_ROWS_WIN // 2):
                kr = wb * NA_ROWS_Q + 2 * m
                ok = (rs <= kr < rs + WIN_H, rs <= kr + 1 < rs + WIN_H)
                if ok == (False, False):
                    blk = masked
                else:
                    d = kr - r + WIN_H
                    src = jnp.broadcast_to(v_ref[0, d:d + 1, :], (GRID_W, LANES))
                    rolled = pltpu.roll(src, LANES - (WIN_W - 1), 1, stride=1, stride_axis=0)
                    blk = jnp.where(half_ok[ok], rolled, MASK_VALUE)
                o_ref[vi, 0, rq * GRID_W:(rq + 1) * GRID_W, m * LANES:(m + 1) * LANES] = blk


def _na_bias(rpb, S):
    heads, nrow, ncol = rpb.shape
    nb = S // NA_QB
    nvar = 2 * NA_EDGE + 1
    padded = jnp.zeros((heads, nrow + 2, GRID_W), f32).at[:, 1:nrow + 1, :ncol].set(rpb.astype(f32))
    pairs = jnp.concatenate([padded[:, :-1], padded[:, 1:]], axis=-1)
    return pl.pallas_call(
        functools.partial(_na_bias_kernel, nb=nb, rows=S // GRID_W),
        grid=(heads,),
        in_specs=[pl.BlockSpec((1, nrow + 1, LANES), lambda h: (h, 0, 0))],
        out_specs=pl.BlockSpec((nvar, 1, NA_QB, NA_KB * NA_QB), lambda h: (0, h, 0, 0)),
        out_shape=jax.ShapeDtypeStruct((nvar, heads, NA_QB, NA_KB * NA_QB), f32),
        compiler_params=_params(("parallel",)),
        name="na_bias",
    )(pairs)


def _na_step_window_start(i, nb):
    return jnp.clip(NA_STEP * i - 2, 0, nb - NA_KB - NA_STEP + 1)


def _na_kernel(q_ref, k_ref, v_ref, *rest, nb):
    tab_refs = rest[:NA_STEP]
    g_ref, o_ref, s_scr, p_scr = rest[NA_STEP:]
    i = pl.program_id(0)
    win0 = _na_step_window_start(i, nb)
    nt = (((1,), (1,)), ((), ()))
    starts = []
    for jb in range(NA_STEP):
        local = _na_window_start(NA_STEP * i + jb, nb) - win0
        starts.append(pl.multiple_of(local * NA_QB, NA_QB))
        for h in range(NA_HEADS):
            sl = slice(h * HEAD_DIM, (h + 1) * HEAD_DIM)
            k = k_ref[pl.ds(starts[jb], NA_KB * NA_QB), sl]
            s_scr[jb, h] = lax.dot_general(q_ref[jb * NA_QB:(jb + 1) * NA_QB, sl], k, nt,
                                           preferred_element_type=f32) + tab_refs[jb][0, h]
    inv = []
    for jb in range(NA_STEP):
        for h in range(NA_HEADS):
            s = s_scr[jb, h]
            p = jnp.exp(s - jnp.max(s, axis=-1, keepdims=True))
            inv.append(1.0 / jnp.sum(p, axis=-1, keepdims=True))
            p_scr[jb, h] = p.astype(bf16)
    for jb in range(NA_STEP):
        outs = []
        for h in range(NA_HEADS):
            sl = slice(h * HEAD_DIM, (h + 1) * HEAD_DIM)
            v = v_ref[pl.ds(starts[jb], NA_KB * NA_QB), sl]
            outs.append(jnp.dot(p_scr[jb, h], v, preferred_element_type=f32) * inv[jb * NA_HEADS + h])
        o_ref[jb * NA_QB:(jb + 1) * NA_QB, :] = _rms(jnp.concatenate(outs, axis=1), g_ref[...]).astype(bf16)


def _na_attn(P, tab, g):
    S = P.shape[0]
    nb = S // NA_QB
    win = (NA_KB + NA_STEP - 1) * NA_QB

    def variant(b):
        return jnp.where(b < NA_EDGE, b, jnp.where(b >= nb - NA_EDGE, b - (nb - 2 * NA_EDGE - 1), NA_EDGE))

    def window(col):
        return pl.BlockSpec((pl.Element(win), pl.Element(NA_WIDTH)),
                            lambda i: (_na_step_window_start(i, nb) * NA_QB, col))

    tab_specs = [pl.BlockSpec((1, NA_HEADS, NA_QB, NA_KB * NA_QB),
                              lambda i, jb=jb: (variant(NA_STEP * i + jb), 0, 0, 0)) for jb in range(NA_STEP)]
    return pl.pallas_call(
        functools.partial(_na_kernel, nb=nb),
        grid=(nb // NA_STEP,),
        in_specs=[pl.BlockSpec((NA_STEP * NA_QB, NA_WIDTH), lambda i: (i, 0)),
                  window(NA_WIDTH), window(2 * NA_WIDTH)] + tab_specs + [
                  pl.BlockSpec((1, NA_WIDTH), lambda i: (0, 0))],
        out_specs=pl.BlockSpec((NA_STEP * NA_QB, NA_WIDTH), lambda i: (i, 0)),
        out_shape=jax.ShapeDtypeStruct((S, NA_WIDTH), bf16),
        scratch_shapes=[pltpu.VMEM((NA_STEP, NA_HEADS, NA_QB, NA_KB * NA_QB), f32),
                        pltpu.VMEM((NA_STEP, NA_HEADS, NA_QB, NA_KB * NA_QB), bf16)],
        compiler_params=_params(("parallel",)),
        name="na_attn",
    )(P, P, P, *([tab] * NA_STEP), g)


def _memkv_kernel(m_ref, g_ref, w_ref, o_ref):
    o_ref[...] = jnp.dot(_rms(m_ref[...], g_ref[...]).astype(bf16), w_ref[...],
                         preferred_element_type=f32).astype(bf16)


def _mem_kv(mem2, g, w_bf):
    return pl.pallas_call(
        _memkv_kernel,
        out_shape=jax.ShapeDtypeStruct((N_MEM, 2 * MEM_WIDTH), bf16),
        compiler_params=pltpu.CompilerParams(vmem_limit_bytes=VMEM_LIMIT),
        name="mem_kv",
    )(mem2, g, w_bf)


def _ssm_prep_kernel(lre_r, lim_r, ldt_r, bre_ref, bim_ref, cre_ref, cim_ref,
                     wtoep_ref, wstate_ref, wcarry_ref, apow_ref, *, seg_len):
    T = SSM_T
    TS = TILE_STATES
    reps = LANES // SSM_CH
    assert T == SUBLANES

    def same_group(shape, row_div, col_div):
        r = lax.broadcasted_iota(jnp.int32, shape, 0) // row_div
        c = lax.broadcasted_iota(jnp.int32, shape, 1) // col_div
        return (r == c).astype(f32)

    mask_state = same_group((LANES, TS), SSM_CH, SSM_STATE)
    mask_chan = same_group((LANES, LANES), SSM_CH, SSM_CH)

    kk = []
    for d in range(2):
        lre = jnp.minimum(lre_r[d, 0], LAM_RE_MAX)
        lim = lim_r[d, 0]
        dt = jnp.exp(ldt_r[d, 0])
        zr = lre * dt
        zi = lim * dt
        er = jnp.exp(zr)
        nr = er * jnp.cos(zi) - 1.0
        ni = er * jnp.sin(zi)
        den = lre * lre + lim * lim
        qr = (nr * lre + ni * lim) / den
        qi = (ni * lre - nr * lim) / den
        bre = bre_ref[d, 0]
        bim = bim_ref[d, 0]
        bbr = qr * bre - qi * bim
        bbi = qr * bim + qi * bre
        e0 = lax.broadcasted_iota(jnp.int32, (T, TS), 0).astype(f32)
        mag0 = jnp.exp(zr * e0)
        p0r, p0i = mag0 * jnp.cos(zi * e0), mag0 * jnp.sin(zi * e0)
        mag1 = jnp.exp(zr * (e0 + 1.0))
        p1r, p1i = mag1 * jnp.cos(zi * (e0 + 1.0)), mag1 * jnp.sin(zi * (e0 + 1.0))
        wc_r = [p0r[e:e + 1] * bbr - p0i[e:e + 1] * bbi for e in range(T)]
        wc_i = [p0r[e:e + 1] * bbi + p0i[e:e + 1] * bbr for e in range(T)]

        for s in range(T):
            e = T - 1 - s if d == 0 else s
            rows = slice(s * LANES, (s + 1) * LANES)
            base = d * 2 * TS
            wstate_ref[0, rows, base:base + TS] = (
                jnp.concatenate([wc_r[e]] * reps, axis=0) * mask_state).astype(bf16)
            wstate_ref[0, rows, base + TS:base + 2 * TS] = (
                jnp.concatenate([wc_i[e]] * reps, axis=0) * mask_state).astype(bf16)

        cre = cre_ref[d, 0]
        cim = cim_ref[d, 0]
        nt = (((1,), (1,)), ((), ()))
        kc = (lax.dot_general(jnp.concatenate(wc_r, axis=0), jnp.concatenate([cre] * reps, axis=0) * mask_state,
                              nt, precision=lax.Precision.HIGHEST, preferred_element_type=f32)
              - lax.dot_general(jnp.concatenate(wc_i, axis=0), jnp.concatenate([cim] * reps, axis=0) * mask_state,
                                nt, precision=lax.Precision.HIGHEST, preferred_element_type=f32))
        kk.append([jnp.concatenate([kc[e * SSM_CH:(e + 1) * SSM_CH]] * reps, axis=0) * mask_chan
                   for e in range(T)])

        for t in range(T):
            e = t if d == 0 else T - 1 - t
            pr, pi = p1r[e:e + 1], p1i[e:e + 1]
            rows = slice(t * LANES, (t + 1) * LANES)
            base = d * 2 * TS
            wcarry_ref[0, rows, base:base + TS] = (
                jnp.concatenate([cre * pr - cim * pi] * reps, axis=0) * mask_state).astype(bf16)
            wcarry_ref[0, rows, base + TS:base + 2 * TS] = (
                jnp.concatenate([-(cre * pi + cim * pr)] * reps, axis=0) * mask_state).astype(bf16)

        for part, steps in enumerate((T, T * seg_len)):
            k1 = (lax.broadcasted_iota(jnp.int32, (SUBLANES, TS), 0) + 1).astype(f32) * float(steps)
            mag = jnp.exp(zr * k1)
            rows = slice(part * SUBLANES, (part + 1) * SUBLANES)
            apow_ref[0, rows, d * 2 * TS:d * 2 * TS + TS] = mag * jnp.cos(zi * k1)
            apow_ref[0, rows, d * 2 * TS + TS:(d + 1) * 2 * TS] = mag * jnp.sin(zi * k1)

    for s in range(T):
        for t in range(T):
            if t > s:
                blk = kk[0][t - s]
            elif t < s:
                blk = kk[1][s - t]
            else:
                blk = kk[0][0] + kk[1][0]
            wtoep_ref[0, s * LANES:(s + 1) * LANES, t * LANES:(t + 1) * LANES] = blk.astype(bf16)


def _ssm_prep(lam_re, lam_im, log_dt, b_re, b_im, c_re, c_im, S):
    TS, NT, T = TILE_STATES, N_TILES, SSM_T
    gp = GROUPS_PER_TILE
    ldt = jnp.broadcast_to(log_dt[:, :, None], lam_re.shape)
    rows = [a.reshape(2, NT, 1, TS) for a in (lam_re, lam_im, ldt)]

    def b_layout(b):
        return b.reshape(2, NT, gp, SSM_STATE, SSM_CH).transpose(0, 1, 4, 2, 3).reshape(2, NT, SSM_CH, TS)

    def c_layout(c):
        return c.reshape(2, NT, gp, SSM_CH, SSM_STATE).transpose(0, 1, 3, 2, 4).reshape(2, NT, SSM_CH, TS)

    row_spec = pl.BlockSpec((2, 1, 1, TS), lambda g: (0, g, 0, 0))
    bc_spec = pl.BlockSpec((2, 1, SSM_CH, TS), lambda g: (0, g, 0, 0))
    return pl.pallas_call(
        functools.partial(_ssm_prep_kernel, seg_len=S // T // SUBLANES),
        grid=(NT,),
        in_specs=[row_spec] * 3 + [bc_spec] * 4,
        out_specs=[pl.BlockSpec((1, T * LANES, T * LANES), lambda g: (g, 0, 0)),
                   pl.BlockSpec((1, T * LANES, STATE_COLS), lambda g: (g, 0, 0)),
                   pl.BlockSpec((1, T * LANES, STATE_COLS), lambda g: (g, 0, 0)),
                   pl.BlockSpec((1, 2 * SUBLANES, STATE_COLS), lambda g: (g, 0, 0))],
        out_shape=[jax.ShapeDtypeStruct((NT, T * LANES, T * LANES), bf16),
                   jax.ShapeDtypeStruct((NT, T * LANES, STATE_COLS), bf16),
                   jax.ShapeDtypeStruct((NT, T * LANES, STATE_COLS), bf16),
                   jax.ShapeDtypeStruct((NT, 2 * SUBLANES, STATE_COLS), f32)],
        compiler_params=_params(("parallel",)),
        name="ssm_prep",
    )(*rows, b_layout(b_re), b_layout(b_im), c_layout(c_re), c_layout(c_im))


def _cmul_add(xr, xi, ar, ai, sr, si):
    return xr + ar * sr - ai * si, xi + ar * si + ai * sr


def _scan_rows(xr, xi, apr, api, cr, ci, reverse):
    n = SUBLANES
    rows = lax.broadcasted_iota(jnp.int32, xr.shape, 0)
    for k in (1, 2, 4):
        ar, ai = apr[k - 1:k], api[k - 1:k]
        if reverse:
            sr, si = pltpu.roll(xr, n - k, 0), pltpu.roll(xi, n - k, 0)
            keep = rows < n - k
        else:
            sr, si = pltpu.roll(xr, k, 0), pltpu.roll(xi, k, 0)
            keep = rows >= k
        xr, xi = _cmul_add(xr, xi, ar, ai, jnp.where(keep, sr, 0.0), jnp.where(keep, si, 0.0))
    if reverse:
        pwr = jnp.concatenate([apr[n - 1 - j:n - j] for j in range(n)], axis=0)
        pwi = jnp.concatenate([api[n - 1 - j:n - j] for j in range(n)], axis=0)
    else:
        pwr, pwi = apr, api
    hr, hi = _cmul_add(xr, xi, pwr, pwi, cr, ci)
    if reverse:
        inr = jnp.where(rows == n - 1, cr, pltpu.roll(hr, n - 1, 0))
        ini = jnp.where(rows == n - 1, ci, pltpu.roll(hi, n - 1, 0))
        return inr, ini, hr[0:1], hi[0:1]
    inr = jnp.where(rows == 0, cr, pltpu.roll(hr, 1, 0))
    ini = jnp.where(rows == 0, ci, pltpu.roll(hi, 1, 0))
    return inr, ini, hr[n - 1:n], hi[n - 1:n]


def _ssm_kernel(u_ref, d_ref, wtoep_ref, wstate_ref, wcarry_ref, apow_ref, o_ref, h_scr, *, nb, cb):
    T = SSM_T
    TS = TILE_STATES
    j = pl.program_id(1)

    nseg = SUBLANES
    seg = nb * cb // nseg
    assert cb % seg == 0

    def chunk_rows():
        return [u_ref[pl.ds(t, cb, stride=T), :] for t in range(T)]

    def batch_rows(b):
        return [pl.ds(b * (cb // seg) + part, seg, stride=nseg) for part in range(cb // seg)]

    @pl.when(j < nb)
    def _():
        x = jnp.concatenate([ut.astype(bf16) for ut in chunk_rows()], axis=1)
        res = jnp.dot(x, wstate_ref[0], preferred_element_type=f32)
        for part, rows in enumerate(batch_rows(j)):
            for c in range(STATE_COLS // LANES):
                h_scr[c, rows, :] = res[part * seg:(part + 1) * seg, c * LANES:(c + 1) * LANES]

    @pl.when(j == nb - 1)
    def _():
        tiles = TS // LANES

        def cols(d, ri):
            return slice((2 * d + ri) * TS, (2 * d + ri + 1) * TS)

        def rows_at(p):
            return pl.ds(pl.multiple_of(p * nseg, nseg), nseg)

        def load(rows, d, ri):
            return jnp.concatenate([h_scr[(2 * d + ri) * tiles + c, rows, :] for c in range(tiles)], axis=1)

        def store(rows, d, ri, val):
            for c in range(tiles):
                h_scr[(2 * d + ri) * tiles + c, rows, :] = val[:, c * LANES:(c + 1) * LANES]

        a = [[jnp.broadcast_to(apow_ref[0, 0:1, cols(d, ri)], (nseg, TS)) for ri in range(2)] for d in range(2)]
        zero = jnp.zeros((nseg, TS), f32)

        def step(p, h, write):
            hfr, hfi, hbr, hbi = h
            rf, rb = rows_at(p), rows_at(seg - 1 - p)
            xfr, xfi, xbr, xbi = load(rf, 0, 0), load(rf, 0, 1), load(rb, 1, 0), load(rb, 1, 1)
            if write:
                store(rf, 0, 0, hfr)
                store(rf, 0, 1, hfi)
                store(rb, 1, 0, hbr)
                store(rb, 1, 1, hbi)
            hfr, hfi = _cmul_add(xfr, xfi, a[0][0], a[0][1], hfr, hfi)
            hbr, hbi = _cmul_add(xbr, xbi, a[1][0], a[1][1], hbr, hbi)
            return hfr, hfi, hbr, hbi

        ends = lax.fori_loop(0, seg, lambda p, h: step(p, h, False), (zero, zero, zero, zero))
        z1 = jnp.zeros((1, TS), f32)
        enter = []
        for d in range(2):
            cr, ci, _, _ = _scan_rows(ends[2 * d], ends[2 * d + 1], apow_ref[0, nseg:2 * nseg, cols(d, 0)],
                                      apow_ref[0, nseg:2 * nseg, cols(d, 1)], z1, z1, d == 1)
            enter += [cr, ci]
        lax.fori_loop(0, seg, lambda p, h: step(p, h, True), tuple(enter))

    @pl.when(j >= nb)
    def _():
        us = chunk_rows()
        x = jnp.concatenate([ut.astype(bf16) for ut in us], axis=1)
        hc = jnp.concatenate(
            [jnp.concatenate([h_scr[c, rows, :].astype(bf16) for c in range(STATE_COLS // LANES)], axis=1)
             for rows in batch_rows(j - nb)], axis=0)
        y = (jnp.dot(x, wtoep_ref[0], preferred_element_type=f32)
             + lax.dot_general(hc, wcarry_ref[0], (((1,), (1,)), ((), ())), preferred_element_type=f32))
        for t in range(T):
            yt = y[:, t * LANES:(t + 1) * LANES] + d_ref[...] * us[t]
            o_ref[pl.ds(t, cb, stride=T), :] = jax.nn.gelu(yt)


def _ssm_main(u, d_row, wtoep, wstate, wcarry, apow, tb=4096):
    S = u.shape[0]
    T = SSM_T
    nb = S // tb
    cb = tb // T
    return pl.pallas_call(
        functools.partial(_ssm_kernel, nb=nb, cb=cb),
        grid=(N_TILES, 2 * nb),
        in_specs=[pl.BlockSpec((tb, LANES), lambda g, j: (j % nb, g)),
                  pl.BlockSpec((1, LANES), lambda g, j: (0, g)),
                  pl.BlockSpec((1, T * LANES, T * LANES), lambda g, j: (g, 0, 0), pipeline_mode=pl.Buffered(1)),
                  pl.BlockSpec((1, T * LANES, STATE_COLS), lambda g, j: (g, 0, 0), pipeline_mode=pl.Buffered(1)),
                  pl.BlockSpec((1, T * LANES, STATE_COLS), lambda g, j: (g, 0, 0), pipeline_mode=pl.Buffered(1)),
                  pl.BlockSpec((1, 2 * SUBLANES, STATE_COLS), lambda g, j: (g, 0, 0))],
        out_specs=pl.BlockSpec((tb, LANES), lambda g, j: (jnp.maximum(j - nb, 0), g)),
        out_shape=jax.ShapeDtypeStruct((S, SSM_WIDTH), f32),
        scratch_shapes=[pltpu.VMEM((STATE_COLS // LANES, S // T, LANES), f32)],
        compiler_params=_params(("arbitrary", "arbitrary")),
        name="ssm_main",
    )(u, d_row, wtoep, wstate, wcarry, apow)


def _outproj_kernel(yna_ref, yssm_ref, ymem_ref, x_ref, wglu_ref, bglu_ref, gssm_ref, wout_ref,
                    gpost_ref, gmlp_ref, x1_ref, h2_ref, *, nsub):
    a, b = NA_WIDTH, NA_WIDTH + SSM_WIDTH
    sub = x_ref.shape[0] // nsub
    tiles = [slice(r * sub, (r + 1) * sub) for r in range(nsub)]
    ys = []
    for rows in tiles:
        ya = yssm_ref[rows, :]
        gate = jax.nn.sigmoid(jnp.dot(ya.astype(bf16), wglu_ref[...], preferred_element_type=f32)
                              + bglu_ref[...])
        ys.append(_rms(ya * gate, gssm_ref[...]).astype(bf16))
    accs = [jnp.dot(yna_ref[rows, :], wout_ref[0:a, :], preferred_element_type=f32)
            + jnp.dot(ys[r], wout_ref[a:b, :], preferred_element_type=f32)
            + jnp.dot(ymem_ref[rows, :], wout_ref[b:, :], preferred_element_type=f32)
            for r, rows in enumerate(tiles)]
    for rows, acc in zip(tiles, accs):
        x1 = x_ref[rows, :] + _rms(acc, gpost_ref[...])
        x1_ref[rows, :] = x1
        h2_ref[rows, :] = _rms(x1, gmlp_ref[...]).astype(bf16)


def _out_proj(y_na, y_ssm, y_mem, x2, wglu_bf, b_glu, g_ssm, wout_bf, g_post, g_mlp, tm=512):
    S = x2.shape[0]
    row = lambda w: pl.BlockSpec((tm, w), lambda i: (i, 0))
    vec = lambda w: pl.BlockSpec((1, w), lambda i: (0, 0))
    return pl.pallas_call(
        functools.partial(_outproj_kernel, nsub=2),
        grid=(S // tm,),
        in_specs=[row(NA_WIDTH), row(SSM_WIDTH), row(MEM_WIDTH), row(D_MODEL),
                  pl.BlockSpec((SSM_WIDTH, SSM_WIDTH), lambda i: (0, 0)), vec(SSM_WIDTH), vec(SSM_WIDTH),
                  pl.BlockSpec((D_MODEL, D_MODEL), lambda i: (0, 0), pipeline_mode=pl.Buffered(1)),
                  vec(D_MODEL), vec(D_MODEL)],
        out_specs=[row(D_MODEL), row(D_MODEL)],
        out_shape=[jax.ShapeDtypeStruct((S, D_MODEL), f32), jax.ShapeDtypeStruct((S, D_MODEL), bf16)],
        compiler_params=_params(("parallel",)),
        name="out_proj",
    )(y_na, y_ssm, y_mem, x2, wglu_bf, b_glu, g_ssm, wout_bf, g_post, g_mlp)


def _mlp_kernel(h_ref, w1_ref, w2_ref, x1_hbm, g_ref, o_ref, x1_buf, x1_sem, *, nsplit):
    i = pl.program_id(0)
    k = pl.program_id(1)
    tm = o_ref.shape[0]

    def x1_copy():
        return pltpu.make_async_copy(x1_hbm.at[pl.ds(pl.multiple_of(i * tm, tm), tm), :], x1_buf, x1_sem)

    @pl.when(k == 0)
    def _():
        x1_copy().start()
        o_ref[...] = jnp.zeros_like(o_ref)

    hid = jnp.dot(h_ref[...], w1_ref[...], preferred_element_type=f32)
    hid = jnp.square(jnp.maximum(hid, 0.0)).astype(bf16)
    wn = D_MODEL // nsplit
    for n in range(nsplit):
        cols = slice(n * wn, (n + 1) * wn)
        o_ref[:, cols] += jnp.dot(hid, w2_ref[:, cols], preferred_element_type=f32)

    @pl.when(k == pl.num_programs(1) - 1)
    def _():
        x1_copy().wait()
        o_ref[...] = x1_buf[...] + _rms(o_ref[...], g_ref[...])


def _mlp(h2, w1_bf, w2_bf, x1, g, tm=1024, tk=512):
    S = h2.shape[0]
    return pl.pallas_call(
        functools.partial(_mlp_kernel, nsplit=4),
        grid=(S // tm, D_FF // tk),
        in_specs=[pl.BlockSpec((tm, D_MODEL), lambda i, k: (i, 0)),
                  pl.BlockSpec((D_MODEL, tk), lambda i, k: (0, k)),
                  pl.BlockSpec((tk, D_MODEL), lambda i, k: (k, 0)),
                  pl.BlockSpec(memory_space=pl.ANY),
                  pl.BlockSpec((1, D_MODEL), lambda i, k: (0, 0))],
        out_specs=pl.BlockSpec((tm, D_MODEL), lambda i, k: (i, 0)),
        out_shape=jax.ShapeDtypeStruct((S, D_MODEL), f32),
        scratch_shapes=[pltpu.VMEM((tm, D_MODEL), f32), pltpu.SemaphoreType.DMA(())],
        compiler_params=_params(("arbitrary", "arbitrary")),
        name="mlp",
    )(h2, w1_bf, w2_bf, x1, g)


def kernel(x, mem, norm_mix_pre, w_in, na_rpb, ssm_lam_re, ssm_lam_im, ssm_log_dt, ssm_b_re, ssm_b_im, ssm_c_re, ssm_c_im, ssm_d, w_glu, b_glu, mem_norm, w_mem_kv, out_norm_na, out_norm_ssm, out_norm_mem, w_out, norm_mix_post, norm_mlp_pre, w_ff1, w_ff2, norm_mlp_post):
    B, S, _ = x.shape
    assert B == 1 and x.shape[2] == D_MODEL and S % 4096 == 0 and w_in.shape[0] == 1
    x2 = x[0]
    l = 0
    vec = lambda a: a[l].reshape(1, -1)

    kv = _mem_kv(mem[0], vec(mem_norm), w_mem_kv[l].astype(bf16))
    P, u, y_mem, wglu_bf, wout_bf, w1_bf, w2_bf = _proj(
        x2, vec(norm_mix_pre), w_in[l].astype(bf16), kv, vec(out_norm_mem),
        [w_glu[l], w_out[l], w_ff1[l], w_ff2[l]])

    y_na = _na_attn(P, _na_bias(na_rpb[l], S), vec(out_norm_na))

    wtoep, wstate, wcarry, apow = _ssm_prep(ssm_lam_re[l], ssm_lam_im[l], ssm_log_dt[l], ssm_b_re[l],
                                            ssm_b_im[l], ssm_c_re[l], ssm_c_im[l], S)
    y_ssm = _ssm_main(u, ssm_d[l].reshape(1, SSM_WIDTH), wtoep, wstate, wcarry, apow)

    x1, h2 = _out_proj(y_na, y_ssm, y_mem, x2, wglu_bf, vec(b_glu), vec(out_norm_ssm),
                       wout_bf, vec(norm_mix_post), vec(norm_mlp_pre))
    out = _mlp(h2, w1_bf, w2_bf, x1, vec(norm_mlp_post))
    return out[None]
```

```python
import functools

import numpy as np
import jax
import jax.numpy as jnp
from jax import lax
from jax.experimental import pallas as pl
from jax.experimental.pallas import tpu as pltpu

f32 = jnp.float32
bf16 = jnp.bfloat16

D_MODEL = 2048
GRID_W = 64
WIN_H = 8
WIN_W = 16
HEAD_DIM = 128
NA_WIDTH = 1024
NA_HEADS = 8
SSM_WIDTH = 512
SSM_CH = 16
SSM_GROUPS = 32
SSM_STATE = 64
MEM_WIDTH = 512
MEM_HEADS = 4
N_MEM = 256
IN_WIDTH = 4096
D_FF = 8192
EPS = 1e-6
LAM_RE_MAX = -1e-4

LANES = 128
SUBLANES = 8
GROUPS_PER_TILE = LANES // SSM_CH
N_TILES = SSM_WIDTH // LANES
TILE_STATES = GROUPS_PER_TILE * SSM_STATE
STATE_COLS = 4 * TILE_STATES
SSM_T = 8
MASK_VALUE = -1e30
VMEM_LIMIT = 56 * 1024 * 1024


def _rms(x, g):
    ms = jnp.mean(x * x, axis=-1, keepdims=True)
    return x * lax.rsqrt(ms + EPS) * g


def _params(sem):
    return pltpu.CompilerParams(dimension_semantics=sem, vmem_limit_bytes=VMEM_LIMIT)


def _softmax_pv(s, v):
    p = jnp.exp(s - jnp.max(s, axis=-1, keepdims=True))
    inv = 1.0 / jnp.sum(p, axis=-1, keepdims=True)
    return jnp.dot(p.astype(bf16), v, preferred_element_type=f32) * inv


def _proj_kernel(x_ref, g_ref, w_ref, kv_ref, gmem_ref, *rest, ncast, scale):
    cast_in, rest = rest[:ncast], rest[ncast:]
    p_ref, u_ref, ymem_ref = rest[:3]
    cast_out, h_scr = rest[3:3 + ncast], rest[3 + ncast]
    for src, dst in zip(cast_in, cast_out):
        dst[...] = src[...].astype(bf16)
    h_scr[...] = _rms(x_ref[...], g_ref[...]).astype(bf16)
    nt = (((1,), (1,)), ((), ()))

    acc = jnp.dot(h_scr[...], w_ref[:, 3 * NA_WIDTH:], preferred_element_type=f32)
    u_ref[...] = acc[:, :SSM_WIDTH]
    qm = (acc[:, SSM_WIDTH:] * scale).astype(bf16)
    outs = []
    for h in range(MEM_HEADS):
        sl = slice(h * HEAD_DIM, (h + 1) * HEAD_DIM)
        s = lax.dot_general(qm[:, sl], kv_ref[:, sl], nt, preferred_element_type=f32)
        outs.append(_softmax_pv(s, kv_ref[:, MEM_WIDTH + h * HEAD_DIM:MEM_WIDTH + (h + 1) * HEAD_DIM]))
    ymem_ref[...] = _rms(jnp.concatenate(outs, axis=1), gmem_ref[...]).astype(bf16)

    for j in range(3):
        cols = slice(j * NA_WIDTH, (j + 1) * NA_WIDTH)
        acc = jnp.dot(h_scr[...], w_ref[:, cols], preferred_element_type=f32)
        if j == 0:
            acc = acc * scale
        p_ref[:, cols] = acc.astype(bf16)


def _proj(x2, g, w_bf, kv, g_mem, cast_weights, tm=512):
    S = x2.shape[0]
    steps = S // tm
    cast_specs = [pl.BlockSpec((w.shape[0] // steps, w.shape[1]), lambda i: (i, 0)) for w in cast_weights]
    return pl.pallas_call(
        functools.partial(_proj_kernel, ncast=len(cast_weights), scale=HEAD_DIM ** -0.5),
        grid=(steps,),
        in_specs=[pl.BlockSpec((tm, D_MODEL), lambda i: (i, 0)),
                  pl.BlockSpec((1, D_MODEL), lambda i: (0, 0)),
                  pl.BlockSpec((D_MODEL, IN_WIDTH), lambda i: (0, 0), pipeline_mode=pl.Buffered(1)),
                  pl.BlockSpec((N_MEM, 2 * MEM_WIDTH), lambda i: (0, 0)),
                  pl.BlockSpec((1, MEM_WIDTH), lambda i: (0, 0))] + cast_specs,
        out_specs=[pl.BlockSpec((tm, 3 * NA_WIDTH), lambda i: (i, 0)),
                   pl.BlockSpec((tm, SSM_WIDTH), lambda i: (i, 0)),
                   pl.BlockSpec((tm, MEM_WIDTH), lambda i: (i, 0))] + cast_specs,
        out_shape=[jax.ShapeDtypeStruct((S, 3 * NA_WIDTH), bf16),
                   jax.ShapeDtypeStruct((S, SSM_WIDTH), f32),
                   jax.ShapeDtypeStruct((S, MEM_WIDTH), bf16)]
                  + [jax.ShapeDtypeStruct(w.shape, bf16) for w in cast_weights],
        scratch_shapes=[pltpu.VMEM((tm, D_MODEL), bf16)],
        compiler_params=_params(("parallel",)),
        name="proj",
    )(x2, g, w_bf, kv, g_mem, *cast_weights)


NA_QB = 128
NA_KB = 5
NA_EDGE = 2
NA_STEP = 2
NA_ROWS_Q = NA_QB // GRID_W
NA_ROWS_WIN = NA_KB * NA_QB // GRID_W


def _na_window_start(b, nb):
    return jnp.clip(b - 2, 0, nb - NA_KB)


def _na_bias_kernel(v_ref, o_ref, *, nb, rows):
    c = lax.broadcasted_iota(jnp.int32, (GRID_W, LANES), 0)
    lane = lax.broadcasted_iota(jnp.int32, (GRID_W, LANES), 1)
    kc = lane % GRID_W
    cs = jnp.clip(c - WIN_W // 2, 0, GRID_W - WIN_W)
    col_ok = (kc >= cs) & (kc < cs + WIN_W)
    half_ok = {(True, True): col_ok, (True, False): col_ok & (lane < GRID_W), (False, True): col_ok & (lane >= GRID_W)}
    masked = jnp.full((GRID_W, LANES), MASK_VALUE, f32)
    variants = list(range(NA_EDGE)) + [NA_EDGE] + list(range(nb - NA_EDGE, nb))
    for vi, b in enumerate(variants):
        wb = min(max(b - 2, 0), nb - NA_KB)
        for rq in range(NA_ROWS_Q):
            r = b * NA_ROWS_Q + rq
            rs = min(max(r - WIN_H // 2, 0), rows - WIN_H)
            for m in range(NA_ROWS_WIN // 2):
                kr = wb * NA_ROWS_Q + 2 * m
                ok = (rs <= kr < rs + WIN_H, rs <= kr + 1 < rs + WIN_H)
                if ok == (False, False):
                    blk = masked
                else:
                    d = kr - r + WIN_H
                    src = jnp.broadcast_to(v_ref[0, d:d + 1, :], (GRID_W, LANES))
                    rolled = pltpu.roll(src, LANES - (WIN_W - 1), 1, stride=1, stride_axis=0)
                    blk = jnp.where(half_ok[ok], rolled, MASK_VALUE)
                o_ref[vi, 0, rq * GRID_W:(rq + 1) * GRID_W, m * LANES:(m + 1) * LANES] = blk


def _na_bias(rpb, S):
    heads, nrow, ncol = rpb.shape
    nb = S // NA_QB
    nvar = 2 * NA_EDGE + 1
    padded = jnp.zeros((heads, nrow + 2, GRID_W), f32).at[:, 1:nrow + 1, :ncol].set(rpb.astype(f32))
    pairs = jnp.concatenate([padded[:, :-1], padded[:, 1:]], axis=-1)
    return pl.pallas_call(
        functools.partial(_na_bias_kernel, nb=nb, rows=S // GRID_W),
        grid=(heads,),
        in_specs=[pl.BlockSpec((1, nrow + 1, LANES), lambda h: (h, 0, 0))],
        out_specs=pl.BlockSpec((nvar, 1, NA_QB, NA_KB * NA_QB), lambda h: (0, h, 0, 0)),
        out_shape=jax.ShapeDtypeStruct((nvar, heads, NA_QB, NA_KB * NA_QB), f32),
        compiler_params=_params(("parallel",)),
        name="na_bias",
    )(pairs)


def _na_step_window_start(i, nb):
    return jnp.clip(NA_STEP * i - 2, 0, nb - NA_KB - NA_STEP + 1)


def _na_kernel(q_ref, k_ref, v_ref, *rest, nb):
    tab_refs = rest[:NA_STEP]
    g_ref, o_ref, s_scr, p_scr = rest[NA_STEP:]
    i = pl.program_id(0)
    win0 = _na_step_window_start(i, nb)
    nt = (((1,), (1,)), ((), ()))
    starts = []
    for jb in range(NA_STEP):
        local = _na_window_start(NA_STEP * i + jb, nb) - win0
        starts.append(pl.multiple_of(local * NA_QB, NA_QB))
        for h in range(NA_HEADS):
            sl = slice(h * HEAD_DIM, (h + 1) * HEAD_DIM)
            k = k_ref[pl.ds(starts[jb], NA_KB * NA_QB), sl]
            s_scr[jb, h] = lax.dot_general(q_ref[jb * NA_QB:(jb + 1) * NA_QB, sl], k, nt,
                                           preferred_element_type=f32) + tab_refs[jb][0, h]
    inv = []
    for jb in range(NA_STEP):
        for h in range(NA_HEADS):
            s = s_scr[jb, h]
            p = jnp.exp(s - jnp.max(s, axis=-1, keepdims=True))
            inv.append(1.0 / jnp.sum(p, axis=-1, keepdims=True))
            p_scr[jb, h] = p.astype(bf16)
    for jb in range(NA_STEP):
        outs = []
        for h in range(NA_HEADS):
            sl = slice(h * HEAD_DIM, (h + 1) * HEAD_DIM)
            v = v_ref[pl.ds(starts[jb], NA_KB * NA_QB), sl]
            outs.append(jnp.dot(p_scr[jb, h], v, preferred_element_type=f32) * inv[jb * NA_HEADS + h])
        o_ref[jb * NA_QB:(jb + 1) * NA_QB, :] = _rms(jnp.concatenate(outs, axis=1), g_ref[...]).astype(bf16)


def _na_attn(P, tab, g):
    S = P.shape[0]
    nb = S // NA_QB
    win = (NA_KB + NA_STEP - 1) * NA_QB

    def variant(b):
        return jnp.where(b < NA_EDGE, b, jnp.where(b >= nb - NA_EDGE, b - (nb - 2 * NA_EDGE - 1), NA_EDGE))

    def window(col):
        return pl.BlockSpec((pl.Element(win), pl.Element(NA_WIDTH)),
                            lambda i: (_na_step_window_start(i, nb) * NA_QB, col))

    tab_specs = [pl.BlockSpec((1, NA_HEADS, NA_QB, NA_KB * NA_QB),
                              lambda i, jb=jb: (variant(NA_STEP * i + jb), 0, 0, 0)) for jb in range(NA_STEP)]
    return pl.pallas_call(
        functools.partial(_na_kernel, nb=nb),
        grid=(nb // NA_STEP,),
        in_specs=[pl.BlockSpec((NA_STEP * NA_QB, NA_WIDTH), lambda i: (i, 0)),
                  window(NA_WIDTH), window(2 * NA_WIDTH)] + tab_specs + [
                  pl.BlockSpec((1, NA_WIDTH), lambda i: (0, 0))],
        out_specs=pl.BlockSpec((NA_STEP * NA_QB, NA_WIDTH), lambda i: (i, 0)),
        out_shape=jax.ShapeDtypeStruct((S, NA_WIDTH), bf16),
        scratch_shapes=[pltpu.VMEM((NA_STEP, NA_HEADS, NA_QB, NA_KB * NA_QB), f32),
                        pltpu.VMEM((NA_STEP, NA_HEADS, NA_QB, NA_KB * NA_QB), bf16)],
        compiler_params=_params(("parallel",)),
        name="na_attn",
    )(P, P, P, *([tab] * NA_STEP), g)


def _memkv_kernel(m_ref, g_ref, w_ref, o_ref):
    o_ref[...] = jnp.dot(_rms(m_ref[...], g_ref[...]).astype(bf16), w_ref[...].astype(bf16),
                         preferred_element_type=f32).astype(bf16)


def _mem_kv(mem2, g, w):
    return pl.pallas_call(
        _memkv_kernel,
        out_shape=jax.ShapeDtypeStruct((N_MEM, 2 * MEM_WIDTH), bf16),
        compiler_params=pltpu.CompilerParams(vmem_limit_bytes=VMEM_LIMIT),
        name="mem_kv",
    )(mem2, g, w)


def _ssm_prep_kernel(lre_r, lim_r, ldt_r, bre_ref, bim_ref, cre_ref, cim_ref, win_ref,
                     wtoep_ref, wstate_ref, wcarry_ref, apow_ref, win_bf_ref, *, seg_len):
    win_bf_ref[...] = win_ref[...].astype(bf16)
    T = SSM_T
    TS = TILE_STATES
    reps = LANES // SSM_CH
    assert T == SUBLANES

    def same_group(shape, row_div, col_div):
        r = lax.broadcasted_iota(jnp.int32, shape, 0) // row_div
        c = lax.broadcasted_iota(jnp.int32, shape, 1) // col_div
        return (r == c).astype(f32)

    mask_state = same_group((LANES, TS), SSM_CH, SSM_STATE)
    mask_chan = same_group((LANES, LANES), SSM_CH, SSM_CH)

    kk = []
    for d in range(2):
        lre = jnp.minimum(lre_r[d, 0], LAM_RE_MAX)
        lim = lim_r[d, 0]
        dt = jnp.exp(ldt_r[d, 0])
        zr = lre * dt
        zi = lim * dt
        er = jnp.exp(zr)
        nr = er * jnp.cos(zi) - 1.0
        ni = er * jnp.sin(zi)
        den = lre * lre + lim * lim
        qr = (nr * lre + ni * lim) / den
        qi = (ni * lre - nr * lim) / den
        bre = bre_ref[d, 0]
        bim = bim_ref[d, 0]
        bbr = qr * bre - qi * bim
        bbi = qr * bim + qi * bre
        e0 = lax.broadcasted_iota(jnp.int32, (T, TS), 0).astype(f32)
        mag0 = jnp.exp(zr * e0)
        p0r, p0i = mag0 * jnp.cos(zi * e0), mag0 * jnp.sin(zi * e0)
        mag1 = jnp.exp(zr * (e0 + 1.0))
        p1r, p1i = mag1 * jnp.cos(zi * (e0 + 1.0)), mag1 * jnp.sin(zi * (e0 + 1.0))
        wc_r = [p0r[e:e + 1] * bbr - p0i[e:e + 1] * bbi for e in range(T)]
        wc_i = [p0r[e:e + 1] * bbi + p0i[e:e + 1] * bbr for e in range(T)]

        for s in range(T):
            e = T - 1 - s if d == 0 else s
            rows = slice(s * LANES, (s + 1) * LANES)
            base = d * 2 * TS
            wstate_ref[0, rows, base:base + TS] = (
                jnp.concatenate([wc_r[e]] * reps, axis=0) * mask_state).astype(bf16)
            wstate_ref[0, rows, base + TS:base + 2 * TS] = (
                jnp.concatenate([wc_i[e]] * reps, axis=0) * mask_state).astype(bf16)

        cre = cre_ref[d, 0]
        cim = cim_ref[d, 0]
        nt = (((1,), (1,)), ((), ()))
        kc = (lax.dot_general(jnp.concatenate(wc_r, axis=0), jnp.concatenate([cre] * reps, axis=0) * mask_state,
                              nt, precision=lax.Precision.HIGHEST, preferred_element_type=f32)
              - lax.dot_general(jnp.concatenate(wc_i, axis=0), jnp.concatenate([cim] * reps, axis=0) * mask_state,
                                nt, precision=lax.Precision.HIGHEST, preferred_element_type=f32))
        kk.append([jnp.concatenate([kc[e * SSM_CH:(e + 1) * SSM_CH]] * reps, axis=0) * mask_chan
                   for e in range(T)])

        for t in range(T):
            e = t if d == 0 else T - 1 - t
            pr, pi = p1r[e:e + 1], p1i[e:e + 1]
            rows = slice(t * LANES, (t + 1) * LANES)
            base = d * 2 * TS
            wcarry_ref[0, rows, base:base + TS] = (
                jnp.concatenate([cre * pr - cim * pi] * reps, axis=0) * mask_state).astype(bf16)
            wcarry_ref[0, rows, base + TS:base + 2 * TS] = (
                jnp.concatenate([-(cre * pi + cim * pr)] * reps, axis=0) * mask_state).astype(bf16)

        for part, steps in enumerate((T, T * seg_len)):
            k1 = (lax.broadcasted_iota(jnp.int32, (SUBLANES, TS), 0) + 1).astype(f32) * float(steps)
            mag = jnp.exp(zr * k1)
            rows = slice(part * SUBLANES, (part + 1) * SUBLANES)
            apow_ref[0, rows, d * 2 * TS:d * 2 * TS + TS] = mag * jnp.cos(zi * k1)
            apow_ref[0, rows, d * 2 * TS + TS:(d + 1) * 2 * TS] = mag * jnp.sin(zi * k1)

    for s in range(T):
        for t in range(T):
            if t > s:
                blk = kk[0][t - s]
            elif t < s:
                blk = kk[1][s - t]
            else:
                blk = kk[0][0] + kk[1][0]
            wtoep_ref[0, s * LANES:(s + 1) * LANES, t * LANES:(t + 1) * LANES] = blk.astype(bf16)


def _ssm_prep(lam_re, lam_im, log_dt, b_re, b_im, c_re, c_im, S, w_in):
    TS, NT, T = TILE_STATES, N_TILES, SSM_T
    gp = GROUPS_PER_TILE
    ldt = jnp.broadcast_to(log_dt[:, :, None], lam_re.shape)
    rows = [a.reshape(2, NT, 1, TS) for a in (lam_re, lam_im, ldt)]

    def b_layout(b):
        return b.reshape(2, NT, gp, SSM_STATE, SSM_CH).transpose(0, 1, 4, 2, 3).reshape(2, NT, SSM_CH, TS)

    def c_layout(c):
        return c.reshape(2, NT, gp, SSM_CH, SSM_STATE).transpose(0, 1, 3, 2, 4).reshape(2, NT, SSM_CH, TS)

    row_spec = pl.BlockSpec((2, 1, 1, TS), lambda g: (0, g, 0, 0))
    bc_spec = pl.BlockSpec((2, 1, SSM_CH, TS), lambda g: (0, g, 0, 0))
    win_spec = pl.BlockSpec((w_in.shape[0] // NT, w_in.shape[1]), lambda g: (g, 0))
    return pl.pallas_call(
        functools.partial(_ssm_prep_kernel, seg_len=S // T // SUBLANES),
        grid=(NT,),
        in_specs=[row_spec] * 3 + [bc_spec] * 4 + [win_spec],
        out_specs=[pl.BlockSpec((1, T * LANES, T * LANES), lambda g: (g, 0, 0)),
                   pl.BlockSpec((1, T * LANES, STATE_COLS), lambda g: (g, 0, 0)),
                   pl.BlockSpec((1, T * LANES, STATE_COLS), lambda g: (g, 0, 0)),
                   pl.BlockSpec((1, 2 * SUBLANES, STATE_COLS), lambda g: (g, 0, 0)), win_spec],
        out_shape=[jax.ShapeDtypeStruct((NT, T * LANES, T * LANES), bf16),
                   jax.ShapeDtypeStruct((NT, T * LANES, STATE_COLS), bf16),
                   jax.ShapeDtypeStruct((NT, T * LANES, STATE_COLS), bf16),
                   jax.ShapeDtypeStruct((NT, 2 * SUBLANES, STATE_COLS), f32),
                   jax.ShapeDtypeStruct(w_in.shape, bf16)],
        compiler_params=_params(("parallel",)),
        name="ssm_prep",
    )(*rows, b_layout(b_re), b_layout(b_im), c_layout(c_re), c_layout(c_im), w_in)


def _cmul_add(xr, xi, ar, ai, sr, si):
    return xr + ar * sr - ai * si, xi + ar * si + ai * sr


def _scan_rows(xr, xi, apr, api, cr, ci, reverse):
    n = SUBLANES
    rows = lax.broadcasted_iota(jnp.int32, xr.shape, 0)
    for k in (1, 2, 4):
        ar, ai = apr[k - 1:k], api[k - 1:k]
        if reverse:
            sr, si = pltpu.roll(xr, n - k, 0), pltpu.roll(xi, n - k, 0)
            keep = rows < n - k
        else:
            sr, si = pltpu.roll(xr, k, 0), pltpu.roll(xi, k, 0)
            keep = rows >= k
        xr, xi = _cmul_add(xr, xi, ar, ai, jnp.where(keep, sr, 0.0), jnp.where(keep, si, 0.0))
    if reverse:
        pwr = jnp.concatenate([apr[n - 1 - j:n - j] for j in range(n)], axis=0)
        pwi = jnp.concatenate([api[n - 1 - j:n - j] for j in range(n)], axis=0)
    else:
        pwr, pwi = apr, api
    hr, hi = _cmul_add(xr, xi, pwr, pwi, cr, ci)
    if reverse:
        inr = jnp.where(rows == n - 1, cr, pltpu.roll(hr, n - 1, 0))
        ini = jnp.where(rows == n - 1, ci, pltpu.roll(hi, n - 1, 0))
        return inr, ini, hr[0:1], hi[0:1]
    inr = jnp.where(rows == 0, cr, pltpu.roll(hr, 1, 0))
    ini = jnp.where(rows == 0, ci, pltpu.roll(hi, 1, 0))
    return inr, ini, hr[n - 1:n], hi[n - 1:n]


def _ssm_kernel(u_ref, d_ref, wtoep_ref, wstate_ref, wcarry_ref, apow_ref, o_ref, h_scr, *, nb, cb):
    T = SSM_T
    TS = TILE_STATES
    j = pl.program_id(1)

    nseg = SUBLANES
    seg = nb * cb // nseg
    assert cb % seg == 0

    def chunk_rows():
        return [u_ref[pl.ds(t, cb, stride=T), :] for t in range(T)]

    def batch_rows(b):
        return [pl.ds(b * (cb // seg) + part, seg, stride=nseg) for part in range(cb // seg)]

    @pl.when(j < nb)
    def _():
        x = jnp.concatenate([ut.astype(bf16) for ut in chunk_rows()], axis=1)
        res = jnp.dot(x, wstate_ref[0], preferred_element_type=f32)
        for part, rows in enumerate(batch_rows(j)):
            for c in range(STATE_COLS // LANES):
                h_scr[c, rows, :] = res[part * seg:(part + 1) * seg, c * LANES:(c + 1) * LANES]

    @pl.when(j == nb - 1)
    def _():
        tiles = TS // LANES

        def cols(d, ri):
            return slice((2 * d + ri) * TS, (2 * d + ri + 1) * TS)

        def rows_at(p):
            return pl.ds(pl.multiple_of(p * nseg, nseg), nseg)

        def load(rows, d, ri):
            return jnp.concatenate([h_scr[(2 * d + ri) * tiles + c, rows, :] for c in range(tiles)], axis=1)

        def store(rows, d, ri, val):
            for c in range(tiles):
                h_scr[(2 * d + ri) * tiles + c, rows, :] = val[:, c * LANES:(c + 1) * LANES]

        a = [[jnp.broadcast_to(apow_ref[0, 0:1, cols(d, ri)], (nseg, TS)) for ri in range(2)] for d in range(2)]
        zero = jnp.zeros((nseg, TS), f32)

        def step(p, h, write):
            hfr, hfi, hbr, hbi = h
            rf, rb = rows_at(p), rows_at(seg - 1 - p)
            xfr, xfi, xbr, xbi = load(rf, 0, 0), load(rf, 0, 1), load(rb, 1, 0), load(rb, 1, 1)
            if write:
                store(rf, 0, 0, hfr)
                store(rf, 0, 1, hfi)
                store(rb, 1, 0, hbr)
                store(rb, 1, 1, hbi)
            hfr, hfi = _cmul_add(xfr, xfi, a[0][0], a[0][1], hfr, hfi)
            hbr, hbi = _cmul_add(xbr, xbi, a[1][0], a[1][1], hbr, hbi)
            return hfr, hfi, hbr, hbi

        ends = lax.fori_loop(0, seg, lambda p, h: step(p, h, False), (zero, zero, zero, zero))
        z1 = jnp.zeros((1, TS), f32)
        enter = []
        for d in range(2):
            cr, ci, _, _ = _scan_rows(ends[2 * d], ends[2 * d + 1], apow_ref[0, nseg:2 * nseg, cols(d, 0)],
                                      apow_ref[0, nseg:2 * nseg, cols(d, 1)], z1, z1, d == 1)
            enter += [cr, ci]
        lax.fori_loop(0, seg, lambda p, h: step(p, h, True), tuple(enter))

    @pl.when(j >= nb)
    def _():
        us = chunk_rows()
        x = jnp.concatenate([ut.astype(bf16) for ut in us], axis=1)
        hc = jnp.concatenate(
            [jnp.concatenate([h_scr[c, rows, :].astype(bf16) for c in range(STATE_COLS // LANES)], axis=1)
             for rows in batch_rows(j - nb)], axis=0)
        y = (jnp.dot(x, wtoep_ref[0], preferred_element_type=f32)
             + lax.dot_general(hc, wcarry_ref[0], (((1,), (1,)), ((), ())), preferred_element_type=f32))
        for t in range(T):
            yt = y[:, t * LANES:(t + 1) * LANES] + d_ref[...] * us[t]
            o_ref[pl.ds(t, cb, stride=T), :] = jax.nn.gelu(yt)


def _ssm_main(u, d_row, wtoep, wstate, wcarry, apow, tb=4096):
    S = u.shape[0]
    T = SSM_T
    nb = S // tb
    cb = tb // T
    return pl.pallas_call(
        functools.partial(_ssm_kernel, nb=nb, cb=cb),
        grid=(N_TILES, 2 * nb),
        in_specs=[pl.BlockSpec((tb, LANES), lambda g, j: (j % nb, g)),
                  pl.BlockSpec((1, LANES), lambda g, j: (0, g)),
                  pl.BlockSpec((1, T * LANES, T * LANES), lambda g, j: (g, 0, 0), pipeline_mode=pl.Buffered(1)),
                  pl.BlockSpec((1, T * LANES, STATE_COLS), lambda g, j: (g, 0, 0), pipeline_mode=pl.Buffered(1)),
                  pl.BlockSpec((1, T * LANES, STATE_COLS), lambda g, j: (g, 0, 0), pipeline_mode=pl.Buffered(1)),
                  pl.BlockSpec((1, 2 * SUBLANES, STATE_COLS), lambda g, j: (g, 0, 0))],
        out_specs=pl.BlockSpec((tb, LANES), lambda g, j: (jnp.maximum(j - nb, 0), g)),
        out_shape=jax.ShapeDtypeStruct((S, SSM_WIDTH), f32),
        scratch_shapes=[pltpu.VMEM((STATE_COLS // LANES, S // T, LANES), f32)],
        compiler_params=_params(("arbitrary", "arbitrary")),
        name="ssm_main",
    )(u, d_row, wtoep, wstate, wcarry, apow)


def _outproj_kernel(yna_ref, yssm_ref, ymem_ref, x_ref, wglu_ref, bglu_ref, gssm_ref, wout_ref,
                    gpost_ref, gmlp_ref, x1_ref, h2_ref, *, nsub):
    a, b = NA_WIDTH, NA_WIDTH + SSM_WIDTH
    sub = x_ref.shape[0] // nsub
    tiles = [slice(r * sub, (r + 1) * sub) for r in range(nsub)]
    ys = []
    for rows in tiles:
        ya = yssm_ref[rows, :]
        gate = jax.nn.sigmoid(jnp.dot(ya.astype(bf16), wglu_ref[...], preferred_element_type=f32)
                              + bglu_ref[...])
        ys.append(_rms(ya * gate, gssm_ref[...]).astype(bf16))
    accs = [jnp.dot(yna_ref[rows, :], wout_ref[0:a, :], preferred_element_type=f32)
            + jnp.dot(ys[r], wout_ref[a:b, :], preferred_element_type=f32)
            + jnp.dot(ymem_ref[rows, :], wout_ref[b:, :], preferred_element_type=f32)
            for r, rows in enumerate(tiles)]
    for rows, acc in zip(tiles, accs):
        x1 = x_ref[rows, :] + _rms(acc, gpost_ref[...])
        x1_ref[rows, :] = x1
        h2_ref[rows, :] = _rms(x1, gmlp_ref[...]).astype(bf16)


def _out_proj(y_na, y_ssm, y_mem, x2, wglu_bf, b_glu, g_ssm, wout_bf, g_post, g_mlp, tm=512):
    S = x2.shape[0]
    row = lambda w: pl.BlockSpec((tm, w), lambda i: (i, 0))
    vec = lambda w: pl.BlockSpec((1, w), lambda i: (0, 0))
    return pl.pallas_call(
        functools.partial(_outproj_kernel, nsub=4),
        grid=(S // tm,),
        in_specs=[row(NA_WIDTH), row(SSM_WIDTH), row(MEM_WIDTH), row(D_MODEL),
                  pl.BlockSpec((SSM_WIDTH, SSM_WIDTH), lambda i: (0, 0)), vec(SSM_WIDTH), vec(SSM_WIDTH),
                  pl.BlockSpec((D_MODEL, D_MODEL), lambda i: (0, 0), pipeline_mode=pl.Buffered(1)),
                  vec(D_MODEL), vec(D_MODEL)],
        out_specs=[row(D_MODEL), row(D_MODEL)],
        out_shape=[jax.ShapeDtypeStruct((S, D_MODEL), f32), jax.ShapeDtypeStruct((S, D_MODEL), bf16)],
        compiler_params=_params(("parallel",)),
        name="out_proj",
    )(y_na, y_ssm, y_mem, x2, wglu_bf, b_glu, g_ssm, wout_bf, g_post, g_mlp)


def _mlp_kernel(h_ref, w1_ref, w2_ref, x1_hbm, g_ref, o_ref, hid_scr, x1_buf, x1_sem, *, nsplit):
    i = pl.program_id(0)
    k = pl.program_id(1)
    nk = pl.num_programs(1) - 1
    tm = o_ref.shape[0]
    wn = D_MODEL // nsplit

    def x1_copy():
        return pltpu.make_async_copy(x1_hbm.at[pl.ds(pl.multiple_of(i * tm, tm), tm), :], x1_buf, x1_sem)

    def hidden():
        hid = jnp.dot(h_ref[...], w1_ref[...], preferred_element_type=f32)
        return jnp.square(jnp.maximum(hid, 0.0)).astype(bf16)

    def partial_out(n):
        cols = slice(n * wn, (n + 1) * wn)
        return o_ref[:, cols] + jnp.dot(hid_scr[(k + 1) % 2], w2_ref[:, cols], preferred_element_type=f32)

    @pl.when(k == 0)
    def _():
        x1_copy().start()
        o_ref[...] = jnp.zeros_like(o_ref)
        hid_scr[0] = hidden()

    @pl.when((k > 0) & (k < nk))
    def _():
        for n in range(nsplit):
            o_ref[:, n * wn:(n + 1) * wn] = partial_out(n)
        hid_scr[k % 2] = hidden()

    @pl.when(k == nk)
    def _():
        ssq = jnp.zeros((tm, 1), f32)
        for n in range(nsplit):
            f = partial_out(n)
            o_ref[:, n * wn:(n + 1) * wn] = f
            ssq = ssq + jnp.sum(f * f, axis=-1, keepdims=True)
        scale = lax.rsqrt(ssq * (1.0 / D_MODEL) + EPS)
        x1_copy().wait()
        o_ref[...] = x1_buf[...] + o_ref[...] * scale * g_ref[...]


def _mlp(h2, w1_bf, w2_bf, x1, g, tm=1024, tk=512):
    S = h2.shape[0]
    nk = D_FF // tk
    return pl.pallas_call(
        functools.partial(_mlp_kernel, nsplit=4),
        grid=(S // tm, nk + 1),
        in_specs=[pl.BlockSpec((tm, D_MODEL), lambda i, k: (i, 0)),
                  pl.BlockSpec((D_MODEL, tk), lambda i, k: (0, jnp.minimum(k, nk - 1))),
                  pl.BlockSpec((tk, D_MODEL), lambda i, k: (jnp.maximum(k - 1, 0), 0)),
                  pl.BlockSpec(memory_space=pl.ANY),
                  pl.BlockSpec((1, D_MODEL), lambda i, k: (0, 0))],
        out_specs=pl.BlockSpec((tm, D_MODEL), lambda i, k: (i, 0)),
        out_shape=jax.ShapeDtypeStruct((S, D_MODEL), f32),
        scratch_shapes=[pltpu.VMEM((2, tm, tk), bf16), pltpu.VMEM((tm, D_MODEL), f32),
                        pltpu.SemaphoreType.DMA(())],
        compiler_params=_params(("arbitrary", "arbitrary")),
        name="mlp",
    )(h2, w1_bf, w2_bf, x1, g)


def kernel(x, mem, norm_mix_pre, w_in, na_rpb, ssm_lam_re, ssm_lam_im, ssm_log_dt, ssm_b_re, ssm_b_im, ssm_c_re, ssm_c_im, ssm_d, w_glu, b_glu, mem_norm, w_mem_kv, out_norm_na, out_norm_ssm, out_norm_mem, w_out, norm_mix_post, norm_mlp_pre, w_ff1, w_ff2, norm_mlp_post):
    B, S, _ = x.shape
    assert B == 1 and x.shape[2] == D_MODEL and S % 4096 == 0 and w_in.shape[0] == 1
    x2 = x[0]
    l = 0
    vec = lambda a: a[l].reshape(1, -1)

    kv = _mem_kv(mem[0], vec(mem_norm), w_mem_kv[l])
    wtoep, wstate, wcarry, apow, win_bf = _ssm_prep(ssm_lam_re[l], ssm_lam_im[l], ssm_log_dt[l], ssm_b_re[l],
                                                    ssm_b_im[l], ssm_c_re[l], ssm_c_im[l], S, w_in[l])
    P, u, y_mem, wglu_bf, wout_bf, w1_bf, w2_bf = _proj(
        x2, vec(norm_mix_pre), win_bf, kv, vec(out_norm_mem), [w_glu[l], w_out[l], w_ff1[l], w_ff2[l]])

    y_na = _na_attn(P, _na_bias(na_rpb[l], S), vec(out_norm_na))

    y_ssm = _ssm_main(u, ssm_d[l].reshape(1, SSM_WIDTH), wtoep, wstate, wcarry, apow)

    x1, h2 = _out_proj(y_na, y_ssm, y_mem, x2, wglu_bf, vec(b_glu), vec(out_norm_ssm),
                       wout_bf, vec(norm_mix_post), vec(norm_mlp_pre))
    out = _mlp(h2, w1_bf, w2_bf, x1, vec(norm_mlp_post))
    return out[None]
```

```python
import functools

import jax
import jax.numpy as jnp
from jax import lax
from jax.experimental import pallas as pl
from jax.experimental.pallas import tpu as pltpu

f32 = jnp.float32
bf16 = jnp.bfloat16

D_MODEL = 2048
GRID_W = 64
WIN_H = 8
WIN_W = 16
HEAD_DIM = 128
NA_WIDTH = 1024
NA_HEADS = 8
SSM_WIDTH = 512
SSM_CH = 16
SSM_GROUPS = 32
SSM_STATE = 64
MEM_WIDTH = 512
MEM_HEADS = 4
N_MEM = 256
IN_WIDTH = 4096
D_FF = 8192
EPS = 1e-6
LAM_RE_MAX = -1e-4

LANES = 128
SUBLANES = 8
GROUPS_PER_TILE = LANES // SSM_CH
N_TILES = SSM_WIDTH // LANES
TILE_STATES = GROUPS_PER_TILE * SSM_STATE
STATE_COLS = 4 * TILE_STATES
SSM_T = 8
MASK_VALUE = -1e30
VMEM_LIMIT = 58 * 1024 * 1024


def _rms(x, g):
    ms = jnp.mean(x * x, axis=-1, keepdims=True)
    return x * lax.rsqrt(ms + EPS) * g


def _params(sem):
    return pltpu.CompilerParams(dimension_semantics=sem, vmem_limit_bytes=VMEM_LIMIT)


def _softmax_pv(s, v):
    p = jnp.exp(s - jnp.max(s, axis=-1, keepdims=True))
    inv = 1.0 / jnp.sum(p, axis=-1, keepdims=True)
    return jnp.dot(p.astype(bf16), v, preferred_element_type=f32) * inv


def _proj_kernel(x_ref, g_ref, w_ref, kv_ref, gmem_ref, *rest, ncast, scale):
    cast_in, rest = rest[:ncast], rest[ncast:]
    p_ref, u_ref, ymem_ref = rest[:3]
    cast_out, h_scr = rest[3:3 + ncast], rest[3 + ncast]
    for src, dst in zip(cast_in, cast_out):
        dst[...] = src[...].astype(bf16)
    h_scr[...] = _rms(x_ref[...], g_ref[...]).astype(bf16)
    nt = (((1,), (1,)), ((), ()))

    acc = jnp.dot(h_scr[...], w_ref[:, 3 * NA_WIDTH:], preferred_element_type=f32)
    u_ref[...] = acc[:, :SSM_WIDTH]
    qm = (acc[:, SSM_WIDTH:] * scale).astype(bf16)
    outs = []
    for h in range(MEM_HEADS):
        sl = slice(h * HEAD_DIM, (h + 1) * HEAD_DIM)
        s = lax.dot_general(qm[:, sl], kv_ref[:, sl], nt, preferred_element_type=f32)
        outs.append(_softmax_pv(s, kv_ref[:, MEM_WIDTH + h * HEAD_DIM:MEM_WIDTH + (h + 1) * HEAD_DIM]))
    ymem_ref[...] = _rms(jnp.concatenate(outs, axis=1), gmem_ref[...]).astype(bf16)

    for j in range(3):
        cols = slice(j * NA_WIDTH, (j + 1) * NA_WIDTH)
        acc = jnp.dot(h_scr[...], w_ref[:, cols], preferred_element_type=f32)
        if j == 0:
            acc = acc * scale
        p_ref[:, cols] = acc.astype(bf16)


def _proj(x2, g, w_bf, kv, g_mem, cast_weights, tm=512):
    S = x2.shape[0]
    steps = S // tm
    cast_specs = [pl.BlockSpec((w.shape[0] // steps, w.shape[1]), lambda i: (i, 0)) for w in cast_weights]
    return pl.pallas_call(
        functools.partial(_proj_kernel, ncast=len(cast_weights), scale=HEAD_DIM ** -0.5),
        grid=(steps,),
        in_specs=[pl.BlockSpec((tm, D_MODEL), lambda i: (i, 0)),
                  pl.BlockSpec((1, D_MODEL), lambda i: (0, 0)),
                  pl.BlockSpec((D_MODEL, IN_WIDTH), lambda i: (0, 0), pipeline_mode=pl.Buffered(1)),
                  pl.BlockSpec((N_MEM, 2 * MEM_WIDTH), lambda i: (0, 0)),
                  pl.BlockSpec((1, MEM_WIDTH), lambda i: (0, 0))] + cast_specs,
        out_specs=[pl.BlockSpec((tm, 3 * NA_WIDTH), lambda i: (i, 0)),
                   pl.BlockSpec((tm, SSM_WIDTH), lambda i: (i, 0)),
                   pl.BlockSpec((tm, MEM_WIDTH), lambda i: (i, 0))] + cast_specs,
        out_shape=[jax.ShapeDtypeStruct((S, 3 * NA_WIDTH), bf16),
                   jax.ShapeDtypeStruct((S, SSM_WIDTH), f32),
                   jax.ShapeDtypeStruct((S, MEM_WIDTH), bf16)]
                  + [jax.ShapeDtypeStruct(w.shape, bf16) for w in cast_weights],
        scratch_shapes=[pltpu.VMEM((tm, D_MODEL), bf16)],
        compiler_params=_params(("parallel",)),
        name="proj",
    )(x2, g, w_bf, kv, g_mem, *cast_weights)


NA_QB = 128
NA_KB = 5
NA_EDGE = 2
NA_STEP = 4
NA_ROWS_Q = NA_QB // GRID_W
NA_ROWS_WIN = NA_KB * NA_QB // GRID_W


def _na_window_start(b, nb):
    return jnp.clip(b - 2, 0, nb - NA_KB)


def _na_bias_kernel(v_ref, o_ref, *, nb, rows):
    c = lax.broadcasted_iota(jnp.int32, (GRID_W, LANES), 0)
    lane = lax.broadcasted_iota(jnp.int32, (GRID_W, LANES), 1)
    kc = lane % GRID_W
    cs = jnp.clip(c - WIN_W // 2, 0, GRID_W - WIN_W)
    col_ok = (kc >= cs) & (kc < cs + WIN_W)
    half_ok = {(True, True): col_ok, (True, False): col_ok & (lane < GRID_W), (False, True): col_ok & (lane >= GRID_W)}
    masked = jnp.full((GRID_W, LANES), MASK_VALUE, f32)
    variants = list(range(NA_EDGE)) + [NA_EDGE] + list(range(nb - NA_EDGE, nb))
    for vi, b in enumerate(variants):
        wb = min(max(b - 2, 0), nb - NA_KB)
        for rq in range(NA_ROWS_Q):
            r = b * NA_ROWS_Q + rq
            rs = min(max(r - WIN_H // 2, 0), rows - WIN_H)
            for m in range(NA_ROWS_WIN // 2):
                kr = wb * NA_ROWS_Q + 2 * m
                ok = (rs <= kr < rs + WIN_H, rs <= kr + 1 < rs + WIN_H)
                if ok == (False, False):
                    blk = masked
                else:
                    d = kr - r + WIN_H
                    src = jnp.broadcast_to(v_ref[0, d:d + 1, :], (GRID_W, LANES))
                    rolled = pltpu.roll(src, LANES - (WIN_W - 1), 1, stride=1, stride_axis=0)
                    blk = jnp.where(half_ok[ok], rolled, MASK_VALUE)
                o_ref[vi, 0, rq * GRID_W:(rq + 1) * GRID_W, m * LANES:(m + 1) * LANES] = blk


def _na_bias(rpb, S):
    heads, nrow, ncol = rpb.shape
    nb = S // NA_QB
    nvar = 2 * NA_EDGE + 1
    padded = jnp.zeros((heads, nrow + 2, GRID_W), f32).at[:, 1:nrow + 1, :ncol].set(rpb.astype(f32))
    pairs = jnp.concatenate([padded[:, :-1], padded[:, 1:]], axis=-1)
    return pl.pallas_call(
        functools.partial(_na_bias_kernel, nb=nb, rows=S // GRID_W),
        grid=(heads,),
        in_specs=[pl.BlockSpec((1, nrow + 1, LANES), lambda h: (h, 0, 0))],
        out_specs=pl.BlockSpec((nvar, 1, NA_QB, NA_KB * NA_QB), lambda h: (0, h, 0, 0)),
        out_shape=jax.ShapeDtypeStruct((nvar, heads, NA_QB, NA_KB * NA_QB), f32),
        compiler_params=_params(("parallel",)),
        name="na_bias",
    )(pairs)


def _na_step_window_start(i, nb):
    return jnp.clip(NA_STEP * i - 2, 0, nb - NA_KB - NA_STEP + 1)


def _na_kernel(q_ref, k_ref, v_ref, *rest, nb):
    tab_refs = rest[:NA_STEP]
    g_ref, o_ref, s_scr, p_scr = rest[NA_STEP:]
    i = pl.program_id(0)
    win0 = _na_step_window_start(i, nb)
    nt = (((1,), (1,)), ((), ()))
    starts = []
    for jb in range(NA_STEP):
        local = _na_window_start(NA_STEP * i + jb, nb) - win0
        starts.append(pl.multiple_of(local * NA_QB, NA_QB))
        for h in range(NA_HEADS):
            sl = slice(h * HEAD_DIM, (h + 1) * HEAD_DIM)
            k = k_ref[pl.ds(starts[jb], NA_KB * NA_QB), sl]
            s_scr[jb, h] = lax.dot_general(q_ref[jb * NA_QB:(jb + 1) * NA_QB, sl], k, nt,
                                           preferred_element_type=f32) + tab_refs[jb][0, h]
    inv = []
    for jb in range(NA_STEP):
        for h in range(NA_HEADS):
            s = s_scr[jb, h]
            p = jnp.exp(s - jnp.max(s, axis=-1, keepdims=True))
            inv.append(1.0 / jnp.sum(p, axis=-1, keepdims=True))
            p_scr[jb, h] = p.astype(bf16)
    for jb in range(NA_STEP):
        outs = []
        for h in range(NA_HEADS):
            sl = slice(h * HEAD_DIM, (h + 1) * HEAD_DIM)
            v = v_ref[pl.ds(starts[jb], NA_KB * NA_QB), sl]
            outs.append(jnp.dot(p_scr[jb, h], v, preferred_element_type=f32) * inv[jb * NA_HEADS + h])
        o_ref[jb * NA_QB:(jb + 1) * NA_QB, :] = _rms(jnp.concatenate(outs, axis=1), g_ref[...]).astype(bf16)


def _na_attn(P, tab, g):
    S = P.shape[0]
    nb = S // NA_QB
    win = (NA_KB + NA_STEP - 1) * NA_QB

    def variant(b):
        return jnp.where(b < NA_EDGE, b, jnp.where(b >= nb - NA_EDGE, b - (nb - 2 * NA_EDGE - 1), NA_EDGE))

    def window(col):
        return pl.BlockSpec((pl.Element(win), pl.Element(NA_WIDTH)),
                            lambda i: (_na_step_window_start(i, nb) * NA_QB, col))

    tab_specs = [pl.BlockSpec((1, NA_HEADS, NA_QB, NA_KB * NA_QB),
                              lambda i, jb=jb: (variant(NA_STEP * i + jb), 0, 0, 0)) for jb in range(NA_STEP)]
    return pl.pallas_call(
        functools.partial(_na_kernel, nb=nb),
        grid=(nb // NA_STEP,),
        in_specs=[pl.BlockSpec((NA_STEP * NA_QB, NA_WIDTH), lambda i: (i, 0)),
                  window(NA_WIDTH), window(2 * NA_WIDTH)] + tab_specs + [
                  pl.BlockSpec((1, NA_WIDTH), lambda i: (0, 0))],
        out_specs=pl.BlockSpec((NA_STEP * NA_QB, NA_WIDTH), lambda i: (i, 0)),
        out_shape=jax.ShapeDtypeStruct((S, NA_WIDTH), bf16),
        scratch_shapes=[pltpu.VMEM((NA_STEP, NA_HEADS, NA_QB, NA_KB * NA_QB), f32),
                        pltpu.VMEM((NA_STEP, NA_HEADS, NA_QB, NA_KB * NA_QB), bf16)],
        compiler_params=_params(("parallel",)),
        name="na_attn",
    )(P, P, P, *([tab] * NA_STEP), g)


def _memkv_kernel(m_ref, g_ref, w_ref, o_ref):
    o_ref[...] = jnp.dot(_rms(m_ref[...], g_ref[...]).astype(bf16), w_ref[...].astype(bf16),
                         preferred_element_type=f32).astype(bf16)


def _mem_kv(mem2, g, w):
    return pl.pallas_call(
        _memkv_kernel,
        out_shape=jax.ShapeDtypeStruct((N_MEM, 2 * MEM_WIDTH), bf16),
        compiler_params=pltpu.CompilerParams(vmem_limit_bytes=VMEM_LIMIT),
        name="mem_kv",
    )(mem2, g, w)


def _ssm_prep_kernel(lre_r, lim_r, ldt_r, bre_ref, bim_ref, cre_ref, cim_ref, win_ref,
                     wtoep_ref, wstate_ref, wcarry_ref, apow_ref, win_bf_ref, *, seg_len):
    win_bf_ref[...] = win_ref[...].astype(bf16)
    T = SSM_T
    TS = TILE_STATES
    reps = LANES // SSM_CH
    assert T == SUBLANES

    def same_group(shape, row_div, col_div):
        r = lax.broadcasted_iota(jnp.int32, shape, 0) // row_div
        c = lax.broadcasted_iota(jnp.int32, shape, 1) // col_div
        return (r == c).astype(f32)

    mask_state = same_group((LANES, TS), SSM_CH, SSM_STATE)
    mask_chan = same_group((LANES, LANES), SSM_CH, SSM_CH)

    kk = []
    for d in range(2):
        lre = jnp.minimum(lre_r[d, 0], LAM_RE_MAX)
        lim = lim_r[d, 0]
        dt = jnp.exp(ldt_r[d, 0])
        zr = lre * dt
        zi = lim * dt
        er = jnp.exp(zr)
        nr = er * jnp.cos(zi) - 1.0
        ni = er * jnp.sin(zi)
        den = lre * lre + lim * lim
        qr = (nr * lre + ni * lim) / den
        qi = (ni * lre - nr * lim) / den
        bre = bre_ref[d, 0]
        bim = bim_ref[d, 0]
        bbr = qr * bre - qi * bim
        bbi = qr * bim + qi * bre
        e0 = lax.broadcasted_iota(jnp.int32, (T, TS), 0).astype(f32)
        mag0 = jnp.exp(zr * e0)
        p0r, p0i = mag0 * jnp.cos(zi * e0), mag0 * jnp.sin(zi * e0)
        mag1 = jnp.exp(zr * (e0 + 1.0))
        p1r, p1i = mag1 * jnp.cos(zi * (e0 + 1.0)), mag1 * jnp.sin(zi * (e0 + 1.0))
        wc_r = [p0r[e:e + 1] * bbr - p0i[e:e + 1] * bbi for e in range(T)]
        wc_i = [p0r[e:e + 1] * bbi + p0i[e:e + 1] * bbr for e in range(T)]

        for s in range(T):
            e = T - 1 - s if d == 0 else s
            rows = slice(s * LANES, (s + 1) * LANES)
            base = d * 2 * TS
            wstate_ref[0, rows, base:base + TS] = (
                jnp.concatenate([wc_r[e]] * reps, axis=0) * mask_state).astype(bf16)
            wstate_ref[0, rows, base + TS:base + 2 * TS] = (
                jnp.concatenate([wc_i[e]] * reps, axis=0) * mask_state).astype(bf16)

        cre = cre_ref[d, 0]
        cim = cim_ref[d, 0]
        nt = (((1,), (1,)), ((), ()))
        kc = (lax.dot_general(jnp.concatenate(wc_r, axis=0), jnp.concatenate([cre] * reps, axis=0) * mask_state,
                              nt, precision=lax.Precision.HIGHEST, preferred_element_type=f32)
              - lax.dot_general(jnp.concatenate(wc_i, axis=0), jnp.concatenate([cim] * reps, axis=0) * mask_state,
                                nt, precision=lax.Precision.HIGHEST, preferred_element_type=f32))
        kk.append([jnp.concatenate([kc[e * SSM_CH:(e + 1) * SSM_CH]] * reps, axis=0) * mask_chan
                   for e in range(T)])

        for t in range(T):
            e = t if d == 0 else T - 1 - t
            pr, pi = p1r[e:e + 1], p1i[e:e + 1]
            rows = slice(t * LANES, (t + 1) * LANES)
            base = d * 2 * TS
            wcarry_ref[0, rows, base:base + TS] = (
                jnp.concatenate([cre * pr - cim * pi] * reps, axis=0) * mask_state).astype(bf16)
            wcarry_ref[0, rows, base + TS:base + 2 * TS] = (
                jnp.concatenate([-(cre * pi + cim * pr)] * reps, axis=0) * mask_state).astype(bf16)

        for part, steps in enumerate((T, T * seg_len)):
            k1 = (lax.broadcasted_iota(jnp.int32, (SUBLANES, TS), 0) + 1).astype(f32) * float(steps)
            mag = jnp.exp(zr * k1)
            rows = slice(part * SUBLANES, (part + 1) * SUBLANES)
            apow_ref[0, rows, d * 2 * TS:d * 2 * TS + TS] = mag * jnp.cos(zi * k1)
            apow_ref[0, rows, d * 2 * TS + TS:(d + 1) * 2 * TS] = mag * jnp.sin(zi * k1)

    for s in range(T):
        for t in range(T):
            if t > s:
                blk = kk[0][t - s]
            elif t < s:
                blk = kk[1][s - t]
            else:
                blk = kk[0][0] + kk[1][0]
            wtoep_ref[0, s * LANES:(s + 1) * LANES, t * LANES:(t + 1) * LANES] = blk.astype(bf16)


def _ssm_prep(lam_re, lam_im, log_dt, b_re, b_im, c_re, c_im, S, w_in):
    TS, NT, T = TILE_STATES, N_TILES, SSM_T
    gp = GROUPS_PER_TILE
    ldt = jnp.broadcast_to(log_dt[:, :, None], lam_re.shape)
    rows = [a.reshape(2, NT, 1, TS) for a in (lam_re, lam_im, ldt)]

    def b_layout(b):
        return b.reshape(2, NT, gp, SSM_STATE, SSM_CH).transpose(0, 1, 4, 2, 3).reshape(2, NT, SSM_CH, TS)

    def c_layout(c):
        return c.reshape(2, NT, gp, SSM_CH, SSM_STATE).transpose(0, 1, 3, 2, 4).reshape(2, NT, SSM_CH, TS)

    row_spec = pl.BlockSpec((2, 1, 1, TS), lambda g: (0, g, 0, 0))
    bc_spec = pl.BlockSpec((2, 1, SSM_CH, TS), lambda g: (0, g, 0, 0))
    win_spec = pl.BlockSpec((w_in.shape[0] // NT, w_in.shape[1]), lambda g: (g, 0))
    return pl.pallas_call(
        functools.partial(_ssm_prep_kernel, seg_len=S // T // SUBLANES),
        grid=(NT,),
        in_specs=[row_spec] * 3 + [bc_spec] * 4 + [win_spec],
        out_specs=[pl.BlockSpec((1, T * LANES, T * LANES), lambda g: (g, 0, 0)),
                   pl.BlockSpec((1, T * LANES, STATE_COLS), lambda g: (g, 0, 0)),
                   pl.BlockSpec((1, T * LANES, STATE_COLS), lambda g: (g, 0, 0)),
                   pl.BlockSpec((1, 2 * SUBLANES, STATE_COLS), lambda g: (g, 0, 0)), win_spec],
        out_shape=[jax.ShapeDtypeStruct((NT, T * LANES, T * LANES), bf16),
                   jax.ShapeDtypeStruct((NT, T * LANES, STATE_COLS), bf16),
                   jax.ShapeDtypeStruct((NT, T * LANES, STATE_COLS), bf16),
                   jax.ShapeDtypeStruct((NT, 2 * SUBLANES, STATE_COLS), f32),
                   jax.ShapeDtypeStruct(w_in.shape, bf16)],
        compiler_params=_params(("parallel",)),
        name="ssm_prep",
    )(*rows, b_layout(b_re), b_layout(b_im), c_layout(c_re), c_layout(c_im), w_in)


def _cmul_add(xr, xi, ar, ai, sr, si):
    return xr + ar * sr - ai * si, xi + ar * si + ai * sr


def _scan_rows(xr, xi, apr, api, cr, ci, reverse):
    n = SUBLANES
    rows = lax.broadcasted_iota(jnp.int32, xr.shape, 0)
    for k in (1, 2, 4):
        ar, ai = apr[k - 1:k], api[k - 1:k]
        if reverse:
            sr, si = pltpu.roll(xr, n - k, 0), pltpu.roll(xi, n - k, 0)
            keep = rows < n - k
        else:
            sr, si = pltpu.roll(xr, k, 0), pltpu.roll(xi, k, 0)
            keep = rows >= k
        xr, xi = _cmul_add(xr, xi, ar, ai, jnp.where(keep, sr, 0.0), jnp.where(keep, si, 0.0))
    if reverse:
        pwr = jnp.concatenate([apr[n - 1 - j:n - j] for j in range(n)], axis=0)
        pwi = jnp.concatenate([api[n - 1 - j:n - j] for j in range(n)], axis=0)
    else:
        pwr, pwi = apr, api
    hr, hi = _cmul_add(xr, xi, pwr, pwi, cr, ci)
    if reverse:
        inr = jnp.where(rows == n - 1, cr, pltpu.roll(hr, n - 1, 0))
        ini = jnp.where(rows == n - 1, ci, pltpu.roll(hi, n - 1, 0))
        return inr, ini, hr[0:1], hi[0:1]
    inr = jnp.where(rows == 0, cr, pltpu.roll(hr, 1, 0))
    ini = jnp.where(rows == 0, ci, pltpu.roll(hi, 1, 0))
    return inr, ini, hr[n - 1:n], hi[n - 1:n]


def _ssm_kernel(u_ref, d_ref, wtoep_ref, wstate_ref, wcarry_ref, apow_ref, o_ref, h_scr, *, nb, cb):
    T = SSM_T
    TS = TILE_STATES
    j = pl.program_id(1)

    nseg = SUBLANES
    seg = nb * cb // nseg
    assert cb % seg == 0

    def chunk_rows():
        return [u_ref[pl.ds(t, cb, stride=T), :] for t in range(T)]

    def batch_rows(b):
        return [pl.ds(b * (cb // seg) + part, seg, stride=nseg) for part in range(cb // seg)]

    @pl.when(j < nb)
    def _():
        x = jnp.concatenate([ut.astype(bf16) for ut in chunk_rows()], axis=1)
        res = jnp.dot(x, wstate_ref[0], preferred_element_type=f32)
        for part, rows in enumerate(batch_rows(j)):
            for c in range(STATE_COLS // LANES):
                h_scr[c, rows, :] = res[part * seg:(part + 1) * seg, c * LANES:(c + 1) * LANES]

    @pl.when(j == nb - 1)
    def _():
        tiles = TS // LANES

        def cols(d, ri):
            return slice((2 * d + ri) * TS, (2 * d + ri + 1) * TS)

        def rows_at(p):
            return pl.ds(pl.multiple_of(p * nseg, nseg), nseg)

        def load(rows, d, ri):
            return jnp.concatenate([h_scr[(2 * d + ri) * tiles + c, rows, :] for c in range(tiles)], axis=1)

        def store(rows, d, ri, val):
            for c in range(tiles):
                h_scr[(2 * d + ri) * tiles + c, rows, :] = val[:, c * LANES:(c + 1) * LANES]

        a = [[jnp.broadcast_to(apow_ref[0, 0:1, cols(d, ri)], (nseg, TS)) for ri in range(2)] for d in range(2)]
        zero = jnp.zeros((nseg, TS), f32)

        def step(p, h, write):
            hfr, hfi, hbr, hbi = h
            rf, rb = rows_at(p), rows_at(seg - 1 - p)
            xfr, xfi, xbr, xbi = load(rf, 0, 0), load(rf, 0, 1), load(rb, 1, 0), load(rb, 1, 1)
            if write:
                store(rf, 0, 0, hfr)
                store(rf, 0, 1, hfi)
                store(rb, 1, 0, hbr)
                store(rb, 1, 1, hbi)
            hfr, hfi = _cmul_add(xfr, xfi, a[0][0], a[0][1], hfr, hfi)
            hbr, hbi = _cmul_add(xbr, xbi, a[1][0], a[1][1], hbr, hbi)
            return hfr, hfi, hbr, hbi

        ends = lax.fori_loop(0, seg, lambda p, h: step(p, h, False), (zero, zero, zero, zero))
        z1 = jnp.zeros((1, TS), f32)
        enter = []
        for d in range(2):
            cr, ci, _, _ = _scan_rows(ends[2 * d], ends[2 * d + 1], apow_ref[0, nseg:2 * nseg, cols(d, 0)],
                                      apow_ref[0, nseg:2 * nseg, cols(d, 1)], z1, z1, d == 1)
            enter += [cr, ci]
        lax.fori_loop(0, seg, lambda p, h: step(p, h, True), tuple(enter))

    @pl.when(j >= nb)
    def _():
        us = chunk_rows()
        x = jnp.concatenate([ut.astype(bf16) for ut in us], axis=1)
        hc = jnp.concatenate(
            [jnp.concatenate([h_scr[c, rows, :].astype(bf16) for c in range(STATE_COLS // LANES)], axis=1)
             for rows in batch_rows(j - nb)], axis=0)
        y = (jnp.dot(x, wtoep_ref[0], preferred_element_type=f32)
             + lax.dot_general(hc, wcarry_ref[0], (((1,), (1,)), ((), ())), preferred_element_type=f32))
        for t in range(T):
            yt = y[:, t * LANES:(t + 1) * LANES] + d_ref[...] * us[t]
            o_ref[pl.ds(t, cb, stride=T), :] = jax.nn.gelu(yt)


def _ssm_main(u, d_row, wtoep, wstate, wcarry, apow, tb=4096):
    S = u.shape[0]
    T = SSM_T
    nb = S // tb
    cb = tb // T
    return pl.pallas_call(
        functools.partial(_ssm_kernel, nb=nb, cb=cb),
        grid=(N_TILES, 2 * nb),
        in_specs=[pl.BlockSpec((tb, LANES), lambda g, j: (j % nb, g)),
                  pl.BlockSpec((1, LANES), lambda g, j: (0, g)),
                  pl.BlockSpec((1, T * LANES, T * LANES), lambda g, j: (g, 0, 0), pipeline_mode=pl.Buffered(1)),
                  pl.BlockSpec((1, T * LANES, STATE_COLS), lambda g, j: (g, 0, 0), pipeline_mode=pl.Buffered(1)),
                  pl.BlockSpec((1, T * LANES, STATE_COLS), lambda g, j: (g, 0, 0), pipeline_mode=pl.Buffered(1)),
                  pl.BlockSpec((1, 2 * SUBLANES, STATE_COLS), lambda g, j: (g, 0, 0))],
        out_specs=pl.BlockSpec((tb, LANES), lambda g, j: (jnp.maximum(j - nb, 0), g)),
        out_shape=jax.ShapeDtypeStruct((S, SSM_WIDTH), f32),
        scratch_shapes=[pltpu.VMEM((STATE_COLS // LANES, S // T, LANES), f32)],
        compiler_params=_params(("arbitrary", "arbitrary")),
        name="ssm_main",
    )(u, d_row, wtoep, wstate, wcarry, apow)


def _outproj_kernel(yna_ref, yssm_ref, ymem_ref, x_ref, wglu_ref, bglu_ref, gssm_ref, wout_ref,
                    gpost_ref, gmlp_ref, x1_ref, h2_ref, *, nsub):
    a, b = NA_WIDTH, NA_WIDTH + SSM_WIDTH
    sub = x_ref.shape[0] // nsub
    tiles = [slice(r * sub, (r + 1) * sub) for r in range(nsub)]
    ys = []
    for rows in tiles:
        ya = yssm_ref[rows, :]
        gate = jax.nn.sigmoid(jnp.dot(ya.astype(bf16), wglu_ref[...], preferred_element_type=f32)
                              + bglu_ref[...])
        ys.append(_rms(ya * gate, gssm_ref[...]).astype(bf16))
    accs = [jnp.dot(yna_ref[rows, :], wout_ref[0:a, :], preferred_element_type=f32)
            + jnp.dot(ys[r], wout_ref[a:b, :], preferred_element_type=f32)
            + jnp.dot(ymem_ref[rows, :], wout_ref[b:, :], preferred_element_type=f32)
            for r, rows in enumerate(tiles)]
    for rows, acc in zip(tiles, accs):
        x1 = x_ref[rows, :] + _rms(acc, gpost_ref[...])
        x1_ref[rows, :] = x1
        h2_ref[rows, :] = _rms(x1, gmlp_ref[...]).astype(bf16)


def _out_proj(y_na, y_ssm, y_mem, x2, wglu_bf, b_glu, g_ssm, wout_bf, g_post, g_mlp, tm=512):
    S = x2.shape[0]
    row = lambda w: pl.BlockSpec((tm, w), lambda i: (i, 0))
    vec = lambda w: pl.BlockSpec((1, w), lambda i: (0, 0))
    return pl.pallas_call(
        functools.partial(_outproj_kernel, nsub=4),
        grid=(S // tm,),
        in_specs=[row(NA_WIDTH), row(SSM_WIDTH), row(MEM_WIDTH), row(D_MODEL),
                  pl.BlockSpec((SSM_WIDTH, SSM_WIDTH), lambda i: (0, 0)), vec(SSM_WIDTH), vec(SSM_WIDTH),
                  pl.BlockSpec((D_MODEL, D_MODEL), lambda i: (0, 0), pipeline_mode=pl.Buffered(1)),
                  vec(D_MODEL), vec(D_MODEL)],
        out_specs=[row(D_MODEL), row(D_MODEL)],
        out_shape=[jax.ShapeDtypeStruct((S, D_MODEL), f32), jax.ShapeDtypeStruct((S, D_MODEL), bf16)],
        compiler_params=_params(("parallel",)),
        name="out_proj",
    )(y_na, y_ssm, y_mem, x2, wglu_bf, b_glu, g_ssm, wout_bf, g_post, g_mlp)


def _mlp_kernel(h_ref, w1_ref, w2_ref, x1_hbm, g_ref, o_ref, hid_scr, x1_buf, x1_sem, *, nsplit):
    i = pl.program_id(0)
    k = pl.program_id(1)
    nk = pl.num_programs(1) - 1
    tm = o_ref.shape[0]
    wn = D_MODEL // nsplit

    def x1_copy():
        return pltpu.make_async_copy(x1_hbm.at[pl.ds(pl.multiple_of(i * tm, tm), tm), :], x1_buf, x1_sem)

    def hidden():
        hid = jnp.dot(h_ref[...], w1_ref[...], preferred_element_type=f32)
        return jnp.square(jnp.maximum(hid, 0.0)).astype(bf16)

    def partial_out(n):
        cols = slice(n * wn, (n + 1) * wn)
        return o_ref[:, cols] + jnp.dot(hid_scr[(k + 1) % 2], w2_ref[:, cols], preferred_element_type=f32)

    @pl.when(k == 0)
    def _():
        x1_copy().start()
        o_ref[...] = jnp.zeros_like(o_ref)
        hid_scr[0] = hidden()

    @pl.when((k > 0) & (k < nk))
    def _():
        for n in range(nsplit):
            o_ref[:, n * wn:(n + 1) * wn] = partial_out(n)
        hid_scr[k % 2] = hidden()

    @pl.when(k == nk)
    def _():
        ssq = jnp.zeros((tm, 1), f32)
        for n in range(nsplit):
            f = partial_out(n)
            o_ref[:, n * wn:(n + 1) * wn] = f
            ssq = ssq + jnp.sum(f * f, axis=-1, keepdims=True)
        scale = lax.rsqrt(ssq * (1.0 / D_MODEL) + EPS)
        x1_copy().wait()
        o_ref[...] = x1_buf[...] + o_ref[...] * scale * g_ref[...]


def _mlp(h2, w1_bf, w2_bf, x1, g, tm=1024, tk=1024):
    S = h2.shape[0]
    nk = D_FF // tk
    return pl.pallas_call(
        functools.partial(_mlp_kernel, nsplit=4),
        grid=(S // tm, nk + 1),
        in_specs=[pl.BlockSpec((tm, D_MODEL), lambda i, k: (i, 0)),
                  pl.BlockSpec((D_MODEL, tk), lambda i, k: (0, jnp.minimum(k, nk - 1))),
                  pl.BlockSpec((tk, D_MODEL), lambda i, k: (jnp.maximum(k - 1, 0), 0)),
                  pl.BlockSpec(memory_space=pl.ANY),
                  pl.BlockSpec((1, D_MODEL), lambda i, k: (0, 0))],
        out_specs=pl.BlockSpec((tm, D_MODEL), lambda i, k: (i, 0)),
        out_shape=jax.ShapeDtypeStruct((S, D_MODEL), f32),
        scratch_shapes=[pltpu.VMEM((2, tm, tk), bf16), pltpu.VMEM((tm, D_MODEL), f32),
                        pltpu.SemaphoreType.DMA(())],
        compiler_params=_params(("arbitrary", "arbitrary")),
        name="mlp",
    )(h2, w1_bf, w2_bf, x1, g)


def kernel(x, mem, norm_mix_pre, w_in, na_rpb, ssm_lam_re, ssm_lam_im, ssm_log_dt, ssm_b_re, ssm_b_im, ssm_c_re, ssm_c_im, ssm_d, w_glu, b_glu, mem_norm, w_mem_kv, out_norm_na, out_norm_ssm, out_norm_mem, w_out, norm_mix_post, norm_mlp_pre, w_ff1, w_ff2, norm_mlp_post):
    B, S, _ = x.shape
    assert B == 1 and x.shape[2] == D_MODEL and S % 4096 == 0 and w_in.shape[0] == 1
    x2 = x[0]
    l = 0
    vec = lambda a: a[l].reshape(1, -1)

    kv = _mem_kv(mem[0], vec(mem_norm), w_mem_kv[l])
    wtoep, wstate, wcarry, apow, win_bf = _ssm_prep(ssm_lam_re[l], ssm_lam_im[l], ssm_log_dt[l], ssm_b_re[l],
                                                    ssm_b_im[l], ssm_c_re[l], ssm_c_im[l], S, w_in[l])
    P, u, y_mem, wglu_bf, wout_bf, w1_bf, w2_bf = _proj(
        x2, vec(norm_mix_pre), win_bf, kv, vec(out_norm_mem), [w_glu[l], w_out[l], w_ff1[l], w_ff2[l]])

    y_na = _na_attn(P, _na_bias(na_rpb[l], S), vec(out_norm_na))

    y_ssm = _ssm_main(u, ssm_d[l].reshape(1, SSM_WIDTH), wtoep, wstate, wcarry, apow)

    x1, h2 = _out_proj(y_na, y_ssm, y_mem, x2, wglu_bf, vec(b_glu), vec(out_norm_ssm),
                       wout_bf, vec(norm_mix_post), vec(norm_mlp_pre))
    out = _mlp(h2, w1_bf, w2_bf, x1, vec(norm_mlp_post))
    return out[None]
```

```python
import functools

import jax
import jax.numpy as jnp
from jax import lax
from jax.experimental import pallas as pl
from jax.experimental.pallas import tpu as pltpu

f32 = jnp.float32
bf16 = jnp.bfloat16

D_MODEL = 2048
GRID_W = 64
WIN_H = 8
WIN_W = 16
HEAD_DIM = 128
NA_WIDTH = 1024
NA_HEADS = 8
SSM_WIDTH = 512
SSM_CH = 16
SSM_GROUPS = 32
SSM_STATE = 64
MEM_WIDTH = 512
MEM_HEADS = 4
N_MEM = 256
IN_WIDTH = 4096
D_FF = 8192
EPS = 1e-6
LAM_RE_MAX = -1e-4

LANES = 128
SUBLANES = 8
GROUPS_PER_TILE = LANES // SSM_CH
N_TILES = SSM_WIDTH // LANES
TILE_STATES = GROUPS_PER_TILE * SSM_STATE
STATE_COLS = 4 * TILE_STATES
SSM_T = 8
MASK_VALUE = -1e30
VMEM_LIMIT = 58 * 1024 * 1024


def _rms(x, g):
    ms = jnp.mean(x * x, axis=-1, keepdims=True)
    return x * lax.rsqrt(ms + EPS) * g


def _params(sem):
    return pltpu.CompilerParams(dimension_semantics=sem, vmem_limit_bytes=VMEM_LIMIT)


def _softmax_pv(s, v):
    p = jnp.exp(s - jnp.max(s, axis=-1, keepdims=True))
    inv = 1.0 / jnp.sum(p, axis=-1, keepdims=True)
    return jnp.dot(p.astype(bf16), v, preferred_element_type=f32) * inv


def _proj_kernel(x_ref, g_ref, w_ref, kv_ref, gmem_ref, *rest, ncast, scale):
    cast_in, rest = rest[:ncast], rest[ncast:]
    p_ref, kt_ref, u_ref, ymem_ref = rest[:4]
    cast_out, h_scr = rest[4:4 + ncast], rest[4 + ncast]
    for src, dst in zip(cast_in, cast_out):
        dst[...] = src[...].astype(bf16)
    h_scr[...] = _rms(x_ref[...], g_ref[...]).astype(bf16)
    nt = (((1,), (1,)), ((), ()))

    acc = jnp.dot(h_scr[...], w_ref[:, 3 * NA_WIDTH:], preferred_element_type=f32)
    u_ref[...] = acc[:, :SSM_WIDTH]
    qm = (acc[:, SSM_WIDTH:] * scale).astype(bf16)
    outs = []
    for h in range(MEM_HEADS):
        sl = slice(h * HEAD_DIM, (h + 1) * HEAD_DIM)
        s = lax.dot_general(qm[:, sl], kv_ref[:, sl], nt, preferred_element_type=f32)
        outs.append(_softmax_pv(s, kv_ref[:, MEM_WIDTH + h * HEAD_DIM:MEM_WIDTH + (h + 1) * HEAD_DIM]))
    ymem_ref[...] = _rms(jnp.concatenate(outs, axis=1), gmem_ref[...]).astype(bf16)

    for j in range(3):
        acc = jnp.dot(h_scr[...], w_ref[:, j * NA_WIDTH:(j + 1) * NA_WIDTH], preferred_element_type=f32)
        if j == 0:
            p_ref[:, :NA_WIDTH] = (acc * scale).astype(bf16)
        elif j == 1:
            kt_ref[...] = acc.T.astype(bf16)
        else:
            p_ref[:, NA_WIDTH:] = acc.astype(bf16)


def _proj(x2, g, w_bf, kv, g_mem, cast_weights, tm=512):
    S = x2.shape[0]
    steps = S // tm
    cast_specs = [pl.BlockSpec((w.shape[0] // steps, w.shape[1]), lambda i: (i, 0)) for w in cast_weights]
    return pl.pallas_call(
        functools.partial(_proj_kernel, ncast=len(cast_weights), scale=HEAD_DIM ** -0.5),
        grid=(steps,),
        in_specs=[pl.BlockSpec((tm, D_MODEL), lambda i: (i, 0)),
                  pl.BlockSpec((1, D_MODEL), lambda i: (0, 0)),
                  pl.BlockSpec((D_MODEL, IN_WIDTH), lambda i: (0, 0), pipeline_mode=pl.Buffered(1)),
                  pl.BlockSpec((N_MEM, 2 * MEM_WIDTH), lambda i: (0, 0)),
                  pl.BlockSpec((1, MEM_WIDTH), lambda i: (0, 0))] + cast_specs,
        out_specs=[pl.BlockSpec((tm, 2 * NA_WIDTH), lambda i: (i, 0)),
                   pl.BlockSpec((NA_WIDTH, tm), lambda i: (0, i)),
                   pl.BlockSpec((tm, SSM_WIDTH), lambda i: (i, 0)),
                   pl.BlockSpec((tm, MEM_WIDTH), lambda i: (i, 0))] + cast_specs,
        out_shape=[jax.ShapeDtypeStruct((S, 2 * NA_WIDTH), bf16),
                   jax.ShapeDtypeStruct((NA_WIDTH, S), bf16),
                   jax.ShapeDtypeStruct((S, SSM_WIDTH), f32),
                   jax.ShapeDtypeStruct((S, MEM_WIDTH), bf16)]
                  + [jax.ShapeDtypeStruct(w.shape, bf16) for w in cast_weights],
        scratch_shapes=[pltpu.VMEM((tm, D_MODEL), bf16)],
        compiler_params=_params(("parallel",)),
        name="proj",
    )(x2, g, w_bf, kv, g_mem, *cast_weights)


NA_QB = 128
NA_KB = 5
NA_EDGE = 2
NA_STEP = 4
NA_ROWS_Q = NA_QB // GRID_W
NA_ROWS_WIN = NA_KB * NA_QB // GRID_W


def _na_window_start(b, nb):
    return jnp.clip(b - 2, 0, nb - NA_KB)


def _na_bias_kernel(v_ref, o_ref, *, nb, rows):
    c = lax.broadcasted_iota(jnp.int32, (GRID_W, LANES), 0)
    lane = lax.broadcasted_iota(jnp.int32, (GRID_W, LANES), 1)
    kc = lane % GRID_W
    cs = jnp.clip(c - WIN_W // 2, 0, GRID_W - WIN_W)
    col_ok = (kc >= cs) & (kc < cs + WIN_W)
    half_ok = {(True, True): col_ok, (True, False): col_ok & (lane < GRID_W), (False, True): col_ok & (lane >= GRID_W)}
    masked = jnp.full((GRID_W, LANES), MASK_VALUE, f32)
    variants = list(range(NA_EDGE)) + [NA_EDGE] + list(range(nb - NA_EDGE, nb))
    for vi, b in enumerate(variants):
        wb = min(max(b - 2, 0), nb - NA_KB)
        for rq in range(NA_ROWS_Q):
            r = b * NA_ROWS_Q + rq
            rs = min(max(r - WIN_H // 2, 0), rows - WIN_H)
            for m in range(NA_ROWS_WIN // 2):
                kr = wb * NA_ROWS_Q + 2 * m
                ok = (rs <= kr < rs + WIN_H, rs <= kr + 1 < rs + WIN_H)
                if ok == (False, False):
                    blk = masked
                else:
                    d = kr - r + WIN_H
                    src = jnp.broadcast_to(v_ref[0, d:d + 1, :], (GRID_W, LANES))
                    rolled = pltpu.roll(src, LANES - (WIN_W - 1), 1, stride=1, stride_axis=0)
                    blk = jnp.where(half_ok[ok], rolled, MASK_VALUE)
                o_ref[vi, 0, rq * GRID_W:(rq + 1) * GRID_W, m * LANES:(m + 1) * LANES] = blk


def _na_bias(rpb, S):
    heads, nrow, ncol = rpb.shape
    nb = S // NA_QB
    nvar = 2 * NA_EDGE + 1
    padded = jnp.zeros((heads, nrow + 2, GRID_W), f32).at[:, 1:nrow + 1, :ncol].set(rpb.astype(f32))
    pairs = jnp.concatenate([padded[:, :-1], padded[:, 1:]], axis=-1)
    return pl.pallas_call(
        functools.partial(_na_bias_kernel, nb=nb, rows=S // GRID_W),
        grid=(heads,),
        in_specs=[pl.BlockSpec((1, nrow + 1, LANES), lambda h: (h, 0, 0))],
        out_specs=pl.BlockSpec((nvar, 1, NA_QB, NA_KB * NA_QB), lambda h: (0, h, 0, 0)),
        out_shape=jax.ShapeDtypeStruct((nvar, heads, NA_QB, NA_KB * NA_QB), f32),
        compiler_params=_params(("parallel",)),
        name="na_bias",
    )(pairs)


def _na_step_window_start(i, nb):
    return jnp.clip(NA_STEP * i - 2, 0, nb - NA_KB - NA_STEP + 1)


def _na_kernel(q_ref, kt_ref, v_ref, *rest, nb):
    tab_refs = rest[:NA_STEP]
    g_ref, o_ref, s_scr, p_scr = rest[NA_STEP:]
    i = pl.program_id(0)
    win0 = _na_step_window_start(i, nb)
    starts = []
    for jb in range(NA_STEP):
        local = _na_window_start(NA_STEP * i + jb, nb) - win0
        starts.append(pl.multiple_of(local * NA_QB, NA_QB))
        for h in range(NA_HEADS):
            sl = slice(h * HEAD_DIM, (h + 1) * HEAD_DIM)
            kt = kt_ref[sl, pl.ds(starts[jb], NA_KB * NA_QB)]
            s_scr[jb, h] = jnp.dot(q_ref[jb * NA_QB:(jb + 1) * NA_QB, sl], kt,
                                   preferred_element_type=f32) + tab_refs[jb][0, h]
    inv = []
    for jb in range(NA_STEP):
        for h in range(NA_HEADS):
            s = s_scr[jb, h]
            p = jnp.exp(s - jnp.max(s, axis=-1, keepdims=True))
            inv.append(1.0 / jnp.sum(p, axis=-1, keepdims=True))
            p_scr[jb, h] = p.astype(bf16)
    for jb in range(NA_STEP):
        outs = []
        for h in range(NA_HEADS):
            sl = slice(h * HEAD_DIM, (h + 1) * HEAD_DIM)
            v = v_ref[pl.ds(starts[jb], NA_KB * NA_QB), sl]
            outs.append(jnp.dot(p_scr[jb, h], v, preferred_element_type=f32) * inv[jb * NA_HEADS + h])
        o_ref[jb * NA_QB:(jb + 1) * NA_QB, :] = _rms(jnp.concatenate(outs, axis=1), g_ref[...]).astype(bf16)


def _na_attn(P, KT, tab, g):
    S = P.shape[0]
    nb = S // NA_QB
    win = (NA_KB + NA_STEP - 1) * NA_QB

    def variant(b):
        return jnp.where(b < NA_EDGE, b, jnp.where(b >= nb - NA_EDGE, b - (nb - 2 * NA_EDGE - 1), NA_EDGE))

    v_window = pl.BlockSpec((pl.Element(win), pl.Element(NA_WIDTH)),
                            lambda i: (_na_step_window_start(i, nb) * NA_QB, NA_WIDTH))
    kt_window = pl.BlockSpec((pl.Element(NA_WIDTH), pl.Element(win)),
                             lambda i: (0, _na_step_window_start(i, nb) * NA_QB))

    tab_specs = [pl.BlockSpec((1, NA_HEADS, NA_QB, NA_KB * NA_QB),
                              lambda i, jb=jb: (variant(NA_STEP * i + jb), 0, 0, 0)) for jb in range(NA_STEP)]
    return pl.pallas_call(
        functools.partial(_na_kernel, nb=nb),
        grid=(nb // NA_STEP,),
        in_specs=[pl.BlockSpec((NA_STEP * NA_QB, NA_WIDTH), lambda i: (i, 0)),
                  kt_window, v_window] + tab_specs + [
                  pl.BlockSpec((1, NA_WIDTH), lambda i: (0, 0))],
        out_specs=pl.BlockSpec((NA_STEP * NA_QB, NA_WIDTH), lambda i: (i, 0)),
        out_shape=jax.ShapeDtypeStruct((S, NA_WIDTH), bf16),
        scratch_shapes=[pltpu.VMEM((NA_STEP, NA_HEADS, NA_QB, NA_KB * NA_QB), f32),
                        pltpu.VMEM((NA_STEP, NA_HEADS, NA_QB, NA_KB * NA_QB), bf16)],
        compiler_params=_params(("parallel",)),
        name="na_attn",
    )(P, KT, P, *([tab] * NA_STEP), g)


def _memkv_kernel(m_ref, g_ref, w_ref, o_ref):
    o_ref[...] = jnp.dot(_rms(m_ref[...], g_ref[...]).astype(bf16), w_ref[...].astype(bf16),
                         preferred_element_type=f32).astype(bf16)


def _mem_kv(mem2, g, w):
    return pl.pallas_call(
        _memkv_kernel,
        out_shape=jax.ShapeDtypeStruct((N_MEM, 2 * MEM_WIDTH), bf16),
        compiler_params=pltpu.CompilerParams(vmem_limit_bytes=VMEM_LIMIT),
        name="mem_kv",
    )(mem2, g, w)


def _ssm_prep_kernel(lre_r, lim_r, ldt_r, bre_ref, bim_ref, cre_ref, cim_ref, win_ref,
                     wtoep_ref, wstate_ref, wcarry_ref, apow_ref, win_bf_ref, *, seg_len):
    win_bf_ref[...] = win_ref[...].astype(bf16)
    T = SSM_T
    TS = TILE_STATES
    reps = LANES // SSM_CH
    assert T == SUBLANES

    def same_group(shape, row_div, col_div):
        r = lax.broadcasted_iota(jnp.int32, shape, 0) // row_div
        c = lax.broadcasted_iota(jnp.int32, shape, 1) // col_div
        return (r == c).astype(f32)

    mask_state = same_group((LANES, TS), SSM_CH, SSM_STATE)
    mask_chan = same_group((LANES, LANES), SSM_CH, SSM_CH)

    kk = []
    for d in range(2):
        lre = jnp.minimum(lre_r[d, 0], LAM_RE_MAX)
        lim = lim_r[d, 0]
        dt = jnp.exp(ldt_r[d, 0])
        zr = lre * dt
        zi = lim * dt
        er = jnp.exp(zr)
        nr = er * jnp.cos(zi) - 1.0
        ni = er * jnp.sin(zi)
        den = lre * lre + lim * lim
        qr = (nr * lre + ni * lim) / den
        qi = (ni * lre - nr * lim) / den
        bre = bre_ref[d, 0]
        bim = bim_ref[d, 0]
        bbr = qr * bre - qi * bim
        bbi = qr * bim + qi * bre
        e0 = lax.broadcasted_iota(jnp.int32, (T, TS), 0).astype(f32)
        mag0 = jnp.exp(zr * e0)
        p0r, p0i = mag0 * jnp.cos(zi * e0), mag0 * jnp.sin(zi * e0)
        mag1 = jnp.exp(zr * (e0 + 1.0))
        p1r, p1i = mag1 * jnp.cos(zi * (e0 + 1.0)), mag1 * jnp.sin(zi * (e0 + 1.0))
        wc_r = [p0r[e:e + 1] * bbr - p0i[e:e + 1] * bbi for e in range(T)]
        wc_i = [p0r[e:e + 1] * bbi + p0i[e:e + 1] * bbr for e in range(T)]

        for s in range(T):
            e = T - 1 - s if d == 0 else s
            rows = slice(s * LANES, (s + 1) * LANES)
            base = d * 2 * TS
            wstate_ref[0, rows, base:base + TS] = (
                jnp.concatenate([wc_r[e]] * reps, axis=0) * mask_state).astype(bf16)
            wstate_ref[0, rows, base + TS:base + 2 * TS] = (
                jnp.concatenate([wc_i[e]] * reps, axis=0) * mask_state).astype(bf16)

        cre = cre_ref[d, 0]
        cim = cim_ref[d, 0]
        nt = (((1,), (1,)), ((), ()))
        kc = (lax.dot_general(jnp.concatenate(wc_r, axis=0), jnp.concatenate([cre] * reps, axis=0) * mask_state,
                              nt, precision=lax.Precision.HIGHEST, preferred_element_type=f32)
              - lax.dot_general(jnp.concatenate(wc_i, axis=0), jnp.concatenate([cim] * reps, axis=0) * mask_state,
                                nt, precision=lax.Precision.HIGHEST, preferred_element_type=f32))
        kk.append([jnp.concatenate([kc[e * SSM_CH:(e + 1) * SSM_CH]] * reps, axis=0) * mask_chan
                   for e in range(T)])

        for t in range(T):
            e = t if d == 0 else T - 1 - t
            pr, pi = p1r[e:e + 1], p1i[e:e + 1]
            rows = slice(t * LANES, (t + 1) * LANES)
            base = d * 2 * TS
            wcarry_ref[0, rows, base:base + TS] = (
                jnp.concatenate([cre * pr - cim * pi] * reps, axis=0) * mask_state).astype(bf16)
            wcarry_ref[0, rows, base + TS:base + 2 * TS] = (
                jnp.concatenate([-(cre * pi + cim * pr)] * reps, axis=0) * mask_state).astype(bf16)

        for part, steps in enumerate((T, T * seg_len)):
            k1 = (lax.broadcasted_iota(jnp.int32, (SUBLANES, TS), 0) + 1).astype(f32) * float(steps)
            mag = jnp.exp(zr * k1)
            rows = slice(part * SUBLANES, (part + 1) * SUBLANES)
            apow_ref[0, rows, d * 2 * TS:d * 2 * TS + TS] = mag * jnp.cos(zi * k1)
            apow_ref[0, rows, d * 2 * TS + TS:(d + 1) * 2 * TS] = mag * jnp.sin(zi * k1)

    for s in range(T):
        for t in range(T):
            if t > s:
                blk = kk[0][t - s]
            elif t < s:
                blk = kk[1][s - t]
            else:
                blk = kk[0][0] + kk[1][0]
            wtoep_ref[0, s * LANES:(s + 1) * LANES, t * LANES:(t + 1) * LANES] = blk.astype(bf16)


def _ssm_prep(lam_re, lam_im, log_dt, b_re, b_im, c_re, c_im, S, w_in):
    TS, NT, T = TILE_STATES, N_TILES, SSM_T
    gp = GROUPS_PER_TILE
    ldt = jnp.broadcast_to(log_dt[:, :, None], lam_re.shape)
    rows = [a.reshape(2, NT, 1, TS) for a in (lam_re, lam_im, ldt)]

    def b_layout(b):
        return b.reshape(2, NT, gp, SSM_STATE, SSM_CH).transpose(0, 1, 4, 2, 3).reshape(2, NT, SSM_CH, TS)

    def c_layout(c):
        return c.reshape(2, NT, gp, SSM_CH, SSM_STATE).transpose(0, 1, 3, 2, 4).reshape(2, NT, SSM_CH, TS)

    row_spec = pl.BlockSpec((2, 1, 1, TS), lambda g: (0, g, 0, 0))
    bc_spec = pl.BlockSpec((2, 1, SSM_CH, TS), lambda g: (0, g, 0, 0))
    win_spec = pl.BlockSpec((w_in.shape[0] // NT, w_in.shape[1]), lambda g: (g, 0))
    return pl.pallas_call(
        functools.partial(_ssm_prep_kernel, seg_len=S // T // SUBLANES),
        grid=(NT,),
        in_specs=[row_spec] * 3 + [bc_spec] * 4 + [win_spec],
        out_specs=[pl.BlockSpec((1, T * LANES, T * LANES), lambda g: (g, 0, 0)),
                   pl.BlockSpec((1, T * LANES, STATE_COLS), lambda g: (g, 0, 0)),
                   pl.BlockSpec((1, T * LANES, STATE_COLS), lambda g: (g, 0, 0)),
                   pl.BlockSpec((1, 2 * SUBLANES, STATE_COLS), lambda g: (g, 0, 0)), win_spec],
        out_shape=[jax.ShapeDtypeStruct((NT, T * LANES, T * LANES), bf16),
                   jax.ShapeDtypeStruct((NT, T * LANES, STATE_COLS), bf16),
                   jax.ShapeDtypeStruct((NT, T * LANES, STATE_COLS), bf16),
                   jax.ShapeDtypeStruct((NT, 2 * SUBLANES, STATE_COLS), f32),
                   jax.ShapeDtypeStruct(w_in.shape, bf16)],
        compiler_params=_params(("parallel",)),
        name="ssm_prep",
    )(*rows, b_layout(b_re), b_layout(b_im), c_layout(c_re), c_layout(c_im), w_in)


def _cmul_add(xr, xi, ar, ai, sr, si):
    return xr + ar * sr - ai * si, xi + ar * si + ai * sr


def _scan_rows(xr, xi, apr, api, cr, ci, reverse):
    n = SUBLANES
    rows = lax.broadcasted_iota(jnp.int32, xr.shape, 0)
    for k in (1, 2, 4):
        ar, ai = apr[k - 1:k], api[k - 1:k]
        if reverse:
            sr, si = pltpu.roll(xr, n - k, 0), pltpu.roll(xi, n - k, 0)
            keep = rows < n - k
        else:
            sr, si = pltpu.roll(xr, k, 0), pltpu.roll(xi, k, 0)
            keep = rows >= k
        xr, xi = _cmul_add(xr, xi, ar, ai, jnp.where(keep, sr, 0.0), jnp.where(keep, si, 0.0))
    if reverse:
        pwr = jnp.concatenate([apr[n - 1 - j:n - j] for j in range(n)], axis=0)
        pwi = jnp.concatenate([api[n - 1 - j:n - j] for j in range(n)], axis=0)
    else:
        pwr, pwi = apr, api
    hr, hi = _cmul_add(xr, xi, pwr, pwi, cr, ci)
    if reverse:
        inr = jnp.where(rows == n - 1, cr, pltpu.roll(hr, n - 1, 0))
        ini = jnp.where(rows == n - 1, ci, pltpu.roll(hi, n - 1, 0))
        return inr, ini, hr[0:1], hi[0:1]
    inr = jnp.where(rows == 0, cr, pltpu.roll(hr, 1, 0))
    ini = jnp.where(rows == 0, ci, pltpu.roll(hi, 1, 0))
    return inr, ini, hr[n - 1:n], hi[n - 1:n]


def _ssm_kernel(u_ref, d_ref, wtoep_ref, wstate_ref, wcarry_ref, apow_ref, o_ref, h_scr, *, nb, cb):
    T = SSM_T
    TS = TILE_STATES
    j = pl.program_id(1)

    nseg = SUBLANES
    seg = nb * cb // nseg
    assert cb % seg == 0

    def chunk_rows():
        return [u_ref[pl.ds(t, cb, stride=T), :] for t in range(T)]

    def batch_rows(b):
        return [pl.ds(b * (cb // seg) + part, seg, stride=nseg) for part in range(cb // seg)]

    @pl.when(j < nb)
    def _():
        x = jnp.concatenate([ut.astype(bf16) for ut in chunk_rows()], axis=1)
        res = jnp.dot(x, wstate_ref[0], preferred_element_type=f32)
        for part, rows in enumerate(batch_rows(j)):
            for c in range(STATE_COLS // LANES):
                h_scr[c, rows, :] = res[part * seg:(part + 1) * seg, c * LANES:(c + 1) * LANES]

    @pl.when(j == nb - 1)
    def _():
        tiles = TS // LANES

        def cols(d, ri):
            return slice((2 * d + ri) * TS, (2 * d + ri + 1) * TS)

        def rows_at(p):
            return pl.ds(pl.multiple_of(p * nseg, nseg), nseg)

        def load(rows, d, ri):
            return jnp.concatenate([h_scr[(2 * d + ri) * tiles + c, rows, :] for c in range(tiles)], axis=1)

        def store(rows, d, ri, val):
            for c in range(tiles):
                h_scr[(2 * d + ri) * tiles + c, rows, :] = val[:, c * LANES:(c + 1) * LANES]

        a = [[jnp.broadcast_to(apow_ref[0, 0:1, cols(d, ri)], (nseg, TS)) for ri in range(2)] for d in range(2)]
        zero = jnp.zeros((nseg, TS), f32)

        def step(p, h, write):
            hfr, hfi, hbr, hbi = h
            rf, rb = rows_at(p), rows_at(seg - 1 - p)
            xfr, xfi, xbr, xbi = load(rf, 0, 0), load(rf, 0, 1), load(rb, 1, 0), load(rb, 1, 1)
            if write:
                store(rf, 0, 0, hfr)
                store(rf, 0, 1, hfi)
                store(rb, 1, 0, hbr)
                store(rb, 1, 1, hbi)
            hfr, hfi = _cmul_add(xfr, xfi, a[0][0], a[0][1], hfr, hfi)
            hbr, hbi = _cmul_add(xbr, xbi, a[1][0], a[1][1], hbr, hbi)
            return hfr, hfi, hbr, hbi

        ends = lax.fori_loop(0, seg, lambda p, h: step(p, h, False), (zero, zero, zero, zero))
        z1 = jnp.zeros((1, TS), f32)
        enter = []
        for d in range(2):
            cr, ci, _, _ = _scan_rows(ends[2 * d], ends[2 * d + 1], apow_ref[0, nseg:2 * nseg, cols(d, 0)],
                                      apow_ref[0, nseg:2 * nseg, cols(d, 1)], z1, z1, d == 1)
            enter += [cr, ci]
        lax.fori_loop(0, seg, lambda p, h: step(p, h, True), tuple(enter))

    @pl.when(j >= nb)
    def _():
        us = chunk_rows()
        x = jnp.concatenate([ut.astype(bf16) for ut in us], axis=1)
        hc = jnp.concatenate(
            [jnp.concatenate([h_scr[c, rows, :].astype(bf16) for c in range(STATE_COLS // LANES)], axis=1)
             for rows in batch_rows(j - nb)], axis=0)
        y = (jnp.dot(x, wtoep_ref[0], preferred_element_type=f32)
             + lax.dot_general(hc, wcarry_ref[0], (((1,), (1,)), ((), ())), preferred_element_type=f32))
        for t in range(T):
            yt = y[:, t * LANES:(t + 1) * LANES] + d_ref[...] * us[t]
            o_ref[pl.ds(t, cb, stride=T), :] = jax.nn.gelu(yt)


def _ssm_main(u, d_row, wtoep, wstate, wcarry, apow, tb=4096):
    S = u.shape[0]
    T = SSM_T
    nb = S // tb
    cb = tb // T
    return pl.pallas_call(
        functools.partial(_ssm_kernel, nb=nb, cb=cb),
        grid=(N_TILES, 2 * nb),
        in_specs=[pl.BlockSpec((tb, LANES), lambda g, j: (j % nb, g)),
                  pl.BlockSpec((1, LANES), lambda g, j: (0, g)),
                  pl.BlockSpec((1, T * LANES, T * LANES), lambda g, j: (g, 0, 0)),
                  pl.BlockSpec((1, T * LANES, STATE_COLS), lambda g, j: (g, 0, 0)),
                  pl.BlockSpec((1, T * LANES, STATE_COLS), lambda g, j: (g, 0, 0)),
                  pl.BlockSpec((1, 2 * SUBLANES, STATE_COLS), lambda g, j: (g, 0, 0))],
        out_specs=pl.BlockSpec((tb, LANES), lambda g, j: (jnp.maximum(j - nb, 0), g)),
        out_shape=jax.ShapeDtypeStruct((S, SSM_WIDTH), f32),
        scratch_shapes=[pltpu.VMEM((STATE_COLS // LANES, S // T, LANES), f32)],
        compiler_params=_params(("arbitrary", "arbitrary")),
        name="ssm_main",
    )(u, d_row, wtoep, wstate, wcarry, apow)


def _outproj_kernel(yna_ref, yssm_ref, ymem_ref, x_ref, wglu_ref, bglu_ref, gssm_ref, wout_ref,
                    gpost_ref, gmlp_ref, x1_ref, h2_ref, *, nsub):
    a, b = NA_WIDTH, NA_WIDTH + SSM_WIDTH
    sub = x_ref.shape[0] // nsub
    tiles = [slice(r * sub, (r + 1) * sub) for r in range(nsub)]
    ys = []
    for rows in tiles:
        ya = yssm_ref[rows, :]
        gate = jax.nn.sigmoid(jnp.dot(ya.astype(bf16), wglu_ref[...], preferred_element_type=f32)
                              + bglu_ref[...])
        ys.append(_rms(ya * gate, gssm_ref[...]).astype(bf16))
    accs = [jnp.dot(yna_ref[rows, :], wout_ref[0:a, :], preferred_element_type=f32)
            + jnp.dot(ys[r], wout_ref[a:b, :], preferred_element_type=f32)
            + jnp.dot(ymem_ref[rows, :], wout_ref[b:, :], preferred_element_type=f32)
            for r, rows in enumerate(tiles)]
    for rows, acc in zip(tiles, accs):
        x1 = x_ref[rows, :] + _rms(acc, gpost_ref[...])
        x1_ref[rows, :] = x1
        h2_ref[rows, :] = _rms(x1, gmlp_ref[...]).astype(bf16)


def _out_proj(y_na, y_ssm, y_mem, x2, wglu_bf, b_glu, g_ssm, wout_bf, g_post, g_mlp, tm=512):
    S = x2.shape[0]
    row = lambda w: pl.BlockSpec((tm, w), lambda i: (i, 0))
    vec = lambda w: pl.BlockSpec((1, w), lambda i: (0, 0))
    return pl.pallas_call(
        functools.partial(_outproj_kernel, nsub=4),
        grid=(S // tm,),
        in_specs=[row(NA_WIDTH), row(SSM_WIDTH), row(MEM_WIDTH), row(D_MODEL),
                  pl.BlockSpec((SSM_WIDTH, SSM_WIDTH), lambda i: (0, 0)), vec(SSM_WIDTH), vec(SSM_WIDTH),
                  pl.BlockSpec((D_MODEL, D_MODEL), lambda i: (0, 0), pipeline_mode=pl.Buffered(1)),
                  vec(D_MODEL), vec(D_MODEL)],
        out_specs=[row(D_MODEL), row(D_MODEL)],
        out_shape=[jax.ShapeDtypeStruct((S, D_MODEL), f32), jax.ShapeDtypeStruct((S, D_MODEL), bf16)],
        compiler_params=_params(("parallel",)),
        name="out_proj",
    )(y_na, y_ssm, y_mem, x2, wglu_bf, b_glu, g_ssm, wout_bf, g_post, g_mlp)


def _mlp_kernel(h_ref, w1_ref, w2_ref, x1_hbm, g_ref, o_ref, hid_scr, x1_buf, x1_sem, *, nsplit):
    i = pl.program_id(0)
    k = pl.program_id(1)
    nk = pl.num_programs(1) - 1
    tm = o_ref.shape[0]
    wn = D_MODEL // nsplit

    def x1_copy():
        return pltpu.make_async_copy(x1_hbm.at[pl.ds(pl.multiple_of(i * tm, tm), tm), :], x1_buf, x1_sem)

    def hidden():
        hid = jnp.dot(h_ref[...], w1_ref[...], preferred_element_type=f32)
        return jnp.square(jnp.maximum(hid, 0.0)).astype(bf16)

    def partial_out(n):
        cols = slice(n * wn, (n + 1) * wn)
        return o_ref[:, cols] + jnp.dot(hid_scr[(k + 1) % 2], w2_ref[:, cols], preferred_element_type=f32)

    @pl.when(k == 0)
    def _():
        x1_copy().start()
        o_ref[...] = jnp.zeros_like(o_ref)
        hid_scr[0] = hidden()

    @pl.when((k > 0) & (k < nk))
    def _():
        for n in range(nsplit):
            o_ref[:, n * wn:(n + 1) * wn] = partial_out(n)
        hid_scr[k % 2] = hidden()

    @pl.when(k == nk)
    def _():
        ssq = jnp.zeros((tm, 1), f32)
        for n in range(nsplit):
            f = partial_out(n)
            o_ref[:, n * wn:(n + 1) * wn] = f
            ssq = ssq + jnp.sum(f * f, axis=-1, keepdims=True)
        scale = lax.rsqrt(ssq * (1.0 / D_MODEL) + EPS)
        x1_copy().wait()
        o_ref[...] = x1_buf[...] + o_ref[...] * scale * g_ref[...]


def _mlp(h2, w1_bf, w2_bf, x1, g, tm=1024, tk=1024):
    S = h2.shape[0]
    nk = D_FF // tk
    return pl.pallas_call(
        functools.partial(_mlp_kernel, nsplit=4),
        grid=(S // tm, nk + 1),
        in_specs=[pl.BlockSpec((tm, D_MODEL), lambda i, k: (i, 0)),
                  pl.BlockSpec((D_MODEL, tk), lambda i, k: (0, jnp.minimum(k, nk - 1))),
                  pl.BlockSpec((tk, D_MODEL), lambda i, k: (jnp.maximum(k - 1, 0), 0)),
                  pl.BlockSpec(memory_space=pl.ANY),
                  pl.BlockSpec((1, D_MODEL), lambda i, k: (0, 0))],
        out_specs=pl.BlockSpec((tm, D_MODEL), lambda i, k: (i, 0)),
        out_shape=jax.ShapeDtypeStruct((S, D_MODEL), f32),
        scratch_shapes=[pltpu.VMEM((2, tm, tk), bf16), pltpu.VMEM((tm, D_MODEL), f32),
                        pltpu.SemaphoreType.DMA(())],
        compiler_params=_params(("arbitrary", "arbitrary")),
        name="mlp",
    )(h2, w1_bf, w2_bf, x1, g)


def kernel(x, mem, norm_mix_pre, w_in, na_rpb, ssm_lam_re, ssm_lam_im, ssm_log_dt, ssm_b_re, ssm_b_im, ssm_c_re, ssm_c_im, ssm_d, w_glu, b_glu, mem_norm, w_mem_kv, out_norm_na, out_norm_ssm, out_norm_mem, w_out, norm_mix_post, norm_mlp_pre, w_ff1, w_ff2, norm_mlp_post):
    B, S, _ = x.shape
    assert B == 1 and x.shape[2] == D_MODEL and S % 4096 == 0 and w_in.shape[0] == 1
    x2 = x[0]
    l = 0
    vec = lambda a: a[l].reshape(1, -1)

    kv = _mem_kv(mem[0], vec(mem_norm), w_mem_kv[l])
    wtoep, wstate, wcarry, apow, win_bf = _ssm_prep(ssm_lam_re[l], ssm_lam_im[l], ssm_log_dt[l], ssm_b_re[l],
                                                    ssm_b_im[l], ssm_c_re[l], ssm_c_im[l], S, w_in[l])
    P, KT, u, y_mem, wglu_bf, wout_bf, w1_bf, w2_bf = _proj(
        x2, vec(norm_mix_pre), win_bf, kv, vec(out_norm_mem), [w_glu[l], w_out[l], w_ff1[l], w_ff2[l]])

    y_na = _na_attn(P, KT, _na_bias(na_rpb[l], S), vec(out_norm_na))

    y_ssm = _ssm_main(u, ssm_d[l].reshape(1, SSM_WIDTH), wtoep, wstate, wcarry, apow)

    x1, h2 = _out_proj(y_na, y_ssm, y_mem, x2, wglu_bf, vec(b_glu), vec(out_norm_ssm),
                       wout_bf, vec(norm_mix_post), vec(norm_mlp_pre))
    out = _mlp(h2, w1_bf, w2_bf, x1, vec(norm_mlp_post))
    return out[None]
```

```python
import functools

import jax
import jax.numpy as jnp
from jax import lax
from jax.experimental import pallas as pl
from jax.experimental.pallas import tpu as pltpu

f32 = jnp.float32
bf16 = jnp.bfloat16

D_MODEL = 2048
GRID_W = 64
WIN_H = 8
WIN_W = 16
HEAD_DIM = 128
NA_WIDTH = 1024
NA_HEADS = 8
SSM_WIDTH = 512
SSM_CH = 16
SSM_GROUPS = 32
SSM_STATE = 64
MEM_WIDTH = 512
MEM_HEADS = 4
N_MEM = 256
IN_WIDTH = 4096
D_FF = 8192
EPS = 1e-6
LAM_RE_MAX = -1e-4

LANES = 128
SUBLANES = 8
GROUPS_PER_TILE = LANES // SSM_CH
N_TILES = SSM_WIDTH // LANES
TILE_STATES = GROUPS_PER_TILE * SSM_STATE
STATE_COLS = 4 * TILE_STATES
SSM_T = 8
MASK_VALUE = -1e30
LOG2E = 1.4426950408889634
VMEM_LIMIT = 58 * 1024 * 1024


def _rms(x, g):
    ms = jnp.mean(x * x, axis=-1, keepdims=True)
    return x * lax.rsqrt(ms + EPS) * g


def _params(sem):
    return pltpu.CompilerParams(dimension_semantics=sem, vmem_limit_bytes=VMEM_LIMIT)


def _softmax_pv(s, v):
    p = jnp.exp2(s - jnp.max(s, axis=-1, keepdims=True))
    inv = 1.0 / jnp.sum(p, axis=-1, keepdims=True)
    return jnp.dot(p.astype(bf16), v, preferred_element_type=f32) * inv


def _proj_kernel(x_ref, g_ref, w_ref, kv_ref, gmem_ref, *rest, ncast, nsub, scale):
    cast_in, rest = rest[:ncast], rest[ncast:]
    p_ref, kt_ref, u_ref, ymem_ref = rest[:4]
    cast_out, h_scr = rest[4:4 + ncast], rest[4 + ncast]
    for src, dst in zip(cast_in, cast_out):
        dst[...] = src[...].astype(bf16)
    nt = (((1,), (1,)), ((), ()))
    sub = x_ref.shape[0] // nsub
    tiles = [slice(r * sub, (r + 1) * sub) for r in range(nsub)]
    for rows in tiles:
        h_scr[rows, :] = _rms(x_ref[rows, :], g_ref[...]).astype(bf16)
    for rows in tiles:
        acc = jnp.dot(h_scr[rows, :], w_ref[:, 3 * NA_WIDTH:], preferred_element_type=f32)
        u_ref[rows, :] = acc[:, :SSM_WIDTH]
        qm = (acc[:, SSM_WIDTH:] * scale).astype(bf16)
        outs = []
        for h in range(MEM_HEADS):
            sl = slice(h * HEAD_DIM, (h + 1) * HEAD_DIM)
            s = lax.dot_general(qm[:, sl], kv_ref[:, sl], nt, preferred_element_type=f32)
            outs.append(_softmax_pv(s, kv_ref[:, MEM_WIDTH + h * HEAD_DIM:MEM_WIDTH + (h + 1) * HEAD_DIM]))
        ymem_ref[rows, :] = _rms(jnp.concatenate(outs, axis=1), gmem_ref[...]).astype(bf16)

        for j in range(3):
            acc = jnp.dot(h_scr[rows, :], w_ref[:, j * NA_WIDTH:(j + 1) * NA_WIDTH], preferred_element_type=f32)
            if j == 0:
                p_ref[rows, :NA_WIDTH] = (acc * scale).astype(bf16)
            elif j == 1:
                kt_ref[:, rows] = acc.T.astype(bf16)
            else:
                p_ref[rows, NA_WIDTH:] = acc.astype(bf16)


def _proj(x2, g, w_bf, kv, g_mem, cast_weights, tm=512):
    S = x2.shape[0]
    steps = S // tm
    cast_specs = [pl.BlockSpec((w.shape[0] // steps, w.shape[1]), lambda i: (i, 0)) for w in cast_weights]
    return pl.pallas_call(
        functools.partial(_proj_kernel, ncast=len(cast_weights), nsub=1, scale=HEAD_DIM ** -0.5 * LOG2E),
        grid=(steps,),
        in_specs=[pl.BlockSpec((tm, D_MODEL), lambda i: (i, 0)),
                  pl.BlockSpec((1, D_MODEL), lambda i: (0, 0)),
                  pl.BlockSpec((D_MODEL, IN_WIDTH), lambda i: (0, 0), pipeline_mode=pl.Buffered(1)),
                  pl.BlockSpec((N_MEM, 2 * MEM_WIDTH), lambda i: (0, 0)),
                  pl.BlockSpec((1, MEM_WIDTH), lambda i: (0, 0))] + cast_specs,
        out_specs=[pl.BlockSpec((tm, 2 * NA_WIDTH), lambda i: (i, 0)),
                   pl.BlockSpec((NA_WIDTH, tm), lambda i: (0, i)),
                   pl.BlockSpec((tm, SSM_WIDTH), lambda i: (i, 0)),
                   pl.BlockSpec((tm, MEM_WIDTH), lambda i: (i, 0))] + cast_specs,
        out_shape=[jax.ShapeDtypeStruct((S, 2 * NA_WIDTH), bf16),
                   jax.ShapeDtypeStruct((NA_WIDTH, S), bf16),
                   jax.ShapeDtypeStruct((S, SSM_WIDTH), f32),
                   jax.ShapeDtypeStruct((S, MEM_WIDTH), bf16)]
                  + [jax.ShapeDtypeStruct(w.shape, bf16) for w in cast_weights],
        scratch_shapes=[pltpu.VMEM((tm, D_MODEL), bf16)],
        compiler_params=_params(("parallel",)),
        name="proj",
    )(x2, g, w_bf, kv, g_mem, *cast_weights)


NA_QB = 128
NA_KB = 5
NA_EDGE = 2
NA_STEP = 4
NA_ROWS_Q = NA_QB // GRID_W
NA_ROWS_WIN = NA_KB * NA_QB // GRID_W


def _na_window_start(b, nb):
    return jnp.clip(b - 2, 0, nb - NA_KB)


def _na_bias_kernel(v_ref, o_ref, *, nb, rows):
    c = lax.broadcasted_iota(jnp.int32, (GRID_W, LANES), 0)
    lane = lax.broadcasted_iota(jnp.int32, (GRID_W, LANES), 1)
    kc = lane % GRID_W
    cs = jnp.clip(c - WIN_W // 2, 0, GRID_W - WIN_W)
    col_ok = (kc >= cs) & (kc < cs + WIN_W)
    half_ok = {(True, True): col_ok, (True, False): col_ok & (lane < GRID_W), (False, True): col_ok & (lane >= GRID_W)}
    masked = jnp.full((GRID_W, LANES), MASK_VALUE, f32)
    variants = list(range(NA_EDGE)) + [NA_EDGE] + list(range(nb - NA_EDGE, nb))
    for vi, b in enumerate(variants):
        wb = min(max(b - 2, 0), nb - NA_KB)
        for rq in range(NA_ROWS_Q):
            r = b * NA_ROWS_Q + rq
            rs = min(max(r - WIN_H // 2, 0), rows - WIN_H)
            for m in range(NA_ROWS_WIN // 2):
                kr = wb * NA_ROWS_Q + 2 * m
                ok = (rs <= kr < rs + WIN_H, rs <= kr + 1 < rs + WIN_H)
                if ok == (False, False):
                    blk = masked
                else:
                    d = kr - r + WIN_H
                    src = jnp.broadcast_to(v_ref[0, d:d + 1, :], (GRID_W, LANES))
                    rolled = pltpu.roll(src, LANES - (WIN_W - 1), 1, stride=1, stride_axis=0)
                    blk = jnp.where(half_ok[ok], rolled * LOG2E, MASK_VALUE)
                o_ref[vi, 0, rq * GRID_W:(rq + 1) * GRID_W, m * LANES:(m + 1) * LANES] = blk


def _na_bias(rpb, S):
    heads, nrow, ncol = rpb.shape
    nb = S // NA_QB
    nvar = 2 * NA_EDGE + 1
    padded = jnp.zeros((heads, nrow + 2, GRID_W), f32).at[:, 1:nrow + 1, :ncol].set(rpb.astype(f32))
    pairs = jnp.concatenate([padded[:, :-1], padded[:, 1:]], axis=-1)
    return pl.pallas_call(
        functools.partial(_na_bias_kernel, nb=nb, rows=S // GRID_W),
        grid=(heads,),
        in_specs=[pl.BlockSpec((1, nrow + 1, LANES), lambda h: (h, 0, 0))],
        out_specs=pl.BlockSpec((nvar, 1, NA_QB, NA_KB * NA_QB), lambda h: (0, h, 0, 0)),
        out_shape=jax.ShapeDtypeStruct((nvar, heads, NA_QB, NA_KB * NA_QB), f32),
        compiler_params=_params(("parallel",)),
        name="na_bias",
    )(pairs)


def _na_step_window_start(i, nb):
    return jnp.clip(NA_STEP * i - 2, 0, nb - NA_KB - NA_STEP + 1)


def _na_kernel(q_ref, kt_ref, v_ref, *rest, nb):
    tab_refs = rest[:NA_STEP]
    g_ref, o_ref, s_scr, p_scr = rest[NA_STEP:]
    i = pl.program_id(0)
    win0 = _na_step_window_start(i, nb)
    starts = []
    for jb in range(NA_STEP):
        local = _na_window_start(NA_STEP * i + jb, nb) - win0
        starts.append(pl.multiple_of(local * NA_QB, NA_QB))
        for h in range(NA_HEADS):
            sl = slice(h * HEAD_DIM, (h + 1) * HEAD_DIM)
            kt = kt_ref[sl, pl.ds(starts[jb], NA_KB * NA_QB)]
            s_scr[jb, h] = jnp.dot(q_ref[jb * NA_QB:(jb + 1) * NA_QB, sl], kt,
                                   preferred_element_type=f32) + tab_refs[jb][0, h]
    inv = []
    for jb in range(NA_STEP):
        for h in range(NA_HEADS):
            s = s_scr[jb, h]
            p = jnp.exp2(s - jnp.max(s, axis=-1, keepdims=True))
            inv.append(1.0 / jnp.sum(p, axis=-1, keepdims=True))
            p_scr[jb, h] = p.astype(bf16)
    for jb in range(NA_STEP):
        outs = []
        for h in range(NA_HEADS):
            sl = slice(h * HEAD_DIM, (h + 1) * HEAD_DIM)
            v = v_ref[pl.ds(starts[jb], NA_KB * NA_QB), sl]
            outs.append(jnp.dot(p_scr[jb, h], v, preferred_element_type=f32) * inv[jb * NA_HEADS + h])
        o_ref[jb * NA_QB:(jb + 1) * NA_QB, :] = _rms(jnp.concatenate(outs, axis=1), g_ref[...]).astype(bf16)


def _na_attn(P, KT, tab, g):
    S = P.shape[0]
    nb = S // NA_QB
    win = (NA_KB + NA_STEP - 1) * NA_QB

    def variant(b):
        return jnp.where(b < NA_EDGE, b, jnp.where(b >= nb - NA_EDGE, b - (nb - 2 * NA_EDGE - 1), NA_EDGE))

    v_window = pl.BlockSpec((pl.Element(win), pl.Element(NA_WIDTH)),
                            lambda i: (_na_step_window_start(i, nb) * NA_QB, NA_WIDTH))
    kt_window = pl.BlockSpec((pl.Element(NA_WIDTH), pl.Element(win)),
                             lambda i: (0, _na_step_window_start(i, nb) * NA_QB))

    tab_specs = [pl.BlockSpec((1, NA_HEADS, NA_QB, NA_KB * NA_QB),
                              lambda i, jb=jb: (variant(NA_STEP * i + jb), 0, 0, 0)) for jb in range(NA_STEP)]
    return pl.pallas_call(
        functools.partial(_na_kernel, nb=nb),
        grid=(nb // NA_STEP,),
        in_specs=[pl.BlockSpec((NA_STEP * NA_QB, NA_WIDTH), lambda i: (i, 0)),
                  kt_window, v_window] + tab_specs + [
                  pl.BlockSpec((1, NA_WIDTH), lambda i: (0, 0))],
        out_specs=pl.BlockSpec((NA_STEP * NA_QB, NA_WIDTH), lambda i: (i, 0)),
        out_shape=jax.ShapeDtypeStruct((S, NA_WIDTH), bf16),
        scratch_shapes=[pltpu.VMEM((NA_STEP, NA_HEADS, NA_QB, NA_KB * NA_QB), f32),
                        pltpu.VMEM((NA_STEP, NA_HEADS, NA_QB, NA_KB * NA_QB), bf16)],
        compiler_params=_params(("parallel",)),
        name="na_attn",
    )(P, KT, P, *([tab] * NA_STEP), g)


def _memkv_kernel(m_ref, g_ref, w_ref, o_ref):
    o_ref[...] = jnp.dot(_rms(m_ref[...], g_ref[...]).astype(bf16), w_ref[...].astype(bf16),
                         preferred_element_type=f32).astype(bf16)


def _mem_kv(mem2, g, w):
    return pl.pallas_call(
        _memkv_kernel,
        out_shape=jax.ShapeDtypeStruct((N_MEM, 2 * MEM_WIDTH), bf16),
        compiler_params=pltpu.CompilerParams(vmem_limit_bytes=VMEM_LIMIT),
        name="mem_kv",
    )(mem2, g, w)


def _ssm_prep_kernel(lre_r, lim_r, ldt_r, bre_ref, bim_ref, cre_ref, cim_ref, win_ref,
                     wtoep_ref, wstate_ref, wcarry_ref, apow_ref, win_bf_ref, *, seg_len):
    win_bf_ref[...] = win_ref[...].astype(bf16)
    T = SSM_T
    TS = TILE_STATES
    reps = LANES // SSM_CH
    assert T == SUBLANES

    def same_group(shape, row_div, col_div):
        r = lax.broadcasted_iota(jnp.int32, shape, 0) // row_div
        c = lax.broadcasted_iota(jnp.int32, shape, 1) // col_div
        return (r == c).astype(f32)

    mask_state = same_group((LANES, TS), SSM_CH, SSM_STATE)
    mask_chan = same_group((LANES, LANES), SSM_CH, SSM_CH)

    kk = []
    for d in range(2):
        lre = jnp.minimum(lre_r[d, 0], LAM_RE_MAX)
        lim = lim_r[d, 0]
        dt = jnp.exp(ldt_r[d, 0])
        zr = lre * dt
        zi = lim * dt
        er = jnp.exp(zr)
        nr = er * jnp.cos(zi) - 1.0
        ni = er * jnp.sin(zi)
        den = lre * lre + lim * lim
        qr = (nr * lre + ni * lim) / den
        qi = (ni * lre - nr * lim) / den
        bre = bre_ref[d, 0]
        bim = bim_ref[d, 0]
        bbr = qr * bre - qi * bim
        bbi = qr * bim + qi * bre
        e0 = lax.broadcasted_iota(jnp.int32, (T, TS), 0).astype(f32)
        mag0 = jnp.exp(zr * e0)
        p0r, p0i = mag0 * jnp.cos(zi * e0), mag0 * jnp.sin(zi * e0)
        mag1 = jnp.exp(zr * (e0 + 1.0))
        p1r, p1i = mag1 * jnp.cos(zi * (e0 + 1.0)), mag1 * jnp.sin(zi * (e0 + 1.0))
        wc_r = [p0r[e:e + 1] * bbr - p0i[e:e + 1] * bbi for e in range(T)]
        wc_i = [p0r[e:e + 1] * bbi + p0i[e:e + 1] * bbr for e in range(T)]

        for s in range(T):
            e = T - 1 - s if d == 0 else s
            rows = slice(s * LANES, (s + 1) * LANES)
            base = d * 2 * TS
            wstate_ref[0, rows, base:base + TS] = (
                jnp.concatenate([wc_r[e]] * reps, axis=0) * mask_state).astype(bf16)
            wstate_ref[0, rows, base + TS:base + 2 * TS] = (
                jnp.concatenate([wc_i[e]] * reps, axis=0) * mask_state).astype(bf16)

        cre = cre_ref[d, 0]
        cim = cim_ref[d, 0]
        nt = (((1,), (1,)), ((), ()))
        kc = (lax.dot_general(jnp.concatenate(wc_r, axis=0), jnp.concatenate([cre] * reps, axis=0) * mask_state,
                              nt, precision=lax.Precision.HIGHEST, preferred_element_type=f32)
              - lax.dot_general(jnp.concatenate(wc_i, axis=0), jnp.concatenate([cim] * reps, axis=0) * mask_state,
                                nt, precision=lax.Precision.HIGHEST, preferred_element_type=f32))
        kk.append([jnp.concatenate([kc[e * SSM_CH:(e + 1) * SSM_CH]] * reps, axis=0) * mask_chan
                   for e in range(T)])

        for t in range(T):
            e = t if d == 0 else T - 1 - t
            pr, pi = p1r[e:e + 1], p1i[e:e + 1]
            rows = slice(t * LANES, (t + 1) * LANES)
            base = d * 2 * TS
            wcarry_ref[0, rows, base:base + TS] = (
                jnp.concatenate([cre * pr - cim * pi] * reps, axis=0) * mask_state).astype(bf16)
            wcarry_ref[0, rows, base + TS:base + 2 * TS] = (
                jnp.concatenate([-(cre * pi + cim * pr)] * reps, axis=0) * mask_state).astype(bf16)

        for part, steps in enumerate((T, T * seg_len)):
            k1 = (lax.broadcasted_iota(jnp.int32, (SUBLANES, TS), 0) + 1).astype(f32) * float(steps)
            mag = jnp.exp(zr * k1)
            rows = slice(part * SUBLANES, (part + 1) * SUBLANES)
            apow_ref[0, rows, d * 2 * TS:d * 2 * TS + TS] = mag * jnp.cos(zi * k1)
            apow_ref[0, rows, d * 2 * TS + TS:(d + 1) * 2 * TS] = mag * jnp.sin(zi * k1)

    for s in range(T):
        for t in range(T):
            if t > s:
                blk = kk[0][t - s]
            elif t < s:
                blk = kk[1][s - t]
            else:
                blk = kk[0][0] + kk[1][0]
            wtoep_ref[0, s * LANES:(s + 1) * LANES, t * LANES:(t + 1) * LANES] = blk.astype(bf16)


def _ssm_prep(lam_re, lam_im, log_dt, b_re, b_im, c_re, c_im, S, w_in):
    TS, NT, T = TILE_STATES, N_TILES, SSM_T
    gp = GROUPS_PER_TILE
    ldt = jnp.broadcast_to(log_dt[:, :, None], lam_re.shape)
    rows = [a.reshape(2, NT, 1, TS) for a in (lam_re, lam_im, ldt)]

    def b_layout(b):
        return b.reshape(2, NT, gp, SSM_STATE, SSM_CH).transpose(0, 1, 4, 2, 3).reshape(2, NT, SSM_CH, TS)

    def c_layout(c):
        return c.reshape(2, NT, gp, SSM_CH, SSM_STATE).transpose(0, 1, 3, 2, 4).reshape(2, NT, SSM_CH, TS)

    row_spec = pl.BlockSpec((2, 1, 1, TS), lambda g: (0, g, 0, 0))
    bc_spec = pl.BlockSpec((2, 1, SSM_CH, TS), lambda g: (0, g, 0, 0))
    win_spec = pl.BlockSpec((w_in.shape[0] // NT, w_in.shape[1]), lambda g: (g, 0))
    return pl.pallas_call(
        functools.partial(_ssm_prep_kernel, seg_len=S // T // SUBLANES),
        grid=(NT,),
        in_specs=[row_spec] * 3 + [bc_spec] * 4 + [win_spec],
        out_specs=[pl.BlockSpec((1, T * LANES, T * LANES), lambda g: (g, 0, 0)),
                   pl.BlockSpec((1, T * LANES, STATE_COLS), lambda g: (g, 0, 0)),
                   pl.BlockSpec((1, T * LANES, STATE_COLS), lambda g: (g, 0, 0)),
                   pl.BlockSpec((1, 2 * SUBLANES, STATE_COLS), lambda g: (g, 0, 0)), win_spec],
        out_shape=[jax.ShapeDtypeStruct((NT, T * LANES, T * LANES), bf16),
                   jax.ShapeDtypeStruct((NT, T * LANES, STATE_COLS), bf16),
                   jax.ShapeDtypeStruct((NT, T * LANES, STATE_COLS), bf16),
                   jax.ShapeDtypeStruct((NT, 2 * SUBLANES, STATE_COLS), f32),
                   jax.ShapeDtypeStruct(w_in.shape, bf16)],
        compiler_params=_params(("parallel",)),
        name="ssm_prep",
    )(*rows, b_layout(b_re), b_layout(b_im), c_layout(c_re), c_layout(c_im), w_in)


def _cmul_add(xr, xi, ar, ai, sr, si):
    return xr + ar * sr - ai * si, xi + ar * si + ai * sr


def _scan_rows(xr, xi, apr, api, cr, ci, reverse):
    n = SUBLANES
    rows = lax.broadcasted_iota(jnp.int32, xr.shape, 0)
    for k in (1, 2, 4):
        ar, ai = apr[k - 1:k], api[k - 1:k]
        if reverse:
            sr, si = pltpu.roll(xr, n - k, 0), pltpu.roll(xi, n - k, 0)
            keep = rows < n - k
        else:
            sr, si = pltpu.roll(xr, k, 0), pltpu.roll(xi, k, 0)
            keep = rows >= k
        xr, xi = _cmul_add(xr, xi, ar, ai, jnp.where(keep, sr, 0.0), jnp.where(keep, si, 0.0))
    if reverse:
        pwr = jnp.concatenate([apr[n - 1 - j:n - j] for j in range(n)], axis=0)
        pwi = jnp.concatenate([api[n - 1 - j:n - j] for j in range(n)], axis=0)
    else:
        pwr, pwi = apr, api
    hr, hi = _cmul_add(xr, xi, pwr, pwi, cr, ci)
    if reverse:
        inr = jnp.where(rows == n - 1, cr, pltpu.roll(hr, n - 1, 0))
        ini = jnp.where(rows == n - 1, ci, pltpu.roll(hi, n - 1, 0))
        return inr, ini, hr[0:1], hi[0:1]
    inr = jnp.where(rows == 0, cr, pltpu.roll(hr, 1, 0))
    ini = jnp.where(rows == 0, ci, pltpu.roll(hi, 1, 0))
    return inr, ini, hr[n - 1:n], hi[n - 1:n]


def _ssm_kernel(u_ref, d_ref, wtoep_ref, wstate_ref, wcarry_ref, apow_ref, o_ref, h_scr, *, nb, cb):
    T = SSM_T
    TS = TILE_STATES
    j = pl.program_id(1)

    nseg = SUBLANES
    seg = nb * cb // nseg
    assert cb % seg == 0

    def chunk_rows():
        return [u_ref[pl.ds(t, cb, stride=T), :] for t in range(T)]

    def batch_rows(b):
        return [pl.ds(b * (cb // seg) + part, seg, stride=nseg) for part in range(cb // seg)]

    @pl.when(j < nb)
    def _():
        x = jnp.concatenate([ut.astype(bf16) for ut in chunk_rows()], axis=1)
        res = jnp.dot(x, wstate_ref[0], preferred_element_type=f32)
        for part, rows in enumerate(batch_rows(j)):
            for c in range(STATE_COLS // LANES):
                h_scr[c, rows, :] = res[part * seg:(part + 1) * seg, c * LANES:(c + 1) * LANES]

    @pl.when(j == nb - 1)
    def _():
        tiles = TS // LANES

        def cols(d, ri):
            return slice((2 * d + ri) * TS, (2 * d + ri + 1) * TS)

        def rows_at(p):
            return pl.ds(pl.multiple_of(p * nseg, nseg), nseg)

        def load(rows, d, ri):
            return jnp.concatenate([h_scr[(2 * d + ri) * tiles + c, rows, :] for c in range(tiles)], axis=1)

        def store(rows, d, ri, val):
            for c in range(tiles):
                h_scr[(2 * d + ri) * tiles + c, rows, :] = val[:, c * LANES:(c + 1) * LANES]

        a = [[jnp.broadcast_to(apow_ref[0, 0:1, cols(d, ri)], (nseg, TS)) for ri in range(2)] for d in range(2)]
        zero = jnp.zeros((nseg, TS), f32)

        def step(p, h, write):
            hfr, hfi, hbr, hbi = h
            rf, rb = rows_at(p), rows_at(seg - 1 - p)
            xfr, xfi, xbr, xbi = load(rf, 0, 0), load(rf, 0, 1), load(rb, 1, 0), load(rb, 1, 1)
            if write:
                store(rf, 0, 0, hfr)
                store(rf, 0, 1, hfi)
                store(rb, 1, 0, hbr)
                store(rb, 1, 1, hbi)
            hfr, hfi = _cmul_add(xfr, xfi, a[0][0], a[0][1], hfr, hfi)
            hbr, hbi = _cmul_add(xbr, xbi, a[1][0], a[1][1], hbr, hbi)
            return hfr, hfi, hbr, hbi

        ends = lax.fori_loop(0, seg, lambda p, h: step(p, h, False), (zero, zero, zero, zero))
        z1 = jnp.zeros((1, TS), f32)
        enter = []
        for d in range(2):
            cr, ci, _, _ = _scan_rows(ends[2 * d], ends[2 * d + 1], apow_ref[0, nseg:2 * nseg, cols(d, 0)],
                                      apow_ref[0, nseg:2 * nseg, cols(d, 1)], z1, z1, d == 1)
            enter += [cr, ci]
        lax.fori_loop(0, seg, lambda p, h: step(p, h, True), tuple(enter))

    @pl.when(j >= nb)
    def _():
        us = chunk_rows()
        x = jnp.concatenate([ut.astype(bf16) for ut in us], axis=1)
        hc = jnp.concatenate(
            [jnp.concatenate([h_scr[c, rows, :].astype(bf16) for c in range(STATE_COLS // LANES)], axis=1)
             for rows in batch_rows(j - nb)], axis=0)
        y = (jnp.dot(x, wtoep_ref[0], preferred_element_type=f32)
             + lax.dot_general(hc, wcarry_ref[0], (((1,), (1,)), ((), ())), preferred_element_type=f32))
        for t in range(T):
            yt = y[:, t * LANES:(t + 1) * LANES] + d_ref[...] * us[t]
            o_ref[pl.ds(t, cb, stride=T), :] = jax.nn.gelu(yt)


def _ssm_main(u, d_row, wtoep, wstate, wcarry, apow, tb=4096):
    S = u.shape[0]
    T = SSM_T
    nb = S // tb
    cb = tb // T
    return pl.pallas_call(
        functools.partial(_ssm_kernel, nb=nb, cb=cb),
        grid=(N_TILES, 2 * nb),
        in_specs=[pl.BlockSpec((tb, LANES), lambda g, j: (j % nb, g)),
                  pl.BlockSpec((1, LANES), lambda g, j: (0, g)),
                  pl.BlockSpec((1, T * LANES, T * LANES), lambda g, j: (g, 0, 0)),
                  pl.BlockSpec((1, T * LANES, STATE_COLS), lambda g, j: (g, 0, 0)),
                  pl.BlockSpec((1, T * LANES, STATE_COLS), lambda g, j: (g, 0, 0)),
                  pl.BlockSpec((1, 2 * SUBLANES, STATE_COLS), lambda g, j: (g, 0, 0))],
        out_specs=pl.BlockSpec((tb, LANES), lambda g, j: (jnp.maximum(j - nb, 0), g)),
        out_shape=jax.ShapeDtypeStruct((S, SSM_WIDTH), f32),
        scratch_shapes=[pltpu.VMEM((STATE_COLS // LANES, S // T, LANES), f32)],
        compiler_params=_params(("arbitrary", "arbitrary")),
        name="ssm_main",
    )(u, d_row, wtoep, wstate, wcarry, apow)


def _outproj_kernel(yna_ref, yssm_ref, ymem_ref, x_ref, wglu_ref, bglu_ref, gssm_ref, wout_ref,
                    gpost_ref, gmlp_ref, x1_ref, h2_ref, *, nsub):
    a, b = NA_WIDTH, NA_WIDTH + SSM_WIDTH
    sub = x_ref.shape[0] // nsub
    tiles = [slice(r * sub, (r + 1) * sub) for r in range(nsub)]
    ys = []
    for rows in tiles:
        ya = yssm_ref[rows, :]
        gate = jax.nn.sigmoid(jnp.dot(ya.astype(bf16), wglu_ref[...], preferred_element_type=f32)
                              + bglu_ref[...])
        ys.append(_rms(ya * gate, gssm_ref[...]).astype(bf16))
    accs = [jnp.dot(yna_ref[rows, :], wout_ref[0:a, :], preferred_element_type=f32)
            + jnp.dot(ys[r], wout_ref[a:b, :], preferred_element_type=f32)
            + jnp.dot(ymem_ref[rows, :], wout_ref[b:, :], preferred_element_type=f32)
            for r, rows in enumerate(tiles)]
    for rows, acc in zip(tiles, accs):
        x1 = x_ref[rows, :] + _rms(acc, gpost_ref[...])
        x1_ref[rows, :] = x1
        h2_ref[rows, :] = _rms(x1, gmlp_ref[...]).astype(bf16)


def _out_proj(y_na, y_ssm, y_mem, x2, wglu_bf, b_glu, g_ssm, wout_bf, g_post, g_mlp, tm=512):
    S = x2.shape[0]
    row = lambda w: pl.BlockSpec((tm, w), lambda i: (i, 0))
    vec = lambda w: pl.BlockSpec((1, w), lambda i: (0, 0))
    return pl.pallas_call(
        functools.partial(_outproj_kernel, nsub=4),
        grid=(S // tm,),
        in_specs=[row(NA_WIDTH), row(SSM_WIDTH), row(MEM_WIDTH), row(D_MODEL),
                  pl.BlockSpec((SSM_WIDTH, SSM_WIDTH), lambda i: (0, 0)), vec(SSM_WIDTH), vec(SSM_WIDTH),
                  pl.BlockSpec((D_MODEL, D_MODEL), lambda i: (0, 0), pipeline_mode=pl.Buffered(1)),
                  vec(D_MODEL), vec(D_MODEL)],
        out_specs=[row(D_MODEL), row(D_MODEL)],
        out_shape=[jax.ShapeDtypeStruct((S, D_MODEL), f32), jax.ShapeDtypeStruct((S, D_MODEL), bf16)],
        compiler_params=_params(("parallel",)),
        name="out_proj",
    )(y_na, y_ssm, y_mem, x2, wglu_bf, b_glu, g_ssm, wout_bf, g_post, g_mlp)


def _mlp_kernel(h_ref, w1_ref, w2_ref, x1_hbm, g_ref, o_ref, hid_scr, x1_buf, x1_sem, *, nsplit):
    i = pl.program_id(0)
    k = pl.program_id(1)
    nk = pl.num_programs(1) - 1
    tm = o_ref.shape[0]
    wn = D_MODEL // nsplit

    def x1_copy():
        return pltpu.make_async_copy(x1_hbm.at[pl.ds(pl.multiple_of(i * tm, tm), tm), :], x1_buf, x1_sem)

    def hidden():
        hid = jnp.dot(h_ref[...], w1_ref[...], preferred_element_type=f32)
        return jnp.square(jnp.maximum(hid, 0.0)).astype(bf16)

    def partial_out(n):
        cols = slice(n * wn, (n + 1) * wn)
        return o_ref[:, cols] + jnp.dot(hid_scr[(k + 1) % 2], w2_ref[:, cols], preferred_element_type=f32)

    @pl.when(k == 0)
    def _():
        x1_copy().start()
        o_ref[...] = jnp.zeros_like(o_ref)
        hid_scr[0] = hidden()

    @pl.when((k > 0) & (k < nk))
    def _():
        for n in range(nsplit):
            o_ref[:, n * wn:(n + 1) * wn] = partial_out(n)
        hid_scr[k % 2] = hidden()

    @pl.when(k == nk)
    def _():
        ssq = jnp.zeros((tm, 1), f32)
        for n in range(nsplit):
            f = partial_out(n)
            o_ref[:, n * wn:(n + 1) * wn] = f
            ssq = ssq + jnp.sum(f * f, axis=-1, keepdims=True)
        scale = lax.rsqrt(ssq * (1.0 / D_MODEL) + EPS)
        x1_copy().wait()
        o_ref[...] = x1_buf[...] + o_ref[...] * scale * g_ref[...]


def _mlp(h2, w1_bf, w2_bf, x1, g, tm=1024, tk=1024):
    S = h2.shape[0]
    nk = D_FF // tk
    return pl.pallas_call(
        functools.partial(_mlp_kernel, nsplit=4),
        grid=(S // tm, nk + 1),
        in_specs=[pl.BlockSpec((tm, D_MODEL), lambda i, k: (i, 0)),
                  pl.BlockSpec((D_MODEL, tk), lambda i, k: (0, jnp.minimum(k, nk - 1))),
                  pl.BlockSpec((tk, D_MODEL), lambda i, k: (jnp.maximum(k - 1, 0), 0)),
                  pl.BlockSpec(memory_space=pl.ANY),
                  pl.BlockSpec((1, D_MODEL), lambda i, k: (0, 0))],
        out_specs=pl.BlockSpec((tm, D_MODEL), lambda i, k: (i, 0)),
        out_shape=jax.ShapeDtypeStruct((S, D_MODEL), f32),
        scratch_shapes=[pltpu.VMEM((2, tm, tk), bf16), pltpu.VMEM((tm, D_MODEL), f32),
                        pltpu.SemaphoreType.DMA(())],
        compiler_params=_params(("arbitrary", "arbitrary")),
        name="mlp",
    )(h2, w1_bf, w2_bf, x1, g)


def kernel(x, mem, norm_mix_pre, w_in, na_rpb, ssm_lam_re, ssm_lam_im, ssm_log_dt, ssm_b_re, ssm_b_im, ssm_c_re, ssm_c_im, ssm_d, w_glu, b_glu, mem_norm, w_mem_kv, out_norm_na, out_norm_ssm, out_norm_mem, w_out, norm_mix_post, norm_mlp_pre, w_ff1, w_ff2, norm_mlp_post):
    B, S, _ = x.shape
    assert B == 1 and x.shape[2] == D_MODEL and S % 4096 == 0 and w_in.shape[0] == 1
    x2 = x[0]
    l = 0
    vec = lambda a: a[l].reshape(1, -1)

    kv = _mem_kv(mem[0], vec(mem_norm), w_mem_kv[l])
    wtoep, wstate, wcarry, apow, win_bf = _ssm_prep(ssm_lam_re[l], ssm_lam_im[l], ssm_log_dt[l], ssm_b_re[l],
                                                    ssm_b_im[l], ssm_c_re[l], ssm_c_im[l], S, w_in[l])
    P, KT, u, y_mem, wglu_bf, wout_bf, w1_bf, w2_bf = _proj(
        x2, vec(norm_mix_pre), win_bf, kv, vec(out_norm_mem), [w_glu[l], w_out[l], w_ff1[l], w_ff2[l]])

    y_na = _na_attn(P, KT, _na_bias(na_rpb[l], S), vec(out_norm_na))

    y_ssm = _ssm_main(u, ssm_d[l].reshape(1, SSM_WIDTH), wtoep, wstate, wcarry, apow)

    x1, h2 = _out_proj(y_na, y_ssm, y_mem, x2, wglu_bf, vec(b_glu), vec(out_norm_ssm),
                       wout_bf, vec(norm_mix_post), vec(norm_mlp_pre))
    out = _mlp(h2, w1_bf, w2_bf, x1, vec(norm_mlp_post))
    return out[None]
```

```python
import functools

import jax
import jax.numpy as jnp
from jax import lax
from jax.experimental import pallas as pl
from jax.experimental.pallas import tpu as pltpu

f32 = jnp.float32
bf16 = jnp.bfloat16

D_MODEL = 2048
GRID_W = 64
WIN_H = 8
WIN_W = 16
HEAD_DIM = 128
NA_WIDTH = 1024
NA_HEADS = 8
SSM_WIDTH = 512
SSM_CH = 16
SSM_GROUPS = 32
SSM_STATE = 64
MEM_WIDTH = 512
MEM_HEADS = 4
N_MEM = 256
IN_WIDTH = 4096
D_FF = 8192
EPS = 1e-6
LAM_RE_MAX = -1e-4

LANES = 128
SUBLANES = 8
GROUPS_PER_TILE = LANES // SSM_CH
N_TILES = SSM_WIDTH // LANES
TILE_STATES = GROUPS_PER_TILE * SSM_STATE
STATE_COLS = 4 * TILE_STATES
SSM_T = 8
MASK_VALUE = -1e30
LOG2E = 1.4426950408889634
VMEM_LIMIT = 58 * 1024 * 1024


def _rms(x, g):
    ms = jnp.mean(x * x, axis=-1, keepdims=True)
    return x * lax.rsqrt(ms + EPS) * g


def _params(sem):
    return pltpu.CompilerParams(dimension_semantics=sem, vmem_limit_bytes=VMEM_LIMIT)


def _softmax_pv(s, v):
    p = jnp.exp2(s - jnp.max(s, axis=-1, keepdims=True))
    inv = 1.0 / jnp.sum(p, axis=-1, keepdims=True)
    return jnp.dot(p.astype(bf16), v, preferred_element_type=f32) * inv


def _proj_kernel(x_ref, g_ref, w_ref, kv_ref, gmem_ref, *rest, ncast, scale):
    cast_in, rest = rest[:ncast], rest[ncast:]
    p_ref, kt_ref, u_ref, ymem_ref = rest[:4]
    cast_out, h_scr = rest[4:4 + ncast], rest[4 + ncast]
    for src, dst in zip(cast_in, cast_out):
        dst[...] = src[...].astype(bf16)
    nt = (((1,), (1,)), ((), ()))
    x = x_ref[...]
    h_scr[...] = (x * g_ref[...]).astype(bf16)
    r = lax.rsqrt(jnp.mean(x * x, axis=-1, keepdims=True) + EPS)

    acc = jnp.dot(h_scr[...], w_ref[:, 3 * NA_WIDTH:], preferred_element_type=f32) * r
    u_ref[...] = acc[:, :SSM_WIDTH]
    qm = (acc[:, SSM_WIDTH:] * scale).astype(bf16)
    outs = []
    for h in range(MEM_HEADS):
        sl = slice(h * HEAD_DIM, (h + 1) * HEAD_DIM)
        s = lax.dot_general(qm[:, sl], kv_ref[:, sl], nt, preferred_element_type=f32)
        outs.append(_softmax_pv(s, kv_ref[:, MEM_WIDTH + h * HEAD_DIM:MEM_WIDTH + (h + 1) * HEAD_DIM]))
    ymem_ref[...] = _rms(jnp.concatenate(outs, axis=1), gmem_ref[...]).astype(bf16)

    for j in range(3):
        acc = jnp.dot(h_scr[...], w_ref[:, j * NA_WIDTH:(j + 1) * NA_WIDTH], preferred_element_type=f32)
        if j == 0:
            p_ref[:, :NA_WIDTH] = (acc * (r * scale)).astype(bf16)
        elif j == 1:
            kt_ref[...] = (acc * r).T.astype(bf16)
        else:
            p_ref[:, NA_WIDTH:] = (acc * r).astype(bf16)


def _proj(x2, g, w_bf, kv, g_mem, cast_weights, tm=512):
    S = x2.shape[0]
    steps = S // tm
    cast_specs = [pl.BlockSpec((w.shape[0] // steps, w.shape[1]), lambda i: (i, 0)) for w in cast_weights]
    return pl.pallas_call(
        functools.partial(_proj_kernel, ncast=len(cast_weights), scale=HEAD_DIM ** -0.5 * LOG2E),
        grid=(steps,),
        in_specs=[pl.BlockSpec((tm, D_MODEL), lambda i: (i, 0)),
                  pl.BlockSpec((1, D_MODEL), lambda i: (0, 0)),
                  pl.BlockSpec((D_MODEL, IN_WIDTH), lambda i: (0, 0), pipeline_mode=pl.Buffered(1)),
                  pl.BlockSpec((N_MEM, 2 * MEM_WIDTH), lambda i: (0, 0)),
                  pl.BlockSpec((1, MEM_WIDTH), lambda i: (0, 0))] + cast_specs,
        out_specs=[pl.BlockSpec((tm, 2 * NA_WIDTH), lambda i: (i, 0)),
                   pl.BlockSpec((NA_WIDTH, tm), lambda i: (0, i)),
                   pl.BlockSpec((tm, SSM_WIDTH), lambda i: (i, 0)),
                   pl.BlockSpec((tm, MEM_WIDTH), lambda i: (i, 0))] + cast_specs,
        out_shape=[jax.ShapeDtypeStruct((S, 2 * NA_WIDTH), bf16),
                   jax.ShapeDtypeStruct((NA_WIDTH, S), bf16),
                   jax.ShapeDtypeStruct((S, SSM_WIDTH), f32),
                   jax.ShapeDtypeStruct((S, MEM_WIDTH), bf16)]
                  + [jax.ShapeDtypeStruct(w.shape, bf16) for w in cast_weights],
        scratch_shapes=[pltpu.VMEM((tm, D_MODEL), bf16)],
        compiler_params=_params(("parallel",)),
        name="proj",
    )(x2, g, w_bf, kv, g_mem, *cast_weights)


NA_QB = 128
NA_KB = 5
NA_EDGE = 2
NA_STEP = 4
NA_ROWS_Q = NA_QB // GRID_W
NA_ROWS_WIN = NA_KB * NA_QB // GRID_W


def _na_window_start(b, nb):
    return jnp.clip(b - 2, 0, nb - NA_KB)


def _na_bias_kernel(v_ref, o_ref, *, nb, rows):
    c = lax.broadcasted_iota(jnp.int32, (GRID_W, LANES), 0)
    lane = lax.broadcasted_iota(jnp.int32, (GRID_W, LANES), 1)
    kc = lane % GRID_W
    cs = jnp.clip(c - WIN_W // 2, 0, GRID_W - WIN_W)
    col_ok = (kc >= cs) & (kc < cs + WIN_W)
    half_ok = {(True, True): col_ok, (True, False): col_ok & (lane < GRID_W), (False, True): col_ok & (lane >= GRID_W)}
    masked = jnp.full((GRID_W, LANES), MASK_VALUE, f32)
    variants = list(range(NA_EDGE)) + [NA_EDGE] + list(range(nb - NA_EDGE, nb))
    for vi, b in enumerate(variants):
        wb = min(max(b - 2, 0), nb - NA_KB)
        for rq in range(NA_ROWS_Q):
            r = b * NA_ROWS_Q + rq
            rs = min(max(r - WIN_H // 2, 0), rows - WIN_H)
            for m in range(NA_ROWS_WIN // 2):
                kr = wb * NA_ROWS_Q + 2 * m
                ok = (rs <= kr < rs + WIN_H, rs <= kr + 1 < rs + WIN_H)
                if ok == (False, False):
                    blk = masked
                else:
                    d = kr - r + WIN_H
                    src = jnp.broadcast_to(v_ref[0, d:d + 1, :], (GRID_W, LANES))
                    rolled = pltpu.roll(src, LANES - (WIN_W - 1), 1, stride=1, stride_axis=0)
                    blk = jnp.where(half_ok[ok], rolled * LOG2E, MASK_VALUE)
                o_ref[vi, 0, rq * GRID_W:(rq + 1) * GRID_W, m * LANES:(m + 1) * LANES] = blk


def _na_bias(rpb, S):
    heads, nrow, ncol = rpb.shape
    nb = S // NA_QB
    nvar = 2 * NA_EDGE + 1
    padded = jnp.zeros((heads, nrow + 2, GRID_W), f32).at[:, 1:nrow + 1, :ncol].set(rpb.astype(f32))
    pairs = jnp.concatenate([padded[:, :-1], padded[:, 1:]], axis=-1)
    return pl.pallas_call(
        functools.partial(_na_bias_kernel, nb=nb, rows=S // GRID_W),
        grid=(heads,),
        in_specs=[pl.BlockSpec((1, nrow + 1, LANES), lambda h: (h, 0, 0))],
        out_specs=pl.BlockSpec((nvar, 1, NA_QB, NA_KB * NA_QB), lambda h: (0, h, 0, 0)),
        out_shape=jax.ShapeDtypeStruct((nvar, heads, NA_QB, NA_KB * NA_QB), f32),
        compiler_params=_params(("parallel",)),
        name="na_bias",
    )(pairs)


def _na_step_window_start(i, nb):
    return jnp.clip(NA_STEP * i - 2, 0, nb - NA_KB - NA_STEP + 1)


def _na_kernel(q_ref, kt_ref, v_ref, *rest, nb):
    tab_refs = rest[:NA_STEP]
    g_ref, o_ref, s_scr, p_scr = rest[NA_STEP:]
    i = pl.program_id(0)
    win0 = _na_step_window_start(i, nb)
    starts = []
    for jb in range(NA_STEP):
        local = _na_window_start(NA_STEP * i + jb, nb) - win0
        starts.append(pl.multiple_of(local * NA_QB, NA_QB))
        for h in range(NA_HEADS):
            sl = slice(h * HEAD_DIM, (h + 1) * HEAD_DIM)
            kt = kt_ref[sl, pl.ds(starts[jb], NA_KB * NA_QB)]
            s_scr[jb, h] = jnp.dot(q_ref[jb * NA_QB:(jb + 1) * NA_QB, sl], kt,
                                   preferred_element_type=f32) + tab_refs[jb][0, h]
    inv = []
    for jb in range(NA_STEP):
        for h in range(NA_HEADS):
            s = s_scr[jb, h]
            p = jnp.exp2(s - jnp.max(s, axis=-1, keepdims=True))
            inv.append(1.0 / jnp.sum(p, axis=-1, keepdims=True))
            p_scr[jb, h] = p.astype(bf16)
    for jb in range(NA_STEP):
        outs = []
        for h in range(NA_HEADS):
            sl = slice(h * HEAD_DIM, (h + 1) * HEAD_DIM)
            v = v_ref[pl.ds(starts[jb], NA_KB * NA_QB), sl]
            outs.append(jnp.dot(p_scr[jb, h], v, preferred_element_type=f32) * inv[jb * NA_HEADS + h])
        o_ref[jb * NA_QB:(jb + 1) * NA_QB, :] = _rms(jnp.concatenate(outs, axis=1), g_ref[...]).astype(bf16)


def _na_attn(P, KT, tab, g):
    S = P.shape[0]
    nb = S // NA_QB
    win = (NA_KB + NA_STEP - 1) * NA_QB

    def variant(b):
        return jnp.where(b < NA_EDGE, b, jnp.where(b >= nb - NA_EDGE, b - (nb - 2 * NA_EDGE - 1), NA_EDGE))

    v_window = pl.BlockSpec((pl.Element(win), pl.Element(NA_WIDTH)),
                            lambda i: (_na_step_window_start(i, nb) * NA_QB, NA_WIDTH))
    kt_window = pl.BlockSpec((pl.Element(NA_WIDTH), pl.Element(win)),
                             lambda i: (0, _na_step_window_start(i, nb) * NA_QB))

    tab_specs = [pl.BlockSpec((1, NA_HEADS, NA_QB, NA_KB * NA_QB),
                              lambda i, jb=jb: (variant(NA_STEP * i + jb), 0, 0, 0)) for jb in range(NA_STEP)]
    return pl.pallas_call(
        functools.partial(_na_kernel, nb=nb),
        grid=(nb // NA_STEP,),
        in_specs=[pl.BlockSpec((NA_STEP * NA_QB, NA_WIDTH), lambda i: (i, 0)),
                  kt_window, v_window] + tab_specs + [
                  pl.BlockSpec((1, NA_WIDTH), lambda i: (0, 0))],
        out_specs=pl.BlockSpec((NA_STEP * NA_QB, NA_WIDTH), lambda i: (i, 0)),
        out_shape=jax.ShapeDtypeStruct((S, NA_WIDTH), bf16),
        scratch_shapes=[pltpu.VMEM((NA_STEP, NA_HEADS, NA_QB, NA_KB * NA_QB), f32),
                        pltpu.VMEM((NA_STEP, NA_HEADS, NA_QB, NA_KB * NA_QB), bf16)],
        compiler_params=_params(("parallel",)),
        name="na_attn",
    )(P, KT, P, *([tab] * NA_STEP), g)


def _memkv_kernel(m_ref, g_ref, w_ref, o_ref):
    o_ref[...] = jnp.dot(_rms(m_ref[...], g_ref[...]).astype(bf16), w_ref[...].astype(bf16),
                         preferred_element_type=f32).astype(bf16)


def _mem_kv(mem2, g, w):
    return pl.pallas_call(
        _memkv_kernel,
        out_shape=jax.ShapeDtypeStruct((N_MEM, 2 * MEM_WIDTH), bf16),
        compiler_params=pltpu.CompilerParams(vmem_limit_bytes=VMEM_LIMIT),
        name="mem_kv",
    )(mem2, g, w)


def _ssm_prep_kernel(lre_r, lim_r, ldt_r, bre_ref, bim_ref, cre_ref, cim_ref, win_ref,
                     wtoep_ref, wstate_ref, wcarry_ref, apow_ref, win_bf_ref, *, seg_len):
    win_bf_ref[...] = win_ref[...].astype(bf16)
    T = SSM_T
    TS = TILE_STATES
    reps = LANES // SSM_CH
    assert T == SUBLANES

    def same_group(shape, row_div, col_div):
        r = lax.broadcasted_iota(jnp.int32, shape, 0) // row_div
        c = lax.broadcasted_iota(jnp.int32, shape, 1) // col_div
        return (r == c).astype(f32)

    mask_state = same_group((LANES, TS), SSM_CH, SSM_STATE)
    mask_chan = same_group((LANES, LANES), SSM_CH, SSM_CH)

    kk = []
    for d in range(2):
        lre = jnp.minimum(lre_r[d, 0], LAM_RE_MAX)
        lim = lim_r[d, 0]
        dt = jnp.exp(ldt_r[d, 0])
        zr = lre * dt
        zi = lim * dt
        er = jnp.exp(zr)
        nr = er * jnp.cos(zi) - 1.0
        ni = er * jnp.sin(zi)
        den = lre * lre + lim * lim
        qr = (nr * lre + ni * lim) / den
        qi = (ni * lre - nr * lim) / den
        bre = bre_ref[d, 0]
        bim = bim_ref[d, 0]
        bbr = qr * bre - qi * bim
        bbi = qr * bim + qi * bre
        e0 = lax.broadcasted_iota(jnp.int32, (T, TS), 0).astype(f32)
        mag0 = jnp.exp(zr * e0)
        p0r, p0i = mag0 * jnp.cos(zi * e0), mag0 * jnp.sin(zi * e0)
        mag1 = jnp.exp(zr * (e0 + 1.0))
        p1r, p1i = mag1 * jnp.cos(zi * (e0 + 1.0)), mag1 * jnp.sin(zi * (e0 + 1.0))
        wc_r = [p0r[e:e + 1] * bbr - p0i[e:e + 1] * bbi for e in range(T)]
        wc_i = [p0r[e:e + 1] * bbi + p0i[e:e + 1] * bbr for e in range(T)]

        for s in range(T):
            e = T - 1 - s if d == 0 else s
            rows = slice(s * LANES, (s + 1) * LANES)
            base = d * 2 * TS
            wstate_ref[0, rows, base:base + TS] = (
                jnp.concatenate([wc_r[e]] * reps, axis=0) * mask_state).astype(bf16)
            wstate_ref[0, rows, base + TS:base + 2 * TS] = (
                jnp.concatenate([wc_i[e]] * reps, axis=0) * mask_state).astype(bf16)

        cre = cre_ref[d, 0]
        cim = cim_ref[d, 0]
        nt = (((1,), (1,)), ((), ()))
        kc = (lax.dot_general(jnp.concatenate(wc_r, axis=0), jnp.concatenate([cre] * reps, axis=0) * mask_state,
                              nt, precision=lax.Precision.HIGHEST, preferred_element_type=f32)
              - lax.dot_general(jnp.concatenate(wc_i, axis=0), jnp.concatenate([cim] * reps, axis=0) * mask_state,
                                nt, precision=lax.Precision.HIGHEST, preferred_element_type=f32))
        kk.append([jnp.concatenate([kc[e * SSM_CH:(e + 1) * SSM_CH]] * reps, axis=0) * mask_chan
                   for e in range(T)])

        for t in range(T):
            e = t if d == 0 else T - 1 - t
            pr, pi = p1r[e:e + 1], p1i[e:e + 1]
            rows = slice(t * LANES, (t + 1) * LANES)
            base = d * 2 * TS
            wcarry_ref[0, rows, base:base + TS] = (
                jnp.concatenate([cre * pr - cim * pi] * reps, axis=0) * mask_state).astype(bf16)
            wcarry_ref[0, rows, base + TS:base + 2 * TS] = (
                jnp.concatenate([-(cre * pi + cim * pr)] * reps, axis=0) * mask_state).astype(bf16)

        for part, steps in enumerate((T, T * seg_len)):
            k1 = (lax.broadcasted_iota(jnp.int32, (SUBLANES, TS), 0) + 1).astype(f32) * float(steps)
            mag = jnp.exp(zr * k1)
            rows = slice(part * SUBLANES, (part + 1) * SUBLANES)
            apow_ref[0, rows, d * 2 * TS:d * 2 * TS + TS] = mag * jnp.cos(zi * k1)
            apow_ref[0, rows, d * 2 * TS + TS:(d + 1) * 2 * TS] = mag * jnp.sin(zi * k1)

    for s in range(T):
        for t in range(T):
            if t > s:
                blk = kk[0][t - s]
            elif t < s:
                blk = kk[1][s - t]
            else:
                blk = kk[0][0] + kk[1][0]
            wtoep_ref[0, s * LANES:(s + 1) * LANES, t * LANES:(t + 1) * LANES] = blk.astype(bf16)


def _ssm_prep(lam_re, lam_im, log_dt, b_re, b_im, c_re, c_im, S, w_in):
    TS, NT, T = TILE_STATES, N_TILES, SSM_T
    gp = GROUPS_PER_TILE
    ldt = jnp.broadcast_to(log_dt[:, :, None], lam_re.shape)
    rows = [a.reshape(2, NT, 1, TS) for a in (lam_re, lam_im, ldt)]

    def b_layout(b):
        return b.reshape(2, NT, gp, SSM_STATE, SSM_CH).transpose(0, 1, 4, 2, 3).reshape(2, NT, SSM_CH, TS)

    def c_layout(c):
        return c.reshape(2, NT, gp, SSM_CH, SSM_STATE).transpose(0, 1, 3, 2, 4).reshape(2, NT, SSM_CH, TS)

    row_spec = pl.BlockSpec((2, 1, 1, TS), lambda g: (0, g, 0, 0))
    bc_spec = pl.BlockSpec((2, 1, SSM_CH, TS), lambda g: (0, g, 0, 0))
    win_spec = pl.BlockSpec((w_in.shape[0] // NT, w_in.shape[1]), lambda g: (g, 0))
    return pl.pallas_call(
        functools.partial(_ssm_prep_kernel, seg_len=S // T // SUBLANES),
        grid=(NT,),
        in_specs=[row_spec] * 3 + [bc_spec] * 4 + [win_spec],
        out_specs=[pl.BlockSpec((1, T * LANES, T * LANES), lambda g: (g, 0, 0)),
                   pl.BlockSpec((1, T * LANES, STATE_COLS), lambda g: (g, 0, 0)),
                   pl.BlockSpec((1, T * LANES, STATE_COLS), lambda g: (g, 0, 0)),
                   pl.BlockSpec((1, 2 * SUBLANES, STATE_COLS), lambda g: (g, 0, 0)), win_spec],
        out_shape=[jax.ShapeDtypeStruct((NT, T * LANES, T * LANES), bf16),
                   jax.ShapeDtypeStruct((NT, T * LANES, STATE_COLS), bf16),
                   jax.ShapeDtypeStruct((NT, T * LANES, STATE_COLS), bf16),
                   jax.ShapeDtypeStruct((NT, 2 * SUBLANES, STATE_COLS), f32),
                   jax.ShapeDtypeStruct(w_in.shape, bf16)],
        compiler_params=_params(("parallel",)),
        name="ssm_prep",
    )(*rows, b_layout(b_re), b_layout(b_im), c_layout(c_re), c_layout(c_im), w_in)


def _cmul_add(xr, xi, ar, ai, sr, si):
    return xr + ar * sr - ai * si, xi + ar * si + ai * sr


def _scan_rows(xr, xi, apr, api, cr, ci, reverse):
    n = SUBLANES
    rows = lax.broadcasted_iota(jnp.int32, xr.shape, 0)
    for k in (1, 2, 4):
        ar, ai = apr[k - 1:k], api[k - 1:k]
        if reverse:
            sr, si = pltpu.roll(xr, n - k, 0), pltpu.roll(xi, n - k, 0)
            keep = rows < n - k
        else:
            sr, si = pltpu.roll(xr, k, 0), pltpu.roll(xi, k, 0)
            keep = rows >= k
        xr, xi = _cmul_add(xr, xi, ar, ai, jnp.where(keep, sr, 0.0), jnp.where(keep, si, 0.0))
    if reverse:
        pwr = jnp.concatenate([apr[n - 1 - j:n - j] for j in range(n)], axis=0)
        pwi = jnp.concatenate([api[n - 1 - j:n - j] for j in range(n)], axis=0)
    else:
        pwr, pwi = apr, api
    hr, hi = _cmul_add(xr, xi, pwr, pwi, cr, ci)
    if reverse:
        inr = jnp.where(rows == n - 1, cr, pltpu.roll(hr, n - 1, 0))
        ini = jnp.where(rows == n - 1, ci, pltpu.roll(hi, n - 1, 0))
        return inr, ini, hr[0:1], hi[0:1]
    inr = jnp.where(rows == 0, cr, pltpu.roll(hr, 1, 0))
    ini = jnp.where(rows == 0, ci, pltpu.roll(hi, 1, 0))
    return inr, ini, hr[n - 1:n], hi[n - 1:n]


def _ssm_kernel(u_ref, d_ref, wtoep_ref, wstate_ref, wcarry_ref, apow_ref, o_ref, h_scr, *, nb, cb):
    T = SSM_T
    TS = TILE_STATES
    j = pl.program_id(1)

    nseg = SUBLANES
    seg = nb * cb // nseg
    assert cb % seg == 0

    def chunk_rows():
        return [u_ref[pl.ds(t, cb, stride=T), :] for t in range(T)]

    def batch_rows(b):
        return [pl.ds(b * (cb // seg) + part, seg, stride=nseg) for part in range(cb // seg)]

    @pl.when(j < nb)
    def _():
        x = jnp.concatenate([ut.astype(bf16) for ut in chunk_rows()], axis=1)
        res = jnp.dot(x, wstate_ref[0], preferred_element_type=f32)
        for part, rows in enumerate(batch_rows(j)):
            for c in range(STATE_COLS // LANES):
                h_scr[c, rows, :] = res[part * seg:(part + 1) * seg, c * LANES:(c + 1) * LANES]

    @pl.when(j == nb - 1)
    def _():
        tiles = TS // LANES

        def cols(d, ri):
            return slice((2 * d + ri) * TS, (2 * d + ri + 1) * TS)

        def rows_at(p):
            return pl.ds(pl.multiple_of(p * nseg, nseg), nseg)

        def load(rows, d, ri):
            return jnp.concatenate([h_scr[(2 * d + ri) * tiles + c, rows, :] for c in range(tiles)], axis=1)

        def store(rows, d, ri, val):
            for c in range(tiles):
                h_scr[(2 * d + ri) * tiles + c, rows, :] = val[:, c * LANES:(c + 1) * LANES]

        a = [[jnp.broadcast_to(apow_ref[0, 0:1, cols(d, ri)], (nseg, TS)) for ri in range(2)] for d in range(2)]
        zero = jnp.zeros((nseg, TS), f32)

        def step(p, h, write):
            hfr, hfi, hbr, hbi = h
            rf, rb = rows_at(p), rows_at(seg - 1 - p)
            xfr, xfi, xbr, xbi = load(rf, 0, 0), load(rf, 0, 1), load(rb, 1, 0), load(rb, 1, 1)
            if write:
                store(rf, 0, 0, hfr)
                store(rf, 0, 1, hfi)
                store(rb, 1, 0, hbr)
                store(rb, 1, 1, hbi)
            hfr, hfi = _cmul_add(xfr, xfi, a[0][0], a[0][1], hfr, hfi)
            hbr, hbi = _cmul_add(xbr, xbi, a[1][0], a[1][1], hbr, hbi)
            return hfr, hfi, hbr, hbi

        ends = lax.fori_loop(0, seg, lambda p, h: step(p, h, False), (zero, zero, zero, zero))
        z1 = jnp.zeros((1, TS), f32)
        enter = []
        for d in range(2):
            cr, ci, _, _ = _scan_rows(ends[2 * d], ends[2 * d + 1], apow_ref[0, nseg:2 * nseg, cols(d, 0)],
                                      apow_ref[0, nseg:2 * nseg, cols(d, 1)], z1, z1, d == 1)
            enter += [cr, ci]
        lax.fori_loop(0, seg, lambda p, h: step(p, h, True), tuple(enter))

    @pl.when(j >= nb)
    def _():
        us = chunk_rows()
        x = jnp.concatenate([ut.astype(bf16) for ut in us], axis=1)
        hc = jnp.concatenate(
            [jnp.concatenate([h_scr[c, rows, :].astype(bf16) for c in range(STATE_COLS // LANES)], axis=1)
             for rows in batch_rows(j - nb)], axis=0)
        y = (jnp.dot(x, wtoep_ref[0], preferred_element_type=f32)
             + lax.dot_general(hc, wcarry_ref[0], (((1,), (1,)), ((), ())), preferred_element_type=f32))
        for t in range(T):
            yt = y[:, t * LANES:(t + 1) * LANES] + d_ref[...] * us[t]
            o_ref[pl.ds(t, cb, stride=T), :] = jax.nn.gelu(yt)


def _ssm_main(u, d_row, wtoep, wstate, wcarry, apow, tb=4096):
    S = u.shape[0]
    T = SSM_T
    nb = S // tb
    cb = tb // T
    return pl.pallas_call(
        functools.partial(_ssm_kernel, nb=nb, cb=cb),
        grid=(N_TILES, 2 * nb),
        in_specs=[pl.BlockSpec((tb, LANES), lambda g, j: (j % nb, g)),
                  pl.BlockSpec((1, LANES), lambda g, j: (0, g)),
                  pl.BlockSpec((1, T * LANES, T * LANES), lambda g, j: (g, 0, 0)),
                  pl.BlockSpec((1, T * LANES, STATE_COLS), lambda g, j: (g, 0, 0)),
                  pl.BlockSpec((1, T * LANES, STATE_COLS), lambda g, j: (g, 0, 0)),
                  pl.BlockSpec((1, 2 * SUBLANES, STATE_COLS), lambda g, j: (g, 0, 0))],
        out_specs=pl.BlockSpec((tb, LANES), lambda g, j: (jnp.maximum(j - nb, 0), g)),
        out_shape=jax.ShapeDtypeStruct((S, SSM_WIDTH), f32),
        scratch_shapes=[pltpu.VMEM((STATE_COLS // LANES, S // T, LANES), f32)],
        compiler_params=_params(("arbitrary", "arbitrary")),
        name="ssm_main",
    )(u, d_row, wtoep, wstate, wcarry, apow)


def _outproj_kernel(yna_ref, yssm_ref, ymem_ref, x_ref, wglu_ref, bglu_ref, gssm_ref, wout_ref,
                    gpost_ref, gmlp_ref, x1_ref, h2_ref, *, nsub):
    a, b = NA_WIDTH, NA_WIDTH + SSM_WIDTH
    sub = x_ref.shape[0] // nsub
    tiles = [slice(r * sub, (r + 1) * sub) for r in range(nsub)]
    ys = []
    for rows in tiles:
        ya = yssm_ref[rows, :]
        gate = jax.nn.sigmoid(jnp.dot(ya.astype(bf16), wglu_ref[...], preferred_element_type=f32)
                              + bglu_ref[...])
        ys.append(_rms(ya * gate, gssm_ref[...]).astype(bf16))
    accs = [jnp.dot(yna_ref[rows, :], wout_ref[0:a, :], preferred_element_type=f32)
            + jnp.dot(ys[r], wout_ref[a:b, :], preferred_element_type=f32)
            + jnp.dot(ymem_ref[rows, :], wout_ref[b:, :], preferred_element_type=f32)
            for r, rows in enumerate(tiles)]
    for rows, acc in zip(tiles, accs):
        x1 = x_ref[rows, :] + _rms(acc, gpost_ref[...])
        x1_ref[rows, :] = x1
        h2_ref[rows, :] = _rms(x1, gmlp_ref[...]).astype(bf16)


def _out_proj(y_na, y_ssm, y_mem, x2, wglu_bf, b_glu, g_ssm, wout_bf, g_post, g_mlp, tm=512):
    S = x2.shape[0]
    row = lambda w: pl.BlockSpec((tm, w), lambda i: (i, 0))
    vec = lambda w: pl.BlockSpec((1, w), lambda i: (0, 0))
    return pl.pallas_call(
        functools.partial(_outproj_kernel, nsub=4),
        grid=(S // tm,),
        in_specs=[row(NA_WIDTH), row(SSM_WIDTH), row(MEM_WIDTH), row(D_MODEL),
                  pl.BlockSpec((SSM_WIDTH, SSM_WIDTH), lambda i: (0, 0)), vec(SSM_WIDTH), vec(SSM_WIDTH),
                  pl.BlockSpec((D_MODEL, D_MODEL), lambda i: (0, 0), pipeline_mode=pl.Buffered(1)),
                  vec(D_MODEL), vec(D_MODEL)],
        out_specs=[row(D_MODEL), row(D_MODEL)],
        out_shape=[jax.ShapeDtypeStruct((S, D_MODEL), f32), jax.ShapeDtypeStruct((S, D_MODEL), bf16)],
        compiler_params=_params(("parallel",)),
        name="out_proj",
    )(y_na, y_ssm, y_mem, x2, wglu_bf, b_glu, g_ssm, wout_bf, g_post, g_mlp)


def _mlp_kernel(h_ref, w1_ref, w2_ref, x1_hbm, g_ref, o_ref, hid_scr, x1_buf, x1_sem, *, nsplit):
    i = pl.program_id(0)
    k = pl.program_id(1)
    nk = pl.num_programs(1) - 1
    tm = o_ref.shape[0]
    wn = D_MODEL // nsplit

    def x1_copy():
        return pltpu.make_async_copy(x1_hbm.at[pl.ds(pl.multiple_of(i * tm, tm), tm), :], x1_buf, x1_sem)

    def hidden():
        hid = jnp.dot(h_ref[...], w1_ref[...], preferred_element_type=f32)
        return jnp.square(jnp.maximum(hid, 0.0)).astype(bf16)

    def partial_out(n, first=False):
        cols = slice(n * wn, (n + 1) * wn)
        part = jnp.dot(hid_scr[(k + 1) % 2], w2_ref[:, cols], preferred_element_type=f32)
        return part if first else o_ref[:, cols] + part

    @pl.when(k == 0)
    def _():
        x1_copy().start()
        hid_scr[0] = hidden()

    @pl.when(k == 1)
    def _():
        for n in range(nsplit):
            o_ref[:, n * wn:(n + 1) * wn] = partial_out(n, first=True)
        hid_scr[1] = hidden()

    @pl.when((k > 1) & (k < nk))
    def _():
        for n in range(nsplit):
            o_ref[:, n * wn:(n + 1) * wn] = partial_out(n)
        hid_scr[k % 2] = hidden()

    @pl.when(k == nk)
    def _():
        ssq = jnp.zeros((tm, 1), f32)
        for n in range(nsplit):
            f = partial_out(n)
            o_ref[:, n * wn:(n + 1) * wn] = f
            ssq = ssq + jnp.sum(f * f, axis=-1, keepdims=True)
        scale = lax.rsqrt(ssq * (1.0 / D_MODEL) + EPS)
        x1_copy().wait()
        o_ref[...] = x1_buf[...] + o_ref[...] * scale * g_ref[...]


def _mlp(h2, w1_bf, w2_bf, x1, g, tm=1024, tk=1024):
    S = h2.shape[0]
    nk = D_FF // tk
    return pl.pallas_call(
        functools.partial(_mlp_kernel, nsplit=4),
        grid=(S // tm, nk + 1),
        in_specs=[pl.BlockSpec((tm, D_MODEL), lambda i, k: (i, 0)),
                  pl.BlockSpec((D_MODEL, tk), lambda i, k: (0, jnp.minimum(k, nk - 1))),
                  pl.BlockSpec((tk, D_MODEL), lambda i, k: (jnp.maximum(k - 1, 0), 0)),
                  pl.BlockSpec(memory_space=pl.ANY),
                  pl.BlockSpec((1, D_MODEL), lambda i, k: (0, 0))],
        out_specs=pl.BlockSpec((tm, D_MODEL), lambda i, k: (i, 0)),
        out_shape=jax.ShapeDtypeStruct((S, D_MODEL), f32),
        scratch_shapes=[pltpu.VMEM((2, tm, tk), bf16), pltpu.VMEM((tm, D_MODEL), f32),
                        pltpu.SemaphoreType.DMA(())],
        compiler_params=_params(("arbitrary", "arbitrary")),
        name="mlp",
    )(h2, w1_bf, w2_bf, x1, g)


def kernel(x, mem, norm_mix_pre, w_in, na_rpb, ssm_lam_re, ssm_lam_im, ssm_log_dt, ssm_b_re, ssm_b_im, ssm_c_re, ssm_c_im, ssm_d, w_glu, b_glu, mem_norm, w_mem_kv, out_norm_na, out_norm_ssm, out_norm_mem, w_out, norm_mix_post, norm_mlp_pre, w_ff1, w_ff2, norm_mlp_post):
    B, S, _ = x.shape
    assert B == 1 and x.shape[2] == D_MODEL and S % 4096 == 0 and w_in.shape[0] == 1
    x2 = x[0]
    l = 0
    vec = lambda a: a[l].reshape(1, -1)

    kv = _mem_kv(mem[0], vec(mem_norm), w_mem_kv[l])
    wtoep, wstate, wcarry, apow, win_bf = _ssm_prep(ssm_lam_re[l], ssm_lam_im[l], ssm_log_dt[l], ssm_b_re[l],
                                                    ssm_b_im[l], ssm_c_re[l], ssm_c_im[l], S, w_in[l])
    P, KT, u, y_mem, wglu_bf, wout_bf, w1_bf, w2_bf = _proj(
        x2, vec(norm_mix_pre), win_bf, kv, vec(out_norm_mem), [w_glu[l], w_out[l], w_ff1[l], w_ff2[l]])

    y_na = _na_attn(P, KT, _na_bias(na_rpb[l], S), vec(out_norm_na))

    y_ssm = _ssm_main(u, ssm_d[l].reshape(1, SSM_WIDTH), wtoep, wstate, wcarry, apow)

    x1, h2 = _out_proj(y_na, y_ssm, y_mem, x2, wglu_bf, vec(b_glu), vec(out_norm_ssm),
                       wout_bf, vec(norm_mix_post), vec(norm_mlp_pre))
    out = _mlp(h2, w1_bf, w2_bf, x1, vec(norm_mlp_post))
    return out[None]
```

```python
import functools

import jax
import jax.numpy as jnp
from jax import lax
from jax.experimental import pallas as pl
from jax.experimental.pallas import tpu as pltpu

f32 = jnp.float32
bf16 = jnp.bfloat16

D_MODEL = 2048
GRID_W = 64
WIN_H = 8
WIN_W = 16
HEAD_DIM = 128
NA_WIDTH = 1024
NA_HEADS = 8
SSM_WIDTH = 512
SSM_CH = 16
SSM_STATE = 64
MEM_WIDTH = 512
MEM_HEADS = 4
N_MEM = 256
IN_WIDTH = 4096
D_FF = 8192
EPS = 1e-6
LAM_RE_MAX = -1e-4

LANES = 128
SUBLANES = 8
GROUPS_PER_TILE = LANES // SSM_CH
N_TILES = SSM_WIDTH // LANES
TILE_STATES = GROUPS_PER_TILE * SSM_STATE
STATE_COLS = 4 * TILE_STATES
SSM_T = 8
MASK_VALUE = -1e30
LOG2E = 1.4426950408889634
VMEM_LIMIT = 58 * 1024 * 1024


def _rms(x, g):
    ms = jnp.mean(x * x, axis=-1, keepdims=True)
    return x * lax.rsqrt(ms + EPS) * g


def _params(sem):
    return pltpu.CompilerParams(dimension_semantics=sem, vmem_limit_bytes=VMEM_LIMIT)


def _softmax_pv(s, v):
    p = jnp.exp2(s - jnp.max(s, axis=-1, keepdims=True))
    inv = 1.0 / jnp.sum(p, axis=-1, keepdims=True)
    return jnp.dot(p.astype(bf16), v, preferred_element_type=f32) * inv


def _proj_kernel(x_ref, g_ref, wlo_ref, whi_ref, kv_ref, gmem_ref, *rest, ncast, scale):
    cast_in, rest = rest[:ncast], rest[ncast:]
    p_ref, kt_ref, u_ref, ymem_ref = rest[:4]
    cast_out, h_scr = rest[4:4 + ncast], rest[4 + ncast]
    for src, dst in zip(cast_in, cast_out):
        dst[...] = src[...].astype(bf16)
    nt = (((1,), (1,)), ((), ()))
    x = x_ref[...]
    h_scr[...] = (x * g_ref[...]).astype(bf16)
    r = lax.rsqrt(jnp.mean(x * x, axis=-1, keepdims=True) + EPS)
    half = D_MODEL // 2

    def project(cols):
        return (jnp.dot(h_scr[:, :half], wlo_ref[:, cols], preferred_element_type=f32)
                + jnp.dot(h_scr[:, half:], whi_ref[:, cols], preferred_element_type=f32))

    acc = project(slice(3 * NA_WIDTH, IN_WIDTH)) * r
    u_ref[...] = acc[:, :SSM_WIDTH]
    qm = (acc[:, SSM_WIDTH:] * scale).astype(bf16)
    outs = []
    for h in range(MEM_HEADS):
        sl = slice(h * HEAD_DIM, (h + 1) * HEAD_DIM)
        s = lax.dot_general(qm[:, sl], kv_ref[:, sl], nt, preferred_element_type=f32)
        outs.append(_softmax_pv(s, kv_ref[:, MEM_WIDTH + h * HEAD_DIM:MEM_WIDTH + (h + 1) * HEAD_DIM]))
    ymem_ref[...] = _rms(jnp.concatenate(outs, axis=1), gmem_ref[...]).astype(bf16)

    for j in range(3):
        acc = project(slice(j * NA_WIDTH, (j + 1) * NA_WIDTH))
        if j == 0:
            p_ref[:, :NA_WIDTH] = (acc * (r * scale)).astype(bf16)
        elif j == 1:
            kt_ref[...] = (acc * r).T.astype(bf16)
        else:
            p_ref[:, NA_WIDTH:] = (acc * r).astype(bf16)


def _proj(x2, g, w_lo, w_hi, kv, g_mem, cast_weights, tm=512):
    S = x2.shape[0]
    steps = S // tm
    cast_specs = [pl.BlockSpec((w.shape[0] // steps, w.shape[1]), lambda i: (i, 0)) for w in cast_weights]
    return pl.pallas_call(
        functools.partial(_proj_kernel, ncast=len(cast_weights), scale=HEAD_DIM ** -0.5 * LOG2E),
        grid=(steps,),
        in_specs=[pl.BlockSpec((tm, D_MODEL), lambda i: (i, 0)),
                  pl.BlockSpec((1, D_MODEL), lambda i: (0, 0)),
                  pl.BlockSpec((D_MODEL // 2, IN_WIDTH), lambda i: (0, 0), pipeline_mode=pl.Buffered(1)),
                  pl.BlockSpec((D_MODEL // 2, IN_WIDTH), lambda i: (0, 0), pipeline_mode=pl.Buffered(1)),
                  pl.BlockSpec((N_MEM, 2 * MEM_WIDTH), lambda i: (0, 0)),
                  pl.BlockSpec((1, MEM_WIDTH), lambda i: (0, 0))] + cast_specs,
        out_specs=[pl.BlockSpec((tm, 2 * NA_WIDTH), lambda i: (i, 0)),
                   pl.BlockSpec((NA_WIDTH, tm), lambda i: (0, i)),
                   pl.BlockSpec((tm, SSM_WIDTH), lambda i: (i, 0)),
                   pl.BlockSpec((tm, MEM_WIDTH), lambda i: (i, 0))] + cast_specs,
        out_shape=[jax.ShapeDtypeStruct((S, 2 * NA_WIDTH), bf16),
                   jax.ShapeDtypeStruct((NA_WIDTH, S), bf16),
                   jax.ShapeDtypeStruct((S, SSM_WIDTH), f32),
                   jax.ShapeDtypeStruct((S, MEM_WIDTH), bf16)]
                  + [jax.ShapeDtypeStruct(w.shape, bf16) for w in cast_weights],
        scratch_shapes=[pltpu.VMEM((tm, D_MODEL), bf16)],
        compiler_params=_params(("parallel",)),
        name="proj",
    )(x2, g, w_lo, w_hi, kv, g_mem, *cast_weights)


NA_QB = 128
NA_KB = 5
NA_EDGE = 2
NA_STEP = 4
NA_ROWS_Q = NA_QB // GRID_W
NA_ROWS_WIN = NA_KB * NA_QB // GRID_W


def _na_window_start(b, nb):
    return jnp.clip(b - 2, 0, nb - NA_KB)


def _na_bias_kernel(v_ref, wcast_ref, o_ref, wcast_bf_ref, *, nb, rows):
    wcast_bf_ref[...] = wcast_ref[...].astype(bf16)
    c = lax.broadcasted_iota(jnp.int32, (GRID_W, LANES), 0)
    lane = lax.broadcasted_iota(jnp.int32, (GRID_W, LANES), 1)
    kc = lane % GRID_W
    cs = jnp.clip(c - WIN_W // 2, 0, GRID_W - WIN_W)
    col_ok = (kc >= cs) & (kc < cs + WIN_W)
    half_ok = {(True, True): col_ok, (True, False): col_ok & (lane < GRID_W), (False, True): col_ok & (lane >= GRID_W)}
    masked = jnp.full((GRID_W, LANES), MASK_VALUE, f32)
    variants = list(range(NA_EDGE)) + [NA_EDGE] + list(range(nb - NA_EDGE, nb))
    for vi, b in enumerate(variants):
        wb = min(max(b - 2, 0), nb - NA_KB)
        for rq in range(NA_ROWS_Q):
            r = b * NA_ROWS_Q + rq
            rs = min(max(r - WIN_H // 2, 0), rows - WIN_H)
            for m in range(NA_ROWS_WIN // 2):
                kr = wb * NA_ROWS_Q + 2 * m
                ok = (rs <= kr < rs + WIN_H, rs <= kr + 1 < rs + WIN_H)
                if ok == (False, False):
                    blk = masked
                else:
                    d = kr - r + WIN_H
                    src = jnp.broadcast_to(v_ref[0, d:d + 1, :], (GRID_W, LANES))
                    rolled = pltpu.roll(src, LANES - (WIN_W - 1), 1, stride=1, stride_axis=0)
                    blk = jnp.where(half_ok[ok], rolled * LOG2E, MASK_VALUE)
                o_ref[vi, 0, rq * GRID_W:(rq + 1) * GRID_W, m * LANES:(m + 1) * LANES] = blk


def _na_bias(rpb, S, w_cast):
    heads, nrow, ncol = rpb.shape
    nb = S // NA_QB
    nvar = 2 * NA_EDGE + 1
    padded = jnp.zeros((heads, nrow + 2, GRID_W), f32).at[:, 1:nrow + 1, :ncol].set(rpb.astype(f32))
    pairs = jnp.concatenate([padded[:, :-1], padded[:, 1:]], axis=-1)
    half = w_cast.shape[0] // 2
    cast_in = pl.BlockSpec((half // heads, w_cast.shape[1]), lambda h: (heads + h, 0))
    cast_out = pl.BlockSpec((half // heads, w_cast.shape[1]), lambda h: (h, 0))
    return pl.pallas_call(
        functools.partial(_na_bias_kernel, nb=nb, rows=S // GRID_W),
        grid=(heads,),
        in_specs=[pl.BlockSpec((1, nrow + 1, LANES), lambda h: (h, 0, 0)), cast_in],
        out_specs=[pl.BlockSpec((nvar, 1, NA_QB, NA_KB * NA_QB), lambda h: (0, h, 0, 0)), cast_out],
        out_shape=[jax.ShapeDtypeStruct((nvar, heads, NA_QB, NA_KB * NA_QB), f32),
                   jax.ShapeDtypeStruct((half, w_cast.shape[1]), bf16)],
        compiler_params=_params(("parallel",)),
        name="na_bias",
    )(pairs, w_cast)


def _na_step_window_start(i, nb):
    return jnp.clip(NA_STEP * i - 2, 0, nb - NA_KB - NA_STEP + 1)


def _na_kernel(q_ref, kt_ref, v_ref, *rest, nb):
    tab_refs = rest[:NA_STEP]
    g_ref, o_ref, s_scr, p_scr = rest[NA_STEP:]
    i = pl.program_id(0)
    win0 = _na_step_window_start(i, nb)
    starts = []
    for jb in range(NA_STEP):
        local = _na_window_start(NA_STEP * i + jb, nb) - win0
        starts.append(pl.multiple_of(local * NA_QB, NA_QB))
        for h in range(NA_HEADS):
            sl = slice(h * HEAD_DIM, (h + 1) * HEAD_DIM)
            kt = kt_ref[sl, pl.ds(starts[jb], NA_KB * NA_QB)]
            s_scr[jb, h] = jnp.dot(q_ref[jb * NA_QB:(jb + 1) * NA_QB, sl], kt,
                                   preferred_element_type=f32) + tab_refs[jb][0, h]
    inv = []
    for jb in range(NA_STEP):
        for h in range(NA_HEADS):
            s = s_scr[jb, h]
            p = jnp.exp2(s - jnp.max(s, axis=-1, keepdims=True))
            inv.append(1.0 / jnp.sum(p, axis=-1, keepdims=True))
            p_scr[jb, h] = p.astype(bf16)
    for jb in range(NA_STEP):
        outs = []
        for h in range(NA_HEADS):
            sl = slice(h * HEAD_DIM, (h + 1) * HEAD_DIM)
            v = v_ref[pl.ds(starts[jb], NA_KB * NA_QB), sl]
            outs.append(jnp.dot(p_scr[jb, h], v, preferred_element_type=f32) * inv[jb * NA_HEADS + h])
        o_ref[jb * NA_QB:(jb + 1) * NA_QB, :] = _rms(jnp.concatenate(outs, axis=1), g_ref[...]).astype(bf16)


def _na_attn(P, KT, tab, g):
    S = P.shape[0]
    nb = S // NA_QB
    win = (NA_KB + NA_STEP - 1) * NA_QB

    def variant(b):
        return jnp.where(b < NA_EDGE, b, jnp.where(b >= nb - NA_EDGE, b - (nb - 2 * NA_EDGE - 1), NA_EDGE))

    v_window = pl.BlockSpec((pl.Element(win), pl.Element(NA_WIDTH)),
                            lambda i: (_na_step_window_start(i, nb) * NA_QB, NA_WIDTH))
    kt_window = pl.BlockSpec((pl.Element(NA_WIDTH), pl.Element(win)),
                             lambda i: (0, _na_step_window_start(i, nb) * NA_QB))

    tab_specs = [pl.BlockSpec((1, NA_HEADS, NA_QB, NA_KB * NA_QB),
                              lambda i, jb=jb: (variant(NA_STEP * i + jb), 0, 0, 0)) for jb in range(NA_STEP)]
    return pl.pallas_call(
        functools.partial(_na_kernel, nb=nb),
        grid=(nb // NA_STEP,),
        in_specs=[pl.BlockSpec((NA_STEP * NA_QB, NA_WIDTH), lambda i: (i, 0)),
                  kt_window, v_window] + tab_specs + [
                  pl.BlockSpec((1, NA_WIDTH), lambda i: (0, 0))],
        out_specs=pl.BlockSpec((NA_STEP * NA_QB, NA_WIDTH), lambda i: (i, 0)),
        out_shape=jax.ShapeDtypeStruct((S, NA_WIDTH), bf16),
        scratch_shapes=[pltpu.VMEM((NA_STEP, NA_HEADS, NA_QB, NA_KB * NA_QB), f32),
                        pltpu.VMEM((NA_STEP, NA_HEADS, NA_QB, NA_KB * NA_QB), bf16)],
        compiler_params=_params(("parallel",)),
        name="na_attn",
    )(P, KT, P, *([tab] * NA_STEP), g)


def _memkv_kernel(m_ref, g_ref, w_ref, o_ref):
    o_ref[...] = jnp.dot(_rms(m_ref[...], g_ref[...]).astype(bf16), w_ref[...].astype(bf16),
                         preferred_element_type=f32).astype(bf16)


def _mem_kv(mem2, g, w):
    return pl.pallas_call(
        _memkv_kernel,
        out_shape=jax.ShapeDtypeStruct((N_MEM, 2 * MEM_WIDTH), bf16),
        compiler_params=pltpu.CompilerParams(vmem_limit_bytes=VMEM_LIMIT),
        name="mem_kv",
    )(mem2, g, w)


def _ssm_prep_kernel(lre_r, lim_r, ldt_r, bre_ref, bim_ref, cre_ref, cim_ref, win_ref,
                     wtoep_ref, wstate_ref, wcarry_ref, apow_ref, win_bf_ref, *, seg_len):
    win_bf_ref[...] = win_ref[...].astype(bf16)
    T = SSM_T
    TS = TILE_STATES
    reps = LANES // SSM_CH
    assert T == SUBLANES

    def same_group(shape, row_div, col_div):
        r = lax.broadcasted_iota(jnp.int32, shape, 0) // row_div
        c = lax.broadcasted_iota(jnp.int32, shape, 1) // col_div
        return (r == c).astype(f32)

    mask_state = same_group((LANES, TS), SSM_CH, SSM_STATE)
    mask_chan = same_group((LANES, LANES), SSM_CH, SSM_CH)

    kk = []
    for d in range(2):
        lre = jnp.minimum(lre_r[d, 0], LAM_RE_MAX)
        lim = lim_r[d, 0]
        dt = jnp.exp(ldt_r[d, 0])
        zr = lre * dt
        zi = lim * dt
        er = jnp.exp(zr)
        nr = er * jnp.cos(zi) - 1.0
        ni = er * jnp.sin(zi)
        den = lre * lre + lim * lim
        qr = (nr * lre + ni * lim) / den
        qi = (ni * lre - nr * lim) / den
        bre = bre_ref[d, 0]
        bim = bim_ref[d, 0]
        bbr = qr * bre - qi * bim
        bbi = qr * bim + qi * bre
        e0 = lax.broadcasted_iota(jnp.int32, (T, TS), 0).astype(f32)
        mag0 = jnp.exp(zr * e0)
        p0r, p0i = mag0 * jnp.cos(zi * e0), mag0 * jnp.sin(zi * e0)
        mag1 = jnp.exp(zr * (e0 + 1.0))
        p1r, p1i = mag1 * jnp.cos(zi * (e0 + 1.0)), mag1 * jnp.sin(zi * (e0 + 1.0))
        wc_r = [p0r[e:e + 1] * bbr - p0i[e:e + 1] * bbi for e in range(T)]
        wc_i = [p0r[e:e + 1] * bbi + p0i[e:e + 1] * bbr for e in range(T)]

        for s in range(T):
            e = T - 1 - s if d == 0 else s
            rows = slice(s * LANES, (s + 1) * LANES)
            base = d * 2 * TS
            wstate_ref[0, rows, base:base + TS] = (
                jnp.concatenate([wc_r[e]] * reps, axis=0) * mask_state).astype(bf16)
            wstate_ref[0, rows, base + TS:base + 2 * TS] = (
                jnp.concatenate([wc_i[e]] * reps, axis=0) * mask_state).astype(bf16)

        cre = cre_ref[d, 0]
        cim = cim_ref[d, 0]
        nt = (((1,), (1,)), ((), ()))
        kc = (lax.dot_general(jnp.concatenate(wc_r, axis=0), jnp.concatenate([cre] * reps, axis=0) * mask_state,
                              nt, precision=lax.Precision.HIGHEST, preferred_element_type=f32)
              - lax.dot_general(jnp.concatenate(wc_i, axis=0), jnp.concatenate([cim] * reps, axis=0) * mask_state,
                                nt, precision=lax.Precision.HIGHEST, preferred_element_type=f32))
        kk.append([jnp.concatenate([kc[e * SSM_CH:(e + 1) * SSM_CH]] * reps, axis=0) * mask_chan
                   for e in range(T)])

        for t in range(T):
            e = t if d == 0 else T - 1 - t
            pr, pi = p1r[e:e + 1], p1i[e:e + 1]
            rows = slice(t * LANES, (t + 1) * LANES)
            base = d * 2 * TS
            wcarry_ref[0, rows, base:base + TS] = (
                jnp.concatenate([cre * pr - cim * pi] * reps, axis=0) * mask_state).astype(bf16)
            wcarry_ref[0, rows, base + TS:base + 2 * TS] = (
                jnp.concatenate([-(cre * pi + cim * pr)] * reps, axis=0) * mask_state).astype(bf16)

        for part, steps in enumerate((T, T * seg_len)):
            k1 = (lax.broadcasted_iota(jnp.int32, (SUBLANES, TS), 0) + 1).astype(f32) * float(steps)
            mag = jnp.exp(zr * k1)
            rows = slice(part * SUBLANES, (part + 1) * SUBLANES)
            apow_ref[0, rows, d * 2 * TS:d * 2 * TS + TS] = mag * jnp.cos(zi * k1)
            apow_ref[0, rows, d * 2 * TS + TS:(d + 1) * 2 * TS] = mag * jnp.sin(zi * k1)

    for s in range(T):
        for t in range(T):
            if t > s:
                blk = kk[0][t - s]
            elif t < s:
                blk = kk[1][s - t]
            else:
                blk = kk[0][0] + kk[1][0]
            wtoep_ref[0, s * LANES:(s + 1) * LANES, t * LANES:(t + 1) * LANES] = blk.astype(bf16)


def _ssm_prep(lam_re, lam_im, log_dt, b_re, b_im, c_re, c_im, S, w_in):
    TS, NT, T = TILE_STATES, N_TILES, SSM_T
    gp = GROUPS_PER_TILE
    ldt = jnp.broadcast_to(log_dt[:, :, None], lam_re.shape)
    rows = [a.reshape(2, NT, 1, TS) for a in (lam_re, lam_im, ldt)]

    def b_layout(b):
        return b.reshape(2, NT, gp, SSM_STATE, SSM_CH).transpose(0, 1, 4, 2, 3).reshape(2, NT, SSM_CH, TS)

    def c_layout(c):
        return c.reshape(2, NT, gp, SSM_CH, SSM_STATE).transpose(0, 1, 3, 2, 4).reshape(2, NT, SSM_CH, TS)

    row_spec = pl.BlockSpec((2, 1, 1, TS), lambda g: (0, g, 0, 0))
    bc_spec = pl.BlockSpec((2, 1, SSM_CH, TS), lambda g: (0, g, 0, 0))
    half = w_in.shape[0] // 2
    win_spec = pl.BlockSpec((half // NT, w_in.shape[1]), lambda g: (g, 0))
    return pl.pallas_call(
        functools.partial(_ssm_prep_kernel, seg_len=S // T // SUBLANES),
        grid=(NT,),
        in_specs=[row_spec] * 3 + [bc_spec] * 4 + [win_spec],
        out_specs=[pl.BlockSpec((1, T * LANES, T * LANES), lambda g: (g, 0, 0)),
                   pl.BlockSpec((1, T * LANES, STATE_COLS), lambda g: (g, 0, 0)),
                   pl.BlockSpec((1, T * LANES, STATE_COLS), lambda g: (g, 0, 0)),
                   pl.BlockSpec((1, 2 * SUBLANES, STATE_COLS), lambda g: (g, 0, 0)), win_spec],
        out_shape=[jax.ShapeDtypeStruct((NT, T * LANES, T * LANES), bf16),
                   jax.ShapeDtypeStruct((NT, T * LANES, STATE_COLS), bf16),
                   jax.ShapeDtypeStruct((NT, T * LANES, STATE_COLS), bf16),
                   jax.ShapeDtypeStruct((NT, 2 * SUBLANES, STATE_COLS), f32),
                   jax.ShapeDtypeStruct((half, w_in.shape[1]), bf16)],
        compiler_params=_params(("parallel",)),
        name="ssm_prep",
    )(*rows, b_layout(b_re), b_layout(b_im), c_layout(c_re), c_layout(c_im), w_in)


def _cmul_add(xr, xi, ar, ai, sr, si):
    return xr + ar * sr - ai * si, xi + ar * si + ai * sr


def _scan_rows(xr, xi, apr, api, cr, ci, reverse):
    n = SUBLANES
    rows = lax.broadcasted_iota(jnp.int32, xr.shape, 0)
    for k in (1, 2, 4):
        ar, ai = apr[k - 1:k], api[k - 1:k]
        if reverse:
            sr, si = pltpu.roll(xr, n - k, 0), pltpu.roll(xi, n - k, 0)
            keep = rows < n - k
        else:
            sr, si = pltpu.roll(xr, k, 0), pltpu.roll(xi, k, 0)
            keep = rows >= k
        xr, xi = _cmul_add(xr, xi, ar, ai, jnp.where(keep, sr, 0.0), jnp.where(keep, si, 0.0))
    if reverse:
        pwr = jnp.concatenate([apr[n - 1 - j:n - j] for j in range(n)], axis=0)
        pwi = jnp.concatenate([api[n - 1 - j:n - j] for j in range(n)], axis=0)
    else:
        pwr, pwi = apr, api
    hr, hi = _cmul_add(xr, xi, pwr, pwi, cr, ci)
    if reverse:
        inr = jnp.where(rows == n - 1, cr, pltpu.roll(hr, n - 1, 0))
        ini = jnp.where(rows == n - 1, ci, pltpu.roll(hi, n - 1, 0))
        return inr, ini, hr[0:1], hi[0:1]
    inr = jnp.where(rows == 0, cr, pltpu.roll(hr, 1, 0))
    ini = jnp.where(rows == 0, ci, pltpu.roll(hi, 1, 0))
    return inr, ini, hr[n - 1:n], hi[n - 1:n]


def _ssm_kernel(u_ref, d_ref, wtoep_ref, wstate_ref, wcarry_ref, apow_ref, o_ref, h_scr, *, nb, cb):
    T = SSM_T
    TS = TILE_STATES
    j = pl.program_id(1)

    nseg = SUBLANES
    seg = nb * cb // nseg
    assert cb % seg == 0

    def chunk_rows():
        return [u_ref[pl.ds(t, cb, stride=T), :] for t in range(T)]

    def batch_rows(b):
        return [pl.ds(b * (cb // seg) + part, seg, stride=nseg) for part in range(cb // seg)]

    @pl.when(j < nb)
    def _():
        x = jnp.concatenate([ut.astype(bf16) for ut in chunk_rows()], axis=1)
        res = jnp.dot(x, wstate_ref[0], preferred_element_type=f32)
        for part, rows in enumerate(batch_rows(j)):
            for c in range(STATE_COLS // LANES):
                h_scr[c, rows, :] = res[part * seg:(part + 1) * seg, c * LANES:(c + 1) * LANES]

    @pl.when(j == nb - 1)
    def _():
        tiles = TS // LANES

        def cols(d, ri):
            return slice((2 * d + ri) * TS, (2 * d + ri + 1) * TS)

        def rows_at(p):
            return pl.ds(pl.multiple_of(p * nseg, nseg), nseg)

        def load(rows, d, ri):
            return jnp.concatenate([h_scr[(2 * d + ri) * tiles + c, rows, :] for c in range(tiles)], axis=1)

        def store(rows, d, ri, val):
            for c in range(tiles):
                h_scr[(2 * d + ri) * tiles + c, rows, :] = val[:, c * LANES:(c + 1) * LANES]

        a = [[jnp.broadcast_to(apow_ref[0, 0:1, cols(d, ri)], (nseg, TS)) for ri in range(2)] for d in range(2)]
        zero = jnp.zeros((nseg, TS), f32)

        def step(p, h, write):
            hfr, hfi, hbr, hbi = h
            rf, rb = rows_at(p), rows_at(seg - 1 - p)
            xfr, xfi, xbr, xbi = load(rf, 0, 0), load(rf, 0, 1), load(rb, 1, 0), load(rb, 1, 1)
            if write:
                store(rf, 0, 0, hfr)
                store(rf, 0, 1, hfi)
                store(rb, 1, 0, hbr)
                store(rb, 1, 1, hbi)
            hfr, hfi = _cmul_add(xfr, xfi, a[0][0], a[0][1], hfr, hfi)
            hbr, hbi = _cmul_add(xbr, xbi, a[1][0], a[1][1], hbr, hbi)
            return hfr, hfi, hbr, hbi

        ends = lax.fori_loop(0, seg, lambda p, h: step(p, h, False), (zero, zero, zero, zero))
        z1 = jnp.zeros((1, TS), f32)
        enter = []
        for d in range(2):
            cr, ci, _, _ = _scan_rows(ends[2 * d], ends[2 * d + 1], apow_ref[0, nseg:2 * nseg, cols(d, 0)],
                                      apow_ref[0, nseg:2 * nseg, cols(d, 1)], z1, z1, d == 1)
            enter += [cr, ci]
        lax.fori_loop(0, seg, lambda p, h: step(p, h, True), tuple(enter))

    @pl.when(j >= nb)
    def _():
        us = chunk_rows()
        x = jnp.concatenate([ut.astype(bf16) for ut in us], axis=1)
        hc = jnp.concatenate(
            [jnp.concatenate([h_scr[c, rows, :].astype(bf16) for c in range(STATE_COLS // LANES)], axis=1)
             for rows in batch_rows(j - nb)], axis=0)
        y = (jnp.dot(x, wtoep_ref[0], preferred_element_type=f32)
             + lax.dot_general(hc, wcarry_ref[0], (((1,), (1,)), ((), ())), preferred_element_type=f32))
        for t in range(T):
            yt = y[:, t * LANES:(t + 1) * LANES] + d_ref[...] * us[t]
            o_ref[pl.ds(t, cb, stride=T), :] = jax.nn.gelu(yt)


def _ssm_main(u, d_row, wtoep, wstate, wcarry, apow, tb=4096):
    S = u.shape[0]
    T = SSM_T
    nb = S // tb
    cb = tb // T
    return pl.pallas_call(
        functools.partial(_ssm_kernel, nb=nb, cb=cb),
        grid=(N_TILES, 2 * nb),
        in_specs=[pl.BlockSpec((tb, LANES), lambda g, j: (j % nb, g)),
                  pl.BlockSpec((1, LANES), lambda g, j: (0, g)),
                  pl.BlockSpec((1, T * LANES, T * LANES), lambda g, j: (g, 0, 0)),
                  pl.BlockSpec((1, T * LANES, STATE_COLS), lambda g, j: (g, 0, 0)),
                  pl.BlockSpec((1, T * LANES, STATE_COLS), lambda g, j: (g, 0, 0)),
                  pl.BlockSpec((1, 2 * SUBLANES, STATE_COLS), lambda g, j: (g, 0, 0))],
        out_specs=pl.BlockSpec((tb, LANES), lambda g, j: (jnp.maximum(j - nb, 0), g)),
        out_shape=jax.ShapeDtypeStruct((S, SSM_WIDTH), f32),
        scratch_shapes=[pltpu.VMEM((STATE_COLS // LANES, S // T, LANES), f32)],
        compiler_params=_params(("arbitrary", "arbitrary")),
        name="ssm_main",
    )(u, d_row, wtoep, wstate, wcarry, apow)


def _outproj_kernel(yna_ref, yssm_ref, ymem_ref, x_ref, wglu_ref, bglu_ref, gssm_ref, wout_ref,
                    gpost_ref, gmlp_ref, x1_ref, h2_ref, *, nsub):
    a, b = NA_WIDTH, NA_WIDTH + SSM_WIDTH
    sub = x_ref.shape[0] // nsub
    tiles = [slice(r * sub, (r + 1) * sub) for r in range(nsub)]
    ys = []
    for rows in tiles:
        ya = yssm_ref[rows, :]
        gate = jax.nn.sigmoid(jnp.dot(ya.astype(bf16), wglu_ref[...], preferred_element_type=f32)
                              + bglu_ref[...])
        ys.append(_rms(ya * gate, gssm_ref[...]).astype(bf16))
    accs = [jnp.dot(yna_ref[rows, :], wout_ref[0:a, :], preferred_element_type=f32)
            + jnp.dot(ys[r], wout_ref[a:b, :], preferred_element_type=f32)
            + jnp.dot(ymem_ref[rows, :], wout_ref[b:, :], preferred_element_type=f32)
            for r, rows in enumerate(tiles)]
    for rows, acc in zip(tiles, accs):
        x1 = x_ref[rows, :] + _rms(acc, gpost_ref[...])
        x1_ref[rows, :] = x1
        h2_ref[rows, :] = _rms(x1, gmlp_ref[...]).astype(bf16)


def _out_proj(y_na, y_ssm, y_mem, x2, wglu_bf, b_glu, g_ssm, wout_bf, g_post, g_mlp, tm=512):
    S = x2.shape[0]
    row = lambda w: pl.BlockSpec((tm, w), lambda i: (i, 0))
    vec = lambda w: pl.BlockSpec((1, w), lambda i: (0, 0))
    return pl.pallas_call(
        functools.partial(_outproj_kernel, nsub=4),
        grid=(S // tm,),
        in_specs=[row(NA_WIDTH), row(SSM_WIDTH), row(MEM_WIDTH), row(D_MODEL),
                  pl.BlockSpec((SSM_WIDTH, SSM_WIDTH), lambda i: (0, 0)), vec(SSM_WIDTH), vec(SSM_WIDTH),
                  pl.BlockSpec((D_MODEL, D_MODEL), lambda i: (0, 0), pipeline_mode=pl.Buffered(1)),
                  vec(D_MODEL), vec(D_MODEL)],
        out_specs=[row(D_MODEL), row(D_MODEL)],
        out_shape=[jax.ShapeDtypeStruct((S, D_MODEL), f32), jax.ShapeDtypeStruct((S, D_MODEL), bf16)],
        compiler_params=_params(("parallel",)),
        name="out_proj",
    )(y_na, y_ssm, y_mem, x2, wglu_bf, b_glu, g_ssm, wout_bf, g_post, g_mlp)


def _mlp_kernel(h_ref, w1_ref, w2_ref, x1_hbm, g_ref, o_ref, hid_scr, x1_buf, x1_sem, *, nsplit):
    i = pl.program_id(0)
    k = pl.program_id(1)
    nk = pl.num_programs(1) - 1
    tm = o_ref.shape[0]
    wn = D_MODEL // nsplit

    def x1_copy():
        return pltpu.make_async_copy(x1_hbm.at[pl.ds(pl.multiple_of(i * tm, tm), tm), :], x1_buf, x1_sem)

    def hidden():
        hid = jnp.dot(h_ref[...], w1_ref[...], preferred_element_type=f32)
        return jnp.square(jnp.maximum(hid, 0.0)).astype(bf16)

    def partial_out(n, first=False):
        cols = slice(n * wn, (n + 1) * wn)
        part = jnp.dot(hid_scr[(k + 1) % 2], w2_ref[:, cols], preferred_element_type=f32)
        return part if first else o_ref[:, cols] + part

    @pl.when(k == 0)
    def _():
        x1_copy().start()
        hid_scr[0] = hidden()

    @pl.when(k == 1)
    def _():
        for n in range(nsplit):
            o_ref[:, n * wn:(n + 1) * wn] = partial_out(n, first=True)
        hid_scr[1] = hidden()

    @pl.when((k > 1) & (k < nk))
    def _():
        for n in range(nsplit):
            o_ref[:, n * wn:(n + 1) * wn] = partial_out(n)
        hid_scr[k % 2] = hidden()

    @pl.when(k == nk)
    def _():
        ssq = jnp.zeros((tm, 1), f32)
        for n in range(nsplit):
            f = partial_out(n)
            o_ref[:, n * wn:(n + 1) * wn] = f
            ssq = ssq + jnp.sum(f * f, axis=-1, keepdims=True)
        scale = lax.rsqrt(ssq * (1.0 / D_MODEL) + EPS)
        x1_copy().wait()
        o_ref[...] = x1_buf[...] + o_ref[...] * scale * g_ref[...]


def _mlp(h2, w1_bf, w2_bf, x1, g, tm=1024, tk=1024):
    S = h2.shape[0]
    nk = D_FF // tk
    return pl.pallas_call(
        functools.partial(_mlp_kernel, nsplit=4),
        grid=(S // tm, nk + 1),
        in_specs=[pl.BlockSpec((tm, D_MODEL), lambda i, k: (i, 0)),
                  pl.BlockSpec((D_MODEL, tk), lambda i, k: (0, jnp.minimum(k, nk - 1))),
                  pl.BlockSpec((tk, D_MODEL), lambda i, k: (jnp.maximum(k - 1, 0), 0)),
                  pl.BlockSpec(memory_space=pl.ANY),
                  pl.BlockSpec((1, D_MODEL), lambda i, k: (0, 0))],
        out_specs=pl.BlockSpec((tm, D_MODEL), lambda i, k: (i, 0)),
        out_shape=jax.ShapeDtypeStruct((S, D_MODEL), f32),
        scratch_shapes=[pltpu.VMEM((2, tm, tk), bf16), pltpu.VMEM((tm, D_MODEL), f32),
                        pltpu.SemaphoreType.DMA(())],
        compiler_params=_params(("arbitrary", "arbitrary")),
        name="mlp",
    )(h2, w1_bf, w2_bf, x1, g)


def kernel(x, mem, norm_mix_pre, w_in, na_rpb, ssm_lam_re, ssm_lam_im, ssm_log_dt, ssm_b_re, ssm_b_im, ssm_c_re, ssm_c_im, ssm_d, w_glu, b_glu, mem_norm, w_mem_kv, out_norm_na, out_norm_ssm, out_norm_mem, w_out, norm_mix_post, norm_mlp_pre, w_ff1, w_ff2, norm_mlp_post):
    B, S, _ = x.shape
    assert B == 1 and x.shape[2] == D_MODEL and S % 4096 == 0 and w_in.shape[0] == 1
    x2 = x[0]
    l = 0
    vec = lambda a: a[l].reshape(1, -1)

    kv = _mem_kv(mem[0], vec(mem_norm), w_mem_kv[l])
    wtoep, wstate, wcarry, apow, win_lo = _ssm_prep(ssm_lam_re[l], ssm_lam_im[l], ssm_log_dt[l], ssm_b_re[l],
                                                    ssm_b_im[l], ssm_c_re[l], ssm_c_im[l], S, w_in[l])
    na_tab, win_hi = _na_bias(na_rpb[l], S, w_in[l])
    P, KT, u, y_mem, wglu_bf, wout_bf, w1_bf, w2_bf = _proj(
        x2, vec(norm_mix_pre), win_lo, win_hi, kv, vec(out_norm_mem), [w_glu[l], w_out[l], w_ff1[l], w_ff2[l]])

    y_na = _na_attn(P, KT, na_tab, vec(out_norm_na))

    y_ssm = _ssm_main(u, ssm_d[l].reshape(1, SSM_WIDTH), wtoep, wstate, wcarry, apow)

    x1, h2 = _out_proj(y_na, y_ssm, y_mem, x2, wglu_bf, vec(b_glu), vec(out_norm_ssm),
                       wout_bf, vec(norm_mix_post), vec(norm_mlp_pre))
    out = _mlp(h2, w1_bf, w2_bf, x1, vec(norm_mlp_post))
    return out[None]
```

```python
import functools

import jax
import jax.numpy as jnp
from jax import lax
from jax.experimental import pallas as pl
from jax.experimental.pallas import tpu as pltpu

f32 = jnp.float32
bf16 = jnp.bfloat16

D_MODEL = 2048
GRID_W = 64
WIN_H = 8
WIN_W = 16
HEAD_DIM = 128
NA_WIDTH = 1024
NA_HEADS = 8
SSM_WIDTH = 512
SSM_CH = 16
SSM_GROUPS = 32
SSM_STATE = 64
MEM_WIDTH = 512
MEM_HEADS = 4
N_MEM = 256
IN_WIDTH = 4096
D_FF = 8192
EPS = 1e-6
LAM_RE_MAX = -1e-4

LANES = 128
SUBLANES = 8
GROUPS_PER_TILE = LANES // SSM_CH
N_TILES = SSM_WIDTH // LANES
TILE_STATES = GROUPS_PER_TILE * SSM_STATE
STATE_COLS = 4 * TILE_STATES
SSM_T = 8
MASK_VALUE = -1e30
LOG2E = 1.4426950408889634
VMEM_LIMIT = 58 * 1024 * 1024


def _rms(x, g):
    ms = jnp.mean(x * x, axis=-1, keepdims=True)
    return x * lax.rsqrt(ms + EPS) * g


def _params(sem):
    return pltpu.CompilerParams(dimension_semantics=sem, vmem_limit_bytes=VMEM_LIMIT)


def _softmax_pv(s, v):
    p = jnp.exp2(s - jnp.max(s, axis=-1, keepdims=True))
    inv = 1.0 / jnp.sum(p, axis=-1, keepdims=True)
    return jnp.dot(p.astype(bf16), v, preferred_element_type=f32) * inv


def _proj_kernel(x_ref, g_ref, w_ref, kv_ref, gmem_ref, *rest, ncast, scale):
    cast_in, rest = rest[:ncast], rest[ncast:]
    p_ref, kt_ref, u_ref, ymem_ref = rest[:4]
    cast_out, h_scr = rest[4:4 + ncast], rest[4 + ncast]
    for src, dst in zip(cast_in, cast_out):
        dst[...] = src[...].astype(bf16)
    nt = (((1,), (1,)), ((), ()))
    x = x_ref[...]
    h_scr[...] = (x * g_ref[...]).astype(bf16)
    r = lax.rsqrt(jnp.mean(x * x, axis=-1, keepdims=True) + EPS)

    acc = jnp.dot(h_scr[...], w_ref[:, 3 * NA_WIDTH:], preferred_element_type=f32) * r
    u_ref[...] = acc[:, :SSM_WIDTH]
    qm = (acc[:, SSM_WIDTH:] * scale).astype(bf16)
    outs = []
    for h in range(MEM_HEADS):
        sl = slice(h * HEAD_DIM, (h + 1) * HEAD_DIM)
        s = lax.dot_general(qm[:, sl], kv_ref[:, sl], nt, preferred_element_type=f32)
        outs.append(_softmax_pv(s, kv_ref[:, MEM_WIDTH + h * HEAD_DIM:MEM_WIDTH + (h + 1) * HEAD_DIM]))
    ymem_ref[...] = _rms(jnp.concatenate(outs, axis=1), gmem_ref[...]).astype(bf16)

    for j in range(3):
        acc = jnp.dot(h_scr[...], w_ref[:, j * NA_WIDTH:(j + 1) * NA_WIDTH], preferred_element_type=f32)
        if j == 0:
            p_ref[:, :NA_WIDTH] = (acc * (r * scale)).astype(bf16)
        elif j == 1:
            kt_ref[...] = (acc * r).T.astype(bf16)
        else:
            p_ref[:, NA_WIDTH:] = (acc * r).astype(bf16)


def _proj(x2, g, w_bf, kv, g_mem, cast_weights, tm=512):
    S = x2.shape[0]
    steps = S // tm
    cast_specs = [pl.BlockSpec((w.shape[0] // steps, w.shape[1]), lambda i: (i, 0)) for w in cast_weights]
    return pl.pallas_call(
        functools.partial(_proj_kernel, ncast=len(cast_weights), scale=HEAD_DIM ** -0.5 * LOG2E),
        grid=(steps,),
        in_specs=[pl.BlockSpec((tm, D_MODEL), lambda i: (i, 0)),
                  pl.BlockSpec((1, D_MODEL), lambda i: (0, 0)),
                  pl.BlockSpec((D_MODEL, IN_WIDTH), lambda i: (0, 0), pipeline_mode=pl.Buffered(1)),
                  pl.BlockSpec((N_MEM, 2 * MEM_WIDTH), lambda i: (0, 0)),
                  pl.BlockSpec((1, MEM_WIDTH), lambda i: (0, 0))] + cast_specs,
        out_specs=[pl.BlockSpec((tm, 2 * NA_WIDTH), lambda i: (i, 0)),
                   pl.BlockSpec((NA_WIDTH, tm), lambda i: (0, i)),
                   pl.BlockSpec((tm, SSM_WIDTH), lambda i: (i, 0)),
                   pl.BlockSpec((tm, MEM_WIDTH), lambda i: (i, 0))] + cast_specs,
        out_shape=[jax.ShapeDtypeStruct((S, 2 * NA_WIDTH), bf16),
                   jax.ShapeDtypeStruct((NA_WIDTH, S), bf16),
                   jax.ShapeDtypeStruct((S, SSM_WIDTH), f32),
                   jax.ShapeDtypeStruct((S, MEM_WIDTH), bf16)]
                  + [jax.ShapeDtypeStruct(w.shape, bf16) for w in cast_weights],
        scratch_shapes=[pltpu.VMEM((tm, D_MODEL), bf16)],
        compiler_params=_params(("parallel",)),
        name="proj",
    )(x2, g, w_bf, kv, g_mem, *cast_weights)


NA_QB = 128
NA_KB = 5
NA_EDGE = 2
NA_STEP = 4
NA_ROWS_Q = NA_QB // GRID_W
NA_ROWS_WIN = NA_KB * NA_QB // GRID_W


def _na_window_start(b, nb):
    return jnp.clip(b - 2, 0, nb - NA_KB)


def _na_bias_kernel(v_ref, o_ref, *, nb, rows):
    c = lax.broadcasted_iota(jnp.int32, (GRID_W, LANES), 0)
    lane = lax.broadcasted_iota(jnp.int32, (GRID_W, LANES), 1)
    kc = lane % GRID_W
    cs = jnp.clip(c - WIN_W // 2, 0, GRID_W - WIN_W)
    col_ok = (kc >= cs) & (kc < cs + WIN_W)
    half_ok = {(True, True): col_ok, (True, False): col_ok & (lane < GRID_W), (False, True): col_ok & (lane >= GRID_W)}
    masked = jnp.full((GRID_W, LANES), MASK_VALUE, f32)
    variants = list(range(NA_EDGE)) + [NA_EDGE] + list(range(nb - NA_EDGE, nb))
    for vi, b in enumerate(variants):
        wb = min(max(b - 2, 0), nb - NA_KB)
        for rq in range(NA_ROWS_Q):
            r = b * NA_ROWS_Q + rq
            rs = min(max(r - WIN_H // 2, 0), rows - WIN_H)
            for m in range(NA_ROWS_WIN // 2):
                kr = wb * NA_ROWS_Q + 2 * m
                ok = (rs <= kr < rs + WIN_H, rs <= kr + 1 < rs + WIN_H)
                if ok == (False, False):
                    blk = masked
                else:
                    d = kr - r + WIN_H
                    src = jnp.broadcast_to(v_ref[0, d:d + 1, :], (GRID_W, LANES))
                    rolled = pltpu.roll(src, LANES - (WIN_W - 1), 1, stride=1, stride_axis=0)
                    blk = jnp.where(half_ok[ok], rolled * LOG2E, MASK_VALUE)
                o_ref[vi, 0, rq * GRID_W:(rq + 1) * GRID_W, m * LANES:(m + 1) * LANES] = blk


def _na_bias(rpb, S):
    heads, nrow, ncol = rpb.shape
    nb = S // NA_QB
    nvar = 2 * NA_EDGE + 1
    padded = jnp.zeros((heads, nrow + 2, GRID_W), f32).at[:, 1:nrow + 1, :ncol].set(rpb.astype(f32))
    pairs = jnp.concatenate([padded[:, :-1], padded[:, 1:]], axis=-1)
    return pl.pallas_call(
        functools.partial(_na_bias_kernel, nb=nb, rows=S // GRID_W),
        grid=(heads,),
        in_specs=[pl.BlockSpec((1, nrow + 1, LANES), lambda h: (h, 0, 0))],
        out_specs=pl.BlockSpec((nvar, 1, NA_QB, NA_KB * NA_QB), lambda h: (0, h, 0, 0)),
        out_shape=jax.ShapeDtypeStruct((nvar, heads, NA_QB, NA_KB * NA_QB), f32),
        compiler_params=_params(("parallel",)),
        name="na_bias",
    )(pairs)


def _na_step_window_start(i, nb):
    return jnp.clip(NA_STEP * i - 2, 0, nb - NA_KB - NA_STEP + 1)


def _na_kernel(q_ref, kt_ref, v_ref, *rest, nb):
    tab_refs = rest[:NA_STEP]
    g_ref, o_ref, s_scr, p_scr = rest[NA_STEP:]
    i = pl.program_id(0)
    win0 = _na_step_window_start(i, nb)
    starts = []
    for jb in range(NA_STEP):
        local = _na_window_start(NA_STEP * i + jb, nb) - win0
        starts.append(pl.multiple_of(local * NA_QB, NA_QB))
        for h in range(NA_HEADS):
            sl = slice(h * HEAD_DIM, (h + 1) * HEAD_DIM)
            kt = kt_ref[sl, pl.ds(starts[jb], NA_KB * NA_QB)]
            s_scr[jb, h] = jnp.dot(q_ref[jb * NA_QB:(jb + 1) * NA_QB, sl], kt,
                                   preferred_element_type=f32) + tab_refs[jb][0, h]
    inv = []
    for jb in range(NA_STEP):
        for h in range(NA_HEADS):
            s = s_scr[jb, h]
            p = jnp.exp2(s - jnp.max(s, axis=-1, keepdims=True))
            inv.append(1.0 / jnp.sum(p, axis=-1, keepdims=True))
            p_scr[jb, h] = p.astype(bf16)
    for jb in range(NA_STEP):
        outs = []
        for h in range(NA_HEADS):
            sl = slice(h * HEAD_DIM, (h + 1) * HEAD_DIM)
            v = v_ref[pl.ds(starts[jb], NA_KB * NA_QB), sl]
            outs.append(jnp.dot(p_scr[jb, h], v, preferred_element_type=f32) * inv[jb * NA_HEADS + h])
        o_ref[jb * NA_QB:(jb + 1) * NA_QB, :] = _rms(jnp.concatenate(outs, axis=1), g_ref[...]).astype(bf16)


def _na_attn(P, KT, tab, g):
    S = P.shape[0]
    nb = S // NA_QB
    win = (NA_KB + NA_STEP - 1) * NA_QB

    def variant(b):
        return jnp.where(b < NA_EDGE, b, jnp.where(b >= nb - NA_EDGE, b - (nb - 2 * NA_EDGE - 1), NA_EDGE))

    v_window = pl.BlockSpec((pl.Element(win), pl.Element(NA_WIDTH)),
                            lambda i: (_na_step_window_start(i, nb) * NA_QB, NA_WIDTH))
    kt_window = pl.BlockSpec((pl.Element(NA_WIDTH), pl.Element(win)),
                             lambda i: (0, _na_step_window_start(i, nb) * NA_QB))

    tab_specs = [pl.BlockSpec((1, NA_HEADS, NA_QB, NA_KB * NA_QB),
                              lambda i, jb=jb: (variant(NA_STEP * i + jb), 0, 0, 0)) for jb in range(NA_STEP)]
    return pl.pallas_call(
        functools.partial(_na_kernel, nb=nb),
        grid=(nb // NA_STEP,),
        in_specs=[pl.BlockSpec((NA_STEP * NA_QB, NA_WIDTH), lambda i: (i, 0)),
                  kt_window, v_window] + tab_specs + [
                  pl.BlockSpec((1, NA_WIDTH), lambda i: (0, 0))],
        out_specs=pl.BlockSpec((NA_STEP * NA_QB, NA_WIDTH), lambda i: (i, 0)),
        out_shape=jax.ShapeDtypeStruct((S, NA_WIDTH), bf16),
        scratch_shapes=[pltpu.VMEM((NA_STEP, NA_HEADS, NA_QB, NA_KB * NA_QB), f32),
                        pltpu.VMEM((NA_STEP, NA_HEADS, NA_QB, NA_KB * NA_QB), bf16)],
        compiler_params=_params(("parallel",)),
        name="na_attn",
    )(P, KT, P, *([tab] * NA_STEP), g)


def _memkv_kernel(m_ref, g_ref, w_ref, o_ref):
    o_ref[...] = jnp.dot(_rms(m_ref[...], g_ref[...]).astype(bf16), w_ref[...].astype(bf16),
                         preferred_element_type=f32).astype(bf16)


def _mem_kv(mem2, g, w):
    return pl.pallas_call(
        _memkv_kernel,
        out_shape=jax.ShapeDtypeStruct((N_MEM, 2 * MEM_WIDTH), bf16),
        compiler_params=pltpu.CompilerParams(vmem_limit_bytes=VMEM_LIMIT),
        name="mem_kv",
    )(mem2, g, w)


def _ssm_prep_kernel(lre_r, lim_r, ldt_r, bre_ref, bim_ref, cre_ref, cim_ref, win_ref,
                     wtoep_ref, wstate_ref, wcarry_ref, apow_ref, win_bf_ref, *, seg_len):
    win_bf_ref[...] = win_ref[...].astype(bf16)
    T = SSM_T
    TS = TILE_STATES
    reps = LANES // SSM_CH
    assert T == SUBLANES

    def same_group(shape, row_div, col_div):
        r = lax.broadcasted_iota(jnp.int32, shape, 0) // row_div
        c = lax.broadcasted_iota(jnp.int32, shape, 1) // col_div
        return (r == c).astype(f32)

    mask_state = same_group((LANES, TS), SSM_CH, SSM_STATE)
    mask_chan = same_group((LANES, LANES), SSM_CH, SSM_CH)

    kk = []
    for d in range(2):
        lre = jnp.minimum(lre_r[d, 0], LAM_RE_MAX)
        lim = lim_r[d, 0]
        dt = jnp.exp(ldt_r[d, 0])
        zr = lre * dt
        zi = lim * dt
        er = jnp.exp(zr)
        nr = er * jnp.cos(zi) - 1.0
        ni = er * jnp.sin(zi)
        den = lre * lre + lim * lim
        qr = (nr * lre + ni * lim) / den
        qi = (ni * lre - nr * lim) / den
        bre = bre_ref[d, 0]
        bim = bim_ref[d, 0]
        bbr = qr * bre - qi * bim
        bbi = qr * bim + qi * bre
        e0 = lax.broadcasted_iota(jnp.int32, (T, TS), 0).astype(f32)
        mag0 = jnp.exp(zr * e0)
        p0r, p0i = mag0 * jnp.cos(zi * e0), mag0 * jnp.sin(zi * e0)
        mag1 = jnp.exp(zr * (e0 + 1.0))
        p1r, p1i = mag1 * jnp.cos(zi * (e0 + 1.0)), mag1 * jnp.sin(zi * (e0 + 1.0))
        wc_r = [p0r[e:e + 1] * bbr - p0i[e:e + 1] * bbi for e in range(T)]
        wc_i = [p0r[e:e + 1] * bbi + p0i[e:e + 1] * bbr for e in range(T)]

        for s in range(T):
            e = T - 1 - s if d == 0 else s
            rows = slice(s * LANES, (s + 1) * LANES)
            base = d * 2 * TS
            wstate_ref[0, rows, base:base + TS] = (
                jnp.concatenate([wc_r[e]] * reps, axis=0) * mask_state).astype(bf16)
            wstate_ref[0, rows, base + TS:base + 2 * TS] = (
                jnp.concatenate([wc_i[e]] * reps, axis=0) * mask_state).astype(bf16)

        cre = cre_ref[d, 0]
        cim = cim_ref[d, 0]
        nt = (((1,), (1,)), ((), ()))
        kc = (lax.dot_general(jnp.concatenate(wc_r, axis=0), jnp.concatenate([cre] * reps, axis=0) * mask_state,
                              nt, precision=lax.Precision.HIGHEST, preferred_element_type=f32)
              - lax.dot_general(jnp.concatenate(wc_i, axis=0), jnp.concatenate([cim] * reps, axis=0) * mask_state,
                                nt, precision=lax.Precision.HIGHEST, preferred_element_type=f32))
        kk.append([jnp.concatenate([kc[e * SSM_CH:(e + 1) * SSM_CH]] * reps, axis=0) * mask_chan
                   for e in range(T)])

        for t in range(T):
            e = t if d == 0 else T - 1 - t
            pr, pi = p1r[e:e + 1], p1i[e:e + 1]
            rows = slice(t * LANES, (t + 1) * LANES)
            base = d * 2 * TS
            wcarry_ref[0, rows, base:base + TS] = (
                jnp.concatenate([cre * pr - cim * pi] * reps, axis=0) * mask_state).astype(bf16)
            wcarry_ref[0, rows, base + TS:base + 2 * TS] = (
                jnp.concatenate([-(cre * pi + cim * pr)] * reps, axis=0) * mask_state).astype(bf16)

        for part, steps in enumerate((T, T * seg_len)):
            k1 = (lax.broadcasted_iota(jnp.int32, (SUBLANES, TS), 0) + 1).astype(f32) * float(steps)
            mag = jnp.exp(zr * k1)
            rows = slice(part * SUBLANES, (part + 1) * SUBLANES)
            apow_ref[0, rows, d * 2 * TS:d * 2 * TS + TS] = mag * jnp.cos(zi * k1)
            apow_ref[0, rows, d * 2 * TS + TS:(d + 1) * 2 * TS] = mag * jnp.sin(zi * k1)

    for s in range(T):
        for t in range(T):
            if t > s:
                blk = kk[0][t - s]
            elif t < s:
                blk = kk[1][s - t]
            else:
                blk = kk[0][0] + kk[1][0]
            wtoep_ref[0, s * LANES:(s + 1) * LANES, t * LANES:(t + 1) * LANES] = blk.astype(bf16)


def _ssm_prep(lam_re, lam_im, log_dt, b_re, b_im, c_re, c_im, S, w_in):
    TS, NT, T = TILE_STATES, N_TILES, SSM_T
    gp = GROUPS_PER_TILE
    ldt = jnp.broadcast_to(log_dt[:, :, None], lam_re.shape)
    rows = [a.reshape(2, NT, 1, TS) for a in (lam_re, lam_im, ldt)]

    def b_layout(b):
        return b.reshape(2, NT, gp, SSM_STATE, SSM_CH).transpose(0, 1, 4, 2, 3).reshape(2, NT, SSM_CH, TS)

    def c_layout(c):
        return c.reshape(2, NT, gp, SSM_CH, SSM_STATE).transpose(0, 1, 3, 2, 4).reshape(2, NT, SSM_CH, TS)

    row_spec = pl.BlockSpec((2, 1, 1, TS), lambda g: (0, g, 0, 0))
    bc_spec = pl.BlockSpec((2, 1, SSM_CH, TS), lambda g: (0, g, 0, 0))
    win_spec = pl.BlockSpec((w_in.shape[0] // NT, w_in.shape[1]), lambda g: (g, 0))
    return pl.pallas_call(
        functools.partial(_ssm_prep_kernel, seg_len=S // T // SUBLANES),
        grid=(NT,),
        in_specs=[row_spec] * 3 + [bc_spec] * 4 + [win_spec],
        out_specs=[pl.BlockSpec((1, T * LANES, T * LANES), lambda g: (g, 0, 0)),
                   pl.BlockSpec((1, T * LANES, STATE_COLS), lambda g: (g, 0, 0)),
                   pl.BlockSpec((1, T * LANES, STATE_COLS), lambda g: (g, 0, 0)),
                   pl.BlockSpec((1, 2 * SUBLANES, STATE_COLS), lambda g: (g, 0, 0)), win_spec],
        out_shape=[jax.ShapeDtypeStruct((NT, T * LANES, T * LANES), bf16),
                   jax.ShapeDtypeStruct((NT, T * LANES, STATE_COLS), bf16),
                   jax.ShapeDtypeStruct((NT, T * LANES, STATE_COLS), bf16),
                   jax.ShapeDtypeStruct((NT, 2 * SUBLANES, STATE_COLS), f32),
                   jax.ShapeDtypeStruct(w_in.shape, bf16)],
        compiler_params=_params(("parallel",)),
        name="ssm_prep",
    )(*rows, b_layout(b_re), b_layout(b_im), c_layout(c_re), c_layout(c_im), w_in)


def _cmul_add(xr, xi, ar, ai, sr, si):
    return xr + ar * sr - ai * si, xi + ar * si + ai * sr


def _scan_rows(xr, xi, apr, api, cr, ci, reverse):
    n = SUBLANES
    rows = lax.broadcasted_iota(jnp.int32, xr.shape, 0)
    for k in (1, 2, 4):
        ar, ai = apr[k - 1:k], api[k - 1:k]
        if reverse:
            sr, si = pltpu.roll(xr, n - k, 0), pltpu.roll(xi, n - k, 0)
            keep = rows < n - k
        else:
            sr, si = pltpu.roll(xr, k, 0), pltpu.roll(xi, k, 0)
            keep = rows >= k
        xr, xi = _cmul_add(xr, xi, ar, ai, jnp.where(keep, sr, 0.0), jnp.where(keep, si, 0.0))
    if reverse:
        pwr = jnp.concatenate([apr[n - 1 - j:n - j] for j in range(n)], axis=0)
        pwi = jnp.concatenate([api[n - 1 - j:n - j] for j in range(n)], axis=0)
    else:
        pwr, pwi = apr, api
    hr, hi = _cmul_add(xr, xi, pwr, pwi, cr, ci)
    if reverse:
        inr = jnp.where(rows == n - 1, cr, pltpu.roll(hr, n - 1, 0))
        ini = jnp.where(rows == n - 1, ci, pltpu.roll(hi, n - 1, 0))
        return inr, ini, hr[0:1], hi[0:1]
    inr = jnp.where(rows == 0, cr, pltpu.roll(hr, 1, 0))
    ini = jnp.where(rows == 0, ci, pltpu.roll(hi, 1, 0))
    return inr, ini, hr[n - 1:n], hi[n - 1:n]


def _ssm_kernel(u_ref, d_ref, wtoep_ref, wstate_ref, wcarry_ref, apow_ref, o_ref, h_scr, *, nb, cb):
    T = SSM_T
    TS = TILE_STATES
    j = pl.program_id(1)

    nseg = SUBLANES
    seg = nb * cb // nseg
    assert cb % seg == 0

    def chunk_rows():
        return [u_ref[pl.ds(t, cb, stride=T), :] for t in range(T)]

    def batch_rows(b):
        return [pl.ds(b * (cb // seg) + part, seg, stride=nseg) for part in range(cb // seg)]

    @pl.when(j < nb)
    def _():
        x = jnp.concatenate([ut.astype(bf16) for ut in chunk_rows()], axis=1)
        res = jnp.dot(x, wstate_ref[0], preferred_element_type=f32)
        for part, rows in enumerate(batch_rows(j)):
            for c in range(STATE_COLS // LANES):
                h_scr[c, rows, :] = res[part * seg:(part + 1) * seg, c * LANES:(c + 1) * LANES]

    @pl.when(j == nb - 1)
    def _():
        tiles = TS // LANES

        def cols(d, ri):
            return slice((2 * d + ri) * TS, (2 * d + ri + 1) * TS)

        def rows_at(p):
            return pl.ds(pl.multiple_of(p * nseg, nseg), nseg)

        def load(rows, d, ri):
            return jnp.concatenate([h_scr[(2 * d + ri) * tiles + c, rows, :] for c in range(tiles)], axis=1)

        def store(rows, d, ri, val):
            for c in range(tiles):
                h_scr[(2 * d + ri) * tiles + c, rows, :] = val[:, c * LANES:(c + 1) * LANES]

        a = [[jnp.broadcast_to(apow_ref[0, 0:1, cols(d, ri)], (nseg, TS)) for ri in range(2)] for d in range(2)]
        zero = jnp.zeros((nseg, TS), f32)

        def step(p, h, write):
            hfr, hfi, hbr, hbi = h
            rf, rb = rows_at(p), rows_at(seg - 1 - p)
            xfr, xfi, xbr, xbi = load(rf, 0, 0), load(rf, 0, 1), load(rb, 1, 0), load(rb, 1, 1)
            if write:
                store(rf, 0, 0, hfr)
                store(rf, 0, 1, hfi)
                store(rb, 1, 0, hbr)
                store(rb, 1, 1, hbi)
            hfr, hfi = _cmul_add(xfr, xfi, a[0][0], a[0][1], hfr, hfi)
            hbr, hbi = _cmul_add(xbr, xbi, a[1][0], a[1][1], hbr, hbi)
            return hfr, hfi, hbr, hbi

        ends = lax.fori_loop(0, seg, lambda p, h: step(p, h, False), (zero, zero, zero, zero))
        z1 = jnp.zeros((1, TS), f32)
        enter = []
        for d in range(2):
            cr, ci, _, _ = _scan_rows(ends[2 * d], ends[2 * d + 1], apow_ref[0, nseg:2 * nseg, cols(d, 0)],
                                      apow_ref[0, nseg:2 * nseg, cols(d, 1)], z1, z1, d == 1)
            enter += [cr, ci]
        lax.fori_loop(0, seg, lambda p, h: step(p, h, True), tuple(enter))

    @pl.when(j >= nb)
    def _():
        us = chunk_rows()
        x = jnp.concatenate([ut.astype(bf16) for ut in us], axis=1)
        hc = jnp.concatenate(
            [jnp.concatenate([h_scr[c, rows, :].astype(bf16) for c in range(STATE_COLS // LANES)], axis=1)
             for rows in batch_rows(j - nb)], axis=0)
        y = (jnp.dot(x, wtoep_ref[0], preferred_element_type=f32)
             + lax.dot_general(hc, wcarry_ref[0], (((1,), (1,)), ((), ())), preferred_element_type=f32))
        for t in range(T):
            yt = y[:, t * LANES:(t + 1) * LANES] + d_ref[...] * us[t]
            o_ref[pl.ds(t, cb, stride=T), :] = jax.nn.gelu(yt)


def _ssm_main(u, d_row, wtoep, wstate, wcarry, apow, tb=4096):
    S = u.shape[0]
    T = SSM_T
    nb = S // tb
    cb = tb // T
    return pl.pallas_call(
        functools.partial(_ssm_kernel, nb=nb, cb=cb),
        grid=(N_TILES, 2 * nb),
        in_specs=[pl.BlockSpec((tb, LANES), lambda g, j: (j % nb, g)),
                  pl.BlockSpec((1, LANES), lambda g, j: (0, g)),
                  pl.BlockSpec((1, T * LANES, T * LANES), lambda g, j: (g, 0, 0)),
                  pl.BlockSpec((1, T * LANES, STATE_COLS), lambda g, j: (g, 0, 0)),
                  pl.BlockSpec((1, T * LANES, STATE_COLS), lambda g, j: (g, 0, 0)),
                  pl.BlockSpec((1, 2 * SUBLANES, STATE_COLS), lambda g, j: (g, 0, 0))],
        out_specs=pl.BlockSpec((tb, LANES), lambda g, j: (jnp.maximum(j - nb, 0), g)),
        out_shape=jax.ShapeDtypeStruct((S, SSM_WIDTH), f32),
        scratch_shapes=[pltpu.VMEM((STATE_COLS // LANES, S // T, LANES), f32)],
        compiler_params=_params(("arbitrary", "arbitrary")),
        name="ssm_main",
    )(u, d_row, wtoep, wstate, wcarry, apow)


def _outproj_kernel(yna_ref, yssm_ref, ymem_ref, x_ref, wglu_ref, bglu_ref, gssm_ref, wout_ref,
                    gpost_ref, gmlp_ref, x1_ref, h2_ref, *, nsub):
    a, b = NA_WIDTH, NA_WIDTH + SSM_WIDTH
    sub = x_ref.shape[0] // nsub
    tiles = [slice(r * sub, (r + 1) * sub) for r in range(nsub)]
    ys = []
    for rows in tiles:
        ya = yssm_ref[rows, :]
        gate = jax.nn.sigmoid(jnp.dot(ya.astype(bf16), wglu_ref[...], preferred_element_type=f32)
                              + bglu_ref[...])
        ys.append(_rms(ya * gate, gssm_ref[...]).astype(bf16))
    accs = [jnp.dot(yna_ref[rows, :], wout_ref[0:a, :], preferred_element_type=f32)
            + jnp.dot(ys[r], wout_ref[a:b, :], preferred_element_type=f32)
            + jnp.dot(ymem_ref[rows, :], wout_ref[b:, :], preferred_element_type=f32)
            for r, rows in enumerate(tiles)]
    for rows, acc in zip(tiles, accs):
        x1 = x_ref[rows, :] + _rms(acc, gpost_ref[...])
        x1_ref[rows, :] = x1
        h2_ref[rows, :] = _rms(x1, gmlp_ref[...]).astype(bf16)


def _out_proj(y_na, y_ssm, y_mem, x2, wglu_bf, b_glu, g_ssm, wout_bf, g_post, g_mlp, tm=512):
    S = x2.shape[0]
    row = lambda w: pl.BlockSpec((tm, w), lambda i: (i, 0))
    vec = lambda w: pl.BlockSpec((1, w), lambda i: (0, 0))
    return pl.pallas_call(
        functools.partial(_outproj_kernel, nsub=4),
        grid=(S // tm,),
        in_specs=[row(NA_WIDTH), row(SSM_WIDTH), row(MEM_WIDTH), row(D_MODEL),
                  pl.BlockSpec((SSM_WIDTH, SSM_WIDTH), lambda i: (0, 0)), vec(SSM_WIDTH), vec(SSM_WIDTH),
                  pl.BlockSpec((D_MODEL, D_MODEL), lambda i: (0, 0), pipeline_mode=pl.Buffered(1)),
                  vec(D_MODEL), vec(D_MODEL)],
        out_specs=[row(D_MODEL), row(D_MODEL)],
        out_shape=[jax.ShapeDtypeStruct((S, D_MODEL), f32), jax.ShapeDtypeStruct((S, D_MODEL), bf16)],
        compiler_params=_params(("parallel",)),
        name="out_proj",
    )(y_na, y_ssm, y_mem, x2, wglu_bf, b_glu, g_ssm, wout_bf, g_post, g_mlp)


def _mlp_step(h_ref, w1_ref, w2_ref, g_ref, o_ref, *, x1_hbm, hid_scr, x1_buf, x1_sem, step_ref, nk, nsplit):
    step = step_ref[0]
    step_ref[0] = step + 1
    i, k = step // (nk + 1), step % (nk + 1)
    tm = o_ref.shape[0]
    wn = D_MODEL // nsplit

    def x1_copy():
        return pltpu.make_async_copy(x1_hbm.at[pl.ds(pl.multiple_of(i * tm, tm), tm), :], x1_buf, x1_sem)

    def hidden():
        hid = jnp.dot(h_ref[...], w1_ref[...], preferred_element_type=f32)
        return jnp.square(jnp.maximum(hid, 0.0)).astype(bf16)

    def partial_out(n, first=False):
        cols = slice(n * wn, (n + 1) * wn)
        part = jnp.dot(hid_scr[(k + 1) % 2], w2_ref[:, cols], preferred_element_type=f32)
        return part if first else o_ref[:, cols] + part

    @pl.when(k == 0)
    def _():
        x1_copy().start()
        hid_scr[0] = hidden()

    @pl.when(k == 1)
    def _():
        for n in range(nsplit):
            o_ref[:, n * wn:(n + 1) * wn] = partial_out(n, first=True)
        hid_scr[1] = hidden()

    @pl.when((k > 1) & (k < nk))
    def _():
        for n in range(nsplit):
            o_ref[:, n * wn:(n + 1) * wn] = partial_out(n)
        hid_scr[k % 2] = hidden()

    @pl.when(k == nk)
    def _():
        ssq = jnp.zeros((tm, 1), f32)
        for n in range(nsplit):
            f = partial_out(n)
            o_ref[:, n * wn:(n + 1) * wn] = f
            ssq = ssq + jnp.sum(f * f, axis=-1, keepdims=True)
        scale = lax.rsqrt(ssq * (1.0 / D_MODEL) + EPS)
        x1_copy().wait()
        o_ref[...] = x1_buf[...] + o_ref[...] * scale * g_ref[...]


def _mlp_kernel(h_hbm, w1_hbm, w2_hbm, x1_hbm, g_hbm, o_hbm, hid_scr, x1_buf, x1_sem, step_ref, *, tm, tk, nsplit):
    S = h_hbm.shape[0]
    nk = D_FF // tk
    step_ref[0] = 0
    pltpu.emit_pipeline(
        functools.partial(_mlp_step, x1_hbm=x1_hbm, hid_scr=hid_scr, x1_buf=x1_buf, x1_sem=x1_sem,
                          step_ref=step_ref, nk=nk, nsplit=nsplit),
        grid=(S // tm, nk + 1),
        in_specs=[pl.BlockSpec((tm, D_MODEL), lambda i, k: (i, 0)),
                  pl.BlockSpec((D_MODEL, tk), lambda i, k: (0, jnp.minimum(k, nk - 1))),
                  pl.BlockSpec((tk, D_MODEL), lambda i, k: (jnp.maximum(k - 1, 0), 0)),
                  pl.BlockSpec((1, D_MODEL), lambda i, k: (0, 0))],
        out_specs=[pl.BlockSpec((tm, D_MODEL), lambda i, k: (i, 0))],
    )(h_hbm, w1_hbm, w2_hbm, g_hbm, o_hbm)


def _mlp(h2, w1_bf, w2_bf, x1, g, tm=1024, tk=1024):
    S = h2.shape[0]
    any_spec = pl.BlockSpec(memory_space=pl.ANY)
    return pl.pallas_call(
        functools.partial(_mlp_kernel, tm=tm, tk=tk, nsplit=4),
        in_specs=[any_spec] * 5,
        out_specs=any_spec,
        out_shape=jax.ShapeDtypeStruct((S, D_MODEL), f32),
        scratch_shapes=[pltpu.VMEM((2, tm, tk), bf16), pltpu.VMEM((tm, D_MODEL), f32),
                        pltpu.SemaphoreType.DMA(()), pltpu.SMEM((1,), jnp.int32)],
        compiler_params=pltpu.CompilerParams(vmem_limit_bytes=VMEM_LIMIT),
        name="mlp",
    )(h2, w1_bf, w2_bf, x1, g)


def kernel(x, mem, norm_mix_pre, w_in, na_rpb, ssm_lam_re, ssm_lam_im, ssm_log_dt, ssm_b_re, ssm_b_im, ssm_c_re, ssm_c_im, ssm_d, w_glu, b_glu, mem_norm, w_mem_kv, out_norm_na, out_norm_ssm, out_norm_mem, w_out, norm_mix_post, norm_mlp_pre, w_ff1, w_ff2, norm_mlp_post):
    B, S, _ = x.shape
    assert B == 1 and x.shape[2] == D_MODEL and S % 4096 == 0 and w_in.shape[0] == 1
    x2 = x[0]
    l = 0
    vec = lambda a: a[l].reshape(1, -1)

    kv = _mem_kv(mem[0], vec(mem_norm), w_mem_kv[l])
    wtoep, wstate, wcarry, apow, win_bf = _ssm_prep(ssm_lam_re[l], ssm_lam_im[l], ssm_log_dt[l], ssm_b_re[l],
                                                    ssm_b_im[l], ssm_c_re[l], ssm_c_im[l], S, w_in[l])
    P, KT, u, y_mem, wglu_bf, wout_bf, w1_bf, w2_bf = _proj(
        x2, vec(norm_mix_pre), win_bf, kv, vec(out_norm_mem), [w_glu[l], w_out[l], w_ff1[l], w_ff2[l]])

    y_na = _na_attn(P, KT, _na_bias(na_rpb[l], S), vec(out_norm_na))

    y_ssm = _ssm_main(u, ssm_d[l].reshape(1, SSM_WIDTH), wtoep, wstate, wcarry, apow)

    x1, h2 = _out_proj(y_na, y_ssm, y_mem, x2, wglu_bf, vec(b_glu), vec(out_norm_ssm),
                       wout_bf, vec(norm_mix_post), vec(norm_mlp_pre))
    out = _mlp(h2, w1_bf, w2_bf, x1, vec(norm_mlp_post))
    return out[None]
```

```python
import functools

import jax
import jax.numpy as jnp
from jax import lax
from jax.experimental import pallas as pl
from jax.experimental.pallas import tpu as pltpu

f32 = jnp.float32
bf16 = jnp.bfloat16

D_MODEL = 2048
GRID_W = 64
WIN_H = 8
WIN_W = 16
HEAD_DIM = 128
NA_WIDTH = 1024
NA_HEADS = 8
SSM_WIDTH = 512
SSM_CH = 16
SSM_GROUPS = 32
SSM_STATE = 64
MEM_WIDTH = 512
MEM_HEADS = 4
N_MEM = 256
IN_WIDTH = 4096
D_FF = 8192
EPS = 1e-6
LAM_RE_MAX = -1e-4

LANES = 128
SUBLANES = 8
GROUPS_PER_TILE = LANES // SSM_CH
N_TILES = SSM_WIDTH // LANES
TILE_STATES = GROUPS_PER_TILE * SSM_STATE
STATE_COLS = 4 * TILE_STATES
SSM_T = 8
MASK_VALUE = -1e30
LOG2E = 1.4426950408889634
VMEM_LIMIT = 58 * 1024 * 1024


def _rms(x, g):
    ms = jnp.mean(x * x, axis=-1, keepdims=True)
    return x * lax.rsqrt(ms + EPS) * g


def _params(sem):
    return pltpu.CompilerParams(dimension_semantics=sem, vmem_limit_bytes=VMEM_LIMIT)


def _softmax_pv(s, v):
    p = jnp.exp2(s - jnp.max(s, axis=-1, keepdims=True))
    inv = 1.0 / jnp.sum(p, axis=-1, keepdims=True)
    return jnp.dot(p.astype(bf16), v, preferred_element_type=f32) * inv


def _proj_kernel(x_ref, g_ref, w_ref, kv_ref, gmem_ref, *rest, ncast, scale):
    cast_in, rest = rest[:ncast], rest[ncast:]
    p_ref, kt_ref, u_ref, ymem_ref = rest[:4]
    cast_out, h_scr = rest[4:4 + ncast], rest[4 + ncast]
    for src, dst in zip(cast_in, cast_out):
        dst[...] = src[...].astype(bf16)
    nt = (((1,), (1,)), ((), ()))
    x = x_ref[...]
    h_scr[...] = (x * g_ref[...]).astype(bf16)
    r = lax.rsqrt(jnp.mean(x * x, axis=-1, keepdims=True) + EPS)

    acc = jnp.dot(h_scr[...], w_ref[:, 3 * NA_WIDTH:], preferred_element_type=f32) * r
    for t in range(N_TILES):
        u_ref[t] = acc[:, t * LANES:(t + 1) * LANES]
    qm = (acc[:, SSM_WIDTH:] * scale).astype(bf16)
    outs = []
    for h in range(MEM_HEADS):
        sl = slice(h * HEAD_DIM, (h + 1) * HEAD_DIM)
        s = lax.dot_general(qm[:, sl], kv_ref[:, sl], nt, preferred_element_type=f32)
        outs.append(_softmax_pv(s, kv_ref[:, MEM_WIDTH + h * HEAD_DIM:MEM_WIDTH + (h + 1) * HEAD_DIM]))
    ymem_ref[...] = _rms(jnp.concatenate(outs, axis=1), gmem_ref[...]).astype(bf16)

    for j in range(3):
        acc = jnp.dot(h_scr[...], w_ref[:, j * NA_WIDTH:(j + 1) * NA_WIDTH], preferred_element_type=f32)
        if j == 0:
            p_ref[:, :NA_WIDTH] = (acc * (r * scale)).astype(bf16)
        elif j == 1:
            kt_ref[...] = (acc * r).T.astype(bf16)
        else:
            p_ref[:, NA_WIDTH:] = (acc * r).astype(bf16)


def _proj(x2, g, w_bf, kv, g_mem, cast_weights, tm=512):
    S = x2.shape[0]
    steps = S // tm
    cast_specs = [pl.BlockSpec((w.shape[0] // steps, w.shape[1]), lambda i: (i, 0)) for w in cast_weights]
    return pl.pallas_call(
        functools.partial(_proj_kernel, ncast=len(cast_weights), scale=HEAD_DIM ** -0.5 * LOG2E),
        grid=(steps,),
        in_specs=[pl.BlockSpec((tm, D_MODEL), lambda i: (i, 0)),
                  pl.BlockSpec((1, D_MODEL), lambda i: (0, 0)),
                  pl.BlockSpec((D_MODEL, IN_WIDTH), lambda i: (0, 0), pipeline_mode=pl.Buffered(1)),
                  pl.BlockSpec((N_MEM, 2 * MEM_WIDTH), lambda i: (0, 0)),
                  pl.BlockSpec((1, MEM_WIDTH), lambda i: (0, 0))] + cast_specs,
        out_specs=[pl.BlockSpec((tm, 2 * NA_WIDTH), lambda i: (i, 0)),
                   pl.BlockSpec((NA_WIDTH, tm), lambda i: (0, i)),
                   pl.BlockSpec((N_TILES, tm, LANES), lambda i: (0, i, 0)),
                   pl.BlockSpec((tm, MEM_WIDTH), lambda i: (i, 0))] + cast_specs,
        out_shape=[jax.ShapeDtypeStruct((S, 2 * NA_WIDTH), bf16),
                   jax.ShapeDtypeStruct((NA_WIDTH, S), bf16),
                   jax.ShapeDtypeStruct((N_TILES, S, LANES), f32),
                   jax.ShapeDtypeStruct((S, MEM_WIDTH), bf16)]
                  + [jax.ShapeDtypeStruct(w.shape, bf16) for w in cast_weights],
        scratch_shapes=[pltpu.VMEM((tm, D_MODEL), bf16)],
        compiler_params=_params(("parallel",)),
        name="proj",
    )(x2, g, w_bf, kv, g_mem, *cast_weights)


NA_QB = 128
NA_KB = 5
NA_EDGE = 2
NA_STEP = 4
NA_ROWS_Q = NA_QB // GRID_W
NA_ROWS_WIN = NA_KB * NA_QB // GRID_W


def _na_window_start(b, nb):
    return jnp.clip(b - 2, 0, nb - NA_KB)


def _na_bias_kernel(v_ref, o_ref, *, nb, rows):
    c = lax.broadcasted_iota(jnp.int32, (GRID_W, LANES), 0)
    lane = lax.broadcasted_iota(jnp.int32, (GRID_W, LANES), 1)
    kc = lane % GRID_W
    cs = jnp.clip(c - WIN_W // 2, 0, GRID_W - WIN_W)
    col_ok = (kc >= cs) & (kc < cs + WIN_W)
    half_ok = {(True, True): col_ok, (True, False): col_ok & (lane < GRID_W), (False, True): col_ok & (lane >= GRID_W)}
    masked = jnp.full((GRID_W, LANES), MASK_VALUE, f32)
    variants = list(range(NA_EDGE)) + [NA_EDGE] + list(range(nb - NA_EDGE, nb))
    for vi, b in enumerate(variants):
        wb = min(max(b - 2, 0), nb - NA_KB)
        for rq in range(NA_ROWS_Q):
            r = b * NA_ROWS_Q + rq
            rs = min(max(r - WIN_H // 2, 0), rows - WIN_H)
            for m in range(NA_ROWS_WIN // 2):
                kr = wb * NA_ROWS_Q + 2 * m
                ok = (rs <= kr < rs + WIN_H, rs <= kr + 1 < rs + WIN_H)
                if ok == (False, False):
                    blk = masked
                else:
                    d = kr - r + WIN_H
                    src = jnp.broadcast_to(v_ref[0, d:d + 1, :], (GRID_W, LANES))
                    rolled = pltpu.roll(src, LANES - (WIN_W - 1), 1, stride=1, stride_axis=0)
                    blk = jnp.where(half_ok[ok], rolled * LOG2E, MASK_VALUE)
                o_ref[vi, 0, rq * GRID_W:(rq + 1) * GRID_W, m * LANES:(m + 1) * LANES] = blk


def _na_bias(rpb, S):
    heads, nrow, ncol = rpb.shape
    nb = S // NA_QB
    nvar = 2 * NA_EDGE + 1
    padded = jnp.zeros((heads, nrow + 2, GRID_W), f32).at[:, 1:nrow + 1, :ncol].set(rpb.astype(f32))
    pairs = jnp.concatenate([padded[:, :-1], padded[:, 1:]], axis=-1)
    return pl.pallas_call(
        functools.partial(_na_bias_kernel, nb=nb, rows=S // GRID_W),
        grid=(heads,),
        in_specs=[pl.BlockSpec((1, nrow + 1, LANES), lambda h: (h, 0, 0))],
        out_specs=pl.BlockSpec((nvar, 1, NA_QB, NA_KB * NA_QB), lambda h: (0, h, 0, 0)),
        out_shape=jax.ShapeDtypeStruct((nvar, heads, NA_QB, NA_KB * NA_QB), f32),
        compiler_params=_params(("parallel",)),
        name="na_bias",
    )(pairs)


def _na_step_window_start(i, nb):
    return jnp.clip(NA_STEP * i - 2, 0, nb - NA_KB - NA_STEP + 1)


def _na_kernel(q_ref, kt_ref, v_ref, *rest, nb):
    tab_refs = rest[:NA_STEP]
    g_ref, o_ref, s_scr, p_scr = rest[NA_STEP:]
    i = pl.program_id(0)
    win0 = _na_step_window_start(i, nb)
    starts = []
    for jb in range(NA_STEP):
        local = _na_window_start(NA_STEP * i + jb, nb) - win0
        starts.append(pl.multiple_of(local * NA_QB, NA_QB))
        for h in range(NA_HEADS):
            sl = slice(h * HEAD_DIM, (h + 1) * HEAD_DIM)
            kt = kt_ref[sl, pl.ds(starts[jb], NA_KB * NA_QB)]
            s_scr[jb, h] = jnp.dot(q_ref[jb * NA_QB:(jb + 1) * NA_QB, sl], kt,
                                   preferred_element_type=f32) + tab_refs[jb][0, h]
    inv = []
    for jb in range(NA_STEP):
        for h in range(NA_HEADS):
            s = s_scr[jb, h]
            p = jnp.exp2(s - jnp.max(s, axis=-1, keepdims=True))
            inv.append(1.0 / jnp.sum(p, axis=-1, keepdims=True))
            p_scr[jb, h] = p.astype(bf16)
    for jb in range(NA_STEP):
        outs = []
        for h in range(NA_HEADS):
            sl = slice(h * HEAD_DIM, (h + 1) * HEAD_DIM)
            v = v_ref[pl.ds(starts[jb], NA_KB * NA_QB), sl]
            outs.append(jnp.dot(p_scr[jb, h], v, preferred_element_type=f32) * inv[jb * NA_HEADS + h])
        o_ref[jb * NA_QB:(jb + 1) * NA_QB, :] = _rms(jnp.concatenate(outs, axis=1), g_ref[...]).astype(bf16)


def _na_attn(P, KT, tab, g):
    S = P.shape[0]
    nb = S // NA_QB
    win = (NA_KB + NA_STEP - 1) * NA_QB

    def variant(b):
        return jnp.where(b < NA_EDGE, b, jnp.where(b >= nb - NA_EDGE, b - (nb - 2 * NA_EDGE - 1), NA_EDGE))

    v_window = pl.BlockSpec((pl.Element(win), pl.Element(NA_WIDTH)),
                            lambda i: (_na_step_window_start(i, nb) * NA_QB, NA_WIDTH))
    kt_window = pl.BlockSpec((pl.Element(NA_WIDTH), pl.Element(win)),
                             lambda i: (0, _na_step_window_start(i, nb) * NA_QB))

    tab_specs = [pl.BlockSpec((1, NA_HEADS, NA_QB, NA_KB * NA_QB),
                              lambda i, jb=jb: (variant(NA_STEP * i + jb), 0, 0, 0)) for jb in range(NA_STEP)]
    return pl.pallas_call(
        functools.partial(_na_kernel, nb=nb),
        grid=(nb // NA_STEP,),
        in_specs=[pl.BlockSpec((NA_STEP * NA_QB, NA_WIDTH), lambda i: (i, 0)),
                  kt_window, v_window] + tab_specs + [
                  pl.BlockSpec((1, NA_WIDTH), lambda i: (0, 0))],
        out_specs=pl.BlockSpec((NA_STEP * NA_QB, NA_WIDTH), lambda i: (i, 0)),
        out_shape=jax.ShapeDtypeStruct((S, NA_WIDTH), bf16),
        scratch_shapes=[pltpu.VMEM((NA_STEP, NA_HEADS, NA_QB, NA_KB * NA_QB), f32),
                        pltpu.VMEM((NA_STEP, NA_HEADS, NA_QB, NA_KB * NA_QB), bf16)],
        compiler_params=_params(("parallel",)),
        name="na_attn",
    )(P, KT, P, *([tab] * NA_STEP), g)


def _memkv_kernel(m_ref, g_ref, w_ref, o_ref):
    o_ref[...] = jnp.dot(_rms(m_ref[...], g_ref[...]).astype(bf16), w_ref[...].astype(bf16),
                         preferred_element_type=f32).astype(bf16)


def _mem_kv(mem2, g, w):
    return pl.pallas_call(
        _memkv_kernel,
        out_shape=jax.ShapeDtypeStruct((N_MEM, 2 * MEM_WIDTH), bf16),
        compiler_params=pltpu.CompilerParams(vmem_limit_bytes=VMEM_LIMIT),
        name="mem_kv",
    )(mem2, g, w)


def _ssm_prep_kernel(lre_r, lim_r, ldt_r, bre_ref, bim_ref, cre_ref, cim_ref, win_ref,
                     wtoep_ref, wstate_ref, wcarry_ref, apow_ref, win_bf_ref, *, seg_len):
    win_bf_ref[...] = win_ref[...].astype(bf16)
    T = SSM_T
    TS = TILE_STATES
    reps = LANES // SSM_CH
    assert T == SUBLANES

    def same_group(shape, row_div, col_div):
        r = lax.broadcasted_iota(jnp.int32, shape, 0) // row_div
        c = lax.broadcasted_iota(jnp.int32, shape, 1) // col_div
        return (r == c).astype(f32)

    mask_state = same_group((LANES, TS), SSM_CH, SSM_STATE)
    mask_chan = same_group((LANES, LANES), SSM_CH, SSM_CH)

    kk = []
    for d in range(2):
        lre = jnp.minimum(lre_r[d, 0], LAM_RE_MAX)
        lim = lim_r[d, 0]
        dt = jnp.exp(ldt_r[d, 0])
        zr = lre * dt
        zi = lim * dt
        er = jnp.exp(zr)
        nr = er * jnp.cos(zi) - 1.0
        ni = er * jnp.sin(zi)
        den = lre * lre + lim * lim
        qr = (nr * lre + ni * lim) / den
        qi = (ni * lre - nr * lim) / den
        bre = bre_ref[d, 0]
        bim = bim_ref[d, 0]
        bbr = qr * bre - qi * bim
        bbi = qr * bim + qi * bre
        e0 = lax.broadcasted_iota(jnp.int32, (T, TS), 0).astype(f32)
        mag0 = jnp.exp(zr * e0)
        p0r, p0i = mag0 * jnp.cos(zi * e0), mag0 * jnp.sin(zi * e0)
        mag1 = jnp.exp(zr * (e0 + 1.0))
        p1r, p1i = mag1 * jnp.cos(zi * (e0 + 1.0)), mag1 * jnp.sin(zi * (e0 + 1.0))
        wc_r = [p0r[e:e + 1] * bbr - p0i[e:e + 1] * bbi for e in range(T)]
        wc_i = [p0r[e:e + 1] * bbi + p0i[e:e + 1] * bbr for e in range(T)]

        for s in range(T):
            e = T - 1 - s if d == 0 else s
            rows = slice(s * LANES, (s + 1) * LANES)
            base = d * 2 * TS
            wstate_ref[0, rows, base:base + TS] = (
                jnp.concatenate([wc_r[e]] * reps, axis=0) * mask_state).astype(bf16)
            wstate_ref[0, rows, base + TS:base + 2 * TS] = (
                jnp.concatenate([wc_i[e]] * reps, axis=0) * mask_state).astype(bf16)

        cre = cre_ref[d, 0]
        cim = cim_ref[d, 0]
        nt = (((1,), (1,)), ((), ()))
        kc = (lax.dot_general(jnp.concatenate(wc_r, axis=0), jnp.concatenate([cre] * reps, axis=0) * mask_state,
                              nt, precision=lax.Precision.HIGHEST, preferred_element_type=f32)
              - lax.dot_general(jnp.concatenate(wc_i, axis=0), jnp.concatenate([cim] * reps, axis=0) * mask_state,
                                nt, precision=lax.Precision.HIGHEST, preferred_element_type=f32))
        kk.append([jnp.concatenate([kc[e * SSM_CH:(e + 1) * SSM_CH]] * reps, axis=0) * mask_chan
                   for e in range(T)])

        for t in range(T):
            e = t if d == 0 else T - 1 - t
            pr, pi = p1r[e:e + 1], p1i[e:e + 1]
            rows = slice(t * LANES, (t + 1) * LANES)
            base = d * 2 * TS
            wcarry_ref[0, rows, base:base + TS] = (
                jnp.concatenate([cre * pr - cim * pi] * reps, axis=0) * mask_state).astype(bf16)
            wcarry_ref[0, rows, base + TS:base + 2 * TS] = (
                jnp.concatenate([-(cre * pi + cim * pr)] * reps, axis=0) * mask_state).astype(bf16)

        for part, steps in enumerate((T, T * seg_len)):
            k1 = (lax.broadcasted_iota(jnp.int32, (SUBLANES, TS), 0) + 1).astype(f32) * float(steps)
            mag = jnp.exp(zr * k1)
            rows = slice(part * SUBLANES, (part + 1) * SUBLANES)
            apow_ref[0, rows, d * 2 * TS:d * 2 * TS + TS] = mag * jnp.cos(zi * k1)
            apow_ref[0, rows, d * 2 * TS + TS:(d + 1) * 2 * TS] = mag * jnp.sin(zi * k1)

    for s in range(T):
        for t in range(T):
            if t > s:
                blk = kk[0][t - s]
            elif t < s:
                blk = kk[1][s - t]
            else:
                blk = kk[0][0] + kk[1][0]
            wtoep_ref[0, s * LANES:(s + 1) * LANES, t * LANES:(t + 1) * LANES] = blk.astype(bf16)


def _ssm_prep(lam_re, lam_im, log_dt, b_re, b_im, c_re, c_im, S, w_in):
    TS, NT, T = TILE_STATES, N_TILES, SSM_T
    gp = GROUPS_PER_TILE
    ldt = jnp.broadcast_to(log_dt[:, :, None], lam_re.shape)
    rows = [a.reshape(2, NT, 1, TS) for a in (lam_re, lam_im, ldt)]

    def b_layout(b):
        return b.reshape(2, NT, gp, SSM_STATE, SSM_CH).transpose(0, 1, 4, 2, 3).reshape(2, NT, SSM_CH, TS)

    def c_layout(c):
        return c.reshape(2, NT, gp, SSM_CH, SSM_STATE).transpose(0, 1, 3, 2, 4).reshape(2, NT, SSM_CH, TS)

    row_spec = pl.BlockSpec((2, 1, 1, TS), lambda g: (0, g, 0, 0))
    bc_spec = pl.BlockSpec((2, 1, SSM_CH, TS), lambda g: (0, g, 0, 0))
    win_spec = pl.BlockSpec((w_in.shape[0] // NT, w_in.shape[1]), lambda g: (g, 0))
    return pl.pallas_call(
        functools.partial(_ssm_prep_kernel, seg_len=S // T // SUBLANES),
        grid=(NT,),
        in_specs=[row_spec] * 3 + [bc_spec] * 4 + [win_spec],
        out_specs=[pl.BlockSpec((1, T * LANES, T * LANES), lambda g: (g, 0, 0)),
                   pl.BlockSpec((1, T * LANES, STATE_COLS), lambda g: (g, 0, 0)),
                   pl.BlockSpec((1, T * LANES, STATE_COLS), lambda g: (g, 0, 0)),
                   pl.BlockSpec((1, 2 * SUBLANES, STATE_COLS), lambda g: (g, 0, 0)), win_spec],
        out_shape=[jax.ShapeDtypeStruct((NT, T * LANES, T * LANES), bf16),
                   jax.ShapeDtypeStruct((NT, T * LANES, STATE_COLS), bf16),
                   jax.ShapeDtypeStruct((NT, T * LANES, STATE_COLS), bf16),
                   jax.ShapeDtypeStruct((NT, 2 * SUBLANES, STATE_COLS), f32),
                   jax.ShapeDtypeStruct(w_in.shape, bf16)],
        compiler_params=_params(("parallel",)),
        name="ssm_prep",
    )(*rows, b_layout(b_re), b_layout(b_im), c_layout(c_re), c_layout(c_im), w_in)


def _cmul_add(xr, xi, ar, ai, sr, si):
    return xr + ar * sr - ai * si, xi + ar * si + ai * sr


def _scan_rows(xr, xi, apr, api, cr, ci, reverse):
    n = SUBLANES
    rows = lax.broadcasted_iota(jnp.int32, xr.shape, 0)
    for k in (1, 2, 4):
        ar, ai = apr[k - 1:k], api[k - 1:k]
        if reverse:
            sr, si = pltpu.roll(xr, n - k, 0), pltpu.roll(xi, n - k, 0)
            keep = rows < n - k
        else:
            sr, si = pltpu.roll(xr, k, 0), pltpu.roll(xi, k, 0)
            keep = rows >= k
        xr, xi = _cmul_add(xr, xi, ar, ai, jnp.where(keep, sr, 0.0), jnp.where(keep, si, 0.0))
    if reverse:
        pwr = jnp.concatenate([apr[n - 1 - j:n - j] for j in range(n)], axis=0)
        pwi = jnp.concatenate([api[n - 1 - j:n - j] for j in range(n)], axis=0)
    else:
        pwr, pwi = apr, api
    hr, hi = _cmul_add(xr, xi, pwr, pwi, cr, ci)
    if reverse:
        inr = jnp.where(rows == n - 1, cr, pltpu.roll(hr, n - 1, 0))
        ini = jnp.where(rows == n - 1, ci, pltpu.roll(hi, n - 1, 0))
        return inr, ini, hr[0:1], hi[0:1]
    inr = jnp.where(rows == 0, cr, pltpu.roll(hr, 1, 0))
    ini = jnp.where(rows == 0, ci, pltpu.roll(hi, 1, 0))
    return inr, ini, hr[n - 1:n], hi[n - 1:n]


def _ssm_kernel(u_ref, d_ref, wtoep_ref, wstate_ref, wcarry_ref, apow_ref, o_ref, h_scr, *, nb, cb):
    T = SSM_T
    TS = TILE_STATES
    j = pl.program_id(1)

    nseg = SUBLANES
    seg = nb * cb // nseg
    assert cb % seg == 0

    def chunk_rows():
        return [u_ref[0, pl.ds(t, cb, stride=T), :] for t in range(T)]

    def batch_rows(b):
        return [pl.ds(b * (cb // seg) + part, seg, stride=nseg) for part in range(cb // seg)]

    @pl.when(j < nb)
    def _():
        x = jnp.concatenate([ut.astype(bf16) for ut in chunk_rows()], axis=1)
        res = jnp.dot(x, wstate_ref[0], preferred_element_type=f32)
        for part, rows in enumerate(batch_rows(j)):
            for c in range(STATE_COLS // LANES):
                h_scr[c, rows, :] = res[part * seg:(part + 1) * seg, c * LANES:(c + 1) * LANES]

    @pl.when(j == nb - 1)
    def _():
        tiles = TS // LANES

        def cols(d, ri):
            return slice((2 * d + ri) * TS, (2 * d + ri + 1) * TS)

        def rows_at(p):
            return pl.ds(pl.multiple_of(p * nseg, nseg), nseg)

        def load(rows, d, ri):
            return jnp.concatenate([h_scr[(2 * d + ri) * tiles + c, rows, :] for c in range(tiles)], axis=1)

        def store(rows, d, ri, val):
            for c in range(tiles):
                h_scr[(2 * d + ri) * tiles + c, rows, :] = val[:, c * LANES:(c + 1) * LANES]

        a = [[jnp.broadcast_to(apow_ref[0, 0:1, cols(d, ri)], (nseg, TS)) for ri in range(2)] for d in range(2)]
        zero = jnp.zeros((nseg, TS), f32)

        def step(p, h, write):
            hfr, hfi, hbr, hbi = h
            rf, rb = rows_at(p), rows_at(seg - 1 - p)
            xfr, xfi, xbr, xbi = load(rf, 0, 0), load(rf, 0, 1), load(rb, 1, 0), load(rb, 1, 1)
            if write:
                store(rf, 0, 0, hfr)
                store(rf, 0, 1, hfi)
                store(rb, 1, 0, hbr)
                store(rb, 1, 1, hbi)
            hfr, hfi = _cmul_add(xfr, xfi, a[0][0], a[0][1], hfr, hfi)
            hbr, hbi = _cmul_add(xbr, xbi, a[1][0], a[1][1], hbr, hbi)
            return hfr, hfi, hbr, hbi

        ends = lax.fori_loop(0, seg, lambda p, h: step(p, h, False), (zero, zero, zero, zero))
        z1 = jnp.zeros((1, TS), f32)
        enter = []
        for d in range(2):
            cr, ci, _, _ = _scan_rows(ends[2 * d], ends[2 * d + 1], apow_ref[0, nseg:2 * nseg, cols(d, 0)],
                                      apow_ref[0, nseg:2 * nseg, cols(d, 1)], z1, z1, d == 1)
            enter += [cr, ci]
        lax.fori_loop(0, seg, lambda p, h: step(p, h, True), tuple(enter))

    @pl.when(j >= nb)
    def _():
        us = chunk_rows()
        x = jnp.concatenate([ut.astype(bf16) for ut in us], axis=1)
        hc = jnp.concatenate(
            [jnp.concatenate([h_scr[c, rows, :].astype(bf16) for c in range(STATE_COLS // LANES)], axis=1)
             for rows in batch_rows(j - nb)], axis=0)
        y = (jnp.dot(x, wtoep_ref[0], preferred_element_type=f32)
             + lax.dot_general(hc, wcarry_ref[0], (((1,), (1,)), ((), ())), preferred_element_type=f32))
        for t in range(T):
            yt = y[:, t * LANES:(t + 1) * LANES] + d_ref[...] * us[t]
            o_ref[0, pl.ds(t, cb, stride=T), :] = jax.nn.gelu(yt)


def _ssm_main(u, d_row, wtoep, wstate, wcarry, apow, tb=4096):
    S = u.shape[1]
    T = SSM_T
    nb = S // tb
    cb = tb // T
    return pl.pallas_call(
        functools.partial(_ssm_kernel, nb=nb, cb=cb),
        grid=(N_TILES, 2 * nb),
        in_specs=[pl.BlockSpec((1, tb, LANES), lambda g, j: (g, j % nb, 0)),
                  pl.BlockSpec((1, LANES), lambda g, j: (0, g)),
                  pl.BlockSpec((1, T * LANES, T * LANES), lambda g, j: (g, 0, 0)),
                  pl.BlockSpec((1, T * LANES, STATE_COLS), lambda g, j: (g, 0, 0)),
                  pl.BlockSpec((1, T * LANES, STATE_COLS), lambda g, j: (g, 0, 0)),
                  pl.BlockSpec((1, 2 * SUBLANES, STATE_COLS), lambda g, j: (g, 0, 0))],
        out_specs=pl.BlockSpec((1, tb, LANES), lambda g, j: (g, jnp.maximum(j - nb, 0), 0)),
        out_shape=jax.ShapeDtypeStruct((N_TILES, S, LANES), f32),
        scratch_shapes=[pltpu.VMEM((STATE_COLS // LANES, S // T, LANES), f32)],
        compiler_params=_params(("arbitrary", "arbitrary")),
        name="ssm_main",
    )(u, d_row, wtoep, wstate, wcarry, apow)


def _outproj_kernel(yna_ref, yssm_ref, ymem_ref, x_ref, wglu_ref, bglu_ref, gssm_ref, wout_ref,
                    gpost_ref, gmlp_ref, x1_ref, h2_ref, *, nsub):
    a, b = NA_WIDTH, NA_WIDTH + SSM_WIDTH
    sub = x_ref.shape[0] // nsub
    tiles = [slice(r * sub, (r + 1) * sub) for r in range(nsub)]
    ys = []
    for rows in tiles:
        ya = jnp.concatenate([yssm_ref[t, rows, :] for t in range(N_TILES)], axis=1)
        gate = jax.nn.sigmoid(jnp.dot(ya.astype(bf16), wglu_ref[...], preferred_element_type=f32)
                              + bglu_ref[...])
        ys.append(_rms(ya * gate, gssm_ref[...]).astype(bf16))
    accs = [jnp.dot(yna_ref[rows, :], wout_ref[0:a, :], preferred_element_type=f32)
            + jnp.dot(ys[r], wout_ref[a:b, :], preferred_element_type=f32)
            + jnp.dot(ymem_ref[rows, :], wout_ref[b:, :], preferred_element_type=f32)
            for r, rows in enumerate(tiles)]
    for rows, acc in zip(tiles, accs):
        x1 = x_ref[rows, :] + _rms(acc, gpost_ref[...])
        x1_ref[rows, :] = x1
        h2_ref[rows, :] = _rms(x1, gmlp_ref[...]).astype(bf16)


def _out_proj(y_na, y_ssm, y_mem, x2, wglu_bf, b_glu, g_ssm, wout_bf, g_post, g_mlp, tm=512):
    S = x2.shape[0]
    row = lambda w: pl.BlockSpec((tm, w), lambda i: (i, 0))
    vec = lambda w: pl.BlockSpec((1, w), lambda i: (0, 0))
    return pl.pallas_call(
        functools.partial(_outproj_kernel, nsub=4),
        grid=(S // tm,),
        in_specs=[row(NA_WIDTH), pl.BlockSpec((N_TILES, tm, LANES), lambda i: (0, i, 0)), row(MEM_WIDTH), row(D_MODEL),
                  pl.BlockSpec((SSM_WIDTH, SSM_WIDTH), lambda i: (0, 0)), vec(SSM_WIDTH), vec(SSM_WIDTH),
                  pl.BlockSpec((D_MODEL, D_MODEL), lambda i: (0, 0), pipeline_mode=pl.Buffered(1)),
                  vec(D_MODEL), vec(D_MODEL)],
        out_specs=[row(D_MODEL), row(D_MODEL)],
        out_shape=[jax.ShapeDtypeStruct((S, D_MODEL), f32), jax.ShapeDtypeStruct((S, D_MODEL), bf16)],
        compiler_params=_params(("parallel",)),
        name="out_proj",
    )(y_na, y_ssm, y_mem, x2, wglu_bf, b_glu, g_ssm, wout_bf, g_post, g_mlp)


def _mlp_kernel(h_ref, w1_ref, w2_ref, x1_hbm, g_ref, o_ref, hid_scr, x1_buf, x1_sem, *, nsplit):
    i = pl.program_id(0)
    k = pl.program_id(1)
    nk = pl.num_programs(1) - 1
    tm = o_ref.shape[0]
    wn = D_MODEL // nsplit

    def x1_copy():
        return pltpu.make_async_copy(x1_hbm.at[pl.ds(pl.multiple_of(i * tm, tm), tm), :], x1_buf, x1_sem)

    def hidden():
        hid = jnp.dot(h_ref[...], w1_ref[...], preferred_element_type=f32)
        return jnp.square(jnp.maximum(hid, 0.0)).astype(bf16)

    def partial_out(n, first=False):
        cols = slice(n * wn, (n + 1) * wn)
        part = jnp.dot(hid_scr[(k + 1) % 2], w2_ref[:, cols], preferred_element_type=f32)
        return part if first else o_ref[:, cols] + part

    @pl.when(k == 0)
    def _():
        x1_copy().start()
        hid_scr[0] = hidden()

    @pl.when(k == 1)
    def _():
        for n in range(nsplit):
            o_ref[:, n * wn:(n + 1) * wn] = partial_out(n, first=True)
        hid_scr[1] = hidden()

    @pl.when((k > 1) & (k < nk))
    def _():
        for n in range(nsplit):
            o_ref[:, n * wn:(n + 1) * wn] = partial_out(n)
        hid_scr[k % 2] = hidden()

    @pl.when(k == nk)
    def _():
        ssq = jnp.zeros((tm, 1), f32)
        for n in range(nsplit):
            f = partial_out(n)
            o_ref[:, n * wn:(n + 1) * wn] = f
            ssq = ssq + jnp.sum(f * f, axis=-1, keepdims=True)
        scale = lax.rsqrt(ssq * (1.0 / D_MODEL) + EPS)
        x1_copy().wait()
        o_ref[...] = x1_buf[...] + o_ref[...] * scale * g_ref[...]


def _mlp(h2, w1_bf, w2_bf, x1, g, tm=1024, tk=1024):
    S = h2.shape[0]
    nk = D_FF // tk
    return pl.pallas_call(
        functools.partial(_mlp_kernel, nsplit=4),
        grid=(S // tm, nk + 1),
        in_specs=[pl.BlockSpec((tm, D_MODEL), lambda i, k: (i, 0)),
                  pl.BlockSpec((D_MODEL, tk), lambda i, k: (0, jnp.minimum(k, nk - 1))),
                  pl.BlockSpec((tk, D_MODEL), lambda i, k: (jnp.maximum(k - 1, 0), 0)),
                  pl.BlockSpec(memory_space=pl.ANY),
                  pl.BlockSpec((1, D_MODEL), lambda i, k: (0, 0))],
        out_specs=pl.BlockSpec((tm, D_MODEL), lambda i, k: (i, 0)),
        out_shape=jax.ShapeDtypeStruct((S, D_MODEL), f32),
        scratch_shapes=[pltpu.VMEM((2, tm, tk), bf16), pltpu.VMEM((tm, D_MODEL), f32),
                        pltpu.SemaphoreType.DMA(())],
        compiler_params=_params(("arbitrary", "arbitrary")),
        name="mlp",
    )(h2, w1_bf, w2_bf, x1, g)


def kernel(x, mem, norm_mix_pre, w_in, na_rpb, ssm_lam_re, ssm_lam_im, ssm_log_dt, ssm_b_re, ssm_b_im, ssm_c_re, ssm_c_im, ssm_d, w_glu, b_glu, mem_norm, w_mem_kv, out_norm_na, out_norm_ssm, out_norm_mem, w_out, norm_mix_post, norm_mlp_pre, w_ff1, w_ff2, norm_mlp_post):
    B, S, _ = x.shape
    assert B == 1 and x.shape[2] == D_MODEL and S % 4096 == 0 and w_in.shape[0] == 1
    x2 = x[0]
    l = 0
    vec = lambda a: a[l].reshape(1, -1)

    kv = _mem_kv(mem[0], vec(mem_norm), w_mem_kv[l])
    wtoep, wstate, wcarry, apow, win_bf = _ssm_prep(ssm_lam_re[l], ssm_lam_im[l], ssm_log_dt[l], ssm_b_re[l],
                                                    ssm_b_im[l], ssm_c_re[l], ssm_c_im[l], S, w_in[l])
    P, KT, u, y_mem, wglu_bf, wout_bf, w1_bf, w2_bf = _proj(
        x2, vec(norm_mix_pre), win_bf, kv, vec(out_norm_mem), [w_glu[l], w_out[l], w_ff1[l], w_ff2[l]])

    y_na = _na_attn(P, KT, _na_bias(na_rpb[l], S), vec(out_norm_na))

    y_ssm = _ssm_main(u, ssm_d[l].reshape(1, SSM_WIDTH), wtoep, wstate, wcarry, apow)

    x1, h2 = _out_proj(y_na, y_ssm, y_mem, x2, wglu_bf, vec(b_glu), vec(out_norm_ssm),
                       wout_bf, vec(norm_mix_post), vec(norm_mlp_pre))
    out = _mlp(h2, w1_bf, w2_bf, x1, vec(norm_mlp_post))
    return out[None]
```

```python
import functools

import jax
import jax.numpy as jnp
from jax import lax
from jax.experimental import pallas as pl
from jax.experimental.pallas import tpu as pltpu

f32 = jnp.float32
bf16 = jnp.bfloat16

D_MODEL = 2048
GRID_W = 64
WIN_H = 8
WIN_W = 16
HEAD_DIM = 128
NA_WIDTH = 1024
NA_HEADS = 8
SSM_WIDTH = 512
SSM_CH = 16
SSM_GROUPS = 32
SSM_STATE = 64
MEM_WIDTH = 512
MEM_HEADS = 4
N_MEM = 256
IN_WIDTH = 4096
D_FF = 8192
EPS = 1e-6
LAM_RE_MAX = -1e-4

LANES = 128
SUBLANES = 8
GROUPS_PER_TILE = LANES // SSM_CH
N_TILES = SSM_WIDTH // LANES
TILE_STATES = GROUPS_PER_TILE * SSM_STATE
STATE_COLS = 4 * TILE_STATES
SSM_T = 8
MASK_VALUE = -1e30
LOG2E = 1.4426950408889634
VMEM_LIMIT = 58 * 1024 * 1024


def _rms(x, g):
    ms = jnp.mean(x * x, axis=-1, keepdims=True)
    return x * lax.rsqrt(ms + EPS) * g


def _params(sem):
    return pltpu.CompilerParams(dimension_semantics=sem, vmem_limit_bytes=VMEM_LIMIT)


def _softmax_pv(s, v):
    p = jnp.exp2(s - jnp.max(s, axis=-1, keepdims=True))
    inv = 1.0 / jnp.sum(p, axis=-1, keepdims=True)
    return jnp.dot(p.astype(bf16), v, preferred_element_type=f32) * inv


def _proj_kernel(x_ref, g_ref, w_ref, kv_ref, gmem_ref, *rest, ncast, scale):
    cast_in, rest = rest[:ncast], rest[ncast:]
    p_ref, kt_ref, u_ref, ymem_ref = rest[:4]
    cast_out, h_scr = rest[4:4 + ncast], rest[4 + ncast]
    for src, dst in zip(cast_in, cast_out):
        dst[...] = src[...].astype(bf16)
    nt = (((1,), (1,)), ((), ()))
    x = x_ref[...]
    h_scr[...] = (x * g_ref[...]).astype(bf16)
    r = lax.rsqrt(jnp.mean(x * x, axis=-1, keepdims=True) + EPS)

    acc = jnp.dot(h_scr[...], w_ref[:, 3 * NA_WIDTH:], preferred_element_type=f32) * r
    u_ref[...] = acc[:, :SSM_WIDTH]
    qm = (acc[:, SSM_WIDTH:] * scale).astype(bf16)
    outs = []
    for h in range(MEM_HEADS):
        sl = slice(h * HEAD_DIM, (h + 1) * HEAD_DIM)
        s = lax.dot_general(qm[:, sl], kv_ref[:, sl], nt, preferred_element_type=f32)
        outs.append(_softmax_pv(s, kv_ref[:, MEM_WIDTH + h * HEAD_DIM:MEM_WIDTH + (h + 1) * HEAD_DIM]))
    ymem_ref[...] = _rms(jnp.concatenate(outs, axis=1), gmem_ref[...]).astype(bf16)

    for j in range(3):
        acc = jnp.dot(h_scr[...], w_ref[:, j * NA_WIDTH:(j + 1) * NA_WIDTH], preferred_element_type=f32)
        if j == 0:
            p_ref[:, :NA_WIDTH] = (acc * (r * scale)).astype(bf16)
        elif j == 1:
            kt_ref[...] = (acc * r).T.astype(bf16)
        else:
            p_ref[:, NA_WIDTH:] = (acc * r).astype(bf16)


def _proj(x2, g, w_bf, kv, g_mem, cast_weights, tm=512):
    S = x2.shape[0]
    steps = S // tm
    cast_specs = [pl.BlockSpec((w.shape[0] // steps, w.shape[1]), lambda i: (i, 0)) for w in cast_weights]
    return pl.pallas_call(
        functools.partial(_proj_kernel, ncast=len(cast_weights), scale=HEAD_DIM ** -0.5 * LOG2E),
        grid=(steps,),
        in_specs=[pl.BlockSpec((tm, D_MODEL), lambda i: (i, 0)),
                  pl.BlockSpec((1, D_MODEL), lambda i: (0, 0)),
                  pl.BlockSpec((D_MODEL, IN_WIDTH), lambda i: (0, 0), pipeline_mode=pl.Buffered(1)),
                  pl.BlockSpec((N_MEM, 2 * MEM_WIDTH), lambda i: (0, 0)),
                  pl.BlockSpec((1, MEM_WIDTH), lambda i: (0, 0))] + cast_specs,
        out_specs=[pl.BlockSpec((tm, 2 * NA_WIDTH), lambda i: (i, 0)),
                   pl.BlockSpec((NA_WIDTH, tm), lambda i: (0, i)),
                   pl.BlockSpec((tm, SSM_WIDTH), lambda i: (i, 0)),
                   pl.BlockSpec((tm, MEM_WIDTH), lambda i: (i, 0))] + cast_specs,
        out_shape=[jax.ShapeDtypeStruct((S, 2 * NA_WIDTH), bf16),
                   jax.ShapeDtypeStruct((NA_WIDTH, S), bf16),
                   jax.ShapeDtypeStruct((S, SSM_WIDTH), f32),
                   jax.ShapeDtypeStruct((S, MEM_WIDTH), bf16)]
                  + [jax.ShapeDtypeStruct(w.shape, bf16) for w in cast_weights],
        scratch_shapes=[pltpu.VMEM((tm, D_MODEL), bf16)],
        compiler_params=_params(("parallel",)),
        name="proj",
    )(x2, g, w_bf, kv, g_mem, *cast_weights)


NA_QB = 128
NA_KB = 5
NA_EDGE = 2
NA_STEP = 4
NA_ROWS_Q = NA_QB // GRID_W
NA_ROWS_WIN = NA_KB * NA_QB // GRID_W


def _na_window_start(b, nb):
    return jnp.clip(b - 2, 0, nb - NA_KB)


def _na_bias_kernel(v_ref, o_ref, *, nb, rows):
    c = lax.broadcasted_iota(jnp.int32, (GRID_W, LANES), 0)
    lane = lax.broadcasted_iota(jnp.int32, (GRID_W, LANES), 1)
    kc = lane % GRID_W
    cs = jnp.clip(c - WIN_W // 2, 0, GRID_W - WIN_W)
    col_ok = (kc >= cs) & (kc < cs + WIN_W)
    half_ok = {(True, True): col_ok, (True, False): col_ok & (lane < GRID_W), (False, True): col_ok & (lane >= GRID_W)}
    masked = jnp.full((GRID_W, LANES), MASK_VALUE, f32)
    variants = list(range(NA_EDGE)) + [NA_EDGE] + list(range(nb - NA_EDGE, nb))
    for vi, b in enumerate(variants):
        wb = min(max(b - 2, 0), nb - NA_KB)
        for rq in range(NA_ROWS_Q):
            r = b * NA_ROWS_Q + rq
            rs = min(max(r - WIN_H // 2, 0), rows - WIN_H)
            for m in range(NA_ROWS_WIN // 2):
                kr = wb * NA_ROWS_Q + 2 * m
                ok = (rs <= kr < rs + WIN_H, rs <= kr + 1 < rs + WIN_H)
                if ok == (False, False):
                    blk = masked
                else:
                    d = kr - r + WIN_H
                    src = jnp.broadcast_to(v_ref[0, d:d + 1, :], (GRID_W, LANES))
                    rolled = pltpu.roll(src, LANES - (WIN_W - 1), 1, stride=1, stride_axis=0)
                    blk = jnp.where(half_ok[ok], rolled * LOG2E, MASK_VALUE)
                o_ref[vi, 0, rq * GRID_W:(rq + 1) * GRID_W, m * LANES:(m + 1) * LANES] = blk


def _na_bias(rpb, S):
    heads, nrow, ncol = rpb.shape
    nb = S // NA_QB
    nvar = 2 * NA_EDGE + 1
    padded = jnp.zeros((heads, nrow + 2, GRID_W), f32).at[:, 1:nrow + 1, :ncol].set(rpb.astype(f32))
    pairs = jnp.concatenate([padded[:, :-1], padded[:, 1:]], axis=-1)
    return pl.pallas_call(
        functools.partial(_na_bias_kernel, nb=nb, rows=S // GRID_W),
        grid=(heads,),
        in_specs=[pl.BlockSpec((1, nrow + 1, LANES), lambda h: (h, 0, 0))],
        out_specs=pl.BlockSpec((nvar, 1, NA_QB, NA_KB * NA_QB), lambda h: (0, h, 0, 0)),
        out_shape=jax.ShapeDtypeStruct((nvar, heads, NA_QB, NA_KB * NA_QB), f32),
        compiler_params=_params(("parallel",)),
        name="na_bias",
    )(pairs)


def _na_step_window_start(i, nb):
    return jnp.clip(NA_STEP * i - 2, 0, nb - NA_KB - NA_STEP + 1)


def _na_kernel(q_ref, kt_ref, v_ref, tab_ref, g_ref, o_ref, s_scr, p_scr, *, nb):
    i = pl.program_id(0)
    win0 = _na_step_window_start(i, nb)
    starts = []
    for jb in range(NA_STEP):
        b = NA_STEP * i + jb
        local = _na_window_start(b, nb) - win0
        starts.append(pl.multiple_of(local * NA_QB, NA_QB))
        variant = jnp.where(b < NA_EDGE, b, jnp.where(b >= nb - NA_EDGE, b - (nb - 2 * NA_EDGE - 1), NA_EDGE))
        for h in range(NA_HEADS):
            sl = slice(h * HEAD_DIM, (h + 1) * HEAD_DIM)
            kt = kt_ref[sl, pl.ds(starts[jb], NA_KB * NA_QB)]
            s_scr[jb, h] = jnp.dot(q_ref[jb * NA_QB:(jb + 1) * NA_QB, sl], kt,
                                   preferred_element_type=f32) + tab_ref[variant, h]
    inv = []
    for jb in range(NA_STEP):
        for h in range(NA_HEADS):
            s = s_scr[jb, h]
            p = jnp.exp2(s - jnp.max(s, axis=-1, keepdims=True))
            inv.append(1.0 / jnp.sum(p, axis=-1, keepdims=True))
            p_scr[jb, h] = p.astype(bf16)
    for jb in range(NA_STEP):
        outs = []
        for h in range(NA_HEADS):
            sl = slice(h * HEAD_DIM, (h + 1) * HEAD_DIM)
            v = v_ref[pl.ds(starts[jb], NA_KB * NA_QB), sl]
            outs.append(jnp.dot(p_scr[jb, h], v, preferred_element_type=f32) * inv[jb * NA_HEADS + h])
        o_ref[jb * NA_QB:(jb + 1) * NA_QB, :] = _rms(jnp.concatenate(outs, axis=1), g_ref[...]).astype(bf16)


def _na_attn(P, KT, tab, g):
    S = P.shape[0]
    nb = S // NA_QB
    win = (NA_KB + NA_STEP - 1) * NA_QB

    v_window = pl.BlockSpec((pl.Element(win), pl.Element(NA_WIDTH)),
                            lambda i: (_na_step_window_start(i, nb) * NA_QB, NA_WIDTH))
    kt_window = pl.BlockSpec((pl.Element(NA_WIDTH), pl.Element(win)),
                             lambda i: (0, _na_step_window_start(i, nb) * NA_QB))

    tab_spec = pl.BlockSpec(tab.shape, lambda i: (0, 0, 0, 0), pipeline_mode=pl.Buffered(1))
    return pl.pallas_call(
        functools.partial(_na_kernel, nb=nb),
        grid=(nb // NA_STEP,),
        in_specs=[pl.BlockSpec((NA_STEP * NA_QB, NA_WIDTH), lambda i: (i, 0)),
                  kt_window, v_window, tab_spec, pl.BlockSpec((1, NA_WIDTH), lambda i: (0, 0))],
        out_specs=pl.BlockSpec((NA_STEP * NA_QB, NA_WIDTH), lambda i: (i, 0)),
        out_shape=jax.ShapeDtypeStruct((S, NA_WIDTH), bf16),
        scratch_shapes=[pltpu.VMEM((NA_STEP, NA_HEADS, NA_QB, NA_KB * NA_QB), f32),
                        pltpu.VMEM((NA_STEP, NA_HEADS, NA_QB, NA_KB * NA_QB), bf16)],
        compiler_params=_params(("parallel",)),
        name="na_attn",
    )(P, KT, P, tab, g)


def _memkv_kernel(m_ref, g_ref, w_ref, o_ref):
    o_ref[...] = jnp.dot(_rms(m_ref[...], g_ref[...]).astype(bf16), w_ref[...].astype(bf16),
                         preferred_element_type=f32).astype(bf16)


def _mem_kv(mem2, g, w):
    return pl.pallas_call(
        _memkv_kernel,
        out_shape=jax.ShapeDtypeStruct((N_MEM, 2 * MEM_WIDTH), bf16),
        compiler_params=pltpu.CompilerParams(vmem_limit_bytes=VMEM_LIMIT),
        name="mem_kv",
    )(mem2, g, w)


def _ssm_prep_kernel(lre_r, lim_r, ldt_r, bre_ref, bim_ref, cre_ref, cim_ref, win_ref,
                     wtoep_ref, wstate_ref, wcarry_ref, apow_ref, win_bf_ref, *, seg_len):
    win_bf_ref[...] = win_ref[...].astype(bf16)
    T = SSM_T
    TS = TILE_STATES
    reps = LANES // SSM_CH
    assert T == SUBLANES

    def same_group(shape, row_div, col_div):
        r = lax.broadcasted_iota(jnp.int32, shape, 0) // row_div
        c = lax.broadcasted_iota(jnp.int32, shape, 1) // col_div
        return (r == c).astype(f32)

    mask_state = same_group((LANES, TS), SSM_CH, SSM_STATE)
    mask_chan = same_group((LANES, LANES), SSM_CH, SSM_CH)

    kk = []
    for d in range(2):
        lre = jnp.minimum(lre_r[d, 0], LAM_RE_MAX)
        lim = lim_r[d, 0]
        dt = jnp.exp(ldt_r[d, 0])
        zr = lre * dt
        zi = lim * dt
        er = jnp.exp(zr)
        nr = er * jnp.cos(zi) - 1.0
        ni = er * jnp.sin(zi)
        den = lre * lre + lim * lim
        qr = (nr * lre + ni * lim) / den
        qi = (ni * lre - nr * lim) / den
        bre = bre_ref[d, 0]
        bim = bim_ref[d, 0]
        bbr = qr * bre - qi * bim
        bbi = qr * bim + qi * bre
        e0 = lax.broadcasted_iota(jnp.int32, (T, TS), 0).astype(f32)
        mag0 = jnp.exp(zr * e0)
        p0r, p0i = mag0 * jnp.cos(zi * e0), mag0 * jnp.sin(zi * e0)
        mag1 = jnp.exp(zr * (e0 + 1.0))
        p1r, p1i = mag1 * jnp.cos(zi * (e0 + 1.0)), mag1 * jnp.sin(zi * (e0 + 1.0))
        wc_r = [p0r[e:e + 1] * bbr - p0i[e:e + 1] * bbi for e in range(T)]
        wc_i = [p0r[e:e + 1] * bbi + p0i[e:e + 1] * bbr for e in range(T)]

        for s in range(T):
            e = T - 1 - s if d == 0 else s
            rows = slice(s * LANES, (s + 1) * LANES)
            base = d * 2 * TS
            wstate_ref[0, rows, base:base + TS] = (
                jnp.concatenate([wc_r[e]] * reps, axis=0) * mask_state).astype(bf16)
            wstate_ref[0, rows, base + TS:base + 2 * TS] = (
                jnp.concatenate([wc_i[e]] * reps, axis=0) * mask_state).astype(bf16)

        cre = cre_ref[d, 0]
        cim = cim_ref[d, 0]
        nt = (((1,), (1,)), ((), ()))
        kc = (lax.dot_general(jnp.concatenate(wc_r, axis=0), jnp.concatenate([cre] * reps, axis=0) * mask_state,
                              nt, precision=lax.Precision.HIGHEST, preferred_element_type=f32)
              - lax.dot_general(jnp.concatenate(wc_i, axis=0), jnp.concatenate([cim] * reps, axis=0) * mask_state,
                                nt, precision=lax.Precision.HIGHEST, preferred_element_type=f32))
        kk.append([jnp.concatenate([kc[e * SSM_CH:(e + 1) * SSM_CH]] * reps, axis=0) * mask_chan
                   for e in range(T)])

        for t in range(T):
            e = t if d == 0 else T - 1 - t
            pr, pi = p1r[e:e + 1], p1i[e:e + 1]
            rows = slice(t * LANES, (t + 1) * LANES)
            base = d * 2 * TS
            wcarry_ref[0, rows, base:base + TS] = (
                jnp.concatenate([cre * pr - cim * pi] * reps, axis=0) * mask_state).astype(bf16)
            wcarry_ref[0, rows, base + TS:base + 2 * TS] = (
                jnp.concatenate([-(cre * pi + cim * pr)] * reps, axis=0) * mask_state).astype(bf16)

        for part, steps in enumerate((T, T * seg_len)):
            k1 = (lax.broadcasted_iota(jnp.int32, (SUBLANES, TS), 0) + 1).astype(f32) * float(steps)
            mag = jnp.exp(zr * k1)
            rows = slice(part * SUBLANES, (part + 1) * SUBLANES)
            apow_ref[0, rows, d * 2 * TS:d * 2 * TS + TS] = mag * jnp.cos(zi * k1)
            apow_ref[0, rows, d * 2 * TS + TS:(d + 1) * 2 * TS] = mag * jnp.sin(zi * k1)

    for s in range(T):
        for t in range(T):
            if t > s:
                blk = kk[0][t - s]
            elif t < s:
                blk = kk[1][s - t]
            else:
                blk = kk[0][0] + kk[1][0]
            wtoep_ref[0, s * LANES:(s + 1) * LANES, t * LANES:(t + 1) * LANES] = blk.astype(bf16)


def _ssm_prep(lam_re, lam_im, log_dt, b_re, b_im, c_re, c_im, S, w_in):
    TS, NT, T = TILE_STATES, N_TILES, SSM_T
    gp = GROUPS_PER_TILE
    ldt = jnp.broadcast_to(log_dt[:, :, None], lam_re.shape)
    rows = [a.reshape(2, NT, 1, TS) for a in (lam_re, lam_im, ldt)]

    def b_layout(b):
        return b.reshape(2, NT, gp, SSM_STATE, SSM_CH).transpose(0, 1, 4, 2, 3).reshape(2, NT, SSM_CH, TS)

    def c_layout(c):
        return c.reshape(2, NT, gp, SSM_CH, SSM_STATE).transpose(0, 1, 3, 2, 4).reshape(2, NT, SSM_CH, TS)

    row_spec = pl.BlockSpec((2, 1, 1, TS), lambda g: (0, g, 0, 0))
    bc_spec = pl.BlockSpec((2, 1, SSM_CH, TS), lambda g: (0, g, 0, 0))
    win_spec = pl.BlockSpec((w_in.shape[0] // NT, w_in.shape[1]), lambda g: (g, 0))
    return pl.pallas_call(
        functools.partial(_ssm_prep_kernel, seg_len=S // T // SUBLANES),
        grid=(NT,),
        in_specs=[row_spec] * 3 + [bc_spec] * 4 + [win_spec],
        out_specs=[pl.BlockSpec((1, T * LANES, T * LANES), lambda g: (g, 0, 0)),
                   pl.BlockSpec((1, T * LANES, STATE_COLS), lambda g: (g, 0, 0)),
                   pl.BlockSpec((1, T * LANES, STATE_COLS), lambda g: (g, 0, 0)),
                   pl.BlockSpec((1, 2 * SUBLANES, STATE_COLS), lambda g: (g, 0, 0)), win_spec],
        out_shape=[jax.ShapeDtypeStruct((NT, T * LANES, T * LANES), bf16),
                   jax.ShapeDtypeStruct((NT, T * LANES, STATE_COLS), bf16),
                   jax.ShapeDtypeStruct((NT, T * LANES, STATE_COLS), bf16),
                   jax.ShapeDtypeStruct((NT, 2 * SUBLANES, STATE_COLS), f32),
                   jax.ShapeDtypeStruct(w_in.shape, bf16)],
        compiler_params=_params(("parallel",)),
        name="ssm_prep",
    )(*rows, b_layout(b_re), b_layout(b_im), c_layout(c_re), c_layout(c_im), w_in)


def _cmul_add(xr, xi, ar, ai, sr, si):
    return xr + ar * sr - ai * si, xi + ar * si + ai * sr


def _scan_rows(xr, xi, apr, api, cr, ci, reverse):
    n = SUBLANES
    rows = lax.broadcasted_iota(jnp.int32, xr.shape, 0)
    for k in (1, 2, 4):
        ar, ai = apr[k - 1:k], api[k - 1:k]
        if reverse:
            sr, si = pltpu.roll(xr, n - k, 0), pltpu.roll(xi, n - k, 0)
            keep = rows < n - k
        else:
            sr, si = pltpu.roll(xr, k, 0), pltpu.roll(xi, k, 0)
            keep = rows >= k
        xr, xi = _cmul_add(xr, xi, ar, ai, jnp.where(keep, sr, 0.0), jnp.where(keep, si, 0.0))
    if reverse:
        pwr = jnp.concatenate([apr[n - 1 - j:n - j] for j in range(n)], axis=0)
        pwi = jnp.concatenate([api[n - 1 - j:n - j] for j in range(n)], axis=0)
    else:
        pwr, pwi = apr, api
    hr, hi = _cmul_add(xr, xi, pwr, pwi, cr, ci)
    if reverse:
        inr = jnp.where(rows == n - 1, cr, pltpu.roll(hr, n - 1, 0))
        ini = jnp.where(rows == n - 1, ci, pltpu.roll(hi, n - 1, 0))
        return inr, ini, hr[0:1], hi[0:1]
    inr = jnp.where(rows == 0, cr, pltpu.roll(hr, 1, 0))
    ini = jnp.where(rows == 0, ci, pltpu.roll(hi, 1, 0))
    return inr, ini, hr[n - 1:n], hi[n - 1:n]


def _ssm_kernel(u_ref, d_ref, wtoep_ref, wstate_ref, wcarry_ref, apow_ref, o_ref, h_scr, *, nb, cb):
    T = SSM_T
    TS = TILE_STATES
    j = pl.program_id(1)

    nseg = SUBLANES
    seg = nb * cb // nseg
    assert cb % seg == 0

    def chunk_rows():
        return [u_ref[pl.ds(t, cb, stride=T), :] for t in range(T)]

    def batch_rows(b):
        return [pl.ds(b * (cb // seg) + part, seg, stride=nseg) for part in range(cb // seg)]

    @pl.when(j < nb)
    def _():
        x = jnp.concatenate([ut.astype(bf16) for ut in chunk_rows()], axis=1)
        res = jnp.dot(x, wstate_ref[0], preferred_element_type=f32)
        for part, rows in enumerate(batch_rows(j)):
            for c in range(STATE_COLS // LANES):
                h_scr[c, rows, :] = res[part * seg:(part + 1) * seg, c * LANES:(c + 1) * LANES]

    @pl.when(j == nb - 1)
    def _():
        tiles = TS // LANES

        def cols(d, ri):
            return slice((2 * d + ri) * TS, (2 * d + ri + 1) * TS)

        def rows_at(p):
            return pl.ds(pl.multiple_of(p * nseg, nseg), nseg)

        def load(rows, d, ri):
            return jnp.concatenate([h_scr[(2 * d + ri) * tiles + c, rows, :] for c in range(tiles)], axis=1)

        def store(rows, d, ri, val):
            for c in range(tiles):
                h_scr[(2 * d + ri) * tiles + c, rows, :] = val[:, c * LANES:(c + 1) * LANES]

        a = [[jnp.broadcast_to(apow_ref[0, 0:1, cols(d, ri)], (nseg, TS)) for ri in range(2)] for d in range(2)]
        zero = jnp.zeros((nseg, TS), f32)

        def step(p, h, write):
            hfr, hfi, hbr, hbi = h
            rf, rb = rows_at(p), rows_at(seg - 1 - p)
            xfr, xfi, xbr, xbi = load(rf, 0, 0), load(rf, 0, 1), load(rb, 1, 0), load(rb, 1, 1)
            if write:
                store(rf, 0, 0, hfr)
                store(rf, 0, 1, hfi)
                store(rb, 1, 0, hbr)
                store(rb, 1, 1, hbi)
            hfr, hfi = _cmul_add(xfr, xfi, a[0][0], a[0][1], hfr, hfi)
            hbr, hbi = _cmul_add(xbr, xbi, a[1][0], a[1][1], hbr, hbi)
            return hfr, hfi, hbr, hbi

        ends = lax.fori_loop(0, seg, lambda p, h: step(p, h, False), (zero, zero, zero, zero))
        z1 = jnp.zeros((1, TS), f32)
        enter = []
        for d in range(2):
            cr, ci, _, _ = _scan_rows(ends[2 * d], ends[2 * d + 1], apow_ref[0, nseg:2 * nseg, cols(d, 0)],
                                      apow_ref[0, nseg:2 * nseg, cols(d, 1)], z1, z1, d == 1)
            enter += [cr, ci]
        lax.fori_loop(0, seg, lambda p, h: step(p, h, True), tuple(enter))

    @pl.when(j >= nb)
    def _():
        us = chunk_rows()
        x = jnp.concatenate([ut.astype(bf16) for ut in us], axis=1)
        hc = jnp.concatenate(
            [jnp.concatenate([h_scr[c, rows, :].astype(bf16) for c in range(STATE_COLS // LANES)], axis=1)
             for rows in batch_rows(j - nb)], axis=0)
        y = (jnp.dot(x, wtoep_ref[0], preferred_element_type=f32)
             + lax.dot_general(hc, wcarry_ref[0], (((1,), (1,)), ((), ())), preferred_element_type=f32))
        for t in range(T):
            yt = y[:, t * LANES:(t + 1) * LANES] + d_ref[...] * us[t]
            o_ref[pl.ds(t, cb, stride=T), :] = jax.nn.gelu(yt)


def _ssm_main(u, d_row, wtoep, wstate, wcarry, apow, tb=4096):
    S = u.shape[0]
    T = SSM_T
    nb = S // tb
    cb = tb // T
    return pl.pallas_call(
        functools.partial(_ssm_kernel, nb=nb, cb=cb),
        grid=(N_TILES, 2 * nb),
        in_specs=[pl.BlockSpec((tb, LANES), lambda g, j: (j % nb, g)),
                  pl.BlockSpec((1, LANES), lambda g, j: (0, g)),
                  pl.BlockSpec((1, T * LANES, T * LANES), lambda g, j: (g, 0, 0)),
                  pl.BlockSpec((1, T * LANES, STATE_COLS), lambda g, j: (g, 0, 0)),
                  pl.BlockSpec((1, T * LANES, STATE_COLS), lambda g, j: (g, 0, 0)),
                  pl.BlockSpec((1, 2 * SUBLANES, STATE_COLS), lambda g, j: (g, 0, 0))],
        out_specs=pl.BlockSpec((tb, LANES), lambda g, j: (jnp.maximum(j - nb, 0), g)),
        out_shape=jax.ShapeDtypeStruct((S, SSM_WIDTH), f32),
        scratch_shapes=[pltpu.VMEM((STATE_COLS // LANES, S // T, LANES), f32)],
        compiler_params=_params(("arbitrary", "arbitrary")),
        name="ssm_main",
    )(u, d_row, wtoep, wstate, wcarry, apow)


def _outproj_kernel(yna_ref, yssm_ref, ymem_ref, x_ref, wglu_ref, bglu_ref, gssm_ref, wout_ref,
                    gpost_ref, gmlp_ref, x1_ref, h2_ref, *, nsub):
    a, b = NA_WIDTH, NA_WIDTH + SSM_WIDTH
    sub = x_ref.shape[0] // nsub
    tiles = [slice(r * sub, (r + 1) * sub) for r in range(nsub)]
    ys = []
    for rows in tiles:
        ya = yssm_ref[rows, :]
        gate = jax.nn.sigmoid(jnp.dot(ya.astype(bf16), wglu_ref[...], preferred_element_type=f32)
                              + bglu_ref[...])
        ys.append(_rms(ya * gate, gssm_ref[...]).astype(bf16))
    accs = [jnp.dot(yna_ref[rows, :], wout_ref[0:a, :], preferred_element_type=f32)
            + jnp.dot(ys[r], wout_ref[a:b, :], preferred_element_type=f32)
            + jnp.dot(ymem_ref[rows, :], wout_ref[b:, :], preferred_element_type=f32)
            for r, rows in enumerate(tiles)]
    for rows, acc in zip(tiles, accs):
        x1 = x_ref[rows, :] + _rms(acc, gpost_ref[...])
        x1_ref[rows, :] = x1
        h2_ref[rows, :] = _rms(x1, gmlp_ref[...]).astype(bf16)


def _out_proj(y_na, y_ssm, y_mem, x2, wglu_bf, b_glu, g_ssm, wout_bf, g_post, g_mlp, tm=512):
    S = x2.shape[0]
    row = lambda w: pl.BlockSpec((tm, w), lambda i: (i, 0))
    vec = lambda w: pl.BlockSpec((1, w), lambda i: (0, 0))
    return pl.pallas_call(
        functools.partial(_outproj_kernel, nsub=4),
        grid=(S // tm,),
        in_specs=[row(NA_WIDTH), row(SSM_WIDTH), row(MEM_WIDTH), row(D_MODEL),
                  pl.BlockSpec((SSM_WIDTH, SSM_WIDTH), lambda i: (0, 0)), vec(SSM_WIDTH), vec(SSM_WIDTH),
                  pl.BlockSpec((D_MODEL, D_MODEL), lambda i: (0, 0), pipeline_mode=pl.Buffered(1)),
                  vec(D_MODEL), vec(D_MODEL)],
        out_specs=[row(D_MODEL), row(D_MODEL)],
        out_shape=[jax.ShapeDtypeStruct((S, D_MODEL), f32), jax.ShapeDtypeStruct((S, D_MODEL), bf16)],
        compiler_params=_params(("parallel",)),
        name="out_proj",
    )(y_na, y_ssm, y_mem, x2, wglu_bf, b_glu, g_ssm, wout_bf, g_post, g_mlp)


def _mlp_kernel(h_ref, w1_ref, w2_ref, x1_hbm, g_ref, o_ref, hid_scr, x1_buf, x1_sem, *, nsplit):
    i = pl.program_id(0)
    k = pl.program_id(1)
    nk = pl.num_programs(1) - 1
    tm = o_ref.shape[0]
    wn = D_MODEL // nsplit

    def x1_copy():
        return pltpu.make_async_copy(x1_hbm.at[pl.ds(pl.multiple_of(i * tm, tm), tm), :], x1_buf, x1_sem)

    def hidden():
        hid = jnp.dot(h_ref[...], w1_ref[...], preferred_element_type=f32)
        return jnp.square(jnp.maximum(hid, 0.0)).astype(bf16)

    def partial_out(n, first=False):
        cols = slice(n * wn, (n + 1) * wn)
        part = jnp.dot(hid_scr[(k + 1) % 2], w2_ref[:, cols], preferred_element_type=f32)
        return part if first else o_ref[:, cols] + part

    @pl.when(k == 0)
    def _():
        x1_copy().start()
        hid_scr[0] = hidden()

    @pl.when(k == 1)
    def _():
        for n in range(nsplit):
            o_ref[:, n * wn:(n + 1) * wn] = partial_out(n, first=True)
        hid_scr[1] = hidden()

    @pl.when((k > 1) & (k < nk))
    def _():
        for n in range(nsplit):
            o_ref[:, n * wn:(n + 1) * wn] = partial_out(n)
        hid_scr[k % 2] = hidden()

    @pl.when(k == nk)
    def _():
        ssq = jnp.zeros((tm, 1), f32)
        for n in range(nsplit):
            f = partial_out(n)
            o_ref[:, n * wn:(n + 1) * wn] = f
            ssq = ssq + jnp.sum(f * f, axis=-1, keepdims=True)
        scale = lax.rsqrt(ssq * (1.0 / D_MODEL) + EPS)
        x1_copy().wait()
        o_ref[...] = x1_buf[...] + o_ref[...] * scale * g_ref[...]


def _mlp(h2, w1_bf, w2_bf, x1, g, tm=1024, tk=1024):
    S = h2.shape[0]
    nk = D_FF // tk
    return pl.pallas_call(
        functools.partial(_mlp_kernel, nsplit=4),
        grid=(S // tm, nk + 1),
        in_specs=[pl.BlockSpec((tm, D_MODEL), lambda i, k: (i, 0)),
                  pl.BlockSpec((D_MODEL, tk), lambda i, k: (0, jnp.minimum(k, nk - 1))),
                  pl.BlockSpec((tk, D_MODEL), lambda i, k: (jnp.maximum(k - 1, 0), 0)),
                  pl.BlockSpec(memory_space=pl.ANY),
                  pl.BlockSpec((1, D_MODEL), lambda i, k: (0, 0))],
        out_specs=pl.BlockSpec((tm, D_MODEL), lambda i, k: (i, 0)),
        out_shape=jax.ShapeDtypeStruct((S, D_MODEL), f32),
        scratch_shapes=[pltpu.VMEM((2, tm, tk), bf16), pltpu.VMEM((tm, D_MODEL), f32),
                        pltpu.SemaphoreType.DMA(())],
        compiler_params=_params(("arbitrary", "arbitrary")),
        name="mlp",
    )(h2, w1_bf, w2_bf, x1, g)


def kernel(x, mem, norm_mix_pre, w_in, na_rpb, ssm_lam_re, ssm_lam_im, ssm_log_dt, ssm_b_re, ssm_b_im, ssm_c_re, ssm_c_im, ssm_d, w_glu, b_glu, mem_norm, w_mem_kv, out_norm_na, out_norm_ssm, out_norm_mem, w_out, norm_mix_post, norm_mlp_pre, w_ff1, w_ff2, norm_mlp_post):
    B, S, _ = x.shape
    assert B == 1 and x.shape[2] == D_MODEL and S % 4096 == 0 and w_in.shape[0] == 1
    x2 = x[0]
    l = 0
    vec = lambda a: a[l].reshape(1, -1)

    kv = _mem_kv(mem[0], vec(mem_norm), w_mem_kv[l])
    wtoep, wstate, wcarry, apow, win_bf = _ssm_prep(ssm_lam_re[l], ssm_lam_im[l], ssm_log_dt[l], ssm_b_re[l],
                                                    ssm_b_im[l], ssm_c_re[l], ssm_c_im[l], S, w_in[l])
    P, KT, u, y_mem, wglu_bf, wout_bf, w1_bf, w2_bf = _proj(
        x2, vec(norm_mix_pre), win_bf, kv, vec(out_norm_mem), [w_glu[l], w_out[l], w_ff1[l], w_ff2[l]])

    y_na = _na_attn(P, KT, _na_bias(na_rpb[l], S), vec(out_norm_na))

    y_ssm = _ssm_main(u, ssm_d[l].reshape(1, SSM_WIDTH), wtoep, wstate, wcarry, apow)

    x1, h2 = _out_proj(y_na, y_ssm, y_mem, x2, wglu_bf, vec(b_glu), vec(out_norm_ssm),
                       wout_bf, vec(norm_mix_post), vec(norm_mlp_pre))
    out = _mlp(h2, w1_bf, w2_bf, x1, vec(norm_mlp_post))
    return out[None]
```

```python
import functools

import jax
import jax.numpy as jnp
from jax import lax
from jax.experimental import pallas as pl
from jax.experimental.pallas import tpu as pltpu

f32 = jnp.float32
bf16 = jnp.bfloat16

D_MODEL = 2048
GRID_W = 64
WIN_H = 8
WIN_W = 16
HEAD_DIM = 128
NA_WIDTH = 1024
NA_HEADS = 8
SSM_WIDTH = 512
SSM_CH = 16
SSM_GROUPS = 32
SSM_STATE = 64
MEM_WIDTH = 512
MEM_HEADS = 4
N_MEM = 256
IN_WIDTH = 4096
D_FF = 8192
EPS = 1e-6
LAM_RE_MAX = -1e-4

LANES = 128
SUBLANES = 8
GROUPS_PER_TILE = LANES // SSM_CH
N_TILES = SSM_WIDTH // LANES
TILE_STATES = GROUPS_PER_TILE * SSM_STATE
STATE_COLS = 4 * TILE_STATES
SSM_T = 8
MASK_VALUE = -1e30
LOG2E = 1.4426950408889634
VMEM_LIMIT = 58 * 1024 * 1024


def _rms(x, g):
    ms = jnp.mean(x * x, axis=-1, keepdims=True)
    return x * lax.rsqrt(ms + EPS) * g


def _params(sem):
    return pltpu.CompilerParams(dimension_semantics=sem, vmem_limit_bytes=VMEM_LIMIT)


def _softmax_pv(s, v):
    p = jnp.exp2(s - jnp.max(s, axis=-1, keepdims=True))
    inv = 1.0 / jnp.sum(p, axis=-1, keepdims=True)
    return jnp.dot(p.astype(bf16), v, preferred_element_type=f32) * inv


def _proj_kernel(x_ref, g_ref, w_ref, kv_ref, gmem_ref, *rest, ncast, scale):
    cast_in, rest = rest[:ncast], rest[ncast:]
    p_ref, kt_ref, u_ref, ymem_ref = rest[:4]
    cast_out, h_scr = rest[4:4 + ncast], rest[4 + ncast]
    for src, dst in zip(cast_in, cast_out):
        dst[...] = src[...].astype(bf16)
    nt = (((1,), (1,)), ((), ()))
    x = x_ref[...]
    h_scr[...] = (x * g_ref[...]).astype(bf16)
    r = lax.rsqrt(jnp.mean(x * x, axis=-1, keepdims=True) + EPS)

    acc = jnp.dot(h_scr[...], w_ref[:, 3 * NA_WIDTH:], preferred_element_type=f32) * r
    u_ref[...] = acc[:, :SSM_WIDTH]
    qm = (acc[:, SSM_WIDTH:] * scale).astype(bf16)
    outs = []
    for h in range(MEM_HEADS):
        sl = slice(h * HEAD_DIM, (h + 1) * HEAD_DIM)
        s = lax.dot_general(qm[:, sl], kv_ref[:, sl], nt, preferred_element_type=f32)
        outs.append(_softmax_pv(s, kv_ref[:, MEM_WIDTH + h * HEAD_DIM:MEM_WIDTH + (h + 1) * HEAD_DIM]))
    ymem_ref[...] = _rms(jnp.concatenate(outs, axis=1), gmem_ref[...]).astype(bf16)

    for j in range(3):
        acc = jnp.dot(h_scr[...], w_ref[:, j * NA_WIDTH:(j + 1) * NA_WIDTH], preferred_element_type=f32)
        if j == 0:
            p_ref[:, :NA_WIDTH] = (acc * (r * scale)).astype(bf16)
        elif j == 1:
            kt_ref[...] = (acc * r).T.astype(bf16)
        else:
            p_ref[:, NA_WIDTH:] = (acc * r).astype(bf16)


def _proj(x2, g, w_bf, kv, g_mem, cast_weights, tm=512):
    S = x2.shape[0]
    steps = S // tm
    cast_specs = [pl.BlockSpec((w.shape[0] // steps, w.shape[1]), lambda i: (i, 0)) for w in cast_weights]
    return pl.pallas_call(
        functools.partial(_proj_kernel, ncast=len(cast_weights), scale=HEAD_DIM ** -0.5 * LOG2E),
        grid=(steps,),
        in_specs=[pl.BlockSpec((tm, D_MODEL), lambda i: (i, 0)),
                  pl.BlockSpec((1, D_MODEL), lambda i: (0, 0)),
                  pl.BlockSpec((D_MODEL, IN_WIDTH), lambda i: (0, 0), pipeline_mode=pl.Buffered(1)),
                  pl.BlockSpec((N_MEM, 2 * MEM_WIDTH), lambda i: (0, 0)),
                  pl.BlockSpec((1, MEM_WIDTH), lambda i: (0, 0))] + cast_specs,
        out_specs=[pl.BlockSpec((tm, 2 * NA_WIDTH), lambda i: (i, 0)),
                   pl.BlockSpec((NA_WIDTH, tm), lambda i: (0, i)),
                   pl.BlockSpec((tm, SSM_WIDTH), lambda i: (i, 0)),
                   pl.BlockSpec((tm, MEM_WIDTH), lambda i: (i, 0))] + cast_specs,
        out_shape=[jax.ShapeDtypeStruct((S, 2 * NA_WIDTH), bf16),
                   jax.ShapeDtypeStruct((NA_WIDTH, S), bf16),
                   jax.ShapeDtypeStruct((S, SSM_WIDTH), f32),
                   jax.ShapeDtypeStruct((S, MEM_WIDTH), bf16)]
                  + [jax.ShapeDtypeStruct(w.shape, bf16) for w in cast_weights],
        scratch_shapes=[pltpu.VMEM((tm, D_MODEL), bf16)],
        compiler_params=_params(("parallel",)),
        name="proj",
    )(x2, g, w_bf, kv, g_mem, *cast_weights)


NA_QB = 128
NA_KB = 5
NA_EDGE = 2
NA_STEP = 8
NA_HEAD_GROUPS = 8
NA_ROWS_Q = NA_QB // GRID_W
NA_ROWS_WIN = NA_KB * NA_QB // GRID_W


def _na_window_start(b, nb):
    return jnp.clip(b - 2, 0, nb - NA_KB)


def _na_bias_kernel(v_ref, o_ref, *, nb, rows):
    c = lax.broadcasted_iota(jnp.int32, (GRID_W, LANES), 0)
    lane = lax.broadcasted_iota(jnp.int32, (GRID_W, LANES), 1)
    kc = lane % GRID_W
    cs = jnp.clip(c - WIN_W // 2, 0, GRID_W - WIN_W)
    col_ok = (kc >= cs) & (kc < cs + WIN_W)
    half_ok = {(True, True): col_ok, (True, False): col_ok & (lane < GRID_W), (False, True): col_ok & (lane >= GRID_W)}
    masked = jnp.full((GRID_W, LANES), MASK_VALUE, f32)
    variants = list(range(NA_EDGE)) + [NA_EDGE] + list(range(nb - NA_EDGE, nb))
    for vi, b in enumerate(variants):
        wb = min(max(b - 2, 0), nb - NA_KB)
        for rq in range(NA_ROWS_Q):
            r = b * NA_ROWS_Q + rq
            rs = min(max(r - WIN_H // 2, 0), rows - WIN_H)
            for m in range(NA_ROWS_WIN // 2):
                kr = wb * NA_ROWS_Q + 2 * m
                ok = (rs <= kr < rs + WIN_H, rs <= kr + 1 < rs + WIN_H)
                if ok == (False, False):
                    blk = masked
                else:
                    d = kr - r + WIN_H
                    src = jnp.broadcast_to(v_ref[0, d:d + 1, :], (GRID_W, LANES))
                    rolled = pltpu.roll(src, LANES - (WIN_W - 1), 1, stride=1, stride_axis=0)
                    blk = jnp.where(half_ok[ok], rolled * LOG2E, MASK_VALUE)
                o_ref[vi, 0, rq * GRID_W:(rq + 1) * GRID_W, m * LANES:(m + 1) * LANES] = blk


def _na_bias(rpb, S):
    heads, nrow, ncol = rpb.shape
    nb = S // NA_QB
    nvar = 2 * NA_EDGE + 1
    padded = jnp.zeros((heads, nrow + 2, GRID_W), f32).at[:, 1:nrow + 1, :ncol].set(rpb.astype(f32))
    pairs = jnp.concatenate([padded[:, :-1], padded[:, 1:]], axis=-1)
    return pl.pallas_call(
        functools.partial(_na_bias_kernel, nb=nb, rows=S // GRID_W),
        grid=(heads,),
        in_specs=[pl.BlockSpec((1, nrow + 1, LANES), lambda h: (h, 0, 0))],
        out_specs=pl.BlockSpec((nvar, 1, NA_QB, NA_KB * NA_QB), lambda h: (0, h, 0, 0)),
        out_shape=jax.ShapeDtypeStruct((nvar, heads, NA_QB, NA_KB * NA_QB), f32),
        compiler_params=_params(("parallel",)),
        name="na_bias",
    )(pairs)


def _na_step_window_start(i, nb):
    return jnp.clip(NA_STEP * i - 2, 0, nb - NA_KB - NA_STEP + 1)


def _na_kernel(q_ref, kt_ref, v_ref, tab_ref, g_ref, o_ref, s_scr, p_scr, y_scr, *, nb):
    i = pl.program_id(0)
    win0 = _na_step_window_start(i, nb)
    starts = [pl.multiple_of((_na_window_start(NA_STEP * i + jb, nb) - win0) * NA_QB, NA_QB)
              for jb in range(NA_STEP)]
    hg = NA_HEADS // NA_HEAD_GROUPS
    blocks = [NA_STEP * i + jb for jb in range(NA_STEP)]
    variants = [jnp.where(b < NA_EDGE, b, jnp.where(b >= nb - NA_EDGE, b - (nb - 2 * NA_EDGE - 1), NA_EDGE))
                for b in blocks]
    for grp in range(NA_HEAD_GROUPS):
        for jb in range(NA_STEP):
            for hh in range(hg):
                h = grp * hg + hh
                sl = slice(h * HEAD_DIM, (h + 1) * HEAD_DIM)
                kt = kt_ref[sl, pl.ds(starts[jb], NA_KB * NA_QB)]
                s_scr[jb, hh] = jnp.dot(q_ref[jb * NA_QB:(jb + 1) * NA_QB, sl], kt,
                                        preferred_element_type=f32) + tab_ref[variants[jb], h]
        inv = []
        for jb in range(NA_STEP):
            for hh in range(hg):
                s = s_scr[jb, hh]
                p = jnp.exp2(s - jnp.max(s, axis=-1, keepdims=True))
                inv.append(1.0 / jnp.sum(p, axis=-1, keepdims=True))
                p_scr[jb, hh] = p.astype(bf16)
        for jb in range(NA_STEP):
            for hh in range(hg):
                h = grp * hg + hh
                sl = slice(h * HEAD_DIM, (h + 1) * HEAD_DIM)
                v = v_ref[pl.ds(starts[jb], NA_KB * NA_QB), sl]
                y_scr[jb * NA_QB:(jb + 1) * NA_QB, sl] = (
                    jnp.dot(p_scr[jb, hh], v, preferred_element_type=f32) * inv[jb * hg + hh])
    o_ref[...] = _rms(y_scr[...], g_ref[...]).astype(bf16)


def _na_attn(P, KT, tab, g):
    S = P.shape[0]
    nb = S // NA_QB
    win = (NA_KB + NA_STEP - 1) * NA_QB

    v_window = pl.BlockSpec((pl.Element(win), pl.Element(NA_WIDTH)),
                            lambda i: (_na_step_window_start(i, nb) * NA_QB, NA_WIDTH))
    kt_window = pl.BlockSpec((pl.Element(NA_WIDTH), pl.Element(win)),
                             lambda i: (0, _na_step_window_start(i, nb) * NA_QB))

    tab_spec = pl.BlockSpec(tab.shape, lambda i: (0, 0, 0, 0), pipeline_mode=pl.Buffered(1))
    return pl.pallas_call(
        functools.partial(_na_kernel, nb=nb),
        grid=(nb // NA_STEP,),
        in_specs=[pl.BlockSpec((NA_STEP * NA_QB, NA_WIDTH), lambda i: (i, 0)),
                  kt_window, v_window, tab_spec, pl.BlockSpec((1, NA_WIDTH), lambda i: (0, 0))],
        out_specs=pl.BlockSpec((NA_STEP * NA_QB, NA_WIDTH), lambda i: (i, 0)),
        out_shape=jax.ShapeDtypeStruct((S, NA_WIDTH), bf16),
        scratch_shapes=[pltpu.VMEM((NA_STEP, NA_HEADS // NA_HEAD_GROUPS, NA_QB, NA_KB * NA_QB), f32),
                        pltpu.VMEM((NA_STEP, NA_HEADS // NA_HEAD_GROUPS, NA_QB, NA_KB * NA_QB), bf16),
                        pltpu.VMEM((NA_STEP * NA_QB, NA_WIDTH), f32)],
        compiler_params=_params(("parallel",)),
        name="na_attn",
    )(P, KT, P, tab, g)


def _memkv_kernel(m_ref, g_ref, w_ref, o_ref):
    o_ref[...] = jnp.dot(_rms(m_ref[...], g_ref[...]).astype(bf16), w_ref[...].astype(bf16),
                         preferred_element_type=f32).astype(bf16)


def _mem_kv(mem2, g, w):
    return pl.pallas_call(
        _memkv_kernel,
        out_shape=jax.ShapeDtypeStruct((N_MEM, 2 * MEM_WIDTH), bf16),
        compiler_params=pltpu.CompilerParams(vmem_limit_bytes=VMEM_LIMIT),
        name="mem_kv",
    )(mem2, g, w)


def _ssm_prep_kernel(lre_r, lim_r, ldt_r, bre_ref, bim_ref, cre_ref, cim_ref, win_ref,
                     wtoep_ref, wstate_ref, wcarry_ref, apow_ref, win_bf_ref, *, seg_len):
    win_bf_ref[...] = win_ref[...].astype(bf16)
    T = SSM_T
    TS = TILE_STATES
    reps = LANES // SSM_CH
    assert T == SUBLANES

    def same_group(shape, row_div, col_div):
        r = lax.broadcasted_iota(jnp.int32, shape, 0) // row_div
        c = lax.broadcasted_iota(jnp.int32, shape, 1) // col_div
        return (r == c).astype(f32)

    mask_state = same_group((LANES, TS), SSM_CH, SSM_STATE)
    mask_chan = same_group((LANES, LANES), SSM_CH, SSM_CH)

    kk = []
    for d in range(2):
        lre = jnp.minimum(lre_r[d, 0], LAM_RE_MAX)
        lim = lim_r[d, 0]
        dt = jnp.exp(ldt_r[d, 0])
        zr = lre * dt
        zi = lim * dt
        er = jnp.exp(zr)
        nr = er * jnp.cos(zi) - 1.0
        ni = er * jnp.sin(zi)
        den = lre * lre + lim * lim
        qr = (nr * lre + ni * lim) / den
        qi = (ni * lre - nr * lim) / den
        bre = bre_ref[d, 0]
        bim = bim_ref[d, 0]
        bbr = qr * bre - qi * bim
        bbi = qr * bim + qi * bre
        e0 = lax.broadcasted_iota(jnp.int32, (T, TS), 0).astype(f32)
        mag0 = jnp.exp(zr * e0)
        p0r, p0i = mag0 * jnp.cos(zi * e0), mag0 * jnp.sin(zi * e0)
        mag1 = jnp.exp(zr * (e0 + 1.0))
        p1r, p1i = mag1 * jnp.cos(zi * (e0 + 1.0)), mag1 * jnp.sin(zi * (e0 + 1.0))
        wc_r = [p0r[e:e + 1] * bbr - p0i[e:e + 1] * bbi for e in range(T)]
        wc_i = [p0r[e:e + 1] * bbi + p0i[e:e + 1] * bbr for e in range(T)]

        for s in range(T):
            e = T - 1 - s if d == 0 else s
            rows = slice(s * LANES, (s + 1) * LANES)
            base = d * 2 * TS
            wstate_ref[0, rows, base:base + TS] = (
                jnp.concatenate([wc_r[e]] * reps, axis=0) * mask_state).astype(bf16)
            wstate_ref[0, rows, base + TS:base + 2 * TS] = (
                jnp.concatenate([wc_i[e]] * reps, axis=0) * mask_state).astype(bf16)

        cre = cre_ref[d, 0]
        cim = cim_ref[d, 0]
        nt = (((1,), (1,)), ((), ()))
        kc = (lax.dot_general(jnp.concatenate(wc_r, axis=0), jnp.concatenate([cre] * reps, axis=0) * mask_state,
                              nt, precision=lax.Precision.HIGHEST, preferred_element_type=f32)
              - lax.dot_general(jnp.concatenate(wc_i, axis=0), jnp.concatenate([cim] * reps, axis=0) * mask_state,
                                nt, precision=lax.Precision.HIGHEST, preferred_element_type=f32))
        kk.append([jnp.concatenate([kc[e * SSM_CH:(e + 1) * SSM_CH]] * reps, axis=0) * mask_chan
                   for e in range(T)])

        for t in range(T):
            e = t if d == 0 else T - 1 - t
            pr, pi = p1r[e:e + 1], p1i[e:e + 1]
            rows = slice(t * LANES, (t + 1) * LANES)
            base = d * 2 * TS
            wcarry_ref[0, rows, base:base + TS] = (
                jnp.concatenate([cre * pr - cim * pi] * reps, axis=0) * mask_state).astype(bf16)
            wcarry_ref[0, rows, base + TS:base + 2 * TS] = (
                jnp.concatenate([-(cre * pi + cim * pr)] * reps, axis=0) * mask_state).astype(bf16)

        for part, steps in enumerate((T, T * seg_len)):
            k1 = (lax.broadcasted_iota(jnp.int32, (SUBLANES, TS), 0) + 1).astype(f32) * float(steps)
            mag = jnp.exp(zr * k1)
            rows = slice(part * SUBLANES, (part + 1) * SUBLANES)
            apow_ref[0, rows, d * 2 * TS:d * 2 * TS + TS] = mag * jnp.cos(zi * k1)
            apow_ref[0, rows, d * 2 * TS + TS:(d + 1) * 2 * TS] = mag * jnp.sin(zi * k1)

    for s in range(T):
        for t in range(T):
            if t > s:
                blk = kk[0][t - s]
            elif t < s:
                blk = kk[1][s - t]
            else:
                blk = kk[0][0] + kk[1][0]
            wtoep_ref[0, s * LANES:(s + 1) * LANES, t * LANES:(t + 1) * LANES] = blk.astype(bf16)


def _ssm_prep(lam_re, lam_im, log_dt, b_re, b_im, c_re, c_im, S, w_in):
    TS, NT, T = TILE_STATES, N_TILES, SSM_T
    gp = GROUPS_PER_TILE
    ldt = jnp.broadcast_to(log_dt[:, :, None], lam_re.shape)
    rows = [a.reshape(2, NT, 1, TS) for a in (lam_re, lam_im, ldt)]

    def b_layout(b):
        return b.reshape(2, NT, gp, SSM_STATE, SSM_CH).transpose(0, 1, 4, 2, 3).reshape(2, NT, SSM_CH, TS)

    def c_layout(c):
        return c.reshape(2, NT, gp, SSM_CH, SSM_STATE).transpose(0, 1, 3, 2, 4).reshape(2, NT, SSM_CH, TS)

    row_spec = pl.BlockSpec((2, 1, 1, TS), lambda g: (0, g, 0, 0))
    bc_spec = pl.BlockSpec((2, 1, SSM_CH, TS), lambda g: (0, g, 0, 0))
    win_spec = pl.BlockSpec((w_in.shape[0] // NT, w_in.shape[1]), lambda g: (g, 0))
    return pl.pallas_call(
        functools.partial(_ssm_prep_kernel, seg_len=S // T // SUBLANES),
        grid=(NT,),
        in_specs=[row_spec] * 3 + [bc_spec] * 4 + [win_spec],
        out_specs=[pl.BlockSpec((1, T * LANES, T * LANES), lambda g: (g, 0, 0)),
                   pl.BlockSpec((1, T * LANES, STATE_COLS), lambda g: (g, 0, 0)),
                   pl.BlockSpec((1, T * LANES, STATE_COLS), lambda g: (g, 0, 0)),
                   pl.BlockSpec((1, 2 * SUBLANES, STATE_COLS), lambda g: (g, 0, 0)), win_spec],
        out_shape=[jax.ShapeDtypeStruct((NT, T * LANES, T * LANES), bf16),
                   jax.ShapeDtypeStruct((NT, T * LANES, STATE_COLS), bf16),
                   jax.ShapeDtypeStruct((NT, T * LANES, STATE_COLS), bf16),
                   jax.ShapeDtypeStruct((NT, 2 * SUBLANES, STATE_COLS), f32),
                   jax.ShapeDtypeStruct(w_in.shape, bf16)],
        compiler_params=_params(("parallel",)),
        name="ssm_prep",
    )(*rows, b_layout(b_re), b_layout(b_im), c_layout(c_re), c_layout(c_im), w_in)


def _cmul_add(xr, xi, ar, ai, sr, si):
    return xr + ar * sr - ai * si, xi + ar * si + ai * sr


def _scan_rows(xr, xi, apr, api, cr, ci, reverse):
    n = SUBLANES
    rows = lax.broadcasted_iota(jnp.int32, xr.shape, 0)
    for k in (1, 2, 4):
        ar, ai = apr[k - 1:k], api[k - 1:k]
        if reverse:
            sr, si = pltpu.roll(xr, n - k, 0), pltpu.roll(xi, n - k, 0)
            keep = rows < n - k
        else:
            sr, si = pltpu.roll(xr, k, 0), pltpu.roll(xi, k, 0)
            keep = rows >= k
        xr, xi = _cmul_add(xr, xi, ar, ai, jnp.where(keep, sr, 0.0), jnp.where(keep, si, 0.0))
    if reverse:
        pwr = jnp.concatenate([apr[n - 1 - j:n - j] for j in range(n)], axis=0)
        pwi = jnp.concatenate([api[n - 1 - j:n - j] for j in range(n)], axis=0)
    else:
        pwr, pwi = apr, api
    hr, hi = _cmul_add(xr, xi, pwr, pwi, cr, ci)
    if reverse:
        inr = jnp.where(rows == n - 1, cr, pltpu.roll(hr, n - 1, 0))
        ini = jnp.where(rows == n - 1, ci, pltpu.roll(hi, n - 1, 0))
        return inr, ini, hr[0:1], hi[0:1]
    inr = jnp.where(rows == 0, cr, pltpu.roll(hr, 1, 0))
    ini = jnp.where(rows == 0, ci, pltpu.roll(hi, 1, 0))
    return inr, ini, hr[n - 1:n], hi[n - 1:n]


def _ssm_kernel(u_ref, d_ref, wtoep_ref, wstate_ref, wcarry_ref, apow_ref, o_ref, h_scr, *, nb, cb):
    T = SSM_T
    TS = TILE_STATES
    j = pl.program_id(1)

    nseg = SUBLANES
    seg = nb * cb // nseg
    assert cb % seg == 0

    def chunk_rows():
        return [u_ref[pl.ds(t, cb, stride=T), :] for t in range(T)]

    def batch_rows(b):
        return [pl.ds(b * (cb // seg) + part, seg, stride=nseg) for part in range(cb // seg)]

    @pl.when(j < nb)
    def _():
        x = jnp.concatenate([ut.astype(bf16) for ut in chunk_rows()], axis=1)
        res = jnp.dot(x, wstate_ref[0], preferred_element_type=f32)
        for part, rows in enumerate(batch_rows(j)):
            for c in range(STATE_COLS // LANES):
                h_scr[c, rows, :] = res[part * seg:(part + 1) * seg, c * LANES:(c + 1) * LANES]

    @pl.when(j == nb - 1)
    def _():
        tiles = TS // LANES

        def cols(d, ri):
            return slice((2 * d + ri) * TS, (2 * d + ri + 1) * TS)

        def rows_at(p):
            return pl.ds(pl.multiple_of(p * nseg, nseg), nseg)

        def load(rows, d, ri):
            return jnp.concatenate([h_scr[(2 * d + ri) * tiles + c, rows, :] for c in range(tiles)], axis=1)

        def store(rows, d, ri, val):
            for c in range(tiles):
                h_scr[(2 * d + ri) * tiles + c, rows, :] = val[:, c * LANES:(c + 1) * LANES]

        a = [[jnp.broadcast_to(apow_ref[0, 0:1, cols(d, ri)], (nseg, TS)) for ri in range(2)] for d in range(2)]
        zero = jnp.zeros((nseg, TS), f32)

        def step(p, h, write):
            hfr, hfi, hbr, hbi = h
            rf, rb = rows_at(p), rows_at(seg - 1 - p)
            xfr, xfi, xbr, xbi = load(rf, 0, 0), load(rf, 0, 1), load(rb, 1, 0), load(rb, 1, 1)
            if write:
                store(rf, 0, 0, hfr)
                store(rf, 0, 1, hfi)
                store(rb, 1, 0, hbr)
                store(rb, 1, 1, hbi)
            hfr, hfi = _cmul_add(xfr, xfi, a[0][0], a[0][1], hfr, hfi)
            hbr, hbi = _cmul_add(xbr, xbi, a[1][0], a[1][1], hbr, hbi)
            return hfr, hfi, hbr, hbi

        ends = lax.fori_loop(0, seg, lambda p, h: step(p, h, False), (zero, zero, zero, zero))
        z1 = jnp.zeros((1, TS), f32)
        enter = []
        for d in range(2):
            cr, ci, _, _ = _scan_rows(ends[2 * d], ends[2 * d + 1], apow_ref[0, nseg:2 * nseg, cols(d, 0)],
                                      apow_ref[0, nseg:2 * nseg, cols(d, 1)], z1, z1, d == 1)
            enter += [cr, ci]
        lax.fori_loop(0, seg, lambda p, h: step(p, h, True), tuple(enter))

    @pl.when(j >= nb)
    def _():
        us = chunk_rows()
        x = jnp.concatenate([ut.astype(bf16) for ut in us], axis=1)
        hc = jnp.concatenate(
            [jnp.concatenate([h_scr[c, rows, :].astype(bf16) for c in range(STATE_COLS // LANES)], axis=1)
             for rows in batch_rows(j - nb)], axis=0)
        y = (jnp.dot(x, wtoep_ref[0], preferred_element_type=f32)
             + lax.dot_general(hc, wcarry_ref[0], (((1,), (1,)), ((), ())), preferred_element_type=f32))
        for t in range(T):
            yt = y[:, t * LANES:(t + 1) * LANES] + d_ref[...] * us[t]
            o_ref[pl.ds(t, cb, stride=T), :] = jax.nn.gelu(yt)


def _ssm_main(u, d_row, wtoep, wstate, wcarry, apow, tb=4096):
    S = u.shape[0]
    T = SSM_T
    nb = S // tb
    cb = tb // T
    return pl.pallas_call(
        functools.partial(_ssm_kernel, nb=nb, cb=cb),
        grid=(N_TILES, 2 * nb),
        in_specs=[pl.BlockSpec((tb, LANES), lambda g, j: (j % nb, g)),
                  pl.BlockSpec((1, LANES), lambda g, j: (0, g)),
                  pl.BlockSpec((1, T * LANES, T * LANES), lambda g, j: (g, 0, 0)),
                  pl.BlockSpec((1, T * LANES, STATE_COLS), lambda g, j: (g, 0, 0)),
                  pl.BlockSpec((1, T * LANES, STATE_COLS), lambda g, j: (g, 0, 0)),
                  pl.BlockSpec((1, 2 * SUBLANES, STATE_COLS), lambda g, j: (g, 0, 0))],
        out_specs=pl.BlockSpec((tb, LANES), lambda g, j: (jnp.maximum(j - nb, 0), g)),
        out_shape=jax.ShapeDtypeStruct((S, SSM_WIDTH), f32),
        scratch_shapes=[pltpu.VMEM((STATE_COLS // LANES, S // T, LANES), f32)],
        compiler_params=_params(("arbitrary", "arbitrary")),
        name="ssm_main",
    )(u, d_row, wtoep, wstate, wcarry, apow)


def _outproj_kernel(yna_ref, yssm_ref, ymem_ref, x_ref, wglu_ref, bglu_ref, gssm_ref, wout_ref,
                    gpost_ref, gmlp_ref, x1_ref, h2_ref, *, nsub):
    a, b = NA_WIDTH, NA_WIDTH + SSM_WIDTH
    sub = x_ref.shape[0] // nsub
    tiles = [slice(r * sub, (r + 1) * sub) for r in range(nsub)]
    ys = []
    for rows in tiles:
        ya = yssm_ref[rows, :]
        gate = jax.nn.sigmoid(jnp.dot(ya.astype(bf16), wglu_ref[...], preferred_element_type=f32)
                              + bglu_ref[...])
        ys.append(_rms(ya * gate, gssm_ref[...]).astype(bf16))
    accs = [jnp.dot(yna_ref[rows, :], wout_ref[0:a, :], preferred_element_type=f32)
            + jnp.dot(ys[r], wout_ref[a:b, :], preferred_element_type=f32)
            + jnp.dot(ymem_ref[rows, :], wout_ref[b:, :], preferred_element_type=f32)
            for r, rows in enumerate(tiles)]
    for rows, acc in zip(tiles, accs):
        x1 = x_ref[rows, :] + _rms(acc, gpost_ref[...])
        x1_ref[rows, :] = x1
        h2_ref[rows, :] = _rms(x1, gmlp_ref[...]).astype(bf16)


def _out_proj(y_na, y_ssm, y_mem, x2, wglu_bf, b_glu, g_ssm, wout_bf, g_post, g_mlp, tm=512):
    S = x2.shape[0]
    row = lambda w: pl.BlockSpec((tm, w), lambda i: (i, 0))
    vec = lambda w: pl.BlockSpec((1, w), lambda i: (0, 0))
    return pl.pallas_call(
        functools.partial(_outproj_kernel, nsub=4),
        grid=(S // tm,),
        in_specs=[row(NA_WIDTH), row(SSM_WIDTH), row(MEM_WIDTH), row(D_MODEL),
                  pl.BlockSpec((SSM_WIDTH, SSM_WIDTH), lambda i: (0, 0)), vec(SSM_WIDTH), vec(SSM_WIDTH),
                  pl.BlockSpec((D_MODEL, D_MODEL), lambda i: (0, 0), pipeline_mode=pl.Buffered(1)),
                  vec(D_MODEL), vec(D_MODEL)],
        out_specs=[row(D_MODEL), row(D_MODEL)],
        out_shape=[jax.ShapeDtypeStruct((S, D_MODEL), f32), jax.ShapeDtypeStruct((S, D_MODEL), bf16)],
        compiler_params=_params(("parallel",)),
        name="out_proj",
    )(y_na, y_ssm, y_mem, x2, wglu_bf, b_glu, g_ssm, wout_bf, g_post, g_mlp)


def _mlp_kernel(h_ref, w1_ref, w2_ref, x1_hbm, g_ref, o_ref, hid_scr, x1_buf, x1_sem, *, nsplit):
    i = pl.program_id(0)
    k = pl.program_id(1)
    nk = pl.num_programs(1) - 1
    tm = o_ref.shape[0]
    wn = D_MODEL // nsplit

    def x1_copy():
        return pltpu.make_async_copy(x1_hbm.at[pl.ds(pl.multiple_of(i * tm, tm), tm), :], x1_buf, x1_sem)

    def hidden():
        hid = jnp.dot(h_ref[...], w1_ref[...], preferred_element_type=f32)
        return jnp.square(jnp.maximum(hid, 0.0)).astype(bf16)

    def partial_out(n, first=False):
        cols = slice(n * wn, (n + 1) * wn)
        part = jnp.dot(hid_scr[(k + 1) % 2], w2_ref[:, cols], preferred_element_type=f32)
        return part if first else o_ref[:, cols] + part

    @pl.when(k == 0)
    def _():
        x1_copy().start()
        hid_scr[0] = hidden()

    @pl.when(k == 1)
    def _():
        for n in range(nsplit):
            o_ref[:, n * wn:(n + 1) * wn] = partial_out(n, first=True)
        hid_scr[1] = hidden()

    @pl.when((k > 1) & (k < nk))
    def _():
        for n in range(nsplit):
            o_ref[:, n * wn:(n + 1) * wn] = partial_out(n)
        hid_scr[k % 2] = hidden()

    @pl.when(k == nk)
    def _():
        ssq = jnp.zeros((tm, 1), f32)
        for n in range(nsplit):
            f = partial_out(n)
            o_ref[:, n * wn:(n + 1) * wn] = f
            ssq = ssq + jnp.sum(f * f, axis=-1, keepdims=True)
        scale = lax.rsqrt(ssq * (1.0 / D_MODEL) + EPS)
        x1_copy().wait()
        o_ref[...] = x1_buf[...] + o_ref[...] * scale * g_ref[...]


def _mlp(h2, w1_bf, w2_bf, x1, g, tm=1024, tk=1024):
    S = h2.shape[0]
    nk = D_FF // tk
    return pl.pallas_call(
        functools.partial(_mlp_kernel, nsplit=4),
        grid=(S // tm, nk + 1),
        in_specs=[pl.BlockSpec((tm, D_MODEL), lambda i, k: (i, 0)),
                  pl.BlockSpec((D_MODEL, tk), lambda i, k: (0, jnp.minimum(k, nk - 1))),
                  pl.BlockSpec((tk, D_MODEL), lambda i, k: (jnp.maximum(k - 1, 0), 0)),
                  pl.BlockSpec(memory_space=pl.ANY),
                  pl.BlockSpec((1, D_MODEL), lambda i, k: (0, 0))],
        out_specs=pl.BlockSpec((tm, D_MODEL), lambda i, k: (i, 0)),
        out_shape=jax.ShapeDtypeStruct((S, D_MODEL), f32),
        scratch_shapes=[pltpu.VMEM((2, tm, tk), bf16), pltpu.VMEM((tm, D_MODEL), f32),
                        pltpu.SemaphoreType.DMA(())],
        compiler_params=_params(("arbitrary", "arbitrary")),
        name="mlp",
    )(h2, w1_bf, w2_bf, x1, g)


def kernel(x, mem, norm_mix_pre, w_in, na_rpb, ssm_lam_re, ssm_lam_im, ssm_log_dt, ssm_b_re, ssm_b_im, ssm_c_re, ssm_c_im, ssm_d, w_glu, b_glu, mem_norm, w_mem_kv, out_norm_na, out_norm_ssm, out_norm_mem, w_out, norm_mix_post, norm_mlp_pre, w_ff1, w_ff2, norm_mlp_post):
    B, S, _ = x.shape
    assert B == 1 and x.shape[2] == D_MODEL and S % 4096 == 0 and w_in.shape[0] == 1
    x2 = x[0]
    l = 0
    vec = lambda a: a[l].reshape(1, -1)

    kv = _mem_kv(mem[0], vec(mem_norm), w_mem_kv[l])
    wtoep, wstate, wcarry, apow, win_bf = _ssm_prep(ssm_lam_re[l], ssm_lam_im[l], ssm_log_dt[l], ssm_b_re[l],
                                                    ssm_b_im[l], ssm_c_re[l], ssm_c_im[l], S, w_in[l])
    P, KT, u, y_mem, wglu_bf, wout_bf, w1_bf, w2_bf = _proj(
        x2, vec(norm_mix_pre), win_bf, kv, vec(out_norm_mem), [w_glu[l], w_out[l], w_ff1[l], w_ff2[l]])

    y_na = _na_attn(P, KT, _na_bias(na_rpb[l], S), vec(out_norm_na))

    y_ssm = _ssm_main(u, ssm_d[l].reshape(1, SSM_WIDTH), wtoep, wstate, wcarry, apow)

    x1, h2 = _out_proj(y_na, y_ssm, y_mem, x2, wglu_bf, vec(b_glu), vec(out_norm_ssm),
                       wout_bf, vec(norm_mix_post), vec(norm_mlp_pre))
    out = _mlp(h2, w1_bf, w2_bf, x1, vec(norm_mlp_post))
    return out[None]
```

```python
import functools

import jax
import jax.numpy as jnp
from jax import lax
from jax.experimental import pallas as pl
from jax.experimental.pallas import tpu as pltpu

f32 = jnp.float32
bf16 = jnp.bfloat16

D_MODEL = 2048
GRID_W = 64
WIN_H = 8
WIN_W = 16
HEAD_DIM = 128
NA_WIDTH = 1024
NA_HEADS = 8
SSM_WIDTH = 512
SSM_CH = 16
SSM_GROUPS = 32
SSM_STATE = 64
MEM_WIDTH = 512
MEM_HEADS = 4
N_MEM = 256
IN_WIDTH = 4096
D_FF = 8192
EPS = 1e-6
LAM_RE_MAX = -1e-4

LANES = 128
SUBLANES = 8
GROUPS_PER_TILE = LANES // SSM_CH
N_TILES = SSM_WIDTH // LANES
TILE_STATES = GROUPS_PER_TILE * SSM_STATE
STATE_COLS = 4 * TILE_STATES
SSM_T = 8
MASK_VALUE = -1e30
LOG2E = 1.4426950408889634
VMEM_LIMIT = 58 * 1024 * 1024


def _rms(x, g):
    ms = jnp.mean(x * x, axis=-1, keepdims=True)
    return x * lax.rsqrt(ms + EPS) * g


def _params(sem):
    return pltpu.CompilerParams(dimension_semantics=sem, vmem_limit_bytes=VMEM_LIMIT)


def _softmax_pv(s, v):
    p = jnp.exp2(s - jnp.max(s, axis=-1, keepdims=True))
    inv = 1.0 / jnp.sum(p, axis=-1, keepdims=True)
    return jnp.dot(p.astype(bf16), v, preferred_element_type=f32) * inv


def _proj_kernel(x_ref, g_ref, w_ref, kv_ref, gmem_ref, *rest, ncast, scale):
    cast_in, rest = rest[:ncast], rest[ncast:]
    p_ref, kt_ref, u_ref, ymem_ref = rest[:4]
    cast_out, h_scr = rest[4:4 + ncast], rest[4 + ncast]
    for src, dst in zip(cast_in, cast_out):
        dst[...] = src[...].astype(bf16)
    nt = (((1,), (1,)), ((), ()))
    x = x_ref[...]
    h_scr[...] = (x * g_ref[...]).astype(bf16)
    r = lax.rsqrt(jnp.mean(x * x, axis=-1, keepdims=True) + EPS)

    acc = jnp.dot(h_scr[...], w_ref[:, 3 * NA_WIDTH:], preferred_element_type=f32) * r
    u_ref[...] = acc[:, :SSM_WIDTH]
    qm = (acc[:, SSM_WIDTH:] * scale).astype(bf16)
    outs = []
    for h in range(MEM_HEADS):
        sl = slice(h * HEAD_DIM, (h + 1) * HEAD_DIM)
        s = lax.dot_general(qm[:, sl], kv_ref[:, sl], nt, preferred_element_type=f32)
        outs.append(_softmax_pv(s, kv_ref[:, MEM_WIDTH + h * HEAD_DIM:MEM_WIDTH + (h + 1) * HEAD_DIM]))
    ymem_ref[...] = _rms(jnp.concatenate(outs, axis=1), gmem_ref[...]).astype(bf16)

    for j in range(3):
        acc = jnp.dot(h_scr[...], w_ref[:, j * NA_WIDTH:(j + 1) * NA_WIDTH], preferred_element_type=f32)
        if j == 0:
            p_ref[:, :NA_WIDTH] = (acc * (r * scale)).astype(bf16)
        elif j == 1:
            kt_ref[...] = (acc * r).T.astype(bf16)
        else:
            p_ref[:, NA_WIDTH:] = (acc * r).astype(bf16)


def _proj(x2, g, w_bf, kv, g_mem, cast_weights, tm=512):
    S = x2.shape[0]
    steps = S // tm
    cast_specs = [pl.BlockSpec((w.shape[0] // steps, w.shape[1]), lambda i: (i, 0)) for w in cast_weights]
    return pl.pallas_call(
        functools.partial(_proj_kernel, ncast=len(cast_weights), scale=HEAD_DIM ** -0.5 * LOG2E),
        grid=(steps,),
        in_specs=[pl.BlockSpec((tm, D_MODEL), lambda i: (i, 0)),
                  pl.BlockSpec((1, D_MODEL), lambda i: (0, 0)),
                  pl.BlockSpec((D_MODEL, IN_WIDTH), lambda i: (0, 0), pipeline_mode=pl.Buffered(1)),
                  pl.BlockSpec((N_MEM, 2 * MEM_WIDTH), lambda i: (0, 0)),
                  pl.BlockSpec((1, MEM_WIDTH), lambda i: (0, 0))] + cast_specs,
        out_specs=[pl.BlockSpec((tm, 2 * NA_WIDTH), lambda i: (i, 0)),
                   pl.BlockSpec((NA_WIDTH, tm), lambda i: (0, i)),
                   pl.BlockSpec((tm, SSM_WIDTH), lambda i: (i, 0)),
                   pl.BlockSpec((tm, MEM_WIDTH), lambda i: (i, 0))] + cast_specs,
        out_shape=[jax.ShapeDtypeStruct((S, 2 * NA_WIDTH), bf16),
                   jax.ShapeDtypeStruct((NA_WIDTH, S), bf16),
                   jax.ShapeDtypeStruct((S, SSM_WIDTH), f32),
                   jax.ShapeDtypeStruct((S, MEM_WIDTH), bf16)]
                  + [jax.ShapeDtypeStruct(w.shape, bf16) for w in cast_weights],
        scratch_shapes=[pltpu.VMEM((tm, D_MODEL), bf16)],
        compiler_params=_params(("parallel",)),
        name="proj",
    )(x2, g, w_bf, kv, g_mem, *cast_weights)


NA_QB = 128
NA_KB = 5
NA_EDGE = 2
NA_STEP = 8
NA_HEAD_GROUPS = 8
NA_ROWS_Q = NA_QB // GRID_W
NA_ROWS_WIN = NA_KB * NA_QB // GRID_W


def _na_window_start(b, nb):
    return jnp.clip(b - 2, 0, nb - NA_KB)


def _na_bias_kernel(v_ref, o_ref, *, nb, rows):
    c = lax.broadcasted_iota(jnp.int32, (GRID_W, LANES), 0)
    lane = lax.broadcasted_iota(jnp.int32, (GRID_W, LANES), 1)
    kc = lane % GRID_W
    cs = jnp.clip(c - WIN_W // 2, 0, GRID_W - WIN_W)
    col_ok = (kc >= cs) & (kc < cs + WIN_W)
    half_ok = {(True, True): col_ok, (True, False): col_ok & (lane < GRID_W), (False, True): col_ok & (lane >= GRID_W)}
    masked = jnp.full((GRID_W, LANES), MASK_VALUE, f32)
    variants = list(range(NA_EDGE)) + [NA_EDGE] + list(range(nb - NA_EDGE, nb))
    for vi, b in enumerate(variants):
        wb = min(max(b - 2, 0), nb - NA_KB)
        for rq in range(NA_ROWS_Q):
            r = b * NA_ROWS_Q + rq
            rs = min(max(r - WIN_H // 2, 0), rows - WIN_H)
            for m in range(NA_ROWS_WIN // 2):
                kr = wb * NA_ROWS_Q + 2 * m
                ok = (rs <= kr < rs + WIN_H, rs <= kr + 1 < rs + WIN_H)
                if ok == (False, False):
                    blk = masked
                else:
                    d = kr - r + WIN_H
                    src = jnp.broadcast_to(v_ref[0, d:d + 1, :], (GRID_W, LANES))
                    rolled = pltpu.roll(src, LANES - (WIN_W - 1), 1, stride=1, stride_axis=0)
                    blk = jnp.where(half_ok[ok], rolled * LOG2E, MASK_VALUE)
                o_ref[vi, 0, rq * GRID_W:(rq + 1) * GRID_W, m * LANES:(m + 1) * LANES] = blk


def _na_bias(rpb, S):
    heads, nrow, ncol = rpb.shape
    nb = S // NA_QB
    nvar = 2 * NA_EDGE + 1
    padded = jnp.zeros((heads, nrow + 2, GRID_W), f32).at[:, 1:nrow + 1, :ncol].set(rpb.astype(f32))
    pairs = jnp.concatenate([padded[:, :-1], padded[:, 1:]], axis=-1)
    return pl.pallas_call(
        functools.partial(_na_bias_kernel, nb=nb, rows=S // GRID_W),
        grid=(heads,),
        in_specs=[pl.BlockSpec((1, nrow + 1, LANES), lambda h: (h, 0, 0))],
        out_specs=pl.BlockSpec((nvar, 1, NA_QB, NA_KB * NA_QB), lambda h: (0, h, 0, 0)),
        out_shape=jax.ShapeDtypeStruct((nvar, heads, NA_QB, NA_KB * NA_QB), f32),
        compiler_params=_params(("parallel",)),
        name="na_bias",
    )(pairs)


def _na_step_window_start(i, nb):
    return jnp.clip(NA_STEP * i - 2, 0, nb - NA_KB - NA_STEP + 1)


def _na_kernel(q_ref, kt_ref, v_ref, tab_ref, g_ref, o_ref, s_scr, p_scr, y_scr, *, nb):
    i = pl.program_id(0)
    win0 = _na_step_window_start(i, nb)
    starts = [pl.multiple_of((_na_window_start(NA_STEP * i + jb, nb) - win0) * NA_QB, NA_QB)
              for jb in range(NA_STEP)]
    hg = NA_HEADS // NA_HEAD_GROUPS
    blocks = [NA_STEP * i + jb for jb in range(NA_STEP)]
    variants = [jnp.where(b < NA_EDGE, b, jnp.where(b >= nb - NA_EDGE, b - (nb - 2 * NA_EDGE - 1), NA_EDGE))
                for b in blocks]
    for grp in range(NA_HEAD_GROUPS):
        for jb in range(NA_STEP):
            for hh in range(hg):
                h = grp * hg + hh
                sl = slice(h * HEAD_DIM, (h + 1) * HEAD_DIM)
                kt = kt_ref[sl, pl.ds(starts[jb], NA_KB * NA_QB)]
                s_scr[jb, hh] = jnp.dot(q_ref[jb * NA_QB:(jb + 1) * NA_QB, sl], kt,
                                        preferred_element_type=f32) + tab_ref[variants[jb], h]
        inv = []
        for jb in range(NA_STEP):
            for hh in range(hg):
                s = s_scr[jb, hh]
                p = jnp.exp2(s - jnp.max(s, axis=-1, keepdims=True))
                inv.append(1.0 / jnp.sum(p, axis=-1, keepdims=True))
                p_scr[jb, hh] = p.astype(bf16)
        for jb in range(NA_STEP):
            for hh in range(hg):
                h = grp * hg + hh
                sl = slice(h * HEAD_DIM, (h + 1) * HEAD_DIM)
                v = v_ref[pl.ds(starts[jb], NA_KB * NA_QB), sl]
                y_scr[jb * NA_QB:(jb + 1) * NA_QB, sl] = (
                    jnp.dot(p_scr[jb, hh], v, preferred_element_type=f32) * inv[jb * hg + hh])
    o_ref[...] = _rms(y_scr[...], g_ref[...]).astype(bf16)


def _na_attn(P, KT, tab, g):
    S = P.shape[0]
    nb = S // NA_QB
    win = (NA_KB + NA_STEP - 1) * NA_QB

    v_window = pl.BlockSpec((pl.Element(win), pl.Element(NA_WIDTH)),
                            lambda i: (_na_step_window_start(i, nb) * NA_QB, NA_WIDTH))
    kt_window = pl.BlockSpec((pl.Element(NA_WIDTH), pl.Element(win)),
                             lambda i: (0, _na_step_window_start(i, nb) * NA_QB))

    tab_spec = pl.BlockSpec(tab.shape, lambda i: (0, 0, 0, 0), pipeline_mode=pl.Buffered(1))
    return pl.pallas_call(
        functools.partial(_na_kernel, nb=nb),
        grid=(nb // NA_STEP,),
        in_specs=[pl.BlockSpec((NA_STEP * NA_QB, NA_WIDTH), lambda i: (i, 0)),
                  kt_window, v_window, tab_spec, pl.BlockSpec((1, NA_WIDTH), lambda i: (0, 0))],
        out_specs=pl.BlockSpec((NA_STEP * NA_QB, NA_WIDTH), lambda i: (i, 0)),
        out_shape=jax.ShapeDtypeStruct((S, NA_WIDTH), bf16),
        scratch_shapes=[pltpu.VMEM((NA_STEP, NA_HEADS // NA_HEAD_GROUPS, NA_QB, NA_KB * NA_QB), f32),
                        pltpu.VMEM((NA_STEP, NA_HEADS // NA_HEAD_GROUPS, NA_QB, NA_KB * NA_QB), bf16),
                        pltpu.VMEM((NA_STEP * NA_QB, NA_WIDTH), f32)],
        compiler_params=_params(("parallel",)),
        name="na_attn",
    )(P, KT, P, tab, g)


def _memkv_kernel(m_ref, g_ref, w_ref, o_ref):
    o_ref[...] = jnp.dot(_rms(m_ref[...], g_ref[...]).astype(bf16), w_ref[...].astype(bf16),
                         preferred_element_type=f32).astype(bf16)


def _mem_kv(mem2, g, w):
    return pl.pallas_call(
        _memkv_kernel,
        out_shape=jax.ShapeDtypeStruct((N_MEM, 2 * MEM_WIDTH), bf16),
        compiler_params=pltpu.CompilerParams(vmem_limit_bytes=VMEM_LIMIT),
        name="mem_kv",
    )(mem2, g, w)


def _ssm_prep_kernel(lre_r, lim_r, ldt_r, bre_ref, bim_ref, cre_ref, cim_ref, win_ref,
                     wtoep_ref, wstate_ref, wcarry_ref, apow_ref, win_bf_ref, *, seg_len):
    win_bf_ref[...] = win_ref[...].astype(bf16)
    T = SSM_T
    TS = TILE_STATES
    reps = LANES // SSM_CH
    assert T == SUBLANES

    def same_group(shape, row_div, col_div):
        r = lax.broadcasted_iota(jnp.int32, shape, 0) // row_div
        c = lax.broadcasted_iota(jnp.int32, shape, 1) // col_div
        return (r == c).astype(f32)

    mask_state = same_group((LANES, TS), SSM_CH, SSM_STATE)
    mask_chan = same_group((LANES, LANES), SSM_CH, SSM_CH)

    kk = []
    for d in range(2):
        lre = jnp.minimum(lre_r[d, 0], LAM_RE_MAX)
        lim = lim_r[d, 0]
        dt = jnp.exp(ldt_r[d, 0])
        zr = lre * dt
        zi = lim * dt
        er = jnp.exp(zr)
        nr = er * jnp.cos(zi) - 1.0
        ni = er * jnp.sin(zi)
        den = lre * lre + lim * lim
        qr = (nr * lre + ni * lim) / den
        qi = (ni * lre - nr * lim) / den
        bre = bre_ref[d, 0]
        bim = bim_ref[d, 0]
        bbr = qr * bre - qi * bim
        bbi = qr * bim + qi * bre
        e0 = lax.broadcasted_iota(jnp.int32, (T, TS), 0).astype(f32)
        mag0 = jnp.exp(zr * e0)
        p0r, p0i = mag0 * jnp.cos(zi * e0), mag0 * jnp.sin(zi * e0)
        mag1 = jnp.exp(zr * (e0 + 1.0))
        p1r, p1i = mag1 * jnp.cos(zi * (e0 + 1.0)), mag1 * jnp.sin(zi * (e0 + 1.0))
        wc_r = [p0r[e:e + 1] * bbr - p0i[e:e + 1] * bbi for e in range(T)]
        wc_i = [p0r[e:e + 1] * bbi + p0i[e:e + 1] * bbr for e in range(T)]

        for s in range(T):
            e = T - 1 - s if d == 0 else s
            rows = slice(s * LANES, (s + 1) * LANES)
            base = d * 2 * TS
            wstate_ref[0, rows, base:base + TS] = (
                jnp.concatenate([wc_r[e]] * reps, axis=0) * mask_state).astype(bf16)
            wstate_ref[0, rows, base + TS:base + 2 * TS] = (
                jnp.concatenate([wc_i[e]] * reps, axis=0) * mask_state).astype(bf16)

        cre = cre_ref[d, 0]
        cim = cim_ref[d, 0]
        nt = (((1,), (1,)), ((), ()))
        kc = (lax.dot_general(jnp.concatenate(wc_r, axis=0), jnp.concatenate([cre] * reps, axis=0) * mask_state,
                              nt, precision=lax.Precision.HIGHEST, preferred_element_type=f32)
              - lax.dot_general(jnp.concatenate(wc_i, axis=0), jnp.concatenate([cim] * reps, axis=0) * mask_state,
                                nt, precision=lax.Precision.HIGHEST, preferred_element_type=f32))
        kk.append([jnp.concatenate([kc[e * SSM_CH:(e + 1) * SSM_CH]] * reps, axis=0) * mask_chan
                   for e in range(T)])

        for t in range(T):
            e = t if d == 0 else T - 1 - t
            pr, pi = p1r[e:e + 1], p1i[e:e + 1]
            rows = slice(t * LANES, (t + 1) * LANES)
            base = d * 2 * TS
            wcarry_ref[0, rows, base:base + TS] = (
                jnp.concatenate([cre * pr - cim * pi] * reps, axis=0) * mask_state).astype(bf16)
            wcarry_ref[0, rows, base + TS:base + 2 * TS] = (
                jnp.concatenate([-(cre * pi + cim * pr)] * reps, axis=0) * mask_state).astype(bf16)

        for part, steps in enumerate((T, T * seg_len)):
            k1 = (lax.broadcasted_iota(jnp.int32, (SUBLANES, TS), 0) + 1).astype(f32) * float(steps)
            mag = jnp.exp(zr * k1)
            rows = slice(part * SUBLANES, (part + 1) * SUBLANES)
            apow_ref[0, rows, d * 2 * TS:d * 2 * TS + TS] = mag * jnp.cos(zi * k1)
            apow_ref[0, rows, d * 2 * TS + TS:(d + 1) * 2 * TS] = mag * jnp.sin(zi * k1)

    for s in range(T):
        for t in range(T):
            if t > s:
                blk = kk[0][t - s]
            elif t < s:
                blk = kk[1][s - t]
            else:
                blk = kk[0][0] + kk[1][0]
            wtoep_ref[0, s * LANES:(s + 1) * LANES, t * LANES:(t + 1) * LANES] = blk.astype(bf16)


def _ssm_prep(lam_re, lam_im, log_dt, b_re, b_im, c_re, c_im, S, w_in):
    TS, NT, T = TILE_STATES, N_TILES, SSM_T
    gp = GROUPS_PER_TILE
    ldt = jnp.broadcast_to(log_dt[:, :, None], lam_re.shape)
    rows = [a.reshape(2, NT, 1, TS) for a in (lam_re, lam_im, ldt)]

    def b_layout(b):
        return b.reshape(2, NT, gp, SSM_STATE, SSM_CH).transpose(0, 1, 4, 2, 3).reshape(2, NT, SSM_CH, TS)

    def c_layout(c):
        return c.reshape(2, NT, gp, SSM_CH, SSM_STATE).transpose(0, 1, 3, 2, 4).reshape(2, NT, SSM_CH, TS)

    row_spec = pl.BlockSpec((2, 1, 1, TS), lambda g: (0, g, 0, 0))
    bc_spec = pl.BlockSpec((2, 1, SSM_CH, TS), lambda g: (0, g, 0, 0))
    win_spec = pl.BlockSpec((w_in.shape[0] // NT, w_in.shape[1]), lambda g: (g, 0))
    return pl.pallas_call(
        functools.partial(_ssm_prep_kernel, seg_len=S // T // SUBLANES),
        grid=(NT,),
        in_specs=[row_spec] * 3 + [bc_spec] * 4 + [win_spec],
        out_specs=[pl.BlockSpec((1, T * LANES, T * LANES), lambda g: (g, 0, 0)),
                   pl.BlockSpec((1, T * LANES, STATE_COLS), lambda g: (g, 0, 0)),
                   pl.BlockSpec((1, T * LANES, STATE_COLS), lambda g: (g, 0, 0)),
                   pl.BlockSpec((1, 2 * SUBLANES, STATE_COLS), lambda g: (g, 0, 0)), win_spec],
        out_shape=[jax.ShapeDtypeStruct((NT, T * LANES, T * LANES), bf16),
                   jax.ShapeDtypeStruct((NT, T * LANES, STATE_COLS), bf16),
                   jax.ShapeDtypeStruct((NT, T * LANES, STATE_COLS), bf16),
                   jax.ShapeDtypeStruct((NT, 2 * SUBLANES, STATE_COLS), f32),
                   jax.ShapeDtypeStruct(w_in.shape, bf16)],
        compiler_params=_params(("parallel",)),
        name="ssm_prep",
    )(*rows, b_layout(b_re), b_layout(b_im), c_layout(c_re), c_layout(c_im), w_in)


def _cmul_add(xr, xi, ar, ai, sr, si):
    return xr + ar * sr - ai * si, xi + ar * si + ai * sr


def _scan_rows(xr, xi, apr, api, cr, ci, reverse):
    n = SUBLANES
    rows = lax.broadcasted_iota(jnp.int32, xr.shape, 0)
    for k in (1, 2, 4):
        ar, ai = apr[k - 1:k], api[k - 1:k]
        if reverse:
            sr, si = pltpu.roll(xr, n - k, 0), pltpu.roll(xi, n - k, 0)
            keep = rows < n - k
        else:
            sr, si = pltpu.roll(xr, k, 0), pltpu.roll(xi, k, 0)
            keep = rows >= k
        xr, xi = _cmul_add(xr, xi, ar, ai, jnp.where(keep, sr, 0.0), jnp.where(keep, si, 0.0))
    if reverse:
        pwr = jnp.concatenate([apr[n - 1 - j:n - j] for j in range(n)], axis=0)
        pwi = jnp.concatenate([api[n - 1 - j:n - j] for j in range(n)], axis=0)
    else:
        pwr, pwi = apr, api
    hr, hi = _cmul_add(xr, xi, pwr, pwi, cr, ci)
    if reverse:
        inr = jnp.where(rows == n - 1, cr, pltpu.roll(hr, n - 1, 0))
        ini = jnp.where(rows == n - 1, ci, pltpu.roll(hi, n - 1, 0))
        return inr, ini, hr[0:1], hi[0:1]
    inr = jnp.where(rows == 0, cr, pltpu.roll(hr, 1, 0))
    ini = jnp.where(rows == 0, ci, pltpu.roll(hi, 1, 0))
    return inr, ini, hr[n - 1:n], hi[n - 1:n]


def _ssm_kernel(u_ref, d_ref, wtoep_ref, wstate_ref, wcarry_ref, apow_ref, o_ref, h_scr, *, nb, cb):
    T = SSM_T
    TS = TILE_STATES
    j = pl.program_id(1)

    nseg = SUBLANES
    seg = nb * cb // nseg
    assert cb % seg == 0

    def chunk_rows():
        return [u_ref[pl.ds(t, cb, stride=T), :] for t in range(T)]

    def batch_rows(b):
        return [pl.ds(b * (cb // seg) + part, seg, stride=nseg) for part in range(cb // seg)]

    @pl.when(j < nb)
    def _():
        x = jnp.concatenate([ut.astype(bf16) for ut in chunk_rows()], axis=1)
        res = jnp.dot(x, wstate_ref[0], preferred_element_type=f32)
        for part, rows in enumerate(batch_rows(j)):
            for c in range(STATE_COLS // LANES):
                h_scr[c, rows, :] = res[part * seg:(part + 1) * seg, c * LANES:(c + 1) * LANES]

    @pl.when(j == nb - 1)
    def _():
        tiles = TS // LANES

        def cols(d, ri):
            return slice((2 * d + ri) * TS, (2 * d + ri + 1) * TS)

        def rows_at(p):
            return pl.ds(pl.multiple_of(p * nseg, nseg), nseg)

        def load(rows, d, ri):
            return jnp.concatenate([h_scr[(2 * d + ri) * tiles + c, rows, :] for c in range(tiles)], axis=1)

        def store(rows, d, ri, val):
            for c in range(tiles):
                h_scr[(2 * d + ri) * tiles + c, rows, :] = val[:, c * LANES:(c + 1) * LANES]

        a = [[jnp.broadcast_to(apow_ref[0, 0:1, cols(d, ri)], (nseg, TS)) for ri in range(2)] for d in range(2)]
        zero = jnp.zeros((nseg, TS), f32)

        def step(p, h, write):
            hfr, hfi, hbr, hbi = h
            rf, rb = rows_at(p), rows_at(seg - 1 - p)
            xfr, xfi, xbr, xbi = load(rf, 0, 0), load(rf, 0, 1), load(rb, 1, 0), load(rb, 1, 1)
            if write:
                store(rf, 0, 0, hfr)
                store(rf, 0, 1, hfi)
                store(rb, 1, 0, hbr)
                store(rb, 1, 1, hbi)
            hfr, hfi = _cmul_add(xfr, xfi, a[0][0], a[0][1], hfr, hfi)
            hbr, hbi = _cmul_add(xbr, xbi, a[1][0], a[1][1], hbr, hbi)
            return hfr, hfi, hbr, hbi

        ends = lax.fori_loop(0, seg, lambda p, h: step(p, h, False), (zero, zero, zero, zero), unroll=2)
        z1 = jnp.zeros((1, TS), f32)
        enter = []
        for d in range(2):
            cr, ci, _, _ = _scan_rows(ends[2 * d], ends[2 * d + 1], apow_ref[0, nseg:2 * nseg, cols(d, 0)],
                                      apow_ref[0, nseg:2 * nseg, cols(d, 1)], z1, z1, d == 1)
            enter += [cr, ci]
        lax.fori_loop(0, seg, lambda p, h: step(p, h, True), tuple(enter), unroll=2)

    @pl.when(j >= nb)
    def _():
        us = chunk_rows()
        x = jnp.concatenate([ut.astype(bf16) for ut in us], axis=1)
        hc = jnp.concatenate(
            [jnp.concatenate([h_scr[c, rows, :].astype(bf16) for c in range(STATE_COLS // LANES)], axis=1)
             for rows in batch_rows(j - nb)], axis=0)
        y = (jnp.dot(x, wtoep_ref[0], preferred_element_type=f32)
             + lax.dot_general(hc, wcarry_ref[0], (((1,), (1,)), ((), ())), preferred_element_type=f32))
        for t in range(T):
            yt = y[:, t * LANES:(t + 1) * LANES] + d_ref[...] * us[t]
            o_ref[pl.ds(t, cb, stride=T), :] = jax.nn.gelu(yt)


def _ssm_main(u, d_row, wtoep, wstate, wcarry, apow, tb=4096):
    S = u.shape[0]
    T = SSM_T
    nb = S // tb
    cb = tb // T
    return pl.pallas_call(
        functools.partial(_ssm_kernel, nb=nb, cb=cb),
        grid=(N_TILES, 2 * nb),
        in_specs=[pl.BlockSpec((tb, LANES), lambda g, j: (j % nb, g)),
                  pl.BlockSpec((1, LANES), lambda g, j: (0, g)),
                  pl.BlockSpec((1, T * LANES, T * LANES), lambda g, j: (g, 0, 0)),
                  pl.BlockSpec((1, T * LANES, STATE_COLS), lambda g, j: (g, 0, 0)),
                  pl.BlockSpec((1, T * LANES, STATE_COLS), lambda g, j: (g, 0, 0)),
                  pl.BlockSpec((1, 2 * SUBLANES, STATE_COLS), lambda g, j: (g, 0, 0))],
        out_specs=pl.BlockSpec((tb, LANES), lambda g, j: (jnp.maximum(j - nb, 0), g)),
        out_shape=jax.ShapeDtypeStruct((S, SSM_WIDTH), f32),
        scratch_shapes=[pltpu.VMEM((STATE_COLS // LANES, S // T, LANES), f32)],
        compiler_params=_params(("arbitrary", "arbitrary")),
        name="ssm_main",
    )(u, d_row, wtoep, wstate, wcarry, apow)


def _outproj_kernel(yna_ref, yssm_ref, ymem_ref, x_ref, wglu_ref, bglu_ref, gssm_ref, wout_ref,
                    gpost_ref, gmlp_ref, x1_ref, h2_ref, *, nsub):
    a, b = NA_WIDTH, NA_WIDTH + SSM_WIDTH
    sub = x_ref.shape[0] // nsub
    tiles = [slice(r * sub, (r + 1) * sub) for r in range(nsub)]
    ys = []
    for rows in tiles:
        ya = yssm_ref[rows, :]
        gate = jax.nn.sigmoid(jnp.dot(ya.astype(bf16), wglu_ref[...], preferred_element_type=f32)
                              + bglu_ref[...])
        ys.append(_rms(ya * gate, gssm_ref[...]).astype(bf16))
    accs = [jnp.dot(yna_ref[rows, :], wout_ref[0:a, :], preferred_element_type=f32)
            + jnp.dot(ys[r], wout_ref[a:b, :], preferred_element_type=f32)
            + jnp.dot(ymem_ref[rows, :], wout_ref[b:, :], preferred_element_type=f32)
            for r, rows in enumerate(tiles)]
    for rows, acc in zip(tiles, accs):
        x1 = x_ref[rows, :] + _rms(acc, gpost_ref[...])
        x1_ref[rows, :] = x1
        h2_ref[rows, :] = _rms(x1, gmlp_ref[...]).astype(bf16)


def _out_proj(y_na, y_ssm, y_mem, x2, wglu_bf, b_glu, g_ssm, wout_bf, g_post, g_mlp, tm=512):
    S = x2.shape[0]
    row = lambda w: pl.BlockSpec((tm, w), lambda i: (i, 0))
    vec = lambda w: pl.BlockSpec((1, w), lambda i: (0, 0))
    return pl.pallas_call(
        functools.partial(_outproj_kernel, nsub=4),
        grid=(S // tm,),
        in_specs=[row(NA_WIDTH), row(SSM_WIDTH), row(MEM_WIDTH), row(D_MODEL),
                  pl.BlockSpec((SSM_WIDTH, SSM_WIDTH), lambda i: (0, 0)), vec(SSM_WIDTH), vec(SSM_WIDTH),
                  pl.BlockSpec((D_MODEL, D_MODEL), lambda i: (0, 0), pipeline_mode=pl.Buffered(1)),
                  vec(D_MODEL), vec(D_MODEL)],
        out_specs=[row(D_MODEL), row(D_MODEL)],
        out_shape=[jax.ShapeDtypeStruct((S, D_MODEL), f32), jax.ShapeDtypeStruct((S, D_MODEL), bf16)],
        compiler_params=_params(("parallel",)),
        name="out_proj",
    )(y_na, y_ssm, y_mem, x2, wglu_bf, b_glu, g_ssm, wout_bf, g_post, g_mlp)


def _mlp_kernel(h_ref, w1_ref, w2_ref, x1_hbm, g_ref, o_ref, hid_scr, x1_buf, x1_sem, *, nsplit):
    i = pl.program_id(0)
    k = pl.program_id(1)
    nk = pl.num_programs(1) - 1
    tm = o_ref.shape[0]
    wn = D_MODEL // nsplit

    def x1_copy():
        return pltpu.make_async_copy(x1_hbm.at[pl.ds(pl.multiple_of(i * tm, tm), tm), :], x1_buf, x1_sem)

    def hidden():
        hid = jnp.dot(h_ref[...], w1_ref[...], preferred_element_type=f32)
        return jnp.square(jnp.maximum(hid, 0.0)).astype(bf16)

    def partial_out(n, first=False):
        cols = slice(n * wn, (n + 1) * wn)
        part = jnp.dot(hid_scr[(k + 1) % 2], w2_ref[:, cols], preferred_element_type=f32)
        return part if first else o_ref[:, cols] + part

    @pl.when(k == 0)
    def _():
        x1_copy().start()
        hid_scr[0] = hidden()

    @pl.when(k == 1)
    def _():
        for n in range(nsplit):
            o_ref[:, n * wn:(n + 1) * wn] = partial_out(n, first=True)
        hid_scr[1] = hidden()

    @pl.when((k > 1) & (k < nk))
    def _():
        for n in range(nsplit):
            o_ref[:, n * wn:(n + 1) * wn] = partial_out(n)
        hid_scr[k % 2] = hidden()

    @pl.when(k == nk)
    def _():
        ssq = jnp.zeros((tm, 1), f32)
        for n in range(nsplit):
            f = partial_out(n)
            o_ref[:, n * wn:(n + 1) * wn] = f
            ssq = ssq + jnp.sum(f * f, axis=-1, keepdims=True)
        scale = lax.rsqrt(ssq * (1.0 / D_MODEL) + EPS)
        x1_copy().wait()
        o_ref[...] = x1_buf[...] + o_ref[...] * scale * g_ref[...]


def _mlp(h2, w1_bf, w2_bf, x1, g, tm=1024, tk=1024):
    S = h2.shape[0]
    nk = D_FF // tk
    return pl.pallas_call(
        functools.partial(_mlp_kernel, nsplit=4),
        grid=(S // tm, nk + 1),
        in_specs=[pl.BlockSpec((tm, D_MODEL), lambda i, k: (i, 0)),
                  pl.BlockSpec((D_MODEL, tk), lambda i, k: (0, jnp.minimum(k, nk - 1))),
                  pl.BlockSpec((tk, D_MODEL), lambda i, k: (jnp.maximum(k - 1, 0), 0)),
                  pl.BlockSpec(memory_space=pl.ANY),
                  pl.BlockSpec((1, D_MODEL), lambda i, k: (0, 0))],
        out_specs=pl.BlockSpec((tm, D_MODEL), lambda i, k: (i, 0)),
        out_shape=jax.ShapeDtypeStruct((S, D_MODEL), f32),
        scratch_shapes=[pltpu.VMEM((2, tm, tk), bf16), pltpu.VMEM((tm, D_MODEL), f32),
                        pltpu.SemaphoreType.DMA(())],
        compiler_params=_params(("arbitrary", "arbitrary")),
        name="mlp",
    )(h2, w1_bf, w2_bf, x1, g)


def kernel(x, mem, norm_mix_pre, w_in, na_rpb, ssm_lam_re, ssm_lam_im, ssm_log_dt, ssm_b_re, ssm_b_im, ssm_c_re, ssm_c_im, ssm_d, w_glu, b_glu, mem_norm, w_mem_kv, out_norm_na, out_norm_ssm, out_norm_mem, w_out, norm_mix_post, norm_mlp_pre, w_ff1, w_ff2, norm_mlp_post):
    B, S, _ = x.shape
    assert B == 1 and x.shape[2] == D_MODEL and S % 4096 == 0 and w_in.shape[0] == 1
    x2 = x[0]
    l = 0
    vec = lambda a: a[l].reshape(1, -1)

    kv = _mem_kv(mem[0], vec(mem_norm), w_mem_kv[l])
    wtoep, wstate, wcarry, apow, win_bf = _ssm_prep(ssm_lam_re[l], ssm_lam_im[l], ssm_log_dt[l], ssm_b_re[l],
                                                    ssm_b_im[l], ssm_c_re[l], ssm_c_im[l], S, w_in[l])
    P, KT, u, y_mem, wglu_bf, wout_bf, w1_bf, w2_bf = _proj(
        x2, vec(norm_mix_pre), win_bf, kv, vec(out_norm_mem), [w_glu[l], w_out[l], w_ff1[l], w_ff2[l]])

    y_na = _na_attn(P, KT, _na_bias(na_rpb[l], S), vec(out_norm_na))

    y_ssm = _ssm_main(u, ssm_d[l].reshape(1, SSM_WIDTH), wtoep, wstate, wcarry, apow)

    x1, h2 = _out_proj(y_na, y_ssm, y_mem, x2, wglu_bf, vec(b_glu), vec(out_norm_ssm),
                       wout_bf, vec(norm_mix_post), vec(norm_mlp_pre))
    out = _mlp(h2, w1_bf, w2_bf, x1, vec(norm_mlp_post))
    return out[None]
```

```python
import functools

import jax
import jax.numpy as jnp
from jax import lax
from jax.experimental import pallas as pl
from jax.experimental.pallas import tpu as pltpu

f32 = jnp.float32
bf16 = jnp.bfloat16

D_MODEL = 2048
GRID_W = 64
WIN_H = 8
WIN_W = 16
HEAD_DIM = 128
NA_WIDTH = 1024
NA_HEADS = 8
SSM_WIDTH = 512
SSM_CH = 16
SSM_GROUPS = 32
SSM_STATE = 64
MEM_WIDTH = 512
MEM_HEADS = 4
N_MEM = 256
IN_WIDTH = 4096
D_FF = 8192
EPS = 1e-6
LAM_RE_MAX = -1e-4

LANES = 128
SUBLANES = 8
GROUPS_PER_TILE = LANES // SSM_CH
N_TILES = SSM_WIDTH // LANES
TILE_STATES = GROUPS_PER_TILE * SSM_STATE
STATE_COLS = 4 * TILE_STATES
SSM_T = 8
MASK_VALUE = -1e30
LOG2E = 1.4426950408889634
VMEM_LIMIT = 58 * 1024 * 1024


def _rms(x, g):
    ms = jnp.mean(x * x, axis=-1, keepdims=True)
    return x * lax.rsqrt(ms + EPS) * g


def _params(sem):
    return pltpu.CompilerParams(dimension_semantics=sem, vmem_limit_bytes=VMEM_LIMIT)


def _softmax_pv(s, v):
    p = jnp.exp2(s - jnp.max(s, axis=-1, keepdims=True))
    inv = 1.0 / jnp.sum(p, axis=-1, keepdims=True)
    return jnp.dot(p.astype(bf16), v, preferred_element_type=f32) * inv


def _proj_kernel(x_ref, g_ref, w_ref, kv_ref, gmem_ref, *rest, ncast, scale):
    cast_in, rest = rest[:ncast], rest[ncast:]
    p_ref, kt_ref, u_ref, ymem_ref = rest[:4]
    cast_out, h_scr = rest[4:4 + ncast], rest[4 + ncast]
    for src, dst in zip(cast_in, cast_out):
        dst[...] = src[...].astype(bf16)
    nt = (((1,), (1,)), ((), ()))
    rs = []
    for c in range(0, x_ref.shape[0], LANES):
        xc = x_ref[c:c + LANES, :]
        h_scr[c:c + LANES, :] = (xc * g_ref[...]).astype(bf16)
        rs.append(lax.rsqrt(jnp.mean(xc * xc, axis=-1, keepdims=True) + EPS))
    r = jnp.concatenate(rs, axis=0)

    acc = jnp.dot(h_scr[...], w_ref[:, 3 * NA_WIDTH:], preferred_element_type=f32) * r
    u_ref[...] = acc[:, :SSM_WIDTH]
    qm = (acc[:, SSM_WIDTH:] * scale).astype(bf16)
    outs = []
    for h in range(MEM_HEADS):
        sl = slice(h * HEAD_DIM, (h + 1) * HEAD_DIM)
        s = lax.dot_general(qm[:, sl], kv_ref[:, sl], nt, preferred_element_type=f32)
        outs.append(_softmax_pv(s, kv_ref[:, MEM_WIDTH + h * HEAD_DIM:MEM_WIDTH + (h + 1) * HEAD_DIM]))
    ymem_ref[...] = _rms(jnp.concatenate(outs, axis=1), gmem_ref[...]).astype(bf16)

    for j in range(3):
        acc = jnp.dot(h_scr[...], w_ref[:, j * NA_WIDTH:(j + 1) * NA_WIDTH], preferred_element_type=f32)
        if j == 0:
            p_ref[:, :NA_WIDTH] = (acc * (r * scale)).astype(bf16)
        elif j == 1:
            kt_ref[...] = (acc * r).T.astype(bf16)
        else:
            p_ref[:, NA_WIDTH:] = (acc * r).astype(bf16)


def _proj(x2, g, w_bf, kv, g_mem, cast_weights, tm=512):
    S = x2.shape[0]
    steps = S // tm
    cast_specs = [pl.BlockSpec((w.shape[0] // steps, w.shape[1]), lambda i: (i, 0)) for w in cast_weights]
    return pl.pallas_call(
        functools.partial(_proj_kernel, ncast=len(cast_weights), scale=HEAD_DIM ** -0.5 * LOG2E),
        grid=(steps,),
        in_specs=[pl.BlockSpec((tm, D_MODEL), lambda i: (i, 0)),
                  pl.BlockSpec((1, D_MODEL), lambda i: (0, 0)),
                  pl.BlockSpec((D_MODEL, IN_WIDTH), lambda i: (0, 0), pipeline_mode=pl.Buffered(1)),
                  pl.BlockSpec((N_MEM, 2 * MEM_WIDTH), lambda i: (0, 0)),
                  pl.BlockSpec((1, MEM_WIDTH), lambda i: (0, 0))] + cast_specs,
        out_specs=[pl.BlockSpec((tm, 2 * NA_WIDTH), lambda i: (i, 0)),
                   pl.BlockSpec((NA_WIDTH, tm), lambda i: (0, i)),
                   pl.BlockSpec((tm, SSM_WIDTH), lambda i: (i, 0)),
                   pl.BlockSpec((tm, MEM_WIDTH), lambda i: (i, 0))] + cast_specs,
        out_shape=[jax.ShapeDtypeStruct((S, 2 * NA_WIDTH), bf16),
                   jax.ShapeDtypeStruct((NA_WIDTH, S), bf16),
                   jax.ShapeDtypeStruct((S, SSM_WIDTH), f32),
                   jax.ShapeDtypeStruct((S, MEM_WIDTH), bf16)]
                  + [jax.ShapeDtypeStruct(w.shape, bf16) for w in cast_weights],
        scratch_shapes=[pltpu.VMEM((tm, D_MODEL), bf16)],
        compiler_params=_params(("parallel",)),
        name="proj",
    )(x2, g, w_bf, kv, g_mem, *cast_weights)


NA_QB = 128
NA_KB = 5
NA_EDGE = 2
NA_STEP = 8
NA_HEAD_GROUPS = 8
NA_ROWS_Q = NA_QB // GRID_W
NA_ROWS_WIN = NA_KB * NA_QB // GRID_W


def _na_window_start(b, nb):
    return jnp.clip(b - 2, 0, nb - NA_KB)


def _na_bias_kernel(v_ref, o_ref, *, nb, rows):
    c = lax.broadcasted_iota(jnp.int32, (GRID_W, LANES), 0)
    lane = lax.broadcasted_iota(jnp.int32, (GRID_W, LANES), 1)
    kc = lane % GRID_W
    cs = jnp.clip(c - WIN_W // 2, 0, GRID_W - WIN_W)
    col_ok = (kc >= cs) & (kc < cs + WIN_W)
    half_ok = {(True, True): col_ok, (True, False): col_ok & (lane < GRID_W), (False, True): col_ok & (lane >= GRID_W)}
    masked = jnp.full((GRID_W, LANES), MASK_VALUE, f32)
    variants = list(range(NA_EDGE)) + [NA_EDGE] + list(range(nb - NA_EDGE, nb))
    for vi, b in enumerate(variants):
        wb = min(max(b - 2, 0), nb - NA_KB)
        for rq in range(NA_ROWS_Q):
            r = b * NA_ROWS_Q + rq
            rs = min(max(r - WIN_H // 2, 0), rows - WIN_H)
            for m in range(NA_ROWS_WIN // 2):
                kr = wb * NA_ROWS_Q + 2 * m
                ok = (rs <= kr < rs + WIN_H, rs <= kr + 1 < rs + WIN_H)
                if ok == (False, False):
                    blk = masked
                else:
                    d = kr - r + WIN_H
                    src = jnp.broadcast_to(v_ref[0, d:d + 1, :], (GRID_W, LANES))
                    rolled = pltpu.roll(src, LANES - (WIN_W - 1), 1, stride=1, stride_axis=0)
                    blk = jnp.where(half_ok[ok], rolled * LOG2E, MASK_VALUE)
                o_ref[vi, 0, rq * GRID_W:(rq + 1) * GRID_W, m * LANES:(m + 1) * LANES] = blk


def _na_bias(rpb, S):
    heads, nrow, ncol = rpb.shape
    nb = S // NA_QB
    nvar = 2 * NA_EDGE + 1
    padded = jnp.zeros((heads, nrow + 2, GRID_W), f32).at[:, 1:nrow + 1, :ncol].set(rpb.astype(f32))
    pairs = jnp.concatenate([padded[:, :-1], padded[:, 1:]], axis=-1)
    return pl.pallas_call(
        functools.partial(_na_bias_kernel, nb=nb, rows=S // GRID_W),
        grid=(heads,),
        in_specs=[pl.BlockSpec((1, nrow + 1, LANES), lambda h: (h, 0, 0))],
        out_specs=pl.BlockSpec((nvar, 1, NA_QB, NA_KB * NA_QB), lambda h: (0, h, 0, 0)),
        out_shape=jax.ShapeDtypeStruct((nvar, heads, NA_QB, NA_KB * NA_QB), f32),
        compiler_params=_params(("parallel",)),
        name="na_bias",
    )(pairs)


def _na_step_window_start(i, nb):
    return jnp.clip(NA_STEP * i - 2, 0, nb - NA_KB - NA_STEP + 1)


def _na_kernel(q_ref, kt_ref, v_ref, tab_ref, g_ref, o_ref, s_scr, p_scr, y_scr, *, nb):
    i = pl.program_id(0)
    win0 = _na_step_window_start(i, nb)
    starts = [pl.multiple_of((_na_window_start(NA_STEP * i + jb, nb) - win0) * NA_QB, NA_QB)
              for jb in range(NA_STEP)]
    hg = NA_HEADS // NA_HEAD_GROUPS
    blocks = [NA_STEP * i + jb for jb in range(NA_STEP)]
    variants = [jnp.where(b < NA_EDGE, b, jnp.where(b >= nb - NA_EDGE, b - (nb - 2 * NA_EDGE - 1), NA_EDGE))
                for b in blocks]
    for grp in range(NA_HEAD_GROUPS):
        for jb in range(NA_STEP):
            for hh in range(hg):
                h = grp * hg + hh
                sl = slice(h * HEAD_DIM, (h + 1) * HEAD_DIM)
                kt = kt_ref[sl, pl.ds(starts[jb], NA_KB * NA_QB)]
                s_scr[jb, hh] = jnp.dot(q_ref[jb * NA_QB:(jb + 1) * NA_QB, sl], kt,
                                        preferred_element_type=f32) + tab_ref[variants[jb], h]
        inv = []
        for jb in range(NA_STEP):
            for hh in range(hg):
                s = s_scr[jb, hh]
                p = jnp.exp2(s - jnp.max(s, axis=-1, keepdims=True))
                inv.append(1.0 / jnp.sum(p, axis=-1, keepdims=True))
                p_scr[jb, hh] = p.astype(bf16)
        for jb in range(NA_STEP):
            for hh in range(hg):
                h = grp * hg + hh
                sl = slice(h * HEAD_DIM, (h + 1) * HEAD_DIM)
                v = v_ref[pl.ds(starts[jb], NA_KB * NA_QB), sl]
                y_scr[jb * NA_QB:(jb + 1) * NA_QB, sl] = (
                    jnp.dot(p_scr[jb, hh], v, preferred_element_type=f32) * inv[jb * hg + hh])
    o_ref[...] = _rms(y_scr[...], g_ref[...]).astype(bf16)


def _na_attn(P, KT, tab, g):
    S = P.shape[0]
    nb = S // NA_QB
    win = (NA_KB + NA_STEP - 1) * NA_QB

    v_window = pl.BlockSpec((pl.Element(win), pl.Element(NA_WIDTH)),
                            lambda i: (_na_step_window_start(i, nb) * NA_QB, NA_WIDTH))
    kt_window = pl.BlockSpec((pl.Element(NA_WIDTH), pl.Element(win)),
                             lambda i: (0, _na_step_window_start(i, nb) * NA_QB))

    tab_spec = pl.BlockSpec(tab.shape, lambda i: (0, 0, 0, 0), pipeline_mode=pl.Buffered(1))
    return pl.pallas_call(
        functools.partial(_na_kernel, nb=nb),
        grid=(nb // NA_STEP,),
        in_specs=[pl.BlockSpec((NA_STEP * NA_QB, NA_WIDTH), lambda i: (i, 0)),
                  kt_window, v_window, tab_spec, pl.BlockSpec((1, NA_WIDTH), lambda i: (0, 0))],
        out_specs=pl.BlockSpec((NA_STEP * NA_QB, NA_WIDTH), lambda i: (i, 0)),
        out_shape=jax.ShapeDtypeStruct((S, NA_WIDTH), bf16),
        scratch_shapes=[pltpu.VMEM((NA_STEP, NA_HEADS // NA_HEAD_GROUPS, NA_QB, NA_KB * NA_QB), f32),
                        pltpu.VMEM((NA_STEP, NA_HEADS // NA_HEAD_GROUPS, NA_QB, NA_KB * NA_QB), bf16),
                        pltpu.VMEM((NA_STEP * NA_QB, NA_WIDTH), f32)],
        compiler_params=_params(("parallel",)),
        name="na_attn",
    )(P, KT, P, tab, g)


def _memkv_kernel(m_ref, g_ref, w_ref, o_ref):
    o_ref[...] = jnp.dot(_rms(m_ref[...], g_ref[...]).astype(bf16), w_ref[...].astype(bf16),
                         preferred_element_type=f32).astype(bf16)


def _mem_kv(mem2, g, w):
    return pl.pallas_call(
        _memkv_kernel,
        out_shape=jax.ShapeDtypeStruct((N_MEM, 2 * MEM_WIDTH), bf16),
        compiler_params=pltpu.CompilerParams(vmem_limit_bytes=VMEM_LIMIT),
        name="mem_kv",
    )(mem2, g, w)


def _ssm_prep_kernel(lre_r, lim_r, ldt_r, bre_ref, bim_ref, cre_ref, cim_ref, win_ref,
                     wtoep_ref, wstate_ref, wcarry_ref, apow_ref, win_bf_ref, *, seg_len):
    win_bf_ref[...] = win_ref[...].astype(bf16)
    T = SSM_T
    TS = TILE_STATES
    reps = LANES // SSM_CH
    assert T == SUBLANES

    def same_group(shape, row_div, col_div):
        r = lax.broadcasted_iota(jnp.int32, shape, 0) // row_div
        c = lax.broadcasted_iota(jnp.int32, shape, 1) // col_div
        return (r == c).astype(f32)

    mask_state = same_group((LANES, TS), SSM_CH, SSM_STATE)
    mask_chan = same_group((LANES, LANES), SSM_CH, SSM_CH)

    kk = []
    for d in range(2):
        lre = jnp.minimum(lre_r[d, 0], LAM_RE_MAX)
        lim = lim_r[d, 0]
        dt = jnp.exp(ldt_r[d, 0])
        zr = lre * dt
        zi = lim * dt
        er = jnp.exp(zr)
        nr = er * jnp.cos(zi) - 1.0
        ni = er * jnp.sin(zi)
        den = lre * lre + lim * lim
        qr = (nr * lre + ni * lim) / den
        qi = (ni * lre - nr * lim) / den
        bre = bre_ref[d, 0]
        bim = bim_ref[d, 0]
        bbr = qr * bre - qi * bim
        bbi = qr * bim + qi * bre
        e0 = lax.broadcasted_iota(jnp.int32, (T, TS), 0).astype(f32)
        mag0 = jnp.exp(zr * e0)
        p0r, p0i = mag0 * jnp.cos(zi * e0), mag0 * jnp.sin(zi * e0)
        mag1 = jnp.exp(zr * (e0 + 1.0))
        p1r, p1i = mag1 * jnp.cos(zi * (e0 + 1.0)), mag1 * jnp.sin(zi * (e0 + 1.0))
        wc_r = [p0r[e:e + 1] * bbr - p0i[e:e + 1] * bbi for e in range(T)]
        wc_i = [p0r[e:e + 1] * bbi + p0i[e:e + 1] * bbr for e in range(T)]

        for s in range(T):
            e = T - 1 - s if d == 0 else s
            rows = slice(s * LANES, (s + 1) * LANES)
            base = d * 2 * TS
            wstate_ref[0, rows, base:base + TS] = (
                jnp.concatenate([wc_r[e]] * reps, axis=0) * mask_state).astype(bf16)
            wstate_ref[0, rows, base + TS:base + 2 * TS] = (
                jnp.concatenate([wc_i[e]] * reps, axis=0) * mask_state).astype(bf16)

        cre = cre_ref[d, 0]
        cim = cim_ref[d, 0]
        nt = (((1,), (1,)), ((), ()))
        kc = (lax.dot_general(jnp.concatenate(wc_r, axis=0), jnp.concatenate([cre] * reps, axis=0) * mask_state,
                              nt, precision=lax.Precision.HIGHEST, preferred_element_type=f32)
              - lax.dot_general(jnp.concatenate(wc_i, axis=0), jnp.concatenate([cim] * reps, axis=0) * mask_state,
                                nt, precision=lax.Precision.HIGHEST, preferred_element_type=f32))
        kk.append([jnp.concatenate([kc[e * SSM_CH:(e + 1) * SSM_CH]] * reps, axis=0) * mask_chan
                   for e in range(T)])

        for t in range(T):
            e = t if d == 0 else T - 1 - t
            pr, pi = p1r[e:e + 1], p1i[e:e + 1]
            rows = slice(t * LANES, (t + 1) * LANES)
            base = d * 2 * TS
            wcarry_ref[0, rows, base:base + TS] = (
                jnp.concatenate([cre * pr - cim * pi] * reps, axis=0) * mask_state).astype(bf16)
            wcarry_ref[0, rows, base + TS:base + 2 * TS] = (
                jnp.concatenate([-(cre * pi + cim * pr)] * reps, axis=0) * mask_state).astype(bf16)

        for part, steps in enumerate((T, T * seg_len)):
            k1 = (lax.broadcasted_iota(jnp.int32, (SUBLANES, TS), 0) + 1).astype(f32) * float(steps)
            mag = jnp.exp(zr * k1)
            rows = slice(part * SUBLANES, (part + 1) * SUBLANES)
            apow_ref[0, rows, d * 2 * TS:d * 2 * TS + TS] = mag * jnp.cos(zi * k1)
            apow_ref[0, rows, d * 2 * TS + TS:(d + 1) * 2 * TS] = mag * jnp.sin(zi * k1)

    for s in range(T):
        for t in range(T):
            if t > s:
                blk = kk[0][t - s]
            elif t < s:
                blk = kk[1][s - t]
            else:
                blk = kk[0][0] + kk[1][0]
            wtoep_ref[0, s * LANES:(s + 1) * LANES, t * LANES:(t + 1) * LANES] = blk.astype(bf16)


def _ssm_prep(lam_re, lam_im, log_dt, b_re, b_im, c_re, c_im, S, w_in):
    TS, NT, T = TILE_STATES, N_TILES, SSM_T
    gp = GROUPS_PER_TILE
    ldt = jnp.broadcast_to(log_dt[:, :, None], lam_re.shape)
    rows = [a.reshape(2, NT, 1, TS) for a in (lam_re, lam_im, ldt)]

    def b_layout(b):
        return b.reshape(2, NT, gp, SSM_STATE, SSM_CH).transpose(0, 1, 4, 2, 3).reshape(2, NT, SSM_CH, TS)

    def c_layout(c):
        return c.reshape(2, NT, gp, SSM_CH, SSM_STATE).transpose(0, 1, 3, 2, 4).reshape(2, NT, SSM_CH, TS)

    row_spec = pl.BlockSpec((2, 1, 1, TS), lambda g: (0, g, 0, 0))
    bc_spec = pl.BlockSpec((2, 1, SSM_CH, TS), lambda g: (0, g, 0, 0))
    win_spec = pl.BlockSpec((w_in.shape[0] // NT, w_in.shape[1]), lambda g: (g, 0))
    return pl.pallas_call(
        functools.partial(_ssm_prep_kernel, seg_len=S // T // SUBLANES),
        grid=(NT,),
        in_specs=[row_spec] * 3 + [bc_spec] * 4 + [win_spec],
        out_specs=[pl.BlockSpec((1, T * LANES, T * LANES), lambda g: (g, 0, 0)),
                   pl.BlockSpec((1, T * LANES, STATE_COLS), lambda g: (g, 0, 0)),
                   pl.BlockSpec((1, T * LANES, STATE_COLS), lambda g: (g, 0, 0)),
                   pl.BlockSpec((1, 2 * SUBLANES, STATE_COLS), lambda g: (g, 0, 0)), win_spec],
        out_shape=[jax.ShapeDtypeStruct((NT, T * LANES, T * LANES), bf16),
                   jax.ShapeDtypeStruct((NT, T * LANES, STATE_COLS), bf16),
                   jax.ShapeDtypeStruct((NT, T * LANES, STATE_COLS), bf16),
                   jax.ShapeDtypeStruct((NT, 2 * SUBLANES, STATE_COLS), f32),
                   jax.ShapeDtypeStruct(w_in.shape, bf16)],
        compiler_params=_params(("parallel",)),
        name="ssm_prep",
    )(*rows, b_layout(b_re), b_layout(b_im), c_layout(c_re), c_layout(c_im), w_in)


def _cmul_add(xr, xi, ar, ai, sr, si):
    return xr + ar * sr - ai * si, xi + ar * si + ai * sr


def _scan_rows(xr, xi, apr, api, cr, ci, reverse):
    n = SUBLANES
    rows = lax.broadcasted_iota(jnp.int32, xr.shape, 0)
    for k in (1, 2, 4):
        ar, ai = apr[k - 1:k], api[k - 1:k]
        if reverse:
            sr, si = pltpu.roll(xr, n - k, 0), pltpu.roll(xi, n - k, 0)
            keep = rows < n - k
        else:
            sr, si = pltpu.roll(xr, k, 0), pltpu.roll(xi, k, 0)
            keep = rows >= k
        xr, xi = _cmul_add(xr, xi, ar, ai, jnp.where(keep, sr, 0.0), jnp.where(keep, si, 0.0))
    if reverse:
        pwr = jnp.concatenate([apr[n - 1 - j:n - j] for j in range(n)], axis=0)
        pwi = jnp.concatenate([api[n - 1 - j:n - j] for j in range(n)], axis=0)
    else:
        pwr, pwi = apr, api
    hr, hi = _cmul_add(xr, xi, pwr, pwi, cr, ci)
    if reverse:
        inr = jnp.where(rows == n - 1, cr, pltpu.roll(hr, n - 1, 0))
        ini = jnp.where(rows == n - 1, ci, pltpu.roll(hi, n - 1, 0))
        return inr, ini, hr[0:1], hi[0:1]
    inr = jnp.where(rows == 0, cr, pltpu.roll(hr, 1, 0))
    ini = jnp.where(rows == 0, ci, pltpu.roll(hi, 1, 0))
    return inr, ini, hr[n - 1:n], hi[n - 1:n]


def _ssm_kernel(u_ref, d_ref, wtoep_ref, wstate_ref, wcarry_ref, apow_ref, o_ref, h_scr, *, nb, cb):
    T = SSM_T
    TS = TILE_STATES
    j = pl.program_id(1)

    nseg = SUBLANES
    seg = nb * cb // nseg
    assert cb % seg == 0

    def chunk_rows():
        return [u_ref[pl.ds(t, cb, stride=T), :] for t in range(T)]

    def batch_rows(b):
        return [pl.ds(b * (cb // seg) + part, seg, stride=nseg) for part in range(cb // seg)]

    @pl.when(j < nb)
    def _():
        x = jnp.concatenate([ut.astype(bf16) for ut in chunk_rows()], axis=1)
        res = jnp.dot(x, wstate_ref[0], preferred_element_type=f32)
        for part, rows in enumerate(batch_rows(j)):
            for c in range(STATE_COLS // LANES):
                h_scr[c, rows, :] = res[part * seg:(part + 1) * seg, c * LANES:(c + 1) * LANES]

    @pl.when(j == nb - 1)
    def _():
        tiles = TS // LANES

        def cols(d, ri):
            return slice((2 * d + ri) * TS, (2 * d + ri + 1) * TS)

        def rows_at(p):
            return pl.ds(pl.multiple_of(p * nseg, nseg), nseg)

        def load(rows, d, ri):
            return jnp.concatenate([h_scr[(2 * d + ri) * tiles + c, rows, :] for c in range(tiles)], axis=1)

        def store(rows, d, ri, val):
            for c in range(tiles):
                h_scr[(2 * d + ri) * tiles + c, rows, :] = val[:, c * LANES:(c + 1) * LANES]

        a = [[jnp.broadcast_to(apow_ref[0, 0:1, cols(d, ri)], (nseg, TS)) for ri in range(2)] for d in range(2)]
        zero = jnp.zeros((nseg, TS), f32)

        def step(p, h, write):
            hfr, hfi, hbr, hbi = h
            rf, rb = rows_at(p), rows_at(seg - 1 - p)
            xfr, xfi, xbr, xbi = load(rf, 0, 0), load(rf, 0, 1), load(rb, 1, 0), load(rb, 1, 1)
            if write:
                store(rf, 0, 0, hfr)
                store(rf, 0, 1, hfi)
                store(rb, 1, 0, hbr)
                store(rb, 1, 1, hbi)
            hfr, hfi = _cmul_add(xfr, xfi, a[0][0], a[0][1], hfr, hfi)
            hbr, hbi = _cmul_add(xbr, xbi, a[1][0], a[1][1], hbr, hbi)
            return hfr, hfi, hbr, hbi

        ends = lax.fori_loop(0, seg, lambda p, h: step(p, h, False), (zero, zero, zero, zero), unroll=2)
        z1 = jnp.zeros((1, TS), f32)
        enter = []
        for d in range(2):
            cr, ci, _, _ = _scan_rows(ends[2 * d], ends[2 * d + 1], apow_ref[0, nseg:2 * nseg, cols(d, 0)],
                                      apow_ref[0, nseg:2 * nseg, cols(d, 1)], z1, z1, d == 1)
            enter += [cr, ci]
        lax.fori_loop(0, seg, lambda p, h: step(p, h, True), tuple(enter), unroll=2)

    @pl.when(j >= nb)
    def _():
        us = chunk_rows()
        x = jnp.concatenate([ut.astype(bf16) for ut in us], axis=1)
        hc = jnp.concatenate(
            [jnp.concatenate([h_scr[c, rows, :].astype(bf16) for c in range(STATE_COLS // LANES)], axis=1)
             for rows in batch_rows(j - nb)], axis=0)
        y = (jnp.dot(x, wtoep_ref[0], preferred_element_type=f32)
             + lax.dot_general(hc, wcarry_ref[0], (((1,), (1,)), ((), ())), preferred_element_type=f32))
        for t in range(T):
            yt = y[:, t * LANES:(t + 1) * LANES] + d_ref[...] * us[t]
            o_ref[pl.ds(t, cb, stride=T), :] = jax.nn.gelu(yt)


def _ssm_main(u, d_row, wtoep, wstate, wcarry, apow, tb=4096):
    S = u.shape[0]
    T = SSM_T
    nb = S // tb
    cb = tb // T
    return pl.pallas_call(
        functools.partial(_ssm_kernel, nb=nb, cb=cb),
        grid=(N_TILES, 2 * nb),
        in_specs=[pl.BlockSpec((tb, LANES), lambda g, j: (j % nb, g)),
                  pl.BlockSpec((1, LANES), lambda g, j: (0, g)),
                  pl.BlockSpec((1, T * LANES, T * LANES), lambda g, j: (g, 0, 0)),
                  pl.BlockSpec((1, T * LANES, STATE_COLS), lambda g, j: (g, 0, 0)),
                  pl.BlockSpec((1, T * LANES, STATE_COLS), lambda g, j: (g, 0, 0)),
                  pl.BlockSpec((1, 2 * SUBLANES, STATE_COLS), lambda g, j: (g, 0, 0))],
        out_specs=pl.BlockSpec((tb, LANES), lambda g, j: (jnp.maximum(j - nb, 0), g)),
        out_shape=jax.ShapeDtypeStruct((S, SSM_WIDTH), f32),
        scratch_shapes=[pltpu.VMEM((STATE_COLS // LANES, S // T, LANES), f32)],
        compiler_params=_params(("arbitrary", "arbitrary")),
        name="ssm_main",
    )(u, d_row, wtoep, wstate, wcarry, apow)


def _outproj_kernel(yna_ref, yssm_ref, ymem_ref, x_ref, wglu_ref, bglu_ref, gssm_ref, wout_ref,
                    gpost_ref, gmlp_ref, x1_ref, h2_ref, *, nsub):
    a, b = NA_WIDTH, NA_WIDTH + SSM_WIDTH
    sub = x_ref.shape[0] // nsub
    tiles = [slice(r * sub, (r + 1) * sub) for r in range(nsub)]
    ys = []
    for rows in tiles:
        ya = yssm_ref[rows, :]
        gate = jax.nn.sigmoid(jnp.dot(ya.astype(bf16), wglu_ref[...], preferred_element_type=f32)
                              + bglu_ref[...])
        ys.append(_rms(ya * gate, gssm_ref[...]).astype(bf16))
    accs = [jnp.dot(yna_ref[rows, :], wout_ref[0:a, :], preferred_element_type=f32)
            + jnp.dot(ys[r], wout_ref[a:b, :], preferred_element_type=f32)
            + jnp.dot(ymem_ref[rows, :], wout_ref[b:, :], preferred_element_type=f32)
            for r, rows in enumerate(tiles)]
    for rows, acc in zip(tiles, accs):
        x1 = x_ref[rows, :] + _rms(acc, gpost_ref[...])
        x1_ref[rows, :] = x1
        h2_ref[rows, :] = _rms(x1, gmlp_ref[...]).astype(bf16)


def _out_proj(y_na, y_ssm, y_mem, x2, wglu_bf, b_glu, g_ssm, wout_bf, g_post, g_mlp, tm=512):
    S = x2.shape[0]
    row = lambda w: pl.BlockSpec((tm, w), lambda i: (i, 0))
    vec = lambda w: pl.BlockSpec((1, w), lambda i: (0, 0))
    return pl.pallas_call(
        functools.partial(_outproj_kernel, nsub=4),
        grid=(S // tm,),
        in_specs=[row(NA_WIDTH), row(SSM_WIDTH), row(MEM_WIDTH), row(D_MODEL),
                  pl.BlockSpec((SSM_WIDTH, SSM_WIDTH), lambda i: (0, 0)), vec(SSM_WIDTH), vec(SSM_WIDTH),
                  pl.BlockSpec((D_MODEL, D_MODEL), lambda i: (0, 0), pipeline_mode=pl.Buffered(1)),
                  vec(D_MODEL), vec(D_MODEL)],
        out_specs=[row(D_MODEL), row(D_MODEL)],
        out_shape=[jax.ShapeDtypeStruct((S, D_MODEL), f32), jax.ShapeDtypeStruct((S, D_MODEL), bf16)],
        compiler_params=_params(("parallel",)),
        name="out_proj",
    )(y_na, y_ssm, y_mem, x2, wglu_bf, b_glu, g_ssm, wout_bf, g_post, g_mlp)


def _mlp_kernel(h_ref, w1_ref, w2_ref, x1_hbm, g_ref, o_ref, hid_scr, x1_buf, x1_sem, *, nsplit):
    i = pl.program_id(0)
    k = pl.program_id(1)
    nk = pl.num_programs(1) - 1
    tm = o_ref.shape[0]
    wn = D_MODEL // nsplit

    def x1_copy():
        return pltpu.make_async_copy(x1_hbm.at[pl.ds(pl.multiple_of(i * tm, tm), tm), :], x1_buf, x1_sem)

    def hidden():
        hid = jnp.dot(h_ref[...], w1_ref[...], preferred_element_type=f32)
        return jnp.square(jnp.maximum(hid, 0.0)).astype(bf16)

    def partial_out(n, first=False):
        cols = slice(n * wn, (n + 1) * wn)
        part = jnp.dot(hid_scr[(k + 1) % 2], w2_ref[:, cols], preferred_element_type=f32)
        return part if first else o_ref[:, cols] + part

    @pl.when(k == 0)
    def _():
        x1_copy().start()
        hid_scr[0] = hidden()

    @pl.when(k == 1)
    def _():
        for n in range(nsplit):
            o_ref[:, n * wn:(n + 1) * wn] = partial_out(n, first=True)
        hid_scr[1] = hidden()

    @pl.when((k > 1) & (k < nk))
    def _():
        for n in range(nsplit):
            o_ref[:, n * wn:(n + 1) * wn] = partial_out(n)
        hid_scr[k % 2] = hidden()

    @pl.when(k == nk)
    def _():
        ssq = jnp.zeros((tm, 1), f32)
        for n in range(nsplit):
            f = partial_out(n)
            o_ref[:, n * wn:(n + 1) * wn] = f
            ssq = ssq + jnp.sum(f * f, axis=-1, keepdims=True)
        scale = lax.rsqrt(ssq * (1.0 / D_MODEL) + EPS)
        x1_copy().wait()
        o_ref[...] = x1_buf[...] + o_ref[...] * scale * g_ref[...]


def _mlp(h2, w1_bf, w2_bf, x1, g, tm=1024, tk=1024):
    S = h2.shape[0]
    nk = D_FF // tk
    return pl.pallas_call(
        functools.partial(_mlp_kernel, nsplit=4),
        grid=(S // tm, nk + 1),
        in_specs=[pl.BlockSpec((tm, D_MODEL), lambda i, k: (i, 0)),
                  pl.BlockSpec((D_MODEL, tk), lambda i, k: (0, jnp.minimum(k, nk - 1))),
                  pl.BlockSpec((tk, D_MODEL), lambda i, k: (jnp.maximum(k - 1, 0), 0)),
                  pl.BlockSpec(memory_space=pl.ANY),
                  pl.BlockSpec((1, D_MODEL), lambda i, k: (0, 0))],
        out_specs=pl.BlockSpec((tm, D_MODEL), lambda i, k: (i, 0)),
        out_shape=jax.ShapeDtypeStruct((S, D_MODEL), f32),
        scratch_shapes=[pltpu.VMEM((2, tm, tk), bf16), pltpu.VMEM((tm, D_MODEL), f32),
                        pltpu.SemaphoreType.DMA(())],
        compiler_params=_params(("arbitrary", "arbitrary")),
        name="mlp",
    )(h2, w1_bf, w2_bf, x1, g)


def kernel(x, mem, norm_mix_pre, w_in, na_rpb, ssm_lam_re, ssm_lam_im, ssm_log_dt, ssm_b_re, ssm_b_im, ssm_c_re, ssm_c_im, ssm_d, w_glu, b_glu, mem_norm, w_mem_kv, out_norm_na, out_norm_ssm, out_norm_mem, w_out, norm_mix_post, norm_mlp_pre, w_ff1, w_ff2, norm_mlp_post):
    B, S, _ = x.shape
    assert B == 1 and x.shape[2] == D_MODEL and S % 4096 == 0 and w_in.shape[0] == 1
    x2 = x[0]
    l = 0
    vec = lambda a: a[l].reshape(1, -1)

    kv = _mem_kv(mem[0], vec(mem_norm), w_mem_kv[l])
    wtoep, wstate, wcarry, apow, win_bf = _ssm_prep(ssm_lam_re[l], ssm_lam_im[l], ssm_log_dt[l], ssm_b_re[l],
                                                    ssm_b_im[l], ssm_c_re[l], ssm_c_im[l], S, w_in[l])
    P, KT, u, y_mem, wglu_bf, wout_bf, w1_bf, w2_bf = _proj(
        x2, vec(norm_mix_pre), win_bf, kv, vec(out_norm_mem), [w_glu[l], w_out[l], w_ff1[l], w_ff2[l]])

    y_na = _na_attn(P, KT, _na_bias(na_rpb[l], S), vec(out_norm_na))

    y_ssm = _ssm_main(u, ssm_d[l].reshape(1, SSM_WIDTH), wtoep, wstate, wcarry, apow)

    x1, h2 = _out_proj(y_na, y_ssm, y_mem, x2, wglu_bf, vec(b_glu), vec(out_norm_ssm),
                       wout_bf, vec(norm_mix_post), vec(norm_mlp_pre))
    out = _mlp(h2, w1_bf, w2_bf, x1, vec(norm_mlp_post))
    return out[None]
```
